```python
import jax, jax.numpy as jnp
from jax import lax
import numpy as np

D_MODEL = 1024
BATCH = 8
SEQ = 16384
DEPTH = 1

CHUNK = 64
SG_BLOCK = 128
A_WIDTH = 1024
A_GROUPS = 8
A_GROUP_DIM = A_WIDTH // A_GROUPS
B_WIDTH = 1024
B_GROUPS = 8
CONV_WIDTH = 3
D_FF = 4 * D_MODEL
N_BRANCHES = 2
EPS = 1e-6
IN_COLS = 2 * A_WIDTH + 3 * B_WIDTH + N_BRANCHES * D_MODEL

kernel_name = "hybrid_sgmlp_shortconv_gated_block"


def rmsnorm(x, g):
    xf = x.astype(jnp.float32)
    y = xf * lax.rsqrt(jnp.mean(xf * xf, axis=-1, keepdims=True) + EPS)
    return (y * g.astype(jnp.float32)).astype(x.dtype)


def chunk_mask():
    c = jnp.arange(SG_BLOCK) // CHUNK
    return c[None, :] <= c[:, None]


def spatial_gating(u, v, w_s, b_s):
    bsz, s, _ = v.shape
    vb = v.reshape(bsz, s // SG_BLOCK, SG_BLOCK, A_GROUPS, A_GROUP_DIM)
    w = jnp.where(chunk_mask()[None], w_s, jnp.zeros_like(w_s))
    mixed = jnp.einsum('gij,bnjgc->bnigc', w, vb) + b_s.T[None, None, :, :, None]
    return u * mixed.reshape(bsz, s, A_WIDTH)


def causal_dwconv(z, w):
    s = z.shape[1]
    zp = jnp.pad(z, ((0, 0), (CONV_WIDTH - 1, 0), (0, 0)))
    y = w[0] * zp[:, 0:s]
    for k in range(1, CONV_WIDTH):
        y = y + w[k] * zp[:, k:k + s]
    return y


def _fwd_setup_inputs(seed: int = 0) -> dict:
    key = jax.random.key(seed)
    ks = jax.random.split(key, 16)
    f32 = jnp.float32
    nrm = lambda k, shape, scale: jax.random.normal(k, shape, f32) * scale
    return {
        "x": nrm(ks[0], (BATCH, SEQ, D_MODEL), 1.0),
        "norm_mix_g": 1.0 + nrm(ks[1], (DEPTH, D_MODEL), 0.1),
        "w_in": nrm(ks[2], (DEPTH, D_MODEL, IN_COLS), D_MODEL ** -0.5),
        "b_gate": nrm(ks[3], (DEPTH, N_BRANCHES * D_MODEL), 0.01),
        "norm_v_g": 1.0 + nrm(ks[4], (DEPTH, A_WIDTH), 0.1),
        "w_s": nrm(ks[5], (DEPTH, A_GROUPS, SG_BLOCK, SG_BLOCK), SG_BLOCK ** -0.5),
        "b_s": 1.0 + nrm(ks[6], (DEPTH, A_GROUPS, SG_BLOCK), 0.1),
        "conv_w": nrm(ks[7], (DEPTH, CONV_WIDTH, B_WIDTH), CONV_WIDTH ** -0.5),
        "w_proj_a": nrm(ks[8], (DEPTH, A_WIDTH, D_MODEL), A_WIDTH ** -0.5),
        "w_proj_b": nrm(ks[9], (DEPTH, B_WIDTH, D_MODEL), B_WIDTH ** -0.5),
        "w_out": nrm(ks[10], (DEPTH, D_MODEL, D_MODEL), D_MODEL ** -0.5),
        "norm_ff_g": 1.0 + nrm(ks[11], (DEPTH, D_MODEL), 0.1),
        "w_ff1": nrm(ks[12], (DEPTH, D_MODEL, D_FF), D_MODEL ** -0.5),
        "w_ff2": nrm(ks[13], (DEPTH, D_FF, D_MODEL), D_FF ** -0.5),
        "norm_final_g": 1.0 + nrm(ks[14], (D_MODEL,), 0.1),
    }


def _fwd_reference(x, norm_mix_g, w_in, b_gate, norm_v_g, w_s, b_s, conv_w, w_proj_a, w_proj_b,
              w_out, norm_ff_g, w_ff1, w_ff2, norm_final_g):
    split_at = [A_WIDTH, 2 * A_WIDTH, 2 * A_WIDTH + B_WIDTH, 2 * A_WIDTH + 2 * B_WIDTH,
                2 * A_WIDTH + 3 * B_WIDTH, 2 * A_WIDTH + 3 * B_WIDTH + D_MODEL]
    for l in range(DEPTH):
        h = rmsnorm(x, norm_mix_g[l])
        proj = jnp.einsum('bsd,dc->bsc', h, w_in[l])
        u, v, bg, cg, xs, ga, gb = jnp.split(proj, split_at, axis=-1)
        ga = ga + b_gate[l, :D_MODEL]
        gb = gb + b_gate[l, D_MODEL:]

        u = jax.nn.gelu(u, approximate=False)
        v = rmsnorm(jax.nn.gelu(v, approximate=False), norm_v_g[l])
        a = spatial_gating(u, v, w_s[l], b_s[l])

        c = bg * causal_dwconv(cg * xs, conv_w[l])

        m = (jax.nn.sigmoid(ga) * jnp.einsum('bsc,cd->bsd', a, w_proj_a[l])
             + jax.nn.sigmoid(gb) * jnp.einsum('bsc,cd->bsd', c, w_proj_b[l]))
        x = x + jnp.einsum('bsd,de->bse', m, w_out[l])

        hf = rmsnorm(x, norm_ff_g[l])
        z = jax.nn.relu(jnp.einsum('bsd,df->bsf', hf, w_ff1[l]))
        x = x + jnp.einsum('bsf,fd->bsd', z * z, w_ff2[l])
    return rmsnorm(x, norm_final_g)


import jax as _jax
import jax.numpy as _jnp

TWIN_FORMAT = 'train_step'
FWD_PARAMS = ['x', 'norm_mix_g', 'w_in', 'b_gate', 'norm_v_g', 'w_s', 'b_s', 'conv_w', 'w_proj_a', 'w_proj_b', 'w_out', 'norm_ff_g', 'w_ff1', 'w_ff2', 'norm_final_g']
TWIN_WEIGHTS = ['norm_mix_g', 'w_in', 'b_gate', 'norm_v_g', 'w_s', 'b_s', 'conv_w', 'w_proj_a', 'w_proj_b', 'w_out', 'norm_ff_g', 'w_ff1', 'w_ff2', 'norm_final_g']
TWIN_DIFF_INPUT = 'x'
TWIN_INPUTS = ['x', 'norm_mix_g', 'w_in', 'b_gate', 'norm_v_g', 'w_s', 'b_s', 'conv_w', 'w_proj_a', 'w_proj_b', 'w_out', 'norm_ff_g', 'w_ff1', 'w_ff2', 'norm_final_g', 'loss_target', 'm_norm_mix_g', 'm_w_in', 'm_b_gate', 'm_norm_v_g', 'm_w_s', 'm_b_s', 'm_conv_w', 'm_w_proj_a', 'm_w_proj_b', 'm_w_out', 'm_norm_ff_g', 'm_w_ff1', 'm_w_ff2', 'm_norm_final_g', 'v_norm_mix_g', 'v_w_in', 'v_b_gate', 'v_norm_v_g', 'v_w_s', 'v_b_s', 'v_conv_w', 'v_w_proj_a', 'v_w_proj_b', 'v_w_out', 'v_norm_ff_g', 'v_w_ff1', 'v_w_ff2', 'v_norm_final_g']
TWIN_OUTPUTS = ['loss', 'grad_x', 'grad_norm_mix_g', 'grad_w_in', 'grad_b_gate', 'grad_norm_v_g', 'grad_w_s', 'grad_b_s', 'grad_conv_w', 'grad_w_proj_a', 'grad_w_proj_b', 'grad_w_out', 'grad_norm_ff_g', 'grad_w_ff1', 'grad_w_ff2', 'grad_norm_final_g', 'delta_norm_mix_g', 'delta_w_in', 'delta_b_gate', 'delta_norm_v_g', 'delta_w_s', 'delta_b_s', 'delta_conv_w', 'delta_w_proj_a', 'delta_w_proj_b', 'delta_w_out', 'delta_norm_ff_g', 'delta_w_ff1', 'delta_w_ff2', 'delta_norm_final_g', 'new_m_norm_mix_g', 'new_m_w_in', 'new_m_b_gate', 'new_m_norm_v_g', 'new_m_w_s', 'new_m_b_s', 'new_m_conv_w', 'new_m_w_proj_a', 'new_m_w_proj_b', 'new_m_w_out', 'new_m_norm_ff_g', 'new_m_w_ff1', 'new_m_w_ff2', 'new_m_norm_final_g', 'new_v_norm_mix_g', 'new_v_w_in', 'new_v_b_gate', 'new_v_norm_v_g', 'new_v_w_s', 'new_v_b_s', 'new_v_conv_w', 'new_v_w_proj_a', 'new_v_w_proj_b', 'new_v_w_out', 'new_v_norm_ff_g', 'new_v_w_ff1', 'new_v_w_ff2', 'new_v_norm_final_g']
TWIN_LEAF_KINDS = {'loss': 'loss', 'grad_x': 'grad_x', 'grad_norm_mix_g': 'grad_w', 'grad_w_in': 'grad_w', 'grad_b_gate': 'grad_w', 'grad_norm_v_g': 'grad_w', 'grad_w_s': 'grad_w', 'grad_b_s': 'grad_w', 'grad_conv_w': 'grad_w', 'grad_w_proj_a': 'grad_w', 'grad_w_proj_b': 'grad_w', 'grad_w_out': 'grad_w', 'grad_norm_ff_g': 'grad_w', 'grad_w_ff1': 'grad_w', 'grad_w_ff2': 'grad_w', 'grad_norm_final_g': 'grad_w', 'delta_norm_mix_g': 'delta_w', 'delta_w_in': 'delta_w', 'delta_b_gate': 'delta_w', 'delta_norm_v_g': 'delta_w', 'delta_w_s': 'delta_w', 'delta_b_s': 'delta_w', 'delta_conv_w': 'delta_w', 'delta_w_proj_a': 'delta_w', 'delta_w_proj_b': 'delta_w', 'delta_w_out': 'delta_w', 'delta_norm_ff_g': 'delta_w', 'delta_w_ff1': 'delta_w', 'delta_w_ff2': 'delta_w', 'delta_norm_final_g': 'delta_w', 'new_m_norm_mix_g': 'new_m', 'new_m_w_in': 'new_m', 'new_m_b_gate': 'new_m', 'new_m_norm_v_g': 'new_m', 'new_m_w_s': 'new_m', 'new_m_b_s': 'new_m', 'new_m_conv_w': 'new_m', 'new_m_w_proj_a': 'new_m', 'new_m_w_proj_b': 'new_m', 'new_m_w_out': 'new_m', 'new_m_norm_ff_g': 'new_m', 'new_m_w_ff1': 'new_m', 'new_m_w_ff2': 'new_m', 'new_m_norm_final_g': 'new_m', 'new_v_norm_mix_g': 'new_v', 'new_v_w_in': 'new_v', 'new_v_b_gate': 'new_v', 'new_v_norm_v_g': 'new_v', 'new_v_w_s': 'new_v', 'new_v_b_s': 'new_v', 'new_v_conv_w': 'new_v', 'new_v_w_proj_a': 'new_v', 'new_v_w_proj_b': 'new_v', 'new_v_w_out': 'new_v', 'new_v_norm_ff_g': 'new_v', 'new_v_w_ff1': 'new_v', 'new_v_w_ff2': 'new_v', 'new_v_norm_final_g': 'new_v'}


def _forward(args):
    return _fwd_reference(*[args[k] for k in FWD_PARAMS])


def _output_shape():
    def fwd():
        inp = _fwd_setup_inputs(0)
        return _fwd_reference(*[inp[k] for k in FWD_PARAMS])
    out = _jax.eval_shape(fwd)
    return out.shape, out.dtype

N_MICROBATCH = 1
ADAM_LR = 0.001
ADAM_B1 = 0.9
ADAM_B2 = 0.999
ADAM_EPS = 1e-08
ADAM_WD = 0.01
ADAM_STEP = 10
PER_EXAMPLE_BATCH_AXIS = {'x': 0, 'loss_target': 0}
SHARED_INPUTS = []
_WEIGHT_DTYPES = {'norm_mix_g': _jnp.float32, 'w_in': _jnp.float32, 'b_gate': _jnp.float32, 'norm_v_g': _jnp.float32, 'w_s': _jnp.float32, 'b_s': _jnp.float32, 'conv_w': _jnp.float32, 'w_proj_a': _jnp.float32, 'w_proj_b': _jnp.float32, 'w_out': _jnp.float32, 'norm_ff_g': _jnp.float32, 'w_ff1': _jnp.float32, 'w_ff2': _jnp.float32, 'norm_final_g': _jnp.float32}
MOMENT_SCALE = {'norm_mix_g': 3.760274e-01, 'w_in': 1.382613e-01, 'b_gate': 1.243571e-01, 'norm_v_g': 9.695046e-02, 'w_s': 9.764789e-02, 'b_s': 1.153508e-01, 'conv_w': 1.773885e-01, 'w_proj_a': 4.245418e-01, 'w_proj_b': 1.791953e-01, 'w_out': 4.361449e-01, 'norm_ff_g': 3.344564e-01, 'w_ff1': 1.893062e-01, 'w_ff2': 1.671737e+00, 'norm_final_g': 1.296396e+02}


def _to_microbatches(a, axis):
    t = _jnp.moveaxis(a, axis, 0)
    t = t.reshape((N_MICROBATCH, t.shape[0] // N_MICROBATCH) + t.shape[1:])
    return _jnp.moveaxis(t, 1, axis + 1)


def setup_inputs(seed: int = 0) -> dict:
    inp = _fwd_setup_inputs(seed)
    key = _jax.random.fold_in(_jax.random.key(seed), 7919)
    shape, _ = _output_shape()
    out = dict(inp)
    out["loss_target"] = _jax.random.normal(_jax.random.fold_in(key, 0), shape, _jnp.float32)
    for i, name in enumerate(TWIN_WEIGHTS):
        w = inp[name].astype(_jnp.float32)
        if MOMENT_SCALE is None:
            s = _jnp.sqrt(_jnp.mean(_jnp.square(w)) + 1e-30)
        else:
            s = MOMENT_SCALE[name]
        km, kv = _jax.random.split(_jax.random.fold_in(key, i + 1))
        out[name] = w
        out["m_" + name] = s * _jax.random.normal(km, w.shape, _jnp.float32)
        out["v_" + name] = (s * s) * _jax.random.uniform(kv, w.shape, _jnp.float32, 0.5, 1.5)
    if N_MICROBATCH > 1:
        for name, axis in PER_EXAMPLE_BATCH_AXIS.items():
            out[name] = _to_microbatches(out[name], axis)
    return {'x': out['x'], 'norm_mix_g': out['norm_mix_g'], 'w_in': out['w_in'], 'b_gate': out['b_gate'], 'norm_v_g': out['norm_v_g'], 'w_s': out['w_s'], 'b_s': out['b_s'], 'conv_w': out['conv_w'], 'w_proj_a': out['w_proj_a'], 'w_proj_b': out['w_proj_b'], 'w_out': out['w_out'], 'norm_ff_g': out['norm_ff_g'], 'w_ff1': out['w_ff1'], 'w_ff2': out['w_ff2'], 'norm_final_g': out['norm_final_g'], 'loss_target': out['loss_target'], 'm_norm_mix_g': out['m_norm_mix_g'], 'm_w_in': out['m_w_in'], 'm_b_gate': out['m_b_gate'], 'm_norm_v_g': out['m_norm_v_g'], 'm_w_s': out['m_w_s'], 'm_b_s': out['m_b_s'], 'm_conv_w': out['m_conv_w'], 'm_w_proj_a': out['m_w_proj_a'], 'm_w_proj_b': out['m_w_proj_b'], 'm_w_out': out['m_w_out'], 'm_norm_ff_g': out['m_norm_ff_g'], 'm_w_ff1': out['m_w_ff1'], 'm_w_ff2': out['m_w_ff2'], 'm_norm_final_g': out['m_norm_final_g'], 'v_norm_mix_g': out['v_norm_mix_g'], 'v_w_in': out['v_w_in'], 'v_b_gate': out['v_b_gate'], 'v_norm_v_g': out['v_norm_v_g'], 'v_w_s': out['v_w_s'], 'v_b_s': out['v_b_s'], 'v_conv_w': out['v_conv_w'], 'v_w_proj_a': out['v_w_proj_a'], 'v_w_proj_b': out['v_w_proj_b'], 'v_w_out': out['v_w_out'], 'v_norm_ff_g': out['v_norm_ff_g'], 'v_w_ff1': out['v_w_ff1'], 'v_w_ff2': out['v_w_ff2'], 'v_norm_final_g': out['v_norm_final_g']}


def _loss(weights, diff, rest, loss_target):
    with _jax.named_scope("forward"):
        args = {**rest, TWIN_DIFF_INPUT: diff, **{k: w.astype(_WEIGHT_DTYPES[k]) for k, w in weights.items()}}
        y = _forward(args)
    with _jax.named_scope("loss_head"):
        err = _jnp.square(y.astype(_jnp.float32) - loss_target)
        return 0.5 * _jnp.sum(_jnp.mean(err, axis=-1)) if err.ndim else 0.5 * err


def _adamw(w, g, m, v):
    m = ADAM_B1 * m + (1.0 - ADAM_B1) * g
    v = ADAM_B2 * v + (1.0 - ADAM_B2) * _jnp.square(g)
    m_hat = m / (1.0 - ADAM_B1 ** ADAM_STEP)
    v_hat = v / (1.0 - ADAM_B2 ** ADAM_STEP)
    delta = -ADAM_LR * (m_hat / (_jnp.sqrt(v_hat) + ADAM_EPS) + ADAM_WD * w)
    return delta, m, v


def reference(x, norm_mix_g, w_in, b_gate, norm_v_g, w_s, b_s, conv_w, w_proj_a, w_proj_b, w_out, norm_ff_g, w_ff1, w_ff2, norm_final_g, loss_target, m_norm_mix_g, m_w_in, m_b_gate, m_norm_v_g, m_w_s, m_b_s, m_conv_w, m_w_proj_a, m_w_proj_b, m_w_out, m_norm_ff_g, m_w_ff1, m_w_ff2, m_norm_final_g, v_norm_mix_g, v_w_in, v_b_gate, v_norm_v_g, v_w_s, v_b_s, v_conv_w, v_w_proj_a, v_w_proj_b, v_w_out, v_norm_ff_g, v_w_ff1, v_w_ff2, v_norm_final_g):
    given = dict(x=x, norm_mix_g=norm_mix_g, w_in=w_in, b_gate=b_gate, norm_v_g=norm_v_g, w_s=w_s, b_s=b_s, conv_w=conv_w, w_proj_a=w_proj_a, w_proj_b=w_proj_b, w_out=w_out, norm_ff_g=norm_ff_g, w_ff1=w_ff1, w_ff2=w_ff2, norm_final_g=norm_final_g, loss_target=loss_target, m_norm_mix_g=m_norm_mix_g, m_w_in=m_w_in, m_b_gate=m_b_gate, m_norm_v_g=m_norm_v_g, m_w_s=m_w_s, m_b_s=m_b_s, m_conv_w=m_conv_w, m_w_proj_a=m_w_proj_a, m_w_proj_b=m_w_proj_b, m_w_out=m_w_out, m_norm_ff_g=m_norm_ff_g, m_w_ff1=m_w_ff1, m_w_ff2=m_w_ff2, m_norm_final_g=m_norm_final_g, v_norm_mix_g=v_norm_mix_g, v_w_in=v_w_in, v_b_gate=v_b_gate, v_norm_v_g=v_norm_v_g, v_w_s=v_w_s, v_b_s=v_b_s, v_conv_w=v_conv_w, v_w_proj_a=v_w_proj_a, v_w_proj_b=v_w_proj_b, v_w_out=v_w_out, v_norm_ff_g=v_norm_ff_g, v_w_ff1=v_w_ff1, v_w_ff2=v_w_ff2, v_norm_final_g=v_norm_final_g)
    weights = {n: given[n] for n in TWIN_WEIGHTS}
    shared = {n: given[n] for n in SHARED_INPUTS}
    per_example = {n: given[n] for n in ['x']}
    grad_fn = _jax.value_and_grad(_loss, argnums=(0, 1))

    def one_microbatch(ex, loss_target):
        ex = dict(ex)
        diff = ex.pop(TWIN_DIFF_INPUT)
        return grad_fn(weights, diff, {**shared, **ex}, loss_target)

    if N_MICROBATCH == 1:
        loss, (grad_w, grad_x) = one_microbatch(per_example, given["loss_target"])
    else:
        def body(carry, xs):
            loss_sum, grad_sum = carry
            l_k, (gw_k, gx_k) = one_microbatch(xs[0], xs[1])
            with _jax.named_scope("update"):
                return (loss_sum + l_k, _jax.tree.map(_jnp.add, grad_sum, gw_k)), gx_k

        init = (_jnp.zeros((), _jnp.float32), _jax.tree.map(_jnp.zeros_like, weights))
        (loss, grad_w), grad_x = _jax.lax.scan(body, init, (per_example, given["loss_target"]))
    with _jax.named_scope("update"):
        delta_w, new_m, new_v = {}, {}, {}
        for n in TWIN_WEIGHTS:
            delta_w[n], new_m[n], new_v[n] = _adamw(weights[n], grad_w[n], given["m_" + n], given["v_" + n])
    return (loss, grad_x, *[grad_w[n] for n in TWIN_WEIGHTS], *[delta_w[n] for n in TWIN_WEIGHTS],
            *[new_m[n] for n in TWIN_WEIGHTS], *[new_v[n] for n in TWIN_WEIGHTS])
```

```python
import math

import jax
import jax.numpy as jnp
from jax import lax
from jax.experimental import pallas as pl
from jax.experimental.pallas import tpu as pltpu

F32 = jnp.float32
BF16 = jnp.bfloat16

N_DEV = 8
D = 1024
D_FF = 4096
IN_COLS = 7 * D
SG = 128
N_GROUPS = 8
CHUNK = 64
EPS = 1e-6
HALO = 16
LANE = 128
VMEM_LIMIT = 56 * 1024 * 1024

ADAM_LR = 0.001
ADAM_B1 = 0.9
ADAM_B2 = 0.999
ADAM_EPS = 1e-08
ADAM_WD = 0.01
ADAM_STEP = 10

SQRT_HALF = math.sqrt(0.5)
INV_SQRT_2PI = 1.0 / math.sqrt(2.0 * math.pi)

_REL = [(dx, dy, dc) for dx in (0, 1) for dy in (0, 1) for dc in (0, 1)]

_VMEM = pl.BlockSpec(memory_space=pltpu.VMEM)
_ANY = pl.BlockSpec(memory_space=pl.ANY)


def _pcall(body, **kw):
    return pl.pallas_call(body, **kw)


def _params(sem=None):
    if sem is None:
        return pltpu.CompilerParams(vmem_limit_bytes=VMEM_LIMIT)
    return pltpu.CompilerParams(dimension_semantics=sem, vmem_limit_bytes=VMEM_LIMIT)


def _const_spec(shape):
    nd = len(shape)
    return pl.BlockSpec(shape, lambda *_: (0,) * nd, pipeline_mode=pl.Buffered(1))


def _mm(a, b):
    return jnp.dot(a, b, preferred_element_type=F32)


def _nt(a, b):
    return lax.dot_general(a, b, (((1,), (1,)), ((), ())), preferred_element_type=F32)


def _tn(a, b):
    return lax.dot_general(a, b, (((0,), (0,)), ((), ())), preferred_element_type=F32)


def _rms(x):
    r = lax.rsqrt(jnp.mean(x * x, axis=-1, keepdims=True) + EPS)
    return x * r, r


def _rms_bwd(dyg, xh, r):
    return r * (dyg - xh * jnp.mean(dyg * xh, axis=-1, keepdims=True))


def _gelu(x):
    cdf = 0.5 * (1.0 + lax.erf(x * SQRT_HALF))
    return x * cdf, cdf


def _gelu_grad(x, cdf):
    return cdf + x * (jnp.exp(-0.5 * x * x) * INV_SQRT_2PI)


def _masked_ws(ws):
    i = lax.broadcasted_iota(jnp.int32, (SG, SG), 0)
    j = lax.broadcasted_iota(jnp.int32, (SG, SG), 1)
    keep = jnp.logical_or(j < CHUNK, i >= CHUNK)
    return jnp.where(keep[None], ws, jnp.zeros_like(ws))


def _shift_down(halo, q, k):
    ext = jnp.concatenate([halo, q], axis=0)
    return pltpu.roll(ext, k, 0)[halo.shape[0]:]


def _shift_up(q, nxt, k):
    ext = jnp.concatenate([q, nxt], axis=0)
    return pltpu.roll(ext, ext.shape[0] - k, 0)[:q.shape[0]]


def _col(k):
    return slice(k * D, (k + 1) * D)


def _position():
    x, y, c = lax.axis_index("x"), lax.axis_index("y"), lax.axis_index("c")
    return (x, y, c), 4 * x + 2 * y + c


def _exchange(items, send_sems, recv_sems, local_sems):
    (x, y, c), me = _position()
    started = []
    for w, (src_of, dst_of) in enumerate(items):
        own = pltpu.make_async_copy(src_of(me), dst_of(me), local_sems.at[w])
        own.start()
        started.append(own)
        for k in range(1, N_DEV):
            dx, dy, dc = _REL[k]
            peer = (1 - x if dx else x, 1 - y if dy else y, 1 - c if dc else c)
            pid = 4 * peer[0] + 2 * peer[1] + peer[2]
            cp = pltpu.make_async_remote_copy(
                src_ref=src_of(pid), dst_ref=dst_of(me),
                send_sem=send_sems.at[w * N_DEV + k], recv_sem=recv_sems.at[w * N_DEV + k],
                device_id=peer, device_id_type=pl.DeviceIdType.MESH)
            cp.start()
            started.append(cp)
    for cp in started:
        cp.wait()


def _lane_block(ref, p, width):
    return ref.at[:, pl.ds(pl.multiple_of(p * width, LANE), width)]


def _gather_weights(w_in, w_ff1, w_ff2, w_pa, w_pb, w_o, conv8):
    n_items = 7
    c_in, c_ff1 = w_in.shape[1], w_ff1.shape[1]
    r_ff2, r_p = w_ff2.shape[0], w_pa.shape[0]

    def body(win_ref, w1_ref, w2_ref, pa_ref, pb_ref, wo_ref, cw_ref,
             win_o, w1_o, w2_o, pa_o, pb_o, wo_o, cw_o,
             s_win, s_w1, s_w2, s_pa, s_pb, s_wo, send_sems, recv_sems, local_sems):
        for src, stage in ((win_ref, s_win), (w1_ref, s_w1), (w2_ref, s_w2),
                           (pa_ref, s_pa), (pb_ref, s_pb), (wo_ref, s_wo)):
            stage[...] = src[...].astype(BF16)
        items = [
            (lambda p: s_win, lambda p: _lane_block(win_o, p, c_in)),
            (lambda p: s_w1, lambda p: _lane_block(w1_o, p, c_ff1)),
            (lambda p: s_w2, lambda p: w2_o.at[pl.ds(p * r_ff2, r_ff2), :]),
            (lambda p: s_pa, lambda p: pa_o.at[pl.ds(p * r_p, r_p), :]),
            (lambda p: s_pb, lambda p: pb_o.at[pl.ds(p * r_p, r_p), :]),
            (lambda p: s_wo, lambda p: wo_o.at[pl.ds(p * r_p, r_p), :]),
            (lambda p: cw_ref, lambda p: _lane_block(cw_o, p, conv8.shape[1])),
        ]
        _exchange(items, send_sems, recv_sems, local_sems)

    out_shape = (
        jax.ShapeDtypeStruct((D, N_DEV * c_in), BF16),
        jax.ShapeDtypeStruct((D, N_DEV * c_ff1), BF16),
        jax.ShapeDtypeStruct((N_DEV * r_ff2, D), BF16),
        jax.ShapeDtypeStruct((N_DEV * r_p, D), BF16),
        jax.ShapeDtypeStruct((N_DEV * r_p, D), BF16),
        jax.ShapeDtypeStruct((N_DEV * r_p, D), BF16),
        jax.ShapeDtypeStruct((conv8.shape[0], N_DEV * conv8.shape[1]), F32),
    )
    return _pcall(
        body, name="gather_weights", out_shape=out_shape,
        in_specs=[_VMEM] * n_items, out_specs=[_ANY] * n_items,
        scratch_shapes=[pltpu.VMEM(w.shape, BF16) for w in (w_in, w_ff1, w_ff2, w_pa, w_pb, w_o)]
        + [pltpu.SemaphoreType.DMA((n_items * N_DEV,)), pltpu.SemaphoreType.DMA((n_items * N_DEV,)),
           pltpu.SemaphoreType.DMA((n_items,))],
        compiler_params=_params(),
    )(w_in, w_ff1, w_ff2, w_pa, w_pb, w_o, conv8)


def _reduce_scatter(d_in, d_ff1, d_ff2, d_pa, d_pb, d_o, d_conv, small, d_ws):
    n_items = 9
    c_in, c_ff1 = d_in.shape[1] // N_DEV, d_ff1.shape[1] // N_DEV
    r_ff2, r_p = d_ff2.shape[0] // N_DEV, d_pa.shape[0] // N_DEV

    def body(din, dff1, dff2, dpa, dpb, do, dconv, sm, dws,
             rin, rff1, rff2, rpa, rpb, ro, rconv, rsm, rws, send_sems, recv_sems, local_sems):
        items = [
            (lambda p: _lane_block(din, p, c_in), lambda p: rin.at[p]),
            (lambda p: _lane_block(dff1, p, c_ff1), lambda p: rff1.at[p]),
            (lambda p: dff2.at[pl.ds(p * r_ff2, r_ff2), :], lambda p: rff2.at[p]),
            (lambda p: dpa.at[pl.ds(p * r_p, r_p), :], lambda p: rpa.at[p]),
            (lambda p: dpb.at[pl.ds(p * r_p, r_p), :], lambda p: rpb.at[p]),
            (lambda p: do.at[pl.ds(p * r_p, r_p), :], lambda p: ro.at[p]),
            (lambda p: dconv.at[p], lambda p: rconv.at[p]),
            (lambda p: sm, lambda p: rsm.at[p]),
            (lambda p: dws, lambda p: rws.at[p]),
        ]
        _exchange(items, send_sems, recv_sems, local_sems)

    out_shape = (
        jax.ShapeDtypeStruct((N_DEV, D, c_in), d_in.dtype),
        jax.ShapeDtypeStruct((N_DEV, D, c_ff1), d_ff1.dtype),
        jax.ShapeDtypeStruct((N_DEV, r_ff2, D), d_ff2.dtype),
        jax.ShapeDtypeStruct((N_DEV, r_p, D), d_pa.dtype),
        jax.ShapeDtypeStruct((N_DEV, r_p, D), d_pb.dtype),
        jax.ShapeDtypeStruct((N_DEV, r_p, D), d_o.dtype),
        jax.ShapeDtypeStruct((N_DEV,) + d_conv.shape[1:], F32),
        jax.ShapeDtypeStruct((N_DEV,) + small.shape, F32),
        jax.ShapeDtypeStruct((N_DEV,) + d_ws.shape, F32),
    )
    return _pcall(
        body, name="reduce_scatter", out_shape=out_shape,
        in_specs=[_ANY] * n_items, out_specs=[_ANY] * n_items,
        scratch_shapes=[pltpu.SemaphoreType.DMA((n_items * N_DEV,)), pltpu.SemaphoreType.DMA((n_items * N_DEV,)),
                        pltpu.SemaphoreType.DMA((n_items,))],
        compiler_params=_params(),
    )(d_in, d_ff1, d_ff2, d_pa, d_pb, d_o, d_conv, small, d_ws)


def _mixer_fwd(x, g_mix, w_in, b_gate, g_v, w_s, b_s3, conv, w_pa, w_pb, w_o, tt):
    T = x.shape[0]
    nt = T // tt
    nb = tt // SG

    def body(x_ref, gmix_ref, win_ref, bg_ref, gv_ref, ws_ref, bs_ref, cw_ref, pa_w, pb_w, wo_w,
             proj_ref, h_ref, a_ref, c_ref, m_ref, pa_ref, pb_ref, x1_ref, q_carry, mix_s):
        @pl.when(pl.program_id(0) == 0)
        def _():
            q_carry[...] = jnp.zeros_like(q_carry)

        x = x_ref[...]
        xh, _ = _rms(x)
        h = (xh * gmix_ref[...]).astype(BF16)
        h_ref[...] = h

        def proj(k):
            p = _mm(h, win_ref[:, _col(k)])
            proj_ref[:, _col(k)] = p.astype(BF16)
            return p

        vg, _ = _gelu(proj(1))
        vh, _ = _rms(vg)
        vp = (vh * gv_ref[...]).astype(BF16)
        wm = _masked_ws(ws_ref[...]).astype(BF16)
        for n in range(nb):
            rows = slice(n * SG, (n + 1) * SG)
            for g in range(N_GROUPS):
                cols = slice(g * SG, (g + 1) * SG)
                mix_s[rows, cols] = _mm(wm[g], vp[rows, cols]) + bs_ref[g]
        ug, _ = _gelu(proj(0))
        a = (ug * mix_s[...]).astype(BF16)
        a_ref[...] = a
        pa = _mm(a, pa_w[...])
        pa_ref[...] = pa.astype(BF16)
        m = jax.nn.sigmoid(proj(5) + bg_ref[:, :D]) * pa

        bgate = proj(2)
        q = proj(3) * proj(4)
        halo = q_carry[...]
        cv = cw_ref[0:1, :] * _shift_down(halo, q, 2) + cw_ref[1:2, :] * _shift_down(halo, q, 1) + cw_ref[2:3, :] * q
        q_carry[...] = q[tt - q_carry.shape[0]:, :]
        c = (bgate * cv).astype(BF16)
        c_ref[...] = c
        pb = _mm(c, pb_w[...])
        pb_ref[...] = pb.astype(BF16)
        m = (m + jax.nn.sigmoid(proj(6) + bg_ref[:, D:]) * pb).astype(BF16)
        m_ref[...] = m
        x1_ref[...] = x + _mm(m, wo_w[...])

    tile = lambda w: pl.BlockSpec((tt, w), lambda i: (i, 0))
    out_shape = ([jax.ShapeDtypeStruct((T, IN_COLS), BF16)] + [jax.ShapeDtypeStruct((T, D), BF16)] * 6
                 + [jax.ShapeDtypeStruct((T, D), F32)])
    return _pcall(
        body, name="mixer_fwd", grid=(nt,), out_shape=out_shape,
        in_specs=[tile(D), _const_spec(g_mix.shape), _const_spec(w_in.shape), _const_spec(b_gate.shape),
                  _const_spec(g_v.shape), _const_spec(w_s.shape), _const_spec(b_s3.shape), _const_spec(conv.shape),
                  _const_spec(w_pa.shape), _const_spec(w_pb.shape), _const_spec(w_o.shape)],
        out_specs=[tile(IN_COLS)] + [tile(D)] * 7,
        scratch_shapes=[pltpu.VMEM((8, D), F32), pltpu.VMEM((tt, D), F32)],
        compiler_params=_params(("arbitrary",)),
    )(x, g_mix, w_in, b_gate, g_v, w_s, b_s3, conv, w_pa, w_pb, w_o)


ST_GFIN, ST_GFF, ST_LOSS = 0, 1, 2


def _ffn_fwd_bwd(x1, tgt, g_ff, g_fin, w1, w2, tt):
    T = x1.shape[0]
    nt = T // tt
    nk = D_FF // D

    def body(x1_ref, tgt_ref, gff_ref, gfin_ref, w1_ref, w2_ref,
             hf_ref, s_ref, dpre_ref, dx2_ref, dx1_ref, st_ref, z_s):
        @pl.when(pl.program_id(0) == 0)
        def _():
            st_ref[...] = jnp.zeros_like(st_ref)

        x1 = x1_ref[...]
        xh1, r1 = _rms(x1)
        hf = (xh1 * gff_ref[...]).astype(BF16)
        hf_ref[...] = hf
        acc = jnp.zeros((tt, D), F32)
        for k in range(nk):
            z = jnp.maximum(_mm(hf, w1_ref[:, _col(k)]), 0.0)
            z_s[:, _col(k)] = z
            s = (z * z).astype(BF16)
            s_ref[:, _col(k)] = s
            acc = acc + _mm(s, w2_ref[_col(k), :])
        x2 = x1 + acc
        xh2, r2 = _rms(x2)
        diff = xh2 * gfin_ref[...] - tgt_ref[...]
        st_ref[ST_LOSS:ST_LOSS + 1, :] += jnp.sum(diff * diff, axis=0, keepdims=True)
        dy = diff * (1.0 / D)
        st_ref[ST_GFIN:ST_GFIN + 1, :] += jnp.sum(dy * xh2, axis=0, keepdims=True)
        dx2 = _rms_bwd(dy * gfin_ref[...], xh2, r2)
        dx2b = dx2.astype(BF16)
        dx2_ref[...] = dx2b
        dhf = jnp.zeros((tt, D), F32)
        for k in range(nk):
            dpre = (_nt(dx2b, w2_ref[_col(k), :]) * (2.0 * z_s[:, _col(k)])).astype(BF16)
            dpre_ref[:, _col(k)] = dpre
            dhf = dhf + _nt(dpre, w1_ref[:, _col(k)])
        st_ref[ST_GFF:ST_GFF + 1, :] += jnp.sum(dhf * xh1, axis=0, keepdims=True)
        dx1_ref[...] = dx2 + _rms_bwd(dhf * gff_ref[...], xh1, r1)

    tile = lambda w: pl.BlockSpec((tt, w), lambda i: (i, 0))
    out_shape = (jax.ShapeDtypeStruct((T, D), BF16), jax.ShapeDtypeStruct((T, D_FF), BF16),
                 jax.ShapeDtypeStruct((T, D_FF), BF16), jax.ShapeDtypeStruct((T, D), BF16),
                 jax.ShapeDtypeStruct((T, D), F32), jax.ShapeDtypeStruct((8, D), F32))
    return _pcall(
        body, name="ffn_fwd_bwd", grid=(nt,), out_shape=out_shape,
        in_specs=[tile(D), tile(D), _const_spec(g_ff.shape), _const_spec(g_fin.shape),
                  _const_spec(w1.shape), _const_spec(w2.shape)],
        out_specs=[tile(D), tile(D_FF), tile(D_FF), tile(D), tile(D), pl.BlockSpec((8, D), lambda i: (0, 0))],
        scratch_shapes=[pltpu.VMEM((tt, D_FF), F32)],
        compiler_params=_params(("arbitrary",)),
    )(x1, tgt, g_ff, g_fin, w1, w2)


ST_GV, ST_CONV = 0, 1


def _mixer_bwd(dx1, proj, pa, pb, b_gate, g_v, w_s, b_s3, conv, w_pa, w_pb, w_o, tt):
    T = dx1.shape[0]
    nt = T // tt
    nb = tt // SG
    hb = tt // HALO

    def body(dx1_ref, proj_ref, cgh_ref, xsh_ref, pa_ref, pb_ref,
             bg_ref, gv_ref, ws_ref, bs_ref, cw_ref, pa_w, pb_w, wo_w,
             dproj_ref, dpa_ref, dpb_ref, dx1b_ref, st_ref, dbg_ref, dws_ref, dbs_ref, d_carry, mix_s, dvp_s):
        i = pl.program_id(0)

        @pl.when(i == 0)
        def _():
            st_ref[...] = jnp.zeros_like(st_ref)
            dbg_ref[...] = jnp.zeros_like(dbg_ref)
            dws_ref[...] = jnp.zeros_like(dws_ref)
            dbs_ref[...] = jnp.zeros_like(dbs_ref)
            d_carry[...] = jnp.zeros_like(d_carry)

        def pj(k):
            return proj_ref[:, _col(k)].astype(F32)

        def put(k, val):
            dproj_ref[:, _col(k)] = val.astype(BF16)

        dx1b = dx1_ref[...].astype(BF16)
        dx1b_ref[...] = dx1b
        dm = _nt(dx1b, wo_w[...])
        s_a = jax.nn.sigmoid(pj(5) + bg_ref[:, :D])
        s_b = jax.nn.sigmoid(pj(6) + bg_ref[:, D:])
        dpa = dm * s_a
        dpb = dm * s_b
        dpa_b = dpa.astype(BF16)
        dpb_b = dpb.astype(BF16)
        dpa_ref[...] = dpa_b
        dpb_ref[...] = dpb_b
        dga = dpa * pa_ref[...].astype(F32) * (1.0 - s_a)
        dgb = dpb * pb_ref[...].astype(F32) * (1.0 - s_b)
        dbg_ref[0:1, :D] += jnp.sum(dga, axis=0, keepdims=True)
        dbg_ref[0:1, D:] += jnp.sum(dgb, axis=0, keepdims=True)
        put(5, dga)
        put(6, dgb)
        da = _nt(dpa_b, pa_w[...])
        dc = _nt(dpb_b, pb_w[...])

        v = pj(1)
        vg, v_cdf = _gelu(v)
        vh, rv = _rms(vg)
        vp = (vh * gv_ref[...]).astype(BF16)
        wm = _masked_ws(ws_ref[...]).astype(BF16)
        for n in range(nb):
            rows = slice(n * SG, (n + 1) * SG)
            for g in range(N_GROUPS):
                cols = slice(g * SG, (g + 1) * SG)
                mix_s[rows, cols] = _mm(wm[g], vp[rows, cols]) + bs_ref[g]
        u = pj(0)
        ug, u_cdf = _gelu(u)
        put(0, da * mix_s[...] * _gelu_grad(u, u_cdf))
        dmix = da * ug
        dmix_b = dmix.astype(BF16)
        for n in range(nb):
            rows = slice(n * SG, (n + 1) * SG)
            for g in range(N_GROUPS):
                cols = slice(g * SG, (g + 1) * SG)
                blk = dmix_b[rows, cols]
                dws_ref[g] += _nt(blk, vp[rows, cols])
                dbs_ref[g] += dmix[rows, cols]
                dvp_s[rows, cols] = _tn(wm[g], blk)
        dvp = dvp_s[...]
        st_ref[ST_GV:ST_GV + 1, :] += jnp.sum(dvp * vh, axis=0, keepdims=True)
        put(1, _rms_bwd(dvp * gv_ref[...], vh, rv) * _gelu_grad(v, v_cdf))

        bgate, cg, xs = pj(2), pj(3), pj(4)
        q = cg * xs
        has_prev = (i < nt - 1).astype(F32)
        halo = cgh_ref[...].astype(F32) * xsh_ref[...].astype(F32) * has_prev
        q2 = _shift_down(halo, q, 2)
        q1 = _shift_down(halo, q, 1)
        w0, w1, w2 = cw_ref[0:1, :], cw_ref[1:2, :], cw_ref[2:3, :]
        put(2, dc * (w0 * q2 + w1 * q1 + w2 * q))
        dcv = dc * bgate
        st_ref[ST_CONV:ST_CONV + 1, :] += jnp.sum(dcv * q2, axis=0, keepdims=True)
        st_ref[ST_CONV + 1:ST_CONV + 2, :] += jnp.sum(dcv * q1, axis=0, keepdims=True)
        st_ref[ST_CONV + 2:ST_CONV + 3, :] += jnp.sum(dcv * q, axis=0, keepdims=True)
        nxt = d_carry[...]
        dq = w2 * dcv + w1 * _shift_up(dcv, nxt, 1) + w0 * _shift_up(dcv, nxt, 2)
        d_carry[...] = dcv[:d_carry.shape[0], :]
        put(3, dq * xs)
        put(4, dq * cg)

    rev = lambda i: nt - 1 - i
    tile = lambda w: pl.BlockSpec((tt, w), lambda i: (rev(i), 0))
    halo_spec = lambda k: pl.BlockSpec((HALO, D), lambda i: (jnp.maximum(rev(i) * hb - 1, 0), k))
    res = lambda shape: pl.BlockSpec(shape, lambda i: (0,) * len(shape))
    out_shape = (jax.ShapeDtypeStruct((T, IN_COLS), BF16), jax.ShapeDtypeStruct((T, D), BF16),
                 jax.ShapeDtypeStruct((T, D), BF16), jax.ShapeDtypeStruct((T, D), BF16),
                 jax.ShapeDtypeStruct((8, D), F32), jax.ShapeDtypeStruct((8, 2 * D), F32),
                 jax.ShapeDtypeStruct((N_GROUPS, SG, SG), F32), jax.ShapeDtypeStruct((N_GROUPS, SG, SG), F32))
    return _pcall(
        body, name="mixer_bwd", grid=(nt,), out_shape=out_shape,
        in_specs=[tile(D), tile(IN_COLS), halo_spec(3), halo_spec(4), tile(D), tile(D),
                  _const_spec(b_gate.shape), _const_spec(g_v.shape), _const_spec(w_s.shape),
                  _const_spec(b_s3.shape), _const_spec(conv.shape),
                  _const_spec(w_pa.shape), _const_spec(w_pb.shape), _const_spec(w_o.shape)],
        out_specs=[tile(IN_COLS), tile(D), tile(D), tile(D), res((8, D)), res((8, 2 * D)),
                   res((N_GROUPS, SG, SG)), res((N_GROUPS, SG, SG))],
        scratch_shapes=[pltpu.VMEM((8, D), F32), pltpu.VMEM((tt, D), F32), pltpu.VMEM((tt, D), F32)],
        compiler_params=_params(("arbitrary",)),
    )(dx1, proj, proj, proj, pa, pb, b_gate, g_v, w_s, b_s3, conv, w_pa, w_pb, w_o)


def _in_proj_bwd(dproj, x, dx1, g_mix, w_in, tt):
    T = x.shape[0]

    def body(dproj_ref, x_ref, dx1_ref, gmix_ref, win_ref, gx_ref, st_ref):
        @pl.when(pl.program_id(0) == 0)
        def _():
            st_ref[...] = jnp.zeros_like(st_ref)

        dh = _nt(dproj_ref[...], win_ref[...])
        xh, r = _rms(x_ref[...])
        st_ref[0:1, :] += jnp.sum(dh * xh, axis=0, keepdims=True)
        gx_ref[...] = dx1_ref[...] + _rms_bwd(dh * gmix_ref[...], xh, r)

    tile = lambda w: pl.BlockSpec((tt, w), lambda i: (i, 0))
    return _pcall(
        body, name="in_proj_bwd", grid=(T // tt,),
        out_shape=(jax.ShapeDtypeStruct((T, D), F32), jax.ShapeDtypeStruct((8, D), F32)),
        in_specs=[tile(IN_COLS), tile(D), tile(D), _const_spec(g_mix.shape), _const_spec(w_in.shape)],
        out_specs=[tile(D), pl.BlockSpec((8, D), lambda i: (0, 0))],
        compiler_params=_params(("arbitrary",)),
    )(dproj, x, dx1, g_mix, w_in)


def _weight_grad(name, act, dout, bc, tk):
    T, n_in = act.shape
    n_out = dout.shape[1]
    nk = T // tk
    bi = min(n_in, D)

    def body(a_ref, d_ref, o_ref, acc):
        k = pl.program_id(2)
        p = _tn(a_ref[...], d_ref[...])

        @pl.when(k == 0)
        def _():
            acc[...] = p

        @pl.when(k > 0)
        def _():
            acc[...] += p

        @pl.when(k == nk - 1)
        def _():
            o_ref[...] = acc[...].astype(o_ref.dtype)

    return _pcall(
        body, name=name, grid=(n_in // bi, n_out // bc, nk), out_shape=jax.ShapeDtypeStruct((n_in, n_out), BF16),
        in_specs=[pl.BlockSpec((tk, bi), lambda i, j, k: (k, i)), pl.BlockSpec((tk, bc), lambda i, j, k: (k, j))],
        out_specs=pl.BlockSpec((bi, bc), lambda i, j, k: (i, j)),
        scratch_shapes=[pltpu.VMEM((bi, bc), F32)],
        compiler_params=_params(("arbitrary", "arbitrary", "arbitrary")),
    )(act, dout)


def _adamw(w, g, m, v):
    m = ADAM_B1 * m + (1.0 - ADAM_B1) * g
    v = ADAM_B2 * v + (1.0 - ADAM_B2) * (g * g)
    m_hat = m / (1.0 - ADAM_B1 ** ADAM_STEP)
    v_hat = v / (1.0 - ADAM_B2 ** ADAM_STEP)
    delta = -ADAM_LR * (m_hat / (jnp.sqrt(v_hat) + ADAM_EPS) + ADAM_WD * w)
    return delta, m, v


def _slot_sum(ref):
    g = ref[0].astype(F32)
    for s in range(1, N_DEV):
        g = g + ref[s].astype(F32)
    return g


def _reduce_adamw(name, slots, w, m, v, tr):
    rows, cols = w.shape

    def body(slot_ref, w_ref, m_ref, v_ref, g_out, d_out, m_out, v_out):
        g = _slot_sum(slot_ref)
        g_out[...] = g
        d_out[...], m_out[...], v_out[...] = _adamw(w_ref[...], g, m_ref[...], v_ref[...])

    tile = pl.BlockSpec((tr, cols), lambda i: (i, 0))
    return _pcall(
        body, name=name, grid=(rows // tr,), out_shape=[jax.ShapeDtypeStruct((rows, cols), F32)] * 4,
        in_specs=[pl.BlockSpec((N_DEV, tr, cols), lambda i: (0, i, 0)), tile, tile, tile],
        out_specs=[tile] * 4,
        compiler_params=_params(("parallel",)),
    )(slots, w, m, v)


SM_GMIX, SM_GV, SM_GFF, SM_GFIN, SM_LOSS, SM_BGATE, SM_BS, SM_ROWS = 0, 1, 2, 3, 4, 5, 7, 8


def _pack_small(st_ffn, st_mix, st_in, dbg, dbs, conv_rows):
    def body(ffn_ref, mix_ref, in_ref, dbg_ref, dbs_ref, sm_ref, conv_ref):
        sm_ref[SM_GMIX:SM_GMIX + 1, :] = in_ref[0:1, :]
        sm_ref[SM_GV:SM_GV + 1, :] = mix_ref[ST_GV:ST_GV + 1, :]
        sm_ref[SM_GFF:SM_GFF + 1, :] = ffn_ref[ST_GFF:ST_GFF + 1, :]
        sm_ref[SM_GFIN:SM_GFIN + 1, :] = ffn_ref[ST_GFIN:ST_GFIN + 1, :]
        sm_ref[SM_LOSS:SM_LOSS + 1, :] = ffn_ref[ST_LOSS:ST_LOSS + 1, :]
        sm_ref[SM_BGATE:SM_BGATE + 1, :] = dbg_ref[0:1, :D]
        sm_ref[SM_BGATE + 1:SM_BGATE + 2, :] = dbg_ref[0:1, D:]
        for g in range(N_GROUPS):
            sm_ref[SM_BS:SM_BS + 1, g * SG:(g + 1) * SG] = jnp.sum(dbs_ref[g].T, axis=0, keepdims=True)
        conv_ref[...] = jnp.zeros_like(conv_ref)
        for p in range(N_DEV):
            conv_ref[p, 0:conv_rows, :] = mix_ref[ST_CONV:ST_CONV + conv_rows, p * LANE:(p + 1) * LANE]

    return _pcall(
        body, name="pack_small",
        out_shape=(jax.ShapeDtypeStruct((SM_ROWS, D), F32), jax.ShapeDtypeStruct((N_DEV, 8, LANE), F32)),
        in_specs=[_VMEM] * 5, out_specs=[_VMEM] * 2, compiler_params=_params(),
    )(st_ffn, st_mix, st_in, dbg, dbs)


def _small_update(sm_slots, ws_slots, conv_slots, params):
    flat = [a for t in params for a in t]

    def body(sm_ref, ws_ref, conv_ref, *refs):
        ins, outs = refs[:len(flat)], refs[len(flat):]
        loss_ref, outs = outs[0], outs[1:]
        sm = _slot_sum(sm_ref)
        loss_ref[...] = (0.5 / D) * jnp.sum(sm[SM_LOSS:SM_LOSS + 1, :], axis=1, keepdims=True)
        grads = [sm[SM_GMIX:SM_GMIX + 1, :], sm[SM_GV:SM_GV + 1, :], sm[SM_GFF:SM_GFF + 1, :],
                 sm[SM_GFIN:SM_GFIN + 1, :], sm[SM_BGATE:SM_BGATE + 2, :], sm[SM_BS:SM_BS + 1, :],
                 _masked_ws(_slot_sum(ws_ref)), _slot_sum(conv_ref)]
        for n, g in enumerate(grads):
            w_ref, m_ref, v_ref = ins[3 * n:3 * n + 3]
            g_out, d_out, m_out, v_out = outs[4 * n:4 * n + 4]
            g_out[...] = g
            d_out[...], m_out[...], v_out[...] = _adamw(w_ref[...], g, m_ref[...], v_ref[...])

    out_shape = [jax.ShapeDtypeStruct((1, 1), F32)]
    for w, _, _ in params:
        out_shape += [jax.ShapeDtypeStruct(w.shape, F32)] * 4
    return _pcall(
        body, name="small_update", out_shape=out_shape,
        in_specs=[_VMEM] * (3 + len(flat)), out_specs=[_VMEM] * len(out_shape), compiler_params=_params(),
    )(sm_slots, ws_slots, conv_slots, *flat)


def kernel(x, norm_mix_g, w_in, b_gate, norm_v_g, w_s, b_s, conv_w, w_proj_a, w_proj_b, w_out, norm_ff_g, w_ff1, w_ff2, norm_final_g, loss_target, m_norm_mix_g, m_w_in, m_b_gate, m_norm_v_g, m_w_s, m_b_s, m_conv_w, m_w_proj_a, m_w_proj_b, m_w_out, m_norm_ff_g, m_w_ff1, m_w_ff2, m_norm_final_g, v_norm_mix_g, v_w_in, v_b_gate, v_norm_v_g, v_w_s, v_b_s, v_conv_w, v_w_proj_a, v_w_proj_b, v_w_out, v_norm_ff_g, v_w_ff1, v_w_ff2, v_norm_final_g):
    T = x.shape[1]
    tt = min(256, T)
    tk = min(1024, T)
    conv_rows = conv_w.shape[1]

    pad8 = lambda a: jnp.pad(a, ((0, 8 - a.shape[0]), (0, 0)))
    xs = x.reshape(T, D)
    tgt = loss_target.reshape(T, D)
    g_mix, g_v, g_ff, g_fin = norm_mix_g, norm_v_g, norm_ff_g, norm_final_g.reshape(1, D)
    ws = w_s[0]
    bs3 = b_s.reshape(N_GROUPS, SG, 1)

    W_in, W1, W2, PA, PB, WO, conv = _gather_weights(
        w_in[0], w_ff1[0], w_ff2[0], w_proj_a[0], w_proj_b[0], w_out[0], pad8(conv_w[0]))

    proj, h, a, c, m, pa, pb, x1 = _mixer_fwd(xs, g_mix, W_in, b_gate, g_v, ws, bs3, conv, PA, PB, WO, tt)
    hf, s, dpre, dx2, dx1, st_ffn = _ffn_fwd_bwd(x1, tgt, g_ff, g_fin, W1, W2, tt)
    dproj, dpa, dpb, dx1b, st_mix, dbg, dws, dbs = _mixer_bwd(
        dx1, proj, pa, pb, b_gate, g_v, ws, bs3, conv, PA, PB, WO, tt)
    grad_x, st_in = _in_proj_bwd(dproj, xs, dx1, g_mix, W_in, min(512, T))

    d_ff2 = _weight_grad("dw_ff2", s, dx2, D, tk)
    d_ff1 = _weight_grad("dw_ff1", hf, dpre, D, tk)
    d_o = _weight_grad("dw_out", m, dx1b, D, tk)
    d_pa = _weight_grad("dw_proj_a", a, dpa, D, tk)
    d_pb = _weight_grad("dw_proj_b", c, dpb, D, tk)
    d_in = _weight_grad("dw_in", h, dproj, D, tk)

    small, d_conv = _pack_small(st_ffn, st_mix, st_in, dbg, dbs, conv_rows)
    r_in, r_ff1, r_ff2, r_pa, r_pb, r_o, r_conv, r_small, r_ws = _reduce_scatter(
        d_in, d_ff1, d_ff2, d_pa, d_pb, d_o, d_conv, small, dws)

    big = {
        "w_in": _reduce_adamw("adamw_w_in", r_in, w_in[0], m_w_in[0], v_w_in[0], 256),
        "w_ff1": _reduce_adamw("adamw_w_ff1", r_ff1, w_ff1[0], m_w_ff1[0], v_w_ff1[0], 256),
        "w_ff2": _reduce_adamw("adamw_w_ff2", r_ff2, w_ff2[0], m_w_ff2[0], v_w_ff2[0], 128),
        "w_proj_a": _reduce_adamw("adamw_w_proj_a", r_pa, w_proj_a[0], m_w_proj_a[0], v_w_proj_a[0], 128),
        "w_proj_b": _reduce_adamw("adamw_w_proj_b", r_pb, w_proj_b[0], m_w_proj_b[0], v_w_proj_b[0], 128),
        "w_out": _reduce_adamw("adamw_w_out", r_o, w_out[0], m_w_out[0], v_w_out[0], 128),
    }

    two = lambda a: a.reshape(2, D)
    one = lambda a: a.reshape(1, D)
    small_params = [
        (norm_mix_g, m_norm_mix_g, v_norm_mix_g),
        (norm_v_g, m_norm_v_g, v_norm_v_g),
        (norm_ff_g, m_norm_ff_g, v_norm_ff_g),
        (one(norm_final_g), one(m_norm_final_g), one(v_norm_final_g)),
        (two(b_gate), two(m_b_gate), two(v_b_gate)),
        (one(b_s), one(m_b_s), one(v_b_s)),
        (w_s[0], m_w_s[0], v_w_s[0]),
        (pad8(conv_w[0]), pad8(m_conv_w[0]), pad8(v_conv_w[0])),
    ]
    res = _small_update(r_small, r_ws, r_conv, small_params)
    loss = res[0].reshape(())
    names = ["norm_mix_g", "norm_v_g", "norm_ff_g", "norm_final_g", "b_gate", "b_s", "w_s", "conv_w"]
    shapes = {"norm_mix_g": norm_mix_g.shape, "norm_v_g": norm_v_g.shape, "norm_ff_g": norm_ff_g.shape,
              "norm_final_g": norm_final_g.shape, "b_gate": b_gate.shape, "b_s": b_s.shape, "w_s": w_s.shape}
    out = {}
    for n, name in enumerate(names):
        quad = res[1 + 4 * n:5 + 4 * n]
        if name == "conv_w":
            out[name] = [q[:conv_rows][None] for q in quad]
        else:
            out[name] = [q.reshape(shapes[name]) for q in quad]
    for name, quad in big.items():
        out[name] = [q[None] for q in quad]

    order = ["norm_mix_g", "w_in", "b_gate", "norm_v_g", "w_s", "b_s", "conv_w", "w_proj_a", "w_proj_b", "w_out",
             "norm_ff_g", "w_ff1", "w_ff2", "norm_final_g"]
    grads = [out[n][0] for n in order]
    deltas = [out[n][1] for n in order]
    new_m = [out[n][2] for n in order]
    new_v = [out[n][3] for n in order]
    return (loss, grad_x.reshape(x.shape), *grads, *deltas, *new_m, *new_v)
```

```python
import math

import jax
import jax.numpy as jnp
from jax import lax
from jax.experimental import pallas as pl
from jax.experimental.pallas import tpu as pltpu

F32 = jnp.float32
BF16 = jnp.bfloat16

N_DEV = 8
D = 1024
D_FF = 4096
IN_COLS = 7 * D
SG = 128
N_GROUPS = 8
CHUNK = 64
EPS = 1e-6
HALO = 16
LANE = 128
VMEM_LIMIT = 56 * 1024 * 1024

ADAM_LR = 0.001
ADAM_B1 = 0.9
ADAM_B2 = 0.999
ADAM_EPS = 1e-08
ADAM_WD = 0.01
ADAM_STEP = 10

SQRT_HALF = math.sqrt(0.5)
INV_SQRT_2PI = 1.0 / math.sqrt(2.0 * math.pi)

_REL = [(dx, dy, dc) for dx in (0, 1) for dy in (0, 1) for dc in (0, 1)]

_VMEM = pl.BlockSpec(memory_space=pltpu.VMEM)
_ANY = pl.BlockSpec(memory_space=pl.ANY)


def _pcall(body, **kw):
    return pl.pallas_call(body, **kw)


def _params(sem=None):
    if sem is None:
        return pltpu.CompilerParams(vmem_limit_bytes=VMEM_LIMIT)
    return pltpu.CompilerParams(dimension_semantics=sem, vmem_limit_bytes=VMEM_LIMIT)


def _const_spec(shape):
    nd = len(shape)
    return pl.BlockSpec(shape, lambda *_: (0,) * nd, pipeline_mode=pl.Buffered(1))


def _after(body, n_in, deps):
    def wrapped(*refs):
        return body(*refs[:n_in], *refs[n_in + len(deps):])
    return wrapped


def _mm(a, b):
    return jnp.dot(a, b, preferred_element_type=F32)


def _nt(a, b):
    return lax.dot_general(a, b, (((1,), (1,)), ((), ())), preferred_element_type=F32)


def _tn(a, b):
    return lax.dot_general(a, b, (((0,), (0,)), ((), ())), preferred_element_type=F32)


def _rms(x):
    r = lax.rsqrt(jnp.mean(x * x, axis=-1, keepdims=True) + EPS)
    return x * r, r


def _rms_bwd(dyg, xh, r):
    return r * (dyg - xh * jnp.mean(dyg * xh, axis=-1, keepdims=True))


def _gelu(x):
    cdf = 0.5 * (1.0 + lax.erf(x * SQRT_HALF))
    return x * cdf, cdf


def _gelu_grad(x, cdf):
    return cdf + x * (jnp.exp(-0.5 * x * x) * INV_SQRT_2PI)


def _masked_ws(ws):
    i = lax.broadcasted_iota(jnp.int32, (SG, SG), 0)
    j = lax.broadcasted_iota(jnp.int32, (SG, SG), 1)
    keep = jnp.logical_or(j < CHUNK, i >= CHUNK)
    return jnp.where(keep[None], ws, jnp.zeros_like(ws))


def _shift_down(halo, q, k):
    ext = jnp.concatenate([halo, q], axis=0)
    return pltpu.roll(ext, k, 0)[halo.shape[0]:]


def _shift_up(q, nxt, k):
    ext = jnp.concatenate([q, nxt], axis=0)
    return pltpu.roll(ext, ext.shape[0] - k, 0)[:q.shape[0]]


def _col(k):
    return slice(k * D, (k + 1) * D)


def _position():
    x, y, c = lax.axis_index("x"), lax.axis_index("y"), lax.axis_index("c")
    return (x, y, c), 4 * x + 2 * y + c


def _exchange(items, send_sems, recv_sems, local_sems):
    (x, y, c), me = _position()
    started = []
    for w, (src_of, dst_of) in enumerate(items):
        own = pltpu.make_async_copy(src_of(me), dst_of(me), local_sems.at[w])
        own.start()
        started.append(own)
        for k in range(1, N_DEV):
            dx, dy, dc = _REL[k]
            peer = (1 - x if dx else x, 1 - y if dy else y, 1 - c if dc else c)
            pid = 4 * peer[0] + 2 * peer[1] + peer[2]
            cp = pltpu.make_async_remote_copy(
                src_ref=src_of(pid), dst_ref=dst_of(me),
                send_sem=send_sems.at[w * N_DEV + k], recv_sem=recv_sems.at[w * N_DEV + k],
                device_id=peer, device_id_type=pl.DeviceIdType.MESH)
            cp.start()
            started.append(cp)
    for cp in started:
        cp.wait()


def _lane_block(ref, p, width):
    return ref.at[:, pl.ds(pl.multiple_of(p * width, LANE), width)]


def _row_block(ref, p, rows):
    return ref.at[pl.ds(p * rows, rows), :]


def _small_exchange(d_conv, small):
    def body(dconv, sm, rconv, rsm, send_sems, recv_sems, local_sems):
        items = [
            (lambda p: dconv.at[p], lambda p: rconv.at[p]),
            (lambda p: sm, lambda p: rsm.at[p]),
        ]
        _exchange(items, send_sems, recv_sems, local_sems)

    n_items = 2
    return _pcall(
        body, name="small_exchange",
        out_shape=(jax.ShapeDtypeStruct((N_DEV,) + d_conv.shape[1:], F32),
                   jax.ShapeDtypeStruct((N_DEV,) + small.shape, F32)),
        in_specs=[_ANY] * n_items, out_specs=[_ANY] * n_items,
        scratch_shapes=[pltpu.SemaphoreType.DMA((n_items * N_DEV,)), pltpu.SemaphoreType.DMA((n_items * N_DEV,)),
                        pltpu.SemaphoreType.DMA((n_items,))],
        compiler_params=_params(),
    )(d_conv, small)


_HBM = pl.BlockSpec(memory_space=pltpu.HBM)
_SEM = pl.BlockSpec(memory_space=pltpu.SEMAPHORE)
_EFFECT = pltpu.SideEffectType.DATAFLOW_SIDE_EFFECTING
N_PEERS = N_DEV - 1


def _remote_copies(items, send_sems, recv_sems):
    (x, y, c), me = _position()
    copies = []
    for n, (src_of, dst_of) in enumerate(items):
        for k in range(1, N_DEV):
            dx, dy, dc = _REL[k]
            peer = (1 - x if dx else x, 1 - y if dy else y, 1 - c if dc else c)
            pid = 4 * peer[0] + 2 * peer[1] + peer[2]
            copies.append(pltpu.make_async_remote_copy(
                src_ref=src_of(pid, me), dst_ref=dst_of(me),
                send_sem=send_sems.at[n * N_PEERS + k - 1], recv_sem=recv_sems.at[n * N_PEERS + k - 1],
                device_id=peer, device_id_type=pl.DeviceIdType.MESH))
    return copies


def _start_copies(name, arrays, make_items, groups):
    n = len(arrays)

    def body(*refs):
        sems, token = refs[n:n + 2 * len(groups)], refs[-1]
        items = make_items(refs[:n])
        first = 0
        for g, size in enumerate(groups):
            for cp in _remote_copies(items[first:first + size], sems[2 * g], sems[2 * g + 1]):
                cp.start()
            first += size
        token[...] = jnp.zeros_like(token)

    out_shape = []
    for size in groups:
        out_shape += [pltpu.SemaphoreType.DMA((size * N_PEERS,))] * 2
    out_shape += [pltpu.HBM(a.shape, a.dtype) for a in arrays] + [jax.ShapeDtypeStruct((8, LANE), F32)]
    res = _pcall(
        body, name=name, out_shape=out_shape,
        in_specs=[_HBM] * n, out_specs=[_SEM] * (2 * len(groups)) + [_HBM] * n + [_VMEM],
        input_output_aliases={i: 2 * len(groups) + i for i in range(n)},
        compiler_params=pltpu.CompilerParams(has_side_effects=_EFFECT),
    )(*[pltpu.with_memory_space_constraint(a, pltpu.HBM) for a in arrays])
    return res[:2 * len(groups)], res[2 * len(groups):-1], res[-1]


def _wait_copies(name, arrays, send_sems, recv_sems, make_items, after):
    n = len(arrays)

    def body(*refs):
        for cp in _remote_copies(make_items(refs[:n]), refs[n], refs[n + 1]):
            cp.wait_send()
            cp.wait_recv()

    return _pcall(
        body, name=name, out_shape=[pltpu.HBM(a.shape, a.dtype) for a in arrays],
        in_specs=[_HBM] * n + [_SEM, _SEM, _ANY], out_specs=[_HBM] * n,
        input_output_aliases={i: i for i in range(n)},
        compiler_params=pltpu.CompilerParams(has_side_effects=_EFFECT),
    )(*arrays, send_sems, recv_sems, after)


def _place_weights(w_in, w_ff1, w_ff2, w_pa, w_pb, w_o, conv8):
    n_items = 7
    c_in, c_ff1 = w_in.shape[1], w_ff1.shape[1]
    r_ff2, r_p = w_ff2.shape[0], w_pa.shape[0]

    def body(win_ref, w1_ref, w2_ref, pa_ref, pb_ref, wo_ref, cw_ref,
             win_o, pa_o, pb_o, wo_o, cw_o, w1_o, w2_o,
             s_win, s_w1, s_w2, s_pa, s_pb, s_wo, sems):
        _, me = _position()
        for src, stage in ((win_ref, s_win), (w1_ref, s_w1), (w2_ref, s_w2),
                           (pa_ref, s_pa), (pb_ref, s_pb), (wo_ref, s_wo)):
            stage[...] = src[...].astype(BF16)
        pairs = [
            (s_win, _lane_block(win_o, me, c_in)), (s_pa, _row_block(pa_o, me, r_p)),
            (s_pb, _row_block(pb_o, me, r_p)), (s_wo, _row_block(wo_o, me, r_p)),
            (cw_ref, _lane_block(cw_o, me, conv8.shape[1])),
            (s_w1, _lane_block(w1_o, me, c_ff1)), (s_w2, _row_block(w2_o, me, r_ff2)),
        ]
        copies = [pltpu.make_async_copy(s, d, sems.at[n]) for n, (s, d) in enumerate(pairs)]
        for cp in copies:
            cp.start()
        for cp in copies:
            cp.wait()

    out_shape = (
        jax.ShapeDtypeStruct((D, N_DEV * c_in), BF16),
        jax.ShapeDtypeStruct((N_DEV * r_p, D), BF16),
        jax.ShapeDtypeStruct((N_DEV * r_p, D), BF16),
        jax.ShapeDtypeStruct((N_DEV * r_p, D), BF16),
        jax.ShapeDtypeStruct((conv8.shape[0], N_DEV * conv8.shape[1]), F32),
        jax.ShapeDtypeStruct((D, N_DEV * c_ff1), BF16),
        jax.ShapeDtypeStruct((N_DEV * r_ff2, D), BF16),
    )
    return _pcall(
        body, name="place_weights", out_shape=out_shape,
        in_specs=[_VMEM] * n_items, out_specs=[_ANY] * n_items,
        scratch_shapes=[pltpu.VMEM(w.shape, BF16) for w in (w_in, w_ff1, w_ff2, w_pa, w_pb, w_o)]
        + [pltpu.SemaphoreType.DMA((n_items,))],
        compiler_params=_params(),
    )(w_in, w_ff1, w_ff2, w_pa, w_pb, w_o, conv8)


def _gather_items(views):
    def make(refs):
        return [(lambda pid, me, r=r, v=v: v(r, me), lambda me, r=r, v=v: v(r, me)) for r, v in zip(refs, views)]
    return make


def _scatter_items(views):
    def make(refs):
        srcs, lands = refs[:len(views)], refs[len(views):]
        return [(lambda pid, me, r=r, v=v: v(r, pid), lambda me, l=l: l.at[me]) for r, v, l in zip(srcs, views, lands)]
    return make


def _mixer_fwd(x, g_mix, w_in, b_gate, g_v, w_s, b_s3, conv, w_pa, w_pb, w_o, tt):
    T = x.shape[0]
    nt = T // tt
    nb = tt // SG

    def body(x_ref, gmix_ref, win_ref, bg_ref, gv_ref, ws_ref, bs_ref, cw_ref, pa_w, pb_w, wo_w,
             proj_ref, h_ref, a_ref, c_ref, m_ref, pa_ref, pb_ref, x1_ref, q_carry, mix_s):
        @pl.when(pl.program_id(0) == 0)
        def _():
            q_carry[...] = jnp.zeros_like(q_carry)

        x = x_ref[...]
        xh, _ = _rms(x)
        h = (xh * gmix_ref[...]).astype(BF16)
        h_ref[...] = h

        def proj(k):
            p = _mm(h, win_ref[:, _col(k)])
            proj_ref[:, _col(k)] = p.astype(BF16)
            return p

        vg, _ = _gelu(proj(1))
        vh, _ = _rms(vg)
        vp = (vh * gv_ref[...]).astype(BF16)
        wm = _masked_ws(ws_ref[...]).astype(BF16)
        for n in range(nb):
            rows = slice(n * SG, (n + 1) * SG)
            for g in range(N_GROUPS):
                cols = slice(g * SG, (g + 1) * SG)
                mix_s[rows, cols] = _mm(wm[g], vp[rows, cols]) + bs_ref[g]
        ug, _ = _gelu(proj(0))
        a = (ug * mix_s[...]).astype(BF16)
        a_ref[...] = a
        pa = _mm(a, pa_w[...])
        pa_ref[...] = pa.astype(BF16)
        m = jax.nn.sigmoid(proj(5) + bg_ref[:, :D]) * pa

        bgate = proj(2)
        q = proj(3) * proj(4)
        halo = q_carry[...]
        cv = cw_ref[0:1, :] * _shift_down(halo, q, 2) + cw_ref[1:2, :] * _shift_down(halo, q, 1) + cw_ref[2:3, :] * q
        q_carry[...] = q[tt - q_carry.shape[0]:, :]
        c = (bgate * cv).astype(BF16)
        c_ref[...] = c
        pb = _mm(c, pb_w[...])
        pb_ref[...] = pb.astype(BF16)
        m = (m + jax.nn.sigmoid(proj(6) + bg_ref[:, D:]) * pb).astype(BF16)
        m_ref[...] = m
        x1_ref[...] = x + _mm(m, wo_w[...])

    tile = lambda w: pl.BlockSpec((tt, w), lambda i: (i, 0))
    out_shape = ([jax.ShapeDtypeStruct((T, IN_COLS), BF16)] + [jax.ShapeDtypeStruct((T, D), BF16)] * 6
                 + [jax.ShapeDtypeStruct((T, D), F32)])
    return _pcall(
        body, name="mixer_fwd", grid=(nt,), out_shape=out_shape,
        in_specs=[tile(D), _const_spec(g_mix.shape), _const_spec(w_in.shape), _const_spec(b_gate.shape),
                  _const_spec(g_v.shape), _const_spec(w_s.shape), _const_spec(b_s3.shape), _const_spec(conv.shape),
                  _const_spec(w_pa.shape), _const_spec(w_pb.shape), _const_spec(w_o.shape)],
        out_specs=[tile(IN_COLS)] + [tile(D)] * 7,
        scratch_shapes=[pltpu.VMEM((8, D), F32), pltpu.VMEM((tt, D), F32)],
        compiler_params=_params(("arbitrary",)),
    )(x, g_mix, w_in, b_gate, g_v, w_s, b_s3, conv, w_pa, w_pb, w_o)


ST_GFIN, ST_GFF, ST_LOSS = 0, 1, 2


def _ffn_fwd_bwd(x1, tgt, g_ff, g_fin, w1, w2, tt):
    T = x1.shape[0]
    nt = T // tt
    nk = D_FF // D

    def body(x1_ref, tgt_ref, gff_ref, gfin_ref, w1_ref, w2_ref,
             hf_ref, s_ref, dpre_ref, dx2_ref, dx1_ref, st_ref, z_s):
        @pl.when(pl.program_id(0) == 0)
        def _():
            st_ref[...] = jnp.zeros_like(st_ref)

        x1 = x1_ref[...]
        xh1, r1 = _rms(x1)
        hf = (xh1 * gff_ref[...]).astype(BF16)
        hf_ref[...] = hf
        acc = jnp.zeros((tt, D), F32)
        for k in range(nk):
            z = jnp.maximum(_mm(hf, w1_ref[:, _col(k)]), 0.0)
            z_s[:, _col(k)] = z
            s = (z * z).astype(BF16)
            s_ref[:, _col(k)] = s
            acc = acc + _mm(s, w2_ref[_col(k), :])
        x2 = x1 + acc
        xh2, r2 = _rms(x2)
        diff = xh2 * gfin_ref[...] - tgt_ref[...]
        st_ref[ST_LOSS:ST_LOSS + 1, :] += jnp.sum(diff * diff, axis=0, keepdims=True)
        dy = diff * (1.0 / D)
        st_ref[ST_GFIN:ST_GFIN + 1, :] += jnp.sum(dy * xh2, axis=0, keepdims=True)
        dx2 = _rms_bwd(dy * gfin_ref[...], xh2, r2)
        dx2b = dx2.astype(BF16)
        dx2_ref[...] = dx2b
        dhf = jnp.zeros((tt, D), F32)
        for k in range(nk):
            dpre = (_nt(dx2b, w2_ref[_col(k), :]) * (2.0 * z_s[:, _col(k)])).astype(BF16)
            dpre_ref[:, _col(k)] = dpre
            dhf = dhf + _nt(dpre, w1_ref[:, _col(k)])
        st_ref[ST_GFF:ST_GFF + 1, :] += jnp.sum(dhf * xh1, axis=0, keepdims=True)
        dx1_ref[...] = dx2 + _rms_bwd(dhf * gff_ref[...], xh1, r1)

    tile = lambda w: pl.BlockSpec((tt, w), lambda i: (i, 0))
    out_shape = (jax.ShapeDtypeStruct((T, D), BF16), jax.ShapeDtypeStruct((T, D_FF), BF16),
                 jax.ShapeDtypeStruct((T, D_FF), BF16), jax.ShapeDtypeStruct((T, D), BF16),
                 jax.ShapeDtypeStruct((T, D), F32), jax.ShapeDtypeStruct((8, D), F32))
    return _pcall(
        body, name="ffn_fwd_bwd", grid=(nt,), out_shape=out_shape,
        in_specs=[tile(D), tile(D), _const_spec(g_ff.shape), _const_spec(g_fin.shape),
                  _const_spec(w1.shape), _const_spec(w2.shape)],
        out_specs=[tile(D), tile(D_FF), tile(D_FF), tile(D), tile(D), pl.BlockSpec((8, D), lambda i: (0, 0))],
        scratch_shapes=[pltpu.VMEM((tt, D_FF), F32)],
        compiler_params=_params(("arbitrary",)),
    )(x1, tgt, g_ff, g_fin, w1, w2)


ST_GV, ST_CONV = 0, 1


def _mixer_bwd(dx1, proj, pa, pb, b_gate, g_v, w_s, b_s3, conv, w_pa, w_pb, w_o, tt, deps=()):
    T = dx1.shape[0]
    nt = T // tt
    nb = tt // SG
    hb = tt // HALO

    def body(dx1_ref, proj_ref, cgh_ref, xsh_ref, pa_ref, pb_ref,
             bg_ref, gv_ref, ws_ref, bs_ref, cw_ref, pa_w, pb_w, wo_w,
             dproj_ref, dpa_ref, dpb_ref, dx1b_ref, st_ref, dbg_ref, dws_ref, dbs_ref, d_carry, mix_s, dvp_s):
        i = pl.program_id(0)

        @pl.when(i == 0)
        def _():
            st_ref[...] = jnp.zeros_like(st_ref)
            dbg_ref[...] = jnp.zeros_like(dbg_ref)
            dws_ref[...] = jnp.zeros_like(dws_ref)
            dbs_ref[...] = jnp.zeros_like(dbs_ref)
            d_carry[...] = jnp.zeros_like(d_carry)

        def pj(k):
            return proj_ref[:, _col(k)].astype(F32)

        def put(k, val):
            dproj_ref[:, _col(k)] = val.astype(BF16)

        dx1b = dx1_ref[...].astype(BF16)
        dx1b_ref[...] = dx1b
        dm = _nt(dx1b, wo_w[...])
        s_a = jax.nn.sigmoid(pj(5) + bg_ref[:, :D])
        s_b = jax.nn.sigmoid(pj(6) + bg_ref[:, D:])
        dpa = dm * s_a
        dpb = dm * s_b
        dpa_b = dpa.astype(BF16)
        dpb_b = dpb.astype(BF16)
        dpa_ref[...] = dpa_b
        dpb_ref[...] = dpb_b
        dga = dpa * pa_ref[...].astype(F32) * (1.0 - s_a)
        dgb = dpb * pb_ref[...].astype(F32) * (1.0 - s_b)
        dbg_ref[0:1, :D] += jnp.sum(dga, axis=0, keepdims=True)
        dbg_ref[0:1, D:] += jnp.sum(dgb, axis=0, keepdims=True)
        put(5, dga)
        put(6, dgb)
        da = _nt(dpa_b, pa_w[...])
        dc = _nt(dpb_b, pb_w[...])

        v = pj(1)
        vg, v_cdf = _gelu(v)
        vh, rv = _rms(vg)
        vp = (vh * gv_ref[...]).astype(BF16)
        wm = _masked_ws(ws_ref[...]).astype(BF16)
        for n in range(nb):
            rows = slice(n * SG, (n + 1) * SG)
            for g in range(N_GROUPS):
                cols = slice(g * SG, (g + 1) * SG)
                mix_s[rows, cols] = _mm(wm[g], vp[rows, cols]) + bs_ref[g]
        u = pj(0)
        ug, u_cdf = _gelu(u)
        put(0, da * mix_s[...] * _gelu_grad(u, u_cdf))
        dmix = da * ug
        dmix_b = dmix.astype(BF16)
        for n in range(nb):
            rows = slice(n * SG, (n + 1) * SG)
            for g in range(N_GROUPS):
                cols = slice(g * SG, (g + 1) * SG)
                blk = dmix_b[rows, cols]
                dws_ref[g] += _nt(blk, vp[rows, cols])
                dbs_ref[g] += dmix[rows, cols]
                dvp_s[rows, cols] = _tn(wm[g], blk)
        dvp = dvp_s[...]
        st_ref[ST_GV:ST_GV + 1, :] += jnp.sum(dvp * vh, axis=0, keepdims=True)
        put(1, _rms_bwd(dvp * gv_ref[...], vh, rv) * _gelu_grad(v, v_cdf))

        bgate, cg, xs = pj(2), pj(3), pj(4)
        q = cg * xs
        has_prev = (i < nt - 1).astype(F32)
        halo = cgh_ref[...].astype(F32) * xsh_ref[...].astype(F32) * has_prev
        q2 = _shift_down(halo, q, 2)
        q1 = _shift_down(halo, q, 1)
        w0, w1, w2 = cw_ref[0:1, :], cw_ref[1:2, :], cw_ref[2:3, :]
        put(2, dc * (w0 * q2 + w1 * q1 + w2 * q))
        dcv = dc * bgate
        st_ref[ST_CONV:ST_CONV + 1, :] += jnp.sum(dcv * q2, axis=0, keepdims=True)
        st_ref[ST_CONV + 1:ST_CONV + 2, :] += jnp.sum(dcv * q1, axis=0, keepdims=True)
        st_ref[ST_CONV + 2:ST_CONV + 3, :] += jnp.sum(dcv * q, axis=0, keepdims=True)
        nxt = d_carry[...]
        dq = w2 * dcv + w1 * _shift_up(dcv, nxt, 1) + w0 * _shift_up(dcv, nxt, 2)
        d_carry[...] = dcv[:d_carry.shape[0], :]
        put(3, dq * xs)
        put(4, dq * cg)

    rev = lambda i: nt - 1 - i
    tile = lambda w: pl.BlockSpec((tt, w), lambda i: (rev(i), 0))
    halo_spec = lambda k: pl.BlockSpec((HALO, D), lambda i: (jnp.maximum(rev(i) * hb - 1, 0), k))
    res = lambda shape: pl.BlockSpec(shape, lambda i: (0,) * len(shape))
    out_shape = (jax.ShapeDtypeStruct((T, IN_COLS), BF16), jax.ShapeDtypeStruct((T, D), BF16),
                 jax.ShapeDtypeStruct((T, D), BF16), jax.ShapeDtypeStruct((T, D), BF16),
                 jax.ShapeDtypeStruct((8, D), F32), jax.ShapeDtypeStruct((8, 2 * D), F32),
                 jax.ShapeDtypeStruct((N_GROUPS, SG, SG), F32), jax.ShapeDtypeStruct((N_GROUPS, SG, SG), F32))
    return _pcall(
        _after(body, 14, deps), name="mixer_bwd", grid=(nt,), out_shape=out_shape,
        in_specs=[tile(D), tile(IN_COLS), halo_spec(3), halo_spec(4), tile(D), tile(D),
                  _const_spec(b_gate.shape), _const_spec(g_v.shape), _const_spec(w_s.shape),
                  _const_spec(b_s3.shape), _const_spec(conv.shape),
                  _const_spec(w_pa.shape), _const_spec(w_pb.shape), _const_spec(w_o.shape)] + [_ANY] * len(deps),
        out_specs=[tile(IN_COLS), tile(D), tile(D), tile(D), res((8, D)), res((8, 2 * D)),
                   res((N_GROUPS, SG, SG)), res((N_GROUPS, SG, SG))],
        scratch_shapes=[pltpu.VMEM((8, D), F32), pltpu.VMEM((tt, D), F32), pltpu.VMEM((tt, D), F32)],
        compiler_params=_params(("arbitrary",)),
    )(dx1, proj, proj, proj, pa, pb, b_gate, g_v, w_s, b_s3, conv, w_pa, w_pb, w_o, *deps)


def _in_proj_bwd(dproj, x, dx1, g_mix, w_in, tt, deps=()):
    T = x.shape[0]

    def body(dproj_ref, x_ref, dx1_ref, gmix_ref, win_ref, gx_ref, st_ref):
        @pl.when(pl.program_id(0) == 0)
        def _():
            st_ref[...] = jnp.zeros_like(st_ref)

        dh = _nt(dproj_ref[...], win_ref[...])
        xh, r = _rms(x_ref[...])
        st_ref[0:1, :] += jnp.sum(dh * xh, axis=0, keepdims=True)
        gx_ref[...] = dx1_ref[...] + _rms_bwd(dh * gmix_ref[...], xh, r)

    tile = lambda w: pl.BlockSpec((tt, w), lambda i: (i, 0))
    return _pcall(
        _after(body, 5, deps), name="in_proj_bwd", grid=(T // tt,),
        out_shape=(jax.ShapeDtypeStruct((T, D), F32), jax.ShapeDtypeStruct((8, D), F32)),
        in_specs=[tile(IN_COLS), tile(D), tile(D), _const_spec(g_mix.shape), _const_spec(w_in.shape)]
        + [_ANY] * len(deps),
        out_specs=[tile(D), pl.BlockSpec((8, D), lambda i: (0, 0))],
        compiler_params=_params(("arbitrary",)),
    )(dproj, x, dx1, g_mix, w_in, *deps)


def _weight_grad(name, act, dout, bc, tk, deps=()):
    T, n_in = act.shape
    n_out = dout.shape[1]
    nk = T // tk
    bi = min(n_in, D)

    def body(a_ref, d_ref, o_ref, acc):
        k = pl.program_id(2)
        p = _tn(a_ref[...], d_ref[...])

        @pl.when(k == 0)
        def _():
            acc[...] = p

        @pl.when(k > 0)
        def _():
            acc[...] += p

        @pl.when(k == nk - 1)
        def _():
            o_ref[...] = acc[...].astype(o_ref.dtype)

    return _pcall(
        _after(body, 2, deps), name=name, grid=(n_in // bi, n_out // bc, nk),
        out_shape=jax.ShapeDtypeStruct((n_in, n_out), BF16),
        in_specs=[pl.BlockSpec((tk, bi), lambda i, j, k: (k, i)), pl.BlockSpec((tk, bc), lambda i, j, k: (k, j))]
        + [_ANY] * len(deps),
        out_specs=pl.BlockSpec((bi, bc), lambda i, j, k: (i, j)),
        scratch_shapes=[pltpu.VMEM((bi, bc), F32)],
        compiler_params=_params(("arbitrary", "arbitrary", "arbitrary")),
    )(act, dout, *deps)


def _adamw(w, g, m, v):
    m = ADAM_B1 * m + (1.0 - ADAM_B1) * g
    v = ADAM_B2 * v + (1.0 - ADAM_B2) * (g * g)
    m_hat = m / (1.0 - ADAM_B1 ** ADAM_STEP)
    v_hat = v / (1.0 - ADAM_B2 ** ADAM_STEP)
    delta = -ADAM_LR * (m_hat / (jnp.sqrt(v_hat) + ADAM_EPS) + ADAM_WD * w)
    return delta, m, v


def _slot_sum(ref, own=None, me=None):
    g = None
    for s in range(N_DEV):
        term = ref[s] if own is None else jnp.where(me == s, own, ref[s])
        g = term.astype(F32) if g is None else g + term.astype(F32)
    return g


def _reduce_adamw(name, partial, own_rows, slots, w, m, v, tr):
    rows, cols = w.shape

    def body(part_ref, slot_ref, w_ref, m_ref, v_ref, g_out, d_out, m_out, v_out, own, sem):
        _, me = _position()
        cp = pltpu.make_async_copy(own_rows(part_ref, me, pl.program_id(0) * tr, tr), own, sem)
        cp.start()
        cp.wait()
        g = _slot_sum(slot_ref, own[...], me)
        g_out[...] = g
        d_out[...], m_out[...], v_out[...] = _adamw(w_ref[...], g, m_ref[...], v_ref[...])

    tile = pl.BlockSpec((tr, cols), lambda i: (i, 0))
    return _pcall(
        body, name=name, grid=(rows // tr,), out_shape=[jax.ShapeDtypeStruct((rows, cols), F32)] * 4,
        in_specs=[_ANY, pl.BlockSpec((N_DEV, tr, cols), lambda i: (0, i, 0)), tile, tile, tile],
        out_specs=[tile] * 4,
        scratch_shapes=[pltpu.VMEM((tr, cols), partial.dtype), pltpu.SemaphoreType.DMA(())],
        compiler_params=_params(("arbitrary",)),
    )(partial, slots, w, m, v)


SM_GMIX, SM_GV, SM_GFF, SM_GFIN, SM_LOSS, SM_BGATE, SM_BS, SM_ROWS = 0, 1, 2, 3, 4, 5, 7, 8


def _pack_small(st_ffn, st_mix, st_in, dbg, dbs, conv_rows):
    def body(ffn_ref, mix_ref, in_ref, dbg_ref, dbs_ref, sm_ref, conv_ref):
        sm_ref[SM_GMIX:SM_GMIX + 1, :] = in_ref[0:1, :]
        sm_ref[SM_GV:SM_GV + 1, :] = mix_ref[ST_GV:ST_GV + 1, :]
        sm_ref[SM_GFF:SM_GFF + 1, :] = ffn_ref[ST_GFF:ST_GFF + 1, :]
        sm_ref[SM_GFIN:SM_GFIN + 1, :] = ffn_ref[ST_GFIN:ST_GFIN + 1, :]
        sm_ref[SM_LOSS:SM_LOSS + 1, :] = ffn_ref[ST_LOSS:ST_LOSS + 1, :]
        sm_ref[SM_BGATE:SM_BGATE + 1, :] = dbg_ref[0:1, :D]
        sm_ref[SM_BGATE + 1:SM_BGATE + 2, :] = dbg_ref[0:1, D:]
        for g in range(N_GROUPS):
            sm_ref[SM_BS:SM_BS + 1, g * SG:(g + 1) * SG] = jnp.sum(dbs_ref[g].T, axis=0, keepdims=True)
        conv_ref[...] = jnp.zeros_like(conv_ref)
        for p in range(N_DEV):
            conv_ref[p, 0:conv_rows, :] = mix_ref[ST_CONV:ST_CONV + conv_rows, p * LANE:(p + 1) * LANE]

    return _pcall(
        body, name="pack_small",
        out_shape=(jax.ShapeDtypeStruct((SM_ROWS, D), F32), jax.ShapeDtypeStruct((N_DEV, 8, LANE), F32)),
        in_specs=[_VMEM] * 5, out_specs=[_VMEM] * 2, compiler_params=_params(),
    )(st_ffn, st_mix, st_in, dbg, dbs)


def _small_update(sm_slots, ws_own, ws_slots, conv_slots, params):
    flat = [a for t in params for a in t]

    def body(sm_ref, wso_ref, ws_ref, conv_ref, *refs):
        ins, outs = refs[:len(flat)], refs[len(flat):]
        loss_ref, outs = outs[0], outs[1:]
        _, me = _position()
        sm = _slot_sum(sm_ref)
        loss_ref[...] = (0.5 / D) * jnp.sum(sm[SM_LOSS:SM_LOSS + 1, :], axis=1, keepdims=True)
        grads = [sm[SM_GMIX:SM_GMIX + 1, :], sm[SM_GV:SM_GV + 1, :], sm[SM_GFF:SM_GFF + 1, :],
                 sm[SM_GFIN:SM_GFIN + 1, :], sm[SM_BGATE:SM_BGATE + 2, :], sm[SM_BS:SM_BS + 1, :],
                 _masked_ws(_slot_sum(ws_ref, wso_ref[...], me)), _slot_sum(conv_ref)]
        for n, g in enumerate(grads):
            w_ref, m_ref, v_ref = ins[3 * n:3 * n + 3]
            g_out, d_out, m_out, v_out = outs[4 * n:4 * n + 4]
            g_out[...] = g
            d_out[...], m_out[...], v_out[...] = _adamw(w_ref[...], g, m_ref[...], v_ref[...])

    out_shape = [jax.ShapeDtypeStruct((1, 1), F32)]
    for w, _, _ in params:
        out_shape += [jax.ShapeDtypeStruct(w.shape, F32)] * 4
    return _pcall(
        body, name="small_update", out_shape=out_shape,
        in_specs=[_VMEM] * (4 + len(flat)), out_specs=[_VMEM] * len(out_shape), compiler_params=_params(),
    )(sm_slots, ws_own, ws_slots, conv_slots, *flat)


def kernel(x, norm_mix_g, w_in, b_gate, norm_v_g, w_s, b_s, conv_w, w_proj_a, w_proj_b, w_out, norm_ff_g, w_ff1, w_ff2, norm_final_g, loss_target, m_norm_mix_g, m_w_in, m_b_gate, m_norm_v_g, m_w_s, m_b_s, m_conv_w, m_w_proj_a, m_w_proj_b, m_w_out, m_norm_ff_g, m_w_ff1, m_w_ff2, m_norm_final_g, v_norm_mix_g, v_w_in, v_b_gate, v_norm_v_g, v_w_s, v_b_s, v_conv_w, v_w_proj_a, v_w_proj_b, v_w_out, v_norm_ff_g, v_w_ff1, v_w_ff2, v_norm_final_g):
    T = x.shape[1]
    tt = min(256, T)
    tk = min(1024, T)
    conv_rows = conv_w.shape[1]

    pad8 = lambda a: jnp.pad(a, ((0, 8 - a.shape[0]), (0, 0)))
    xs = x.reshape(T, D)
    tgt = loss_target.reshape(T, D)
    g_mix, g_v, g_ff, g_fin = norm_mix_g, norm_v_g, norm_ff_g, norm_final_g.reshape(1, D)
    ws = w_s[0]
    bs3 = b_s.reshape(N_GROUPS, SG, 1)

    c_in, c_ff1 = w_in.shape[2], w_ff1.shape[2]
    r_ff2, r_p = w_ff2.shape[1], w_proj_a.shape[1]
    lane_view = lambda width: (lambda ref, p: _lane_block(ref, p, width))
    row_view = lambda rows: (lambda ref, p: _row_block(ref, p, rows))
    whole = lambda ref, p: ref
    slots = lambda shape, dtype: lax.empty((N_DEV,) + shape, dtype)

    placed = _place_weights(w_in[0], w_ff1[0], w_ff2[0], w_proj_a[0], w_proj_b[0], w_out[0], pad8(conv_w[0]))
    mixer_views = [lane_view(c_in), row_view(r_p), row_view(r_p), row_view(r_p), lane_view(LANE)]
    ffn_views = [lane_view(c_ff1), row_view(r_ff2)]
    sems, placed, g_token = _start_copies("gather_start", placed, _gather_items(mixer_views + ffn_views), [5, 2])
    W_in, PA, PB, WO, conv = _wait_copies(
        "gather_wait_mixer", placed[:5], sems[0], sems[1], _gather_items(mixer_views), g_token)
    proj, h, a, c, m, pa, pb, x1 = _mixer_fwd(xs, g_mix, W_in, b_gate, g_v, ws, bs3, conv, PA, PB, WO, tt)
    W1, W2 = _wait_copies("gather_wait_ffn", placed[5:], sems[2], sems[3], _gather_items(ffn_views), x1)
    hf, s, dpre, dx2, dx1, st_ffn = _ffn_fwd_bwd(x1, tgt, g_ff, g_fin, W1, W2, tt)

    d_ff2 = _weight_grad("dw_ff2", s, dx2, D, tk)
    d_ff1 = _weight_grad("dw_ff1", hf, dpre, D, tk)
    ff_views = [lane_view(c_ff1), row_view(r_ff2)]
    ff_sems, ff_arrays, ff_token = _start_copies(
        "scatter_start_ffn", [d_ff1, d_ff2, slots((D, c_ff1), BF16), slots((r_ff2, D), BF16)],
        _scatter_items(ff_views), [2])

    dproj, dpa, dpb, dx1b, st_mix, dbg, dws, dbs = _mixer_bwd(
        dx1, proj, pa, pb, b_gate, g_v, ws, bs3, conv, PA, PB, WO, tt, deps=(ff_token,))
    d_o = _weight_grad("dw_out", m, dx1b, D, tk)
    d_pa = _weight_grad("dw_proj_a", a, dpa, D, tk)
    d_pb = _weight_grad("dw_proj_b", c, dpb, D, tk)
    p_views = [row_view(r_p), row_view(r_p), row_view(r_p), whole]
    p_sems, p_arrays, p_token = _start_copies(
        "scatter_start_proj", [d_pa, d_pb, d_o, dws] + [slots((r_p, D), BF16)] * 3 + [slots(dws.shape, F32)],
        _scatter_items(p_views), [4])

    d_in = _weight_grad("dw_in", h, dproj, D, tk, deps=(p_token,))
    in_views = [lane_view(c_in)]
    in_sems, in_arrays, in_token = _start_copies(
        "scatter_start_in", [d_in, slots((D, c_in), BF16)], _scatter_items(in_views), [1])
    grad_x, st_in = _in_proj_bwd(dproj, xs, dx1, g_mix, W_in, min(512, T), deps=(in_token,))

    small, d_conv = _pack_small(st_ffn, st_mix, st_in, dbg, dbs, conv_rows)
    r_conv, r_small = _small_exchange(d_conv, small)
    d_ff1, d_ff2, s_ff1, s_ff2 = _wait_copies(
        "scatter_wait_ffn", ff_arrays, ff_sems[0], ff_sems[1], _scatter_items(ff_views), r_small)
    d_pa, d_pb, d_o, dws, s_pa, s_pb, s_o, s_ws = _wait_copies(
        "scatter_wait_proj", p_arrays, p_sems[0], p_sems[1], _scatter_items(p_views), r_small)
    d_in, s_in = _wait_copies("scatter_wait_in", in_arrays, in_sems[0], in_sems[1], _scatter_items(in_views), r_small)

    own_cols = lambda width: (
        lambda ref, me, r0, n: ref.at[pl.ds(r0, n), pl.ds(pl.multiple_of(me * width, LANE), width)])
    own_rows = lambda rows: (lambda ref, me, r0, n: ref.at[pl.ds(me * rows + r0, n), :])
    big = {
        "w_in": _reduce_adamw("adamw_w_in", d_in, own_cols(c_in), s_in, w_in[0], m_w_in[0], v_w_in[0], 256),
        "w_ff1": _reduce_adamw("adamw_w_ff1", d_ff1, own_cols(c_ff1), s_ff1, w_ff1[0], m_w_ff1[0], v_w_ff1[0], 256),
        "w_ff2": _reduce_adamw("adamw_w_ff2", d_ff2, own_rows(r_ff2), s_ff2, w_ff2[0], m_w_ff2[0], v_w_ff2[0], 128),
        "w_proj_a": _reduce_adamw(
            "adamw_w_proj_a", d_pa, own_rows(r_p), s_pa, w_proj_a[0], m_w_proj_a[0], v_w_proj_a[0], 128),
        "w_proj_b": _reduce_adamw(
            "adamw_w_proj_b", d_pb, own_rows(r_p), s_pb, w_proj_b[0], m_w_proj_b[0], v_w_proj_b[0], 128),
        "w_out": _reduce_adamw("adamw_w_out", d_o, own_rows(r_p), s_o, w_out[0], m_w_out[0], v_w_out[0], 128),
    }

    two = lambda a: a.reshape(2, D)
    one = lambda a: a.reshape(1, D)
    small_params = [
        (norm_mix_g, m_norm_mix_g, v_norm_mix_g),
        (norm_v_g, m_norm_v_g, v_norm_v_g),
        (norm_ff_g, m_norm_ff_g, v_norm_ff_g),
        (one(norm_final_g), one(m_norm_final_g), one(v_norm_final_g)),
        (two(b_gate), two(m_b_gate), two(v_b_gate)),
        (one(b_s), one(m_b_s), one(v_b_s)),
        (w_s[0], m_w_s[0], v_w_s[0]),
        (pad8(conv_w[0]), pad8(m_conv_w[0]), pad8(v_conv_w[0])),
    ]
    res = _small_update(r_small, dws, s_ws, r_conv, small_params)
    loss = res[0].reshape(())
    names = ["norm_mix_g", "norm_v_g", "norm_ff_g", "norm_final_g", "b_gate", "b_s", "w_s", "conv_w"]
    shapes = {"norm_mix_g": norm_mix_g.shape, "norm_v_g": norm_v_g.shape, "norm_ff_g": norm_ff_g.shape,
              "norm_final_g": norm_final_g.shape, "b_gate": b_gate.shape, "b_s": b_s.shape, "w_s": w_s.shape}
    out = {}
    for n, name in enumerate(names):
        quad = res[1 + 4 * n:5 + 4 * n]
        if name == "conv_w":
            out[name] = [q[:conv_rows][None] for q in quad]
        else:
            out[name] = [q.reshape(shapes[name]) for q in quad]
    for name, quad in big.items():
        out[name] = [q[None] for q in quad]

    order = ["norm_mix_g", "w_in", "b_gate", "norm_v_g", "w_s", "b_s", "conv_w", "w_proj_a", "w_proj_b", "w_out",
             "norm_ff_g", "w_ff1", "w_ff2", "norm_final_g"]
    grads = [out[n][0] for n in order]
    deltas = [out[n][1] for n in order]
    new_m = [out[n][2] for n in order]
    new_v = [out[n][3] for n in order]
    return (loss, grad_x.reshape(x.shape), *grads, *deltas, *new_m, *new_v)
```

```python
import math

import jax
import jax.numpy as jnp
from jax import lax
from jax.experimental import pallas as pl
from jax.experimental.pallas import tpu as pltpu

F32 = jnp.float32
BF16 = jnp.bfloat16

N_DEV = 8
D = 1024
D_FF = 4096
IN_COLS = 7 * D
SG = 128
N_GROUPS = 8
CHUNK = 64
EPS = 1e-6
HALO = 16
LANE = 128
VMEM_LIMIT = 56 * 1024 * 1024

ADAM_LR = 0.001
ADAM_B1 = 0.9
ADAM_B2 = 0.999
ADAM_EPS = 1e-08
ADAM_WD = 0.01
ADAM_STEP = 10

SQRT_HALF = math.sqrt(0.5)
INV_SQRT_2PI = 1.0 / math.sqrt(2.0 * math.pi)

_REL = [(dx, dy, dc) for dx in (0, 1) for dy in (0, 1) for dc in (0, 1)]

_VMEM = pl.BlockSpec(memory_space=pltpu.VMEM)
_ANY = pl.BlockSpec(memory_space=pl.ANY)


def _pcall(body, **kw):
    return pl.pallas_call(body, **kw)


def _params(sem=None):
    if sem is None:
        return pltpu.CompilerParams(vmem_limit_bytes=VMEM_LIMIT)
    return pltpu.CompilerParams(dimension_semantics=sem, vmem_limit_bytes=VMEM_LIMIT)


def _const_spec(shape):
    nd = len(shape)
    return pl.BlockSpec(shape, lambda *_: (0,) * nd, pipeline_mode=pl.Buffered(1))


def _after(body, n_in, deps):
    def wrapped(*refs):
        return body(*refs[:n_in], *refs[n_in + len(deps):])
    return wrapped


def _mm(a, b):
    return jnp.dot(a, b, preferred_element_type=F32)


def _nt(a, b):
    return lax.dot_general(a, b, (((1,), (1,)), ((), ())), preferred_element_type=F32)


def _tn(a, b):
    return lax.dot_general(a, b, (((0,), (0,)), ((), ())), preferred_element_type=F32)


def _rms(x):
    r = lax.rsqrt(jnp.mean(x * x, axis=-1, keepdims=True) + EPS)
    return x * r, r


def _rms_bwd(dyg, xh, r):
    return r * (dyg - xh * jnp.mean(dyg * xh, axis=-1, keepdims=True))


def _gelu(x):
    cdf = 0.5 * (1.0 + lax.erf(x * SQRT_HALF))
    return x * cdf, cdf


def _gelu_grad(x, cdf):
    return cdf + x * (jnp.exp(-0.5 * x * x) * INV_SQRT_2PI)


def _masked_ws(ws):
    i = lax.broadcasted_iota(jnp.int32, (SG, SG), 0)
    j = lax.broadcasted_iota(jnp.int32, (SG, SG), 1)
    keep = jnp.logical_or(j < CHUNK, i >= CHUNK)
    return jnp.where(keep[None], ws, jnp.zeros_like(ws))


def _shift_down(halo, q, k):
    ext = jnp.concatenate([halo, q], axis=0)
    return pltpu.roll(ext, k, 0)[halo.shape[0]:]


def _shift_up(q, nxt, k):
    ext = jnp.concatenate([q, nxt], axis=0)
    return pltpu.roll(ext, ext.shape[0] - k, 0)[:q.shape[0]]


def _col(k):
    return slice(k * D, (k + 1) * D)


def _position():
    x, y, c = lax.axis_index("x"), lax.axis_index("y"), lax.axis_index("c")
    return (x, y, c), 4 * x + 2 * y + c


def _exchange(items, send_sems, recv_sems, local_sems):
    (x, y, c), me = _position()
    started = []
    for w, (src_of, dst_of) in enumerate(items):
        own = pltpu.make_async_copy(src_of(me), dst_of(me), local_sems.at[w])
        own.start()
        started.append(own)
        for k in range(1, N_DEV):
            dx, dy, dc = _REL[k]
            peer = (1 - x if dx else x, 1 - y if dy else y, 1 - c if dc else c)
            pid = 4 * peer[0] + 2 * peer[1] + peer[2]
            cp = pltpu.make_async_remote_copy(
                src_ref=src_of(pid), dst_ref=dst_of(me),
                send_sem=send_sems.at[w * N_DEV + k], recv_sem=recv_sems.at[w * N_DEV + k],
                device_id=peer, device_id_type=pl.DeviceIdType.MESH)
            cp.start()
            started.append(cp)
    for cp in started:
        cp.wait()


def _lane_block(ref, p, width):
    return ref.at[:, pl.ds(pl.multiple_of(p * width, LANE), width)]


def _row_block(ref, p, rows):
    return ref.at[pl.ds(p * rows, rows), :]


def _small_exchange(d_conv, small):
    def body(dconv, sm, rconv, rsm, send_sems, recv_sems, local_sems):
        items = [
            (lambda p: dconv.at[p], lambda p: rconv.at[p]),
            (lambda p: sm, lambda p: rsm.at[p]),
        ]
        _exchange(items, send_sems, recv_sems, local_sems)

    n_items = 2
    return _pcall(
        body, name="small_exchange",
        out_shape=(jax.ShapeDtypeStruct((N_DEV,) + d_conv.shape[1:], F32),
                   jax.ShapeDtypeStruct((N_DEV,) + small.shape, F32)),
        in_specs=[_ANY] * n_items, out_specs=[_ANY] * n_items,
        scratch_shapes=[pltpu.SemaphoreType.DMA((n_items * N_DEV,)), pltpu.SemaphoreType.DMA((n_items * N_DEV,)),
                        pltpu.SemaphoreType.DMA((n_items,))],
        compiler_params=_params(),
    )(d_conv, small)


_HBM = pl.BlockSpec(memory_space=pltpu.HBM)
_SEM = pl.BlockSpec(memory_space=pltpu.SEMAPHORE)
_EFFECT = pltpu.SideEffectType.DATAFLOW_SIDE_EFFECTING
N_PEERS = N_DEV - 1


def _remote_copies(items, send_sems, recv_sems):
    (x, y, c), me = _position()
    copies = []
    for n, (src_of, dst_of) in enumerate(items):
        for k in range(1, N_DEV):
            dx, dy, dc = _REL[k]
            peer = (1 - x if dx else x, 1 - y if dy else y, 1 - c if dc else c)
            pid = 4 * peer[0] + 2 * peer[1] + peer[2]
            copies.append(pltpu.make_async_remote_copy(
                src_ref=src_of(pid, me), dst_ref=dst_of(me),
                send_sem=send_sems.at[n * N_PEERS + k - 1], recv_sem=recv_sems.at[n * N_PEERS + k - 1],
                device_id=peer, device_id_type=pl.DeviceIdType.MESH))
    return copies


def _start_copies(name, arrays, make_items, groups):
    n = len(arrays)

    def body(*refs):
        sems, token = refs[n:n + 2 * len(groups)], refs[-1]
        items = make_items(refs[:n])
        first = 0
        for g, size in enumerate(groups):
            for cp in _remote_copies(items[first:first + size], sems[2 * g], sems[2 * g + 1]):
                cp.start()
            first += size
        token[...] = jnp.zeros_like(token)

    out_shape = []
    for size in groups:
        out_shape += [pltpu.SemaphoreType.DMA((size * N_PEERS,))] * 2
    out_shape += [pltpu.HBM(a.shape, a.dtype) for a in arrays] + [jax.ShapeDtypeStruct((8, LANE), F32)]
    res = _pcall(
        body, name=name, out_shape=out_shape,
        in_specs=[_HBM] * n, out_specs=[_SEM] * (2 * len(groups)) + [_HBM] * n + [_VMEM],
        input_output_aliases={i: 2 * len(groups) + i for i in range(n)},
        compiler_params=pltpu.CompilerParams(has_side_effects=_EFFECT),
    )(*[pltpu.with_memory_space_constraint(a, pltpu.HBM) for a in arrays])
    return res[:2 * len(groups)], res[2 * len(groups):-1], res[-1]


def _wait_copies(name, arrays, send_sems, recv_sems, make_items, after):
    n = len(arrays)

    def body(*refs):
        for cp in _remote_copies(make_items(refs[:n]), refs[n], refs[n + 1]):
            cp.wait_send()
            cp.wait_recv()

    return _pcall(
        body, name=name, out_shape=[pltpu.HBM(a.shape, a.dtype) for a in arrays],
        in_specs=[_HBM] * n + [_SEM, _SEM, _ANY], out_specs=[_HBM] * n,
        input_output_aliases={i: i for i in range(n)},
        compiler_params=pltpu.CompilerParams(has_side_effects=_EFFECT),
    )(*arrays, send_sems, recv_sems, after)


def _place_weights(w_in, w_ff1, w_ff2, w_pa, w_pb, w_o, conv8):
    n_items = 7
    c_in, c_ff1 = w_in.shape[1], w_ff1.shape[1]
    r_ff2, r_p = w_ff2.shape[0], w_pa.shape[0]

    def body(win_ref, w1_ref, w2_ref, pa_ref, pb_ref, wo_ref, cw_ref,
             win_o, pa_o, pb_o, wo_o, cw_o, w1_o, w2_o,
             s_win, s_w1, s_w2, s_pa, s_pb, s_wo, sems):
        _, me = _position()
        for src, stage in ((win_ref, s_win), (w1_ref, s_w1), (w2_ref, s_w2),
                           (pa_ref, s_pa), (pb_ref, s_pb), (wo_ref, s_wo)):
            stage[...] = src[...].astype(BF16)
        pairs = [
            (s_win, _lane_block(win_o, me, c_in)), (s_pa, _row_block(pa_o, me, r_p)),
            (s_pb, _row_block(pb_o, me, r_p)), (s_wo, _row_block(wo_o, me, r_p)),
            (cw_ref, _lane_block(cw_o, me, conv8.shape[1])),
            (s_w1, _lane_block(w1_o, me, c_ff1)), (s_w2, _row_block(w2_o, me, r_ff2)),
        ]
        copies = [pltpu.make_async_copy(s, d, sems.at[n]) for n, (s, d) in enumerate(pairs)]
        for cp in copies:
            cp.start()
        for cp in copies:
            cp.wait()

    out_shape = (
        jax.ShapeDtypeStruct((D, N_DEV * c_in), BF16),
        jax.ShapeDtypeStruct((N_DEV * r_p, D), BF16),
        jax.ShapeDtypeStruct((N_DEV * r_p, D), BF16),
        jax.ShapeDtypeStruct((N_DEV * r_p, D), BF16),
        jax.ShapeDtypeStruct((conv8.shape[0], N_DEV * conv8.shape[1]), F32),
        jax.ShapeDtypeStruct((D, N_DEV * c_ff1), BF16),
        jax.ShapeDtypeStruct((N_DEV * r_ff2, D), BF16),
    )
    return _pcall(
        body, name="place_weights", out_shape=out_shape,
        in_specs=[_VMEM] * n_items, out_specs=[_ANY] * n_items,
        scratch_shapes=[pltpu.VMEM(w.shape, BF16) for w in (w_in, w_ff1, w_ff2, w_pa, w_pb, w_o)]
        + [pltpu.SemaphoreType.DMA((n_items,))],
        compiler_params=_params(),
    )(w_in, w_ff1, w_ff2, w_pa, w_pb, w_o, conv8)


def _gather_items(views):
    def make(refs):
        return [(lambda pid, me, r=r, v=v: v(r, me), lambda me, r=r, v=v: v(r, me)) for r, v in zip(refs, views)]
    return make


def _scatter_items(views):
    def make(refs):
        srcs, lands = refs[:len(views)], refs[len(views):]
        return [(lambda pid, me, r=r, v=v: v(r, pid), lambda me, l=l: l.at[me]) for r, v, l in zip(srcs, views, lands)]
    return make


def _mixer_fwd(x, g_mix, w_in, b_gate, g_v, w_s, b_s3, conv, w_pa, w_pb, w_o, tt):
    T = x.shape[0]
    nt = T // tt
    nb = tt // SG

    def body(x_ref, gmix_ref, win_ref, bg_ref, gv_ref, ws_ref, bs_ref, cw_ref, pa_w, pb_w, wo_w,
             proj_ref, h_ref, a_ref, c_ref, m_ref, pa_ref, pb_ref, x1_ref, q_carry, mix_s):
        @pl.when(pl.program_id(0) == 0)
        def _():
            q_carry[...] = jnp.zeros_like(q_carry)

        x = x_ref[...]
        xh, _ = _rms(x)
        h = (xh * gmix_ref[...]).astype(BF16)
        h_ref[...] = h

        def proj(k):
            p = _mm(h, win_ref[:, _col(k)])
            proj_ref[:, _col(k)] = p.astype(BF16)
            return p

        vg, _ = _gelu(proj(1))
        vh, _ = _rms(vg)
        vp = (vh * gv_ref[...]).astype(BF16)
        wm = _masked_ws(ws_ref[...]).astype(BF16)
        for n in range(nb):
            rows = slice(n * SG, (n + 1) * SG)
            for g in range(N_GROUPS):
                cols = slice(g * SG, (g + 1) * SG)
                mix_s[rows, cols] = _mm(wm[g], vp[rows, cols]) + bs_ref[g]
        ug, _ = _gelu(proj(0))
        a = (ug * mix_s[...]).astype(BF16)
        a_ref[...] = a
        pa = _mm(a, pa_w[...])
        pa_ref[...] = pa.astype(BF16)
        m = jax.nn.sigmoid(proj(5) + bg_ref[:, :D]) * pa

        bgate = proj(2)
        q = proj(3) * proj(4)
        halo = q_carry[...]
        cv = cw_ref[0:1, :] * _shift_down(halo, q, 2) + cw_ref[1:2, :] * _shift_down(halo, q, 1) + cw_ref[2:3, :] * q
        q_carry[...] = q[tt - q_carry.shape[0]:, :]
        c = (bgate * cv).astype(BF16)
        c_ref[...] = c
        pb = _mm(c, pb_w[...])
        pb_ref[...] = pb.astype(BF16)
        m = (m + jax.nn.sigmoid(proj(6) + bg_ref[:, D:]) * pb).astype(BF16)
        m_ref[...] = m
        x1_ref[...] = x + _mm(m, wo_w[...])

    tile = lambda w: pl.BlockSpec((tt, w), lambda i: (i, 0))
    out_shape = ([jax.ShapeDtypeStruct((T, IN_COLS), BF16)] + [jax.ShapeDtypeStruct((T, D), BF16)] * 6
                 + [jax.ShapeDtypeStruct((T, D), F32)])
    return _pcall(
        body, name="mixer_fwd", grid=(nt,), out_shape=out_shape,
        in_specs=[tile(D), _const_spec(g_mix.shape), _const_spec(w_in.shape), _const_spec(b_gate.shape),
                  _const_spec(g_v.shape), _const_spec(w_s.shape), _const_spec(b_s3.shape), _const_spec(conv.shape),
                  _const_spec(w_pa.shape), _const_spec(w_pb.shape), _const_spec(w_o.shape)],
        out_specs=[tile(IN_COLS)] + [tile(D)] * 7,
        scratch_shapes=[pltpu.VMEM((8, D), F32), pltpu.VMEM((tt, D), F32)],
        compiler_params=_params(("arbitrary",)),
    )(x, g_mix, w_in, b_gate, g_v, w_s, b_s3, conv, w_pa, w_pb, w_o)


ST_GFIN, ST_GFF, ST_LOSS = 0, 1, 2


def _ffn_fwd_bwd(x1, tgt, g_ff, g_fin, w1, w2, tt):
    T = x1.shape[0]
    nt = T // tt
    nk = D_FF // D

    def body(x1_ref, tgt_ref, gff_ref, gfin_ref, w1_ref, w2_ref,
             hf_ref, s_ref, dpre_ref, dx2_ref, dx1_ref, st_ref, z_s):
        @pl.when(pl.program_id(0) == 0)
        def _():
            st_ref[...] = jnp.zeros_like(st_ref)

        x1 = x1_ref[...]
        xh1, r1 = _rms(x1)
        hf = (xh1 * gff_ref[...]).astype(BF16)
        hf_ref[...] = hf
        acc = jnp.zeros((tt, D), F32)
        for k in range(nk):
            z = jnp.maximum(_mm(hf, w1_ref[:, _col(k)]), 0.0)
            z_s[:, _col(k)] = z
            s = (z * z).astype(BF16)
            s_ref[:, _col(k)] = s
            acc = acc + _mm(s, w2_ref[_col(k), :])
        x2 = x1 + acc
        xh2, r2 = _rms(x2)
        diff = xh2 * gfin_ref[...] - tgt_ref[...]
        st_ref[ST_LOSS:ST_LOSS + 1, :] += jnp.sum(diff * diff, axis=0, keepdims=True)
        dy = diff * (1.0 / D)
        st_ref[ST_GFIN:ST_GFIN + 1, :] += jnp.sum(dy * xh2, axis=0, keepdims=True)
        dx2 = _rms_bwd(dy * gfin_ref[...], xh2, r2)
        dx2b = dx2.astype(BF16)
        dx2_ref[...] = dx2b
        dhf = jnp.zeros((tt, D), F32)
        for k in range(nk):
            dpre = (_nt(dx2b, w2_ref[_col(k), :]) * (2.0 * z_s[:, _col(k)])).astype(BF16)
            dpre_ref[:, _col(k)] = dpre
            dhf = dhf + _nt(dpre, w1_ref[:, _col(k)])
        st_ref[ST_GFF:ST_GFF + 1, :] += jnp.sum(dhf * xh1, axis=0, keepdims=True)
        dx1_ref[...] = dx2 + _rms_bwd(dhf * gff_ref[...], xh1, r1)

    tile = lambda w: pl.BlockSpec((tt, w), lambda i: (i, 0))
    out_shape = (jax.ShapeDtypeStruct((T, D), BF16), jax.ShapeDtypeStruct((T, D_FF), BF16),
                 jax.ShapeDtypeStruct((T, D_FF), BF16), jax.ShapeDtypeStruct((T, D), BF16),
                 jax.ShapeDtypeStruct((T, D), F32), jax.ShapeDtypeStruct((8, D), F32))
    return _pcall(
        body, name="ffn_fwd_bwd", grid=(nt,), out_shape=out_shape,
        in_specs=[tile(D), tile(D), _const_spec(g_ff.shape), _const_spec(g_fin.shape),
                  _const_spec(w1.shape), _const_spec(w2.shape)],
        out_specs=[tile(D), tile(D_FF), tile(D_FF), tile(D), tile(D), pl.BlockSpec((8, D), lambda i: (0, 0))],
        scratch_shapes=[pltpu.VMEM((tt, D_FF), F32)],
        compiler_params=_params(("arbitrary",)),
    )(x1, tgt, g_ff, g_fin, w1, w2)


ST_GV, ST_CONV = 0, 1


def _mixer_bwd(dx1, proj, pa, pb, b_gate, g_v, w_s, b_s3, conv, w_pa, w_pb, w_o, tt, deps=()):
    T = dx1.shape[0]
    nt = T // tt
    nb = tt // SG
    hb = tt // HALO

    def body(dx1_ref, proj_ref, cgh_ref, xsh_ref, pa_ref, pb_ref,
             bg_ref, gv_ref, ws_ref, bs_ref, cw_ref, pa_w, pb_w, wo_w,
             dproj_ref, dpa_ref, dpb_ref, dx1b_ref, st_ref, dbg_ref, dws_ref, dbs_ref, d_carry, mix_s, dvp_s):
        i = pl.program_id(0)

        @pl.when(i == 0)
        def _():
            st_ref[...] = jnp.zeros_like(st_ref)
            dbg_ref[...] = jnp.zeros_like(dbg_ref)
            dws_ref[...] = jnp.zeros_like(dws_ref)
            dbs_ref[...] = jnp.zeros_like(dbs_ref)
            d_carry[...] = jnp.zeros_like(d_carry)

        def pj(k):
            return proj_ref[:, _col(k)].astype(F32)

        def put(k, val):
            dproj_ref[:, _col(k)] = val.astype(BF16)

        dx1b = dx1_ref[...].astype(BF16)
        dx1b_ref[...] = dx1b
        dm = _nt(dx1b, wo_w[...])
        s_a = jax.nn.sigmoid(pj(5) + bg_ref[:, :D])
        s_b = jax.nn.sigmoid(pj(6) + bg_ref[:, D:])
        dpa = dm * s_a
        dpb = dm * s_b
        dpa_b = dpa.astype(BF16)
        dpb_b = dpb.astype(BF16)
        dpa_ref[...] = dpa_b
        dpb_ref[...] = dpb_b
        dga = dpa * pa_ref[...].astype(F32) * (1.0 - s_a)
        dgb = dpb * pb_ref[...].astype(F32) * (1.0 - s_b)
        dbg_ref[0:1, :D] += jnp.sum(dga, axis=0, keepdims=True)
        dbg_ref[0:1, D:] += jnp.sum(dgb, axis=0, keepdims=True)
        put(5, dga)
        put(6, dgb)
        da = _nt(dpa_b, pa_w[...])
        dc = _nt(dpb_b, pb_w[...])

        v = pj(1)
        vg, v_cdf = _gelu(v)
        vh, rv = _rms(vg)
        vp = (vh * gv_ref[...]).astype(BF16)
        wm = _masked_ws(ws_ref[...]).astype(BF16)
        for n in range(nb):
            rows = slice(n * SG, (n + 1) * SG)
            for g in range(N_GROUPS):
                cols = slice(g * SG, (g + 1) * SG)
                mix_s[rows, cols] = _mm(wm[g], vp[rows, cols]) + bs_ref[g]
        u = pj(0)
        ug, u_cdf = _gelu(u)
        put(0, da * mix_s[...] * _gelu_grad(u, u_cdf))
        dmix = da * ug
        dmix_b = dmix.astype(BF16)
        for n in range(nb):
            rows = slice(n * SG, (n + 1) * SG)
            for g in range(N_GROUPS):
                cols = slice(g * SG, (g + 1) * SG)
                blk = dmix_b[rows, cols]
                dws_ref[g] += _nt(blk, vp[rows, cols])
                dbs_ref[g] += dmix[rows, cols]
                dvp_s[rows, cols] = _tn(wm[g], blk)
        dvp = dvp_s[...]
        st_ref[ST_GV:ST_GV + 1, :] += jnp.sum(dvp * vh, axis=0, keepdims=True)
        put(1, _rms_bwd(dvp * gv_ref[...], vh, rv) * _gelu_grad(v, v_cdf))

        bgate, cg, xs = pj(2), pj(3), pj(4)
        q = cg * xs
        has_prev = (i < nt - 1).astype(F32)
        halo = cgh_ref[...].astype(F32) * xsh_ref[...].astype(F32) * has_prev
        q2 = _shift_down(halo, q, 2)
        q1 = _shift_down(halo, q, 1)
        w0, w1, w2 = cw_ref[0:1, :], cw_ref[1:2, :], cw_ref[2:3, :]
        put(2, dc * (w0 * q2 + w1 * q1 + w2 * q))
        dcv = dc * bgate
        st_ref[ST_CONV:ST_CONV + 1, :] += jnp.sum(dcv * q2, axis=0, keepdims=True)
        st_ref[ST_CONV + 1:ST_CONV + 2, :] += jnp.sum(dcv * q1, axis=0, keepdims=True)
        st_ref[ST_CONV + 2:ST_CONV + 3, :] += jnp.sum(dcv * q, axis=0, keepdims=True)
        nxt = d_carry[...]
        dq = w2 * dcv + w1 * _shift_up(dcv, nxt, 1) + w0 * _shift_up(dcv, nxt, 2)
        d_carry[...] = dcv[:d_carry.shape[0], :]
        put(3, dq * xs)
        put(4, dq * cg)

    rev = lambda i: nt - 1 - i
    tile = lambda w: pl.BlockSpec((tt, w), lambda i: (rev(i), 0))
    halo_spec = lambda k: pl.BlockSpec((HALO, D), lambda i: (jnp.maximum(rev(i) * hb - 1, 0), k))
    res = lambda shape: pl.BlockSpec(shape, lambda i: (0,) * len(shape))
    out_shape = (jax.ShapeDtypeStruct((T, IN_COLS), BF16), jax.ShapeDtypeStruct((T, D), BF16),
                 jax.ShapeDtypeStruct((T, D), BF16), jax.ShapeDtypeStruct((T, D), BF16),
                 jax.ShapeDtypeStruct((8, D), F32), jax.ShapeDtypeStruct((8, 2 * D), F32),
                 jax.ShapeDtypeStruct((N_GROUPS, SG, SG), F32), jax.ShapeDtypeStruct((N_GROUPS, SG, SG), F32))
    return _pcall(
        _after(body, 14, deps), name="mixer_bwd", grid=(nt,), out_shape=out_shape,
        in_specs=[tile(D), tile(IN_COLS), halo_spec(3), halo_spec(4), tile(D), tile(D),
                  _const_spec(b_gate.shape), _const_spec(g_v.shape), _const_spec(w_s.shape),
                  _const_spec(b_s3.shape), _const_spec(conv.shape),
                  _const_spec(w_pa.shape), _const_spec(w_pb.shape), _const_spec(w_o.shape)] + [_ANY] * len(deps),
        out_specs=[tile(IN_COLS), tile(D), tile(D), tile(D), res((8, D)), res((8, 2 * D)),
                   res((N_GROUPS, SG, SG)), res((N_GROUPS, SG, SG))],
        scratch_shapes=[pltpu.VMEM((8, D), F32), pltpu.VMEM((tt, D), F32), pltpu.VMEM((tt, D), F32)],
        compiler_params=_params(("arbitrary",)),
    )(dx1, proj, proj, proj, pa, pb, b_gate, g_v, w_s, b_s3, conv, w_pa, w_pb, w_o, *deps)


def _in_proj_bwd(dproj, x, dx1, g_mix, w_in, tt, deps=()):
    T = x.shape[0]

    def body(dproj_ref, x_ref, dx1_ref, gmix_ref, win_ref, gx_ref, st_ref):
        @pl.when(pl.program_id(0) == 0)
        def _():
            st_ref[...] = jnp.zeros_like(st_ref)

        dh = _nt(dproj_ref[...], win_ref[...])
        xh, r = _rms(x_ref[...])
        st_ref[0:1, :] += jnp.sum(dh * xh, axis=0, keepdims=True)
        gx_ref[...] = dx1_ref[...] + _rms_bwd(dh * gmix_ref[...], xh, r)

    tile = lambda w: pl.BlockSpec((tt, w), lambda i: (i, 0))
    return _pcall(
        _after(body, 5, deps), name="in_proj_bwd", grid=(T // tt,),
        out_shape=(jax.ShapeDtypeStruct((T, D), F32), jax.ShapeDtypeStruct((8, D), F32)),
        in_specs=[tile(IN_COLS), tile(D), tile(D), _const_spec(g_mix.shape), _const_spec(w_in.shape)]
        + [_ANY] * len(deps),
        out_specs=[tile(D), pl.BlockSpec((8, D), lambda i: (0, 0))],
        compiler_params=_params(("arbitrary",)),
    )(dproj, x, dx1, g_mix, w_in, *deps)


def _weight_grad(name, act, dout, bc, tk, deps=()):
    T, n_in = act.shape
    n_out = dout.shape[1]
    nk = T // tk
    bi = min(n_in, D)

    def body(a_ref, d_ref, o_ref, acc):
        k = pl.program_id(2)
        p = _tn(a_ref[...], d_ref[...])

        @pl.when(k == 0)
        def _():
            acc[...] = p

        @pl.when(k > 0)
        def _():
            acc[...] += p

        @pl.when(k == nk - 1)
        def _():
            o_ref[...] = acc[...].astype(o_ref.dtype)

    return _pcall(
        _after(body, 2, deps), name=name, grid=(n_in // bi, n_out // bc, nk),
        out_shape=jax.ShapeDtypeStruct((n_in, n_out), BF16),
        in_specs=[pl.BlockSpec((tk, bi), lambda i, j, k: (k, i)), pl.BlockSpec((tk, bc), lambda i, j, k: (k, j))]
        + [_ANY] * len(deps),
        out_specs=pl.BlockSpec((bi, bc), lambda i, j, k: (i, j)),
        scratch_shapes=[pltpu.VMEM((bi, bc), F32)],
        compiler_params=_params(("arbitrary", "arbitrary", "arbitrary")),
    )(act, dout, *deps)


def _adamw(w, g, m, v):
    m = ADAM_B1 * m + (1.0 - ADAM_B1) * g
    v = ADAM_B2 * v + (1.0 - ADAM_B2) * (g * g)
    m_hat = m / (1.0 - ADAM_B1 ** ADAM_STEP)
    v_hat = v / (1.0 - ADAM_B2 ** ADAM_STEP)
    delta = -ADAM_LR * (m_hat / (jnp.sqrt(v_hat) + ADAM_EPS) + ADAM_WD * w)
    return delta, m, v


def _slot_sum(ref, own=None, me=None):
    g = None
    for s in range(N_DEV):
        term = ref[s] if own is None else jnp.where(me == s, own, ref[s])
        g = term.astype(F32) if g is None else g + term.astype(F32)
    return g


def _reduce_adamw(name, partial, own_rows, slots, w, m, v, tr):
    rows, cols = w.shape

    def body(part_ref, slot_ref, w_ref, m_ref, v_ref, g_out, d_out, m_out, v_out, own, sem):
        _, me = _position()
        cp = pltpu.make_async_copy(own_rows(part_ref, me, pl.program_id(0) * tr, tr), own, sem)
        cp.start()
        cp.wait()
        g = _slot_sum(slot_ref, own[...], me)
        g_out[...] = g
        d_out[...], m_out[...], v_out[...] = _adamw(w_ref[...], g, m_ref[...], v_ref[...])

    tile = pl.BlockSpec((tr, cols), lambda i: (i, 0))
    return _pcall(
        body, name=name, grid=(rows // tr,), out_shape=[jax.ShapeDtypeStruct((rows, cols), F32)] * 4,
        in_specs=[_ANY, pl.BlockSpec((N_DEV, tr, cols), lambda i: (0, i, 0)), tile, tile, tile],
        out_specs=[tile] * 4,
        scratch_shapes=[pltpu.VMEM((tr, cols), partial.dtype), pltpu.SemaphoreType.DMA(())],
        compiler_params=_params(("arbitrary",)),
    )(partial, slots, w, m, v)


SM_GMIX, SM_GV, SM_GFF, SM_GFIN, SM_LOSS, SM_BGATE, SM_BS, SM_ROWS = 0, 1, 2, 3, 4, 5, 7, 8


def _pack_small(st_ffn, st_mix, st_in, dbg, dbs, conv_rows):
    def body(ffn_ref, mix_ref, in_ref, dbg_ref, dbs_ref, sm_ref, conv_ref):
        sm_ref[SM_GMIX:SM_GMIX + 1, :] = in_ref[0:1, :]
        sm_ref[SM_GV:SM_GV + 1, :] = mix_ref[ST_GV:ST_GV + 1, :]
        sm_ref[SM_GFF:SM_GFF + 1, :] = ffn_ref[ST_GFF:ST_GFF + 1, :]
        sm_ref[SM_GFIN:SM_GFIN + 1, :] = ffn_ref[ST_GFIN:ST_GFIN + 1, :]
        sm_ref[SM_LOSS:SM_LOSS + 1, :] = ffn_ref[ST_LOSS:ST_LOSS + 1, :]
        sm_ref[SM_BGATE:SM_BGATE + 1, :] = dbg_ref[0:1, :D]
        sm_ref[SM_BGATE + 1:SM_BGATE + 2, :] = dbg_ref[0:1, D:]
        for g in range(N_GROUPS):
            sm_ref[SM_BS:SM_BS + 1, g * SG:(g + 1) * SG] = jnp.sum(dbs_ref[g].T, axis=0, keepdims=True)
        conv_ref[...] = jnp.zeros_like(conv_ref)
        for p in range(N_DEV):
            conv_ref[p, 0:conv_rows, :] = mix_ref[ST_CONV:ST_CONV + conv_rows, p * LANE:(p + 1) * LANE]

    return _pcall(
        body, name="pack_small",
        out_shape=(jax.ShapeDtypeStruct((SM_ROWS, D), F32), jax.ShapeDtypeStruct((N_DEV, 8, LANE), F32)),
        in_specs=[_VMEM] * 5, out_specs=[_VMEM] * 2, compiler_params=_params(),
    )(st_ffn, st_mix, st_in, dbg, dbs)


def _small_update(sm_slots, ws_own, ws_slots, conv_slots, params):
    flat = [a for t in params for a in t]

    def body(sm_ref, wso_ref, ws_ref, conv_ref, *refs):
        ins, outs = refs[:len(flat)], refs[len(flat):]
        loss_ref, outs = outs[0], outs[1:]
        _, me = _position()
        sm = _slot_sum(sm_ref)
        loss_ref[...] = (0.5 / D) * jnp.sum(sm[SM_LOSS:SM_LOSS + 1, :], axis=1, keepdims=True)
        grads = [sm[SM_GMIX:SM_GMIX + 1, :], sm[SM_GV:SM_GV + 1, :], sm[SM_GFF:SM_GFF + 1, :],
                 sm[SM_GFIN:SM_GFIN + 1, :], sm[SM_BGATE:SM_BGATE + 2, :], sm[SM_BS:SM_BS + 1, :],
                 _masked_ws(_slot_sum(ws_ref, wso_ref[...], me)), _slot_sum(conv_ref)]
        for n, g in enumerate(grads):
            w_ref, m_ref, v_ref = ins[3 * n:3 * n + 3]
            g_out, d_out, m_out, v_out = outs[4 * n:4 * n + 4]
            g_out[...] = g
            d_out[...], m_out[...], v_out[...] = _adamw(w_ref[...], g, m_ref[...], v_ref[...])

    out_shape = [jax.ShapeDtypeStruct((1, 1), F32)]
    for w, _, _ in params:
        out_shape += [jax.ShapeDtypeStruct(w.shape, F32)] * 4
    return _pcall(
        body, name="small_update", out_shape=out_shape,
        in_specs=[_VMEM] * (4 + len(flat)), out_specs=[_VMEM] * len(out_shape), compiler_params=_params(),
    )(sm_slots, ws_own, ws_slots, conv_slots, *flat)


def kernel(x, norm_mix_g, w_in, b_gate, norm_v_g, w_s, b_s, conv_w, w_proj_a, w_proj_b, w_out, norm_ff_g, w_ff1, w_ff2, norm_final_g, loss_target, m_norm_mix_g, m_w_in, m_b_gate, m_norm_v_g, m_w_s, m_b_s, m_conv_w, m_w_proj_a, m_w_proj_b, m_w_out, m_norm_ff_g, m_w_ff1, m_w_ff2, m_norm_final_g, v_norm_mix_g, v_w_in, v_b_gate, v_norm_v_g, v_w_s, v_b_s, v_conv_w, v_w_proj_a, v_w_proj_b, v_w_out, v_norm_ff_g, v_w_ff1, v_w_ff2, v_norm_final_g):
    T = x.shape[1]
    tt = min(256, T)
    tk = min(4096, T)
    conv_rows = conv_w.shape[1]

    pad8 = lambda a: jnp.pad(a, ((0, 8 - a.shape[0]), (0, 0)))
    xs = x.reshape(T, D)
    tgt = loss_target.reshape(T, D)
    g_mix, g_v, g_ff, g_fin = norm_mix_g, norm_v_g, norm_ff_g, norm_final_g.reshape(1, D)
    ws = w_s[0]
    bs3 = b_s.reshape(N_GROUPS, SG, 1)

    c_in, c_ff1 = w_in.shape[2], w_ff1.shape[2]
    r_ff2, r_p = w_ff2.shape[1], w_proj_a.shape[1]
    lane_view = lambda width: (lambda ref, p: _lane_block(ref, p, width))
    row_view = lambda rows: (lambda ref, p: _row_block(ref, p, rows))
    whole = lambda ref, p: ref
    slots = lambda shape, dtype: lax.empty((N_DEV,) + shape, dtype)

    placed = _place_weights(w_in[0], w_ff1[0], w_ff2[0], w_proj_a[0], w_proj_b[0], w_out[0], pad8(conv_w[0]))
    mixer_views = [lane_view(c_in), row_view(r_p), row_view(r_p), row_view(r_p), lane_view(LANE)]
    ffn_views = [lane_view(c_ff1), row_view(r_ff2)]
    sems, placed, g_token = _start_copies("gather_start", placed, _gather_items(mixer_views + ffn_views), [5, 2])
    W_in, PA, PB, WO, conv = _wait_copies(
        "gather_wait_mixer", placed[:5], sems[0], sems[1], _gather_items(mixer_views), g_token)
    proj, h, a, c, m, pa, pb, x1 = _mixer_fwd(xs, g_mix, W_in, b_gate, g_v, ws, bs3, conv, PA, PB, WO, tt)
    W1, W2 = _wait_copies("gather_wait_ffn", placed[5:], sems[2], sems[3], _gather_items(ffn_views), x1)
    hf, s, dpre, dx2, dx1, st_ffn = _ffn_fwd_bwd(x1, tgt, g_ff, g_fin, W1, W2, tt)

    d_ff2 = _weight_grad("dw_ff2", s, dx2, D, tk)
    d_ff1 = _weight_grad("dw_ff1", hf, dpre, D, tk)
    ff_views = [lane_view(c_ff1), row_view(r_ff2)]
    ff_sems, ff_arrays, ff_token = _start_copies(
        "scatter_start_ffn", [d_ff1, d_ff2, slots((D, c_ff1), BF16), slots((r_ff2, D), BF16)],
        _scatter_items(ff_views), [2])

    dproj, dpa, dpb, dx1b, st_mix, dbg, dws, dbs = _mixer_bwd(
        dx1, proj, pa, pb, b_gate, g_v, ws, bs3, conv, PA, PB, WO, tt, deps=(ff_token,))
    d_o = _weight_grad("dw_out", m, dx1b, D, tk)
    d_pa = _weight_grad("dw_proj_a", a, dpa, D, tk)
    d_pb = _weight_grad("dw_proj_b", c, dpb, D, tk)
    p_views = [row_view(r_p), row_view(r_p), row_view(r_p), whole]
    p_sems, p_arrays, p_token = _start_copies(
        "scatter_start_proj", [d_pa, d_pb, d_o, dws] + [slots((r_p, D), BF16)] * 3 + [slots(dws.shape, F32)],
        _scatter_items(p_views), [4])

    d_in = _weight_grad("dw_in", h, dproj, D, tk, deps=(p_token,))
    in_views = [lane_view(c_in)]
    in_sems, in_arrays, in_token = _start_copies(
        "scatter_start_in", [d_in, slots((D, c_in), BF16)], _scatter_items(in_views), [1])
    grad_x, st_in = _in_proj_bwd(dproj, xs, dx1, g_mix, W_in, min(512, T), deps=(in_token,))

    small, d_conv = _pack_small(st_ffn, st_mix, st_in, dbg, dbs, conv_rows)
    r_conv, r_small = _small_exchange(d_conv, small)
    d_ff1, d_ff2, s_ff1, s_ff2 = _wait_copies(
        "scatter_wait_ffn", ff_arrays, ff_sems[0], ff_sems[1], _scatter_items(ff_views), r_small)
    d_pa, d_pb, d_o, dws, s_pa, s_pb, s_o, s_ws = _wait_copies(
        "scatter_wait_proj", p_arrays, p_sems[0], p_sems[1], _scatter_items(p_views), r_small)
    d_in, s_in = _wait_copies("scatter_wait_in", in_arrays, in_sems[0], in_sems[1], _scatter_items(in_views), r_small)

    own_cols = lambda width: (
        lambda ref, me, r0, n: ref.at[pl.ds(r0, n), pl.ds(pl.multiple_of(me * width, LANE), width)])
    own_rows = lambda rows: (lambda ref, me, r0, n: ref.at[pl.ds(me * rows + r0, n), :])
    big = {
        "w_in": _reduce_adamw("adamw_w_in", d_in, own_cols(c_in), s_in, w_in[0], m_w_in[0], v_w_in[0], 256),
        "w_ff1": _reduce_adamw("adamw_w_ff1", d_ff1, own_cols(c_ff1), s_ff1, w_ff1[0], m_w_ff1[0], v_w_ff1[0], 256),
        "w_ff2": _reduce_adamw("adamw_w_ff2", d_ff2, own_rows(r_ff2), s_ff2, w_ff2[0], m_w_ff2[0], v_w_ff2[0], 128),
        "w_proj_a": _reduce_adamw(
            "adamw_w_proj_a", d_pa, own_rows(r_p), s_pa, w_proj_a[0], m_w_proj_a[0], v_w_proj_a[0], 128),
        "w_proj_b": _reduce_adamw(
            "adamw_w_proj_b", d_pb, own_rows(r_p), s_pb, w_proj_b[0], m_w_proj_b[0], v_w_proj_b[0], 128),
        "w_out": _reduce_adamw("adamw_w_out", d_o, own_rows(r_p), s_o, w_out[0], m_w_out[0], v_w_out[0], 128),
    }

    two = lambda a: a.reshape(2, D)
    one = lambda a: a.reshape(1, D)
    small_params = [
        (norm_mix_g, m_norm_mix_g, v_norm_mix_g),
        (norm_v_g, m_norm_v_g, v_norm_v_g),
        (norm_ff_g, m_norm_ff_g, v_norm_ff_g),
        (one(norm_final_g), one(m_norm_final_g), one(v_norm_final_g)),
        (two(b_gate), two(m_b_gate), two(v_b_gate)),
        (one(b_s), one(m_b_s), one(v_b_s)),
        (w_s[0], m_w_s[0], v_w_s[0]),
        (pad8(conv_w[0]), pad8(m_conv_w[0]), pad8(v_conv_w[0])),
    ]
    res = _small_update(r_small, dws, s_ws, r_conv, small_params)
    loss = res[0].reshape(())
    names = ["norm_mix_g", "norm_v_g", "norm_ff_g", "norm_final_g", "b_gate", "b_s", "w_s", "conv_w"]
    shapes = {"norm_mix_g": norm_mix_g.shape, "norm_v_g": norm_v_g.shape, "norm_ff_g": norm_ff_g.shape,
              "norm_final_g": norm_final_g.shape, "b_gate": b_gate.shape, "b_s": b_s.shape, "w_s": w_s.shape}
    out = {}
    for n, name in enumerate(names):
        quad = res[1 + 4 * n:5 + 4 * n]
        if name == "conv_w":
            out[name] = [q[:conv_rows][None] for q in quad]
        else:
            out[name] = [q.reshape(shapes[name]) for q in quad]
    for name, quad in big.items():
        out[name] = [q[None] for q in quad]

    order = ["norm_mix_g", "w_in", "b_gate", "norm_v_g", "w_s", "b_s", "conv_w", "w_proj_a", "w_proj_b", "w_out",
             "norm_ff_g", "w_ff1", "w_ff2", "norm_final_g"]
    grads = [out[n][0] for n in order]
    deltas = [out[n][1] for n in order]
    new_m = [out[n][2] for n in order]
    new_v = [out[n][3] for n in order]
    return (loss, grad_x.reshape(x.shape), *grads, *deltas, *new_m, *new_v)
```

```python
import math

import jax
import jax.numpy as jnp
from jax import lax
from jax.experimental import pallas as pl
from jax.experimental.pallas import tpu as pltpu

F32 = jnp.float32
BF16 = jnp.bfloat16

N_DEV = 8
D = 1024
D_FF = 4096
IN_COLS = 7 * D
SG = 128
N_GROUPS = 8
CHUNK = 64
EPS = 1e-6
HALO = 16
LANE = 128
VMEM_LIMIT = 56 * 1024 * 1024

ADAM_LR = 0.001
ADAM_B1 = 0.9
ADAM_B2 = 0.999
ADAM_EPS = 1e-08
ADAM_WD = 0.01
ADAM_STEP = 10

SQRT_HALF = math.sqrt(0.5)
INV_SQRT_2PI = 1.0 / math.sqrt(2.0 * math.pi)

_REL = [(dx, dy, dc) for dx in (0, 1) for dy in (0, 1) for dc in (0, 1)]

_VMEM = pl.BlockSpec(memory_space=pltpu.VMEM)
_ANY = pl.BlockSpec(memory_space=pl.ANY)


def _pcall(body, **kw):
    return pl.pallas_call(body, **kw)


def _params(sem=None):
    if sem is None:
        return pltpu.CompilerParams(vmem_limit_bytes=VMEM_LIMIT)
    return pltpu.CompilerParams(dimension_semantics=sem, vmem_limit_bytes=VMEM_LIMIT)


def _const_spec(shape):
    nd = len(shape)
    return pl.BlockSpec(shape, lambda *_: (0,) * nd, pipeline_mode=pl.Buffered(1))


def _after(body, n_in, deps):
    def wrapped(*refs):
        return body(*refs[:n_in], *refs[n_in + len(deps):])
    return wrapped


def _mm(a, b):
    return jnp.dot(a, b, preferred_element_type=F32)


def _nt(a, b):
    return lax.dot_general(a, b, (((1,), (1,)), ((), ())), preferred_element_type=F32)


def _tn(a, b):
    return lax.dot_general(a, b, (((0,), (0,)), ((), ())), preferred_element_type=F32)


def _rms(x):
    r = lax.rsqrt(jnp.mean(x * x, axis=-1, keepdims=True) + EPS)
    return x * r, r


def _rms_bwd(dyg, xh, r):
    return r * (dyg - xh * jnp.mean(dyg * xh, axis=-1, keepdims=True))


def _gelu(x):
    cdf = 0.5 * (1.0 + lax.erf(x * SQRT_HALF))
    return x * cdf, cdf


def _gelu_grad(x, cdf):
    return cdf + x * (jnp.exp(-0.5 * x * x) * INV_SQRT_2PI)


def _sigmoid(x):
    return 0.5 * jnp.tanh(0.5 * x) + 0.5


def _masked_ws(ws):
    i = lax.broadcasted_iota(jnp.int32, (SG, SG), 0)
    j = lax.broadcasted_iota(jnp.int32, (SG, SG), 1)
    keep = jnp.logical_or(j < CHUNK, i >= CHUNK)
    return jnp.where(keep[None], ws, jnp.zeros_like(ws))


def _shift_down(halo, q, k):
    ext = jnp.concatenate([halo, q], axis=0)
    return pltpu.roll(ext, k, 0)[halo.shape[0]:]


def _shift_up(q, nxt, k):
    ext = jnp.concatenate([q, nxt], axis=0)
    return pltpu.roll(ext, ext.shape[0] - k, 0)[:q.shape[0]]


def _col(k):
    return slice(k * D, (k + 1) * D)


def _position():
    x, y, c = lax.axis_index("x"), lax.axis_index("y"), lax.axis_index("c")
    return (x, y, c), 4 * x + 2 * y + c


def _exchange(items, send_sems, recv_sems, local_sems):
    (x, y, c), me = _position()
    started = []
    for w, (src_of, dst_of) in enumerate(items):
        own = pltpu.make_async_copy(src_of(me), dst_of(me), local_sems.at[w])
        own.start()
        started.append(own)
        for k in range(1, N_DEV):
            dx, dy, dc = _REL[k]
            peer = (1 - x if dx else x, 1 - y if dy else y, 1 - c if dc else c)
            pid = 4 * peer[0] + 2 * peer[1] + peer[2]
            cp = pltpu.make_async_remote_copy(
                src_ref=src_of(pid), dst_ref=dst_of(me),
                send_sem=send_sems.at[w * N_DEV + k], recv_sem=recv_sems.at[w * N_DEV + k],
                device_id=peer, device_id_type=pl.DeviceIdType.MESH)
            cp.start()
            started.append(cp)
    for cp in started:
        cp.wait()


def _lane_block(ref, p, width):
    return ref.at[:, pl.ds(pl.multiple_of(p * width, LANE), width)]


def _row_block(ref, p, rows):
    return ref.at[pl.ds(p * rows, rows), :]


def _small_exchange(d_conv, small):
    def body(dconv, sm, rconv, rsm, send_sems, recv_sems, local_sems):
        items = [
            (lambda p: dconv.at[p], lambda p: rconv.at[p]),
            (lambda p: sm, lambda p: rsm.at[p]),
        ]
        _exchange(items, send_sems, recv_sems, local_sems)

    n_items = 2
    return _pcall(
        body, name="small_exchange",
        out_shape=(jax.ShapeDtypeStruct((N_DEV,) + d_conv.shape[1:], F32),
                   jax.ShapeDtypeStruct((N_DEV,) + small.shape, F32)),
        in_specs=[_ANY] * n_items, out_specs=[_ANY] * n_items,
        scratch_shapes=[pltpu.SemaphoreType.DMA((n_items * N_DEV,)), pltpu.SemaphoreType.DMA((n_items * N_DEV,)),
                        pltpu.SemaphoreType.DMA((n_items,))],
        compiler_params=_params(),
    )(d_conv, small)


_HBM = pl.BlockSpec(memory_space=pltpu.HBM)
_SEM = pl.BlockSpec(memory_space=pltpu.SEMAPHORE)
_EFFECT = pltpu.SideEffectType.DATAFLOW_SIDE_EFFECTING
N_PEERS = N_DEV - 1


def _remote_copies(items, send_sems, recv_sems):
    (x, y, c), me = _position()
    copies = []
    for n, (src_of, dst_of) in enumerate(items):
        for k in range(1, N_DEV):
            dx, dy, dc = _REL[k]
            peer = (1 - x if dx else x, 1 - y if dy else y, 1 - c if dc else c)
            pid = 4 * peer[0] + 2 * peer[1] + peer[2]
            copies.append(pltpu.make_async_remote_copy(
                src_ref=src_of(pid, me), dst_ref=dst_of(me),
                send_sem=send_sems.at[n * N_PEERS + k - 1], recv_sem=recv_sems.at[n * N_PEERS + k - 1],
                device_id=peer, device_id_type=pl.DeviceIdType.MESH))
    return copies


def _start_copies(name, arrays, make_items, groups):
    n = len(arrays)

    def body(*refs):
        sems, token = refs[n:n + 2 * len(groups)], refs[-1]
        items = make_items(refs[:n])
        first = 0
        for g, size in enumerate(groups):
            for cp in _remote_copies(items[first:first + size], sems[2 * g], sems[2 * g + 1]):
                cp.start()
            first += size
        token[...] = jnp.zeros_like(token)

    out_shape = []
    for size in groups:
        out_shape += [pltpu.SemaphoreType.DMA((size * N_PEERS,))] * 2
    out_shape += [pltpu.HBM(a.shape, a.dtype) for a in arrays] + [jax.ShapeDtypeStruct((8, LANE), F32)]
    res = _pcall(
        body, name=name, out_shape=out_shape,
        in_specs=[_HBM] * n, out_specs=[_SEM] * (2 * len(groups)) + [_HBM] * n + [_VMEM],
        input_output_aliases={i: 2 * len(groups) + i for i in range(n)},
        compiler_params=pltpu.CompilerParams(has_side_effects=_EFFECT),
    )(*[pltpu.with_memory_space_constraint(a, pltpu.HBM) for a in arrays])
    return res[:2 * len(groups)], res[2 * len(groups):-1], res[-1]


def _wait_copies(name, arrays, send_sems, recv_sems, make_items, after):
    n = len(arrays)

    def body(*refs):
        for cp in _remote_copies(make_items(refs[:n]), refs[n], refs[n + 1]):
            cp.wait_send()
            cp.wait_recv()

    return _pcall(
        body, name=name, out_shape=[pltpu.HBM(a.shape, a.dtype) for a in arrays],
        in_specs=[_HBM] * n + [_SEM, _SEM, _ANY], out_specs=[_HBM] * n,
        input_output_aliases={i: i for i in range(n)},
        compiler_params=pltpu.CompilerParams(has_side_effects=_EFFECT),
    )(*arrays, send_sems, recv_sems, after)


def _place_weights(w_in, w_ff1, w_ff2, w_pa, w_pb, w_o, conv8):
    n_items = 7
    c_in, c_ff1 = w_in.shape[1], w_ff1.shape[1]
    r_ff2, r_p = w_ff2.shape[0], w_pa.shape[0]

    def body(win_ref, w1_ref, w2_ref, pa_ref, pb_ref, wo_ref, cw_ref,
             win_o, pa_o, pb_o, wo_o, cw_o, w1_o, w2_o,
             s_win, s_w1, s_w2, s_pa, s_pb, s_wo, sems):
        _, me = _position()
        for src, stage in ((win_ref, s_win), (w1_ref, s_w1), (w2_ref, s_w2),
                           (pa_ref, s_pa), (pb_ref, s_pb), (wo_ref, s_wo)):
            stage[...] = src[...].astype(BF16)
        pairs = [
            (s_win, _lane_block(win_o, me, c_in)), (s_pa, _row_block(pa_o, me, r_p)),
            (s_pb, _row_block(pb_o, me, r_p)), (s_wo, _row_block(wo_o, me, r_p)),
            (cw_ref, _lane_block(cw_o, me, conv8.shape[1])),
            (s_w1, _lane_block(w1_o, me, c_ff1)), (s_w2, _row_block(w2_o, me, r_ff2)),
        ]
        copies = [pltpu.make_async_copy(s, d, sems.at[n]) for n, (s, d) in enumerate(pairs)]
        for cp in copies:
            cp.start()
        for cp in copies:
            cp.wait()

    out_shape = (
        jax.ShapeDtypeStruct((D, N_DEV * c_in), BF16),
        jax.ShapeDtypeStruct((N_DEV * r_p, D), BF16),
        jax.ShapeDtypeStruct((N_DEV * r_p, D), BF16),
        jax.ShapeDtypeStruct((N_DEV * r_p, D), BF16),
        jax.ShapeDtypeStruct((conv8.shape[0], N_DEV * conv8.shape[1]), F32),
        jax.ShapeDtypeStruct((D, N_DEV * c_ff1), BF16),
        jax.ShapeDtypeStruct((N_DEV * r_ff2, D), BF16),
    )
    return _pcall(
        body, name="place_weights", out_shape=out_shape,
        in_specs=[_VMEM] * n_items, out_specs=[_ANY] * n_items,
        scratch_shapes=[pltpu.VMEM(w.shape, BF16) for w in (w_in, w_ff1, w_ff2, w_pa, w_pb, w_o)]
        + [pltpu.SemaphoreType.DMA((n_items,))],
        compiler_params=_params(),
    )(w_in, w_ff1, w_ff2, w_pa, w_pb, w_o, conv8)


def _gather_items(views):
    def make(refs):
        return [(lambda pid, me, r=r, v=v: v(r, me), lambda me, r=r, v=v: v(r, me)) for r, v in zip(refs, views)]
    return make


def _scatter_items(views):
    def make(refs):
        srcs, lands = refs[:len(views)], refs[len(views):]
        return [(lambda pid, me, r=r, v=v: v(r, pid), lambda me, l=l: l.at[me]) for r, v, l in zip(srcs, views, lands)]
    return make


def _mixer_fwd(x, g_mix, w_in, b_gate, g_v, w_s, b_s3, conv, w_pa, w_pb, w_o, tt):
    T = x.shape[0]
    nt = T // tt
    nb = tt // SG

    def body(x_ref, gmix_ref, win_ref, bg_ref, gv_ref, ws_ref, bs_ref, cw_ref, pa_w, pb_w, wo_w,
             proj_ref, h_ref, a_ref, c_ref, m_ref, pa_ref, pb_ref, x1_ref, q_carry, mix_s):
        @pl.when(pl.program_id(0) == 0)
        def _():
            q_carry[...] = jnp.zeros_like(q_carry)

        x = x_ref[...]
        xh, _ = _rms(x)
        h = (xh * gmix_ref[...]).astype(BF16)
        h_ref[...] = h

        def proj(k):
            p = _mm(h, win_ref[:, _col(k)])
            proj_ref[:, _col(k)] = p.astype(BF16)
            return p

        vg, _ = _gelu(proj(1))
        vh, _ = _rms(vg)
        vp = (vh * gv_ref[...]).astype(BF16)
        wm = _masked_ws(ws_ref[...]).astype(BF16)
        for n in range(nb):
            rows = slice(n * SG, (n + 1) * SG)
            for g in range(N_GROUPS):
                cols = slice(g * SG, (g + 1) * SG)
                mix_s[rows, cols] = _mm(wm[g], vp[rows, cols]) + bs_ref[g]
        ug, _ = _gelu(proj(0))
        a = (ug * mix_s[...]).astype(BF16)
        a_ref[...] = a
        pa = _mm(a, pa_w[...])
        pa_ref[...] = pa.astype(BF16)
        m = _sigmoid(proj(5) + bg_ref[:, :D]) * pa

        bgate = proj(2)
        q = proj(3) * proj(4)
        halo = q_carry[...]
        cv = cw_ref[0:1, :] * _shift_down(halo, q, 2) + cw_ref[1:2, :] * _shift_down(halo, q, 1) + cw_ref[2:3, :] * q
        q_carry[...] = q[tt - q_carry.shape[0]:, :]
        c = (bgate * cv).astype(BF16)
        c_ref[...] = c
        pb = _mm(c, pb_w[...])
        pb_ref[...] = pb.astype(BF16)
        m = (m + _sigmoid(proj(6) + bg_ref[:, D:]) * pb).astype(BF16)
        m_ref[...] = m
        x1_ref[...] = x + _mm(m, wo_w[...])

    tile = lambda w: pl.BlockSpec((tt, w), lambda i: (i, 0))
    out_shape = ([jax.ShapeDtypeStruct((T, IN_COLS), BF16)] + [jax.ShapeDtypeStruct((T, D), BF16)] * 6
                 + [jax.ShapeDtypeStruct((T, D), F32)])
    return _pcall(
        body, name="mixer_fwd", grid=(nt,), out_shape=out_shape,
        in_specs=[tile(D), _const_spec(g_mix.shape), _const_spec(w_in.shape), _const_spec(b_gate.shape),
                  _const_spec(g_v.shape), _const_spec(w_s.shape), _const_spec(b_s3.shape), _const_spec(conv.shape),
                  _const_spec(w_pa.shape), _const_spec(w_pb.shape), _const_spec(w_o.shape)],
        out_specs=[tile(IN_COLS)] + [tile(D)] * 7,
        scratch_shapes=[pltpu.VMEM((8, D), F32), pltpu.VMEM((tt, D), F32)],
        compiler_params=_params(("arbitrary",)),
    )(x, g_mix, w_in, b_gate, g_v, w_s, b_s3, conv, w_pa, w_pb, w_o)


ST_GFIN, ST_GFF, ST_LOSS = 0, 1, 2


def _ffn_fwd_bwd(x1, tgt, g_ff, g_fin, w1, w2, tt):
    T = x1.shape[0]
    nt = T // tt
    nk = D_FF // D

    def body(x1_ref, tgt_ref, gff_ref, gfin_ref, w1_ref, w2_ref,
             hf_ref, s_ref, dpre_ref, dx2_ref, dx1_ref, st_ref, z_s):
        @pl.when(pl.program_id(0) == 0)
        def _():
            st_ref[...] = jnp.zeros_like(st_ref)

        x1 = x1_ref[...]
        xh1, r1 = _rms(x1)
        hf = (xh1 * gff_ref[...]).astype(BF16)
        hf_ref[...] = hf
        acc = jnp.zeros((tt, D), F32)
        for k in range(nk):
            z = jnp.maximum(_mm(hf, w1_ref[:, _col(k)]), 0.0)
            z_s[:, _col(k)] = z
            s = (z * z).astype(BF16)
            s_ref[:, _col(k)] = s
            acc = acc + _mm(s, w2_ref[_col(k), :])
        x2 = x1 + acc
        xh2, r2 = _rms(x2)
        diff = xh2 * gfin_ref[...] - tgt_ref[...]
        st_ref[ST_LOSS:ST_LOSS + 1, :] += jnp.sum(diff * diff, axis=0, keepdims=True)
        dy = diff * (1.0 / D)
        st_ref[ST_GFIN:ST_GFIN + 1, :] += jnp.sum(dy * xh2, axis=0, keepdims=True)
        dx2 = _rms_bwd(dy * gfin_ref[...], xh2, r2)
        dx2b = dx2.astype(BF16)
        dx2_ref[...] = dx2b
        dhf = jnp.zeros((tt, D), F32)
        for k in range(nk):
            dpre = (_nt(dx2b, w2_ref[_col(k), :]) * (2.0 * z_s[:, _col(k)])).astype(BF16)
            dpre_ref[:, _col(k)] = dpre
            dhf = dhf + _nt(dpre, w1_ref[:, _col(k)])
        st_ref[ST_GFF:ST_GFF + 1, :] += jnp.sum(dhf * xh1, axis=0, keepdims=True)
        dx1_ref[...] = dx2 + _rms_bwd(dhf * gff_ref[...], xh1, r1)

    tile = lambda w: pl.BlockSpec((tt, w), lambda i: (i, 0))
    out_shape = (jax.ShapeDtypeStruct((T, D), BF16), jax.ShapeDtypeStruct((T, D_FF), BF16),
                 jax.ShapeDtypeStruct((T, D_FF), BF16), jax.ShapeDtypeStruct((T, D), BF16),
                 jax.ShapeDtypeStruct((T, D), F32), jax.ShapeDtypeStruct((8, D), F32))
    return _pcall(
        body, name="ffn_fwd_bwd", grid=(nt,), out_shape=out_shape,
        in_specs=[tile(D), tile(D), _const_spec(g_ff.shape), _const_spec(g_fin.shape),
                  _const_spec(w1.shape), _const_spec(w2.shape)],
        out_specs=[tile(D), tile(D_FF), tile(D_FF), tile(D), tile(D), pl.BlockSpec((8, D), lambda i: (0, 0))],
        scratch_shapes=[pltpu.VMEM((tt, D_FF), F32)],
        compiler_params=_params(("arbitrary",)),
    )(x1, tgt, g_ff, g_fin, w1, w2)


def _gate_bwd(dx1, proj, pa, pb, b_gate, w_pa, w_pb, w_o, tt, deps=()):
    T = dx1.shape[0]

    def body(dx1_ref, ga_ref, gb_ref, pa_ref, pb_ref, bg_ref, pa_w, pb_w, wo_w,
             dgate_ref, dpa_ref, dpb_ref, da_ref, dc_ref, dx1b_ref, dbg_ref):
        @pl.when(pl.program_id(0) == 0)
        def _():
            dbg_ref[...] = jnp.zeros_like(dbg_ref)

        dx1b = dx1_ref[...].astype(BF16)
        dx1b_ref[...] = dx1b
        dm = _nt(dx1b, wo_w[...])
        for k, (g_ref, p_ref, w_ref, dp_ref, db_ref) in enumerate(
                ((ga_ref, pa_ref, pa_w, dpa_ref, da_ref), (gb_ref, pb_ref, pb_w, dpb_ref, dc_ref))):
            gate = _sigmoid(g_ref[...].astype(F32) + bg_ref[:, _col(k)])
            dp = dm * gate
            dp_b = dp.astype(BF16)
            dp_ref[...] = dp_b
            dg = dp * p_ref[...].astype(F32) * (1.0 - gate)
            dbg_ref[0:1, _col(k)] += jnp.sum(dg, axis=0, keepdims=True)
            dgate_ref[:, _col(k)] = dg.astype(BF16)
            db_ref[...] = _nt(dp_b, w_ref[...]).astype(BF16)

    tile = lambda w: pl.BlockSpec((tt, w), lambda i: (i, 0))
    col = lambda k: pl.BlockSpec((tt, D), lambda i: (i, k))
    out_shape = ([jax.ShapeDtypeStruct((T, 2 * D), BF16)] + [jax.ShapeDtypeStruct((T, D), BF16)] * 5
                 + [jax.ShapeDtypeStruct((8, 2 * D), F32)])
    return _pcall(
        _after(body, 9, deps), name="gate_bwd", grid=(T // tt,), out_shape=out_shape,
        in_specs=[tile(D), col(5), col(6), tile(D), tile(D), _const_spec(b_gate.shape),
                  _const_spec(w_pa.shape), _const_spec(w_pb.shape), _const_spec(w_o.shape)] + [_ANY] * len(deps),
        out_specs=[tile(2 * D)] + [tile(D)] * 5 + [pl.BlockSpec((8, 2 * D), lambda i: (0, 0))],
        compiler_params=_params(("arbitrary",)),
    )(dx1, proj, proj, pa, pb, b_gate, w_pa, w_pb, w_o, *deps)


ST_GV, ST_CONV, ST_GMIX = 0, 1, 4
N_BRANCH_COLS = 5


def _branch_bwd(da, dc, dgate, proj, x, dx1, g_mix, w_in, g_v, w_s, b_s3, conv, tt, deps=()):
    T = x.shape[0]
    nt = T // tt
    nb = tt // SG
    hb = tt // HALO

    def body(da_ref, dc_ref, dgate_ref, proj_ref, cgh_ref, xsh_ref, x_ref, dx1_ref,
             gmix_ref, win_ref, gv_ref, ws_ref, bs_ref, cw_ref,
             dproj_ref, gx_ref, st_ref, dws_ref, dbs_ref, d_carry, mix_s, dvp_s):
        i = pl.program_id(0)

        @pl.when(i == 0)
        def _():
            st_ref[...] = jnp.zeros_like(st_ref)
            dws_ref[...] = jnp.zeros_like(dws_ref)
            dbs_ref[...] = jnp.zeros_like(dbs_ref)
            d_carry[...] = jnp.zeros_like(d_carry)

        def pj(k):
            return proj_ref[:, _col(k)].astype(F32)

        def back(k, val):
            b = val.astype(BF16)
            dproj_ref[:, _col(k)] = b
            return _nt(b, win_ref[:, _col(k)])

        dgate = dgate_ref[...]
        dproj_ref[:, N_BRANCH_COLS * D:] = dgate
        dh = _nt(dgate[:, :D], win_ref[:, _col(5)]) + _nt(dgate[:, D:], win_ref[:, _col(6)])
        da = da_ref[...].astype(F32)
        dc = dc_ref[...].astype(F32)

        v = pj(1)
        vg, v_cdf = _gelu(v)
        vh, rv = _rms(vg)
        vp = (vh * gv_ref[...]).astype(BF16)
        wm = _masked_ws(ws_ref[...]).astype(BF16)
        for n in range(nb):
            rows = slice(n * SG, (n + 1) * SG)
            for g in range(N_GROUPS):
                cols = slice(g * SG, (g + 1) * SG)
                mix_s[rows, cols] = _mm(wm[g], vp[rows, cols]) + bs_ref[g]
        u = pj(0)
        ug, u_cdf = _gelu(u)
        dh = dh + back(0, da * mix_s[...] * _gelu_grad(u, u_cdf))
        dmix = da * ug
        dmix_b = dmix.astype(BF16)
        for n in range(nb):
            rows = slice(n * SG, (n + 1) * SG)
            for g in range(N_GROUPS):
                cols = slice(g * SG, (g + 1) * SG)
                blk = dmix_b[rows, cols]
                dws_ref[g] += _nt(blk, vp[rows, cols])
                dbs_ref[g] += dmix[rows, cols]
                dvp_s[rows, cols] = _tn(wm[g], blk)
        dvp = dvp_s[...]
        st_ref[ST_GV:ST_GV + 1, :] += jnp.sum(dvp * vh, axis=0, keepdims=True)
        dh = dh + back(1, _rms_bwd(dvp * gv_ref[...], vh, rv) * _gelu_grad(v, v_cdf))

        bgate, cg, xs = pj(2), pj(3), pj(4)
        q = cg * xs
        has_prev = (i < nt - 1).astype(F32)
        halo = cgh_ref[...].astype(F32) * xsh_ref[...].astype(F32) * has_prev
        q2 = _shift_down(halo, q, 2)
        q1 = _shift_down(halo, q, 1)
        w0, w1, w2 = cw_ref[0:1, :], cw_ref[1:2, :], cw_ref[2:3, :]
        dh = dh + back(2, dc * (w0 * q2 + w1 * q1 + w2 * q))
        dcv = dc * bgate
        st_ref[ST_CONV:ST_CONV + 1, :] += jnp.sum(dcv * q2, axis=0, keepdims=True)
        st_ref[ST_CONV + 1:ST_CONV + 2, :] += jnp.sum(dcv * q1, axis=0, keepdims=True)
        st_ref[ST_CONV + 2:ST_CONV + 3, :] += jnp.sum(dcv * q, axis=0, keepdims=True)
        nxt = d_carry[...]
        dq = w2 * dcv + w1 * _shift_up(dcv, nxt, 1) + w0 * _shift_up(dcv, nxt, 2)
        d_carry[...] = dcv[:d_carry.shape[0], :]
        dh = dh + back(3, dq * xs) + back(4, dq * cg)

        xh, r = _rms(x_ref[...])
        st_ref[ST_GMIX:ST_GMIX + 1, :] += jnp.sum(dh * xh, axis=0, keepdims=True)
        gx_ref[...] = dx1_ref[...] + _rms_bwd(dh * gmix_ref[...], xh, r)

    rev = lambda i: nt - 1 - i
    tile = lambda w: pl.BlockSpec((tt, w), lambda i: (rev(i), 0))
    halo_spec = lambda k: pl.BlockSpec((HALO, D), lambda i: (jnp.maximum(rev(i) * hb - 1, 0), k))
    res = lambda shape: pl.BlockSpec(shape, lambda i: (0,) * len(shape))
    out_shape = (jax.ShapeDtypeStruct((T, IN_COLS), BF16), jax.ShapeDtypeStruct((T, D), F32),
                 jax.ShapeDtypeStruct((8, D), F32),
                 jax.ShapeDtypeStruct((N_GROUPS, SG, SG), F32), jax.ShapeDtypeStruct((N_GROUPS, SG, SG), F32))
    return _pcall(
        _after(body, 14, deps), name="branch_bwd", grid=(nt,), out_shape=out_shape,
        in_specs=[tile(D), tile(D), tile(2 * D), tile(N_BRANCH_COLS * D), halo_spec(3), halo_spec(4), tile(D), tile(D),
                  _const_spec(g_mix.shape), _const_spec(w_in.shape), _const_spec(g_v.shape), _const_spec(w_s.shape),
                  _const_spec(b_s3.shape), _const_spec(conv.shape)] + [_ANY] * len(deps),
        out_specs=[tile(IN_COLS), tile(D), res((8, D)), res((N_GROUPS, SG, SG)), res((N_GROUPS, SG, SG))],
        scratch_shapes=[pltpu.VMEM((8, D), F32), pltpu.VMEM((tt, D), F32), pltpu.VMEM((tt, D), F32)],
        compiler_params=_params(("arbitrary",)),
    )(da, dc, dgate, proj, proj, proj, x, dx1, g_mix, w_in, g_v, w_s, b_s3, conv, *deps)


def _weight_grad(name, act, dout, bc, tk, deps=(), rows=None):
    T = act.shape[0]
    first, n_in = rows or (0, act.shape[1])
    n_out = dout.shape[1]
    nk = T // tk
    bi = min(n_in, D)
    i0 = first // bi

    def body(a_ref, d_ref, o_ref, acc):
        k = pl.program_id(2)
        p = _tn(a_ref[...], d_ref[...])

        @pl.when(k == 0)
        def _():
            acc[...] = p

        @pl.when(k > 0)
        def _():
            acc[...] += p

        @pl.when(k == nk - 1)
        def _():
            o_ref[...] = acc[...].astype(o_ref.dtype)

    return _pcall(
        _after(body, 2, deps), name=name, grid=(n_in // bi, n_out // bc, nk),
        out_shape=jax.ShapeDtypeStruct((n_in, n_out), BF16),
        in_specs=[pl.BlockSpec((tk, bi), lambda i, j, k: (k, i0 + i)), pl.BlockSpec((tk, bc), lambda i, j, k: (k, j))]
        + [_ANY] * len(deps),
        out_specs=pl.BlockSpec((bi, bc), lambda i, j, k: (i, j)),
        scratch_shapes=[pltpu.VMEM((bi, bc), F32)],
        compiler_params=_params(("arbitrary", "arbitrary", "arbitrary")),
    )(act, dout, *deps)


def _adamw(w, g, m, v):
    m = ADAM_B1 * m + (1.0 - ADAM_B1) * g
    v = ADAM_B2 * v + (1.0 - ADAM_B2) * (g * g)
    m_hat = m / (1.0 - ADAM_B1 ** ADAM_STEP)
    v_hat = v / (1.0 - ADAM_B2 ** ADAM_STEP)
    delta = -ADAM_LR * (m_hat / (jnp.sqrt(v_hat) + ADAM_EPS) + ADAM_WD * w)
    return delta, m, v


def _slot_sum(ref, own=None, me=None):
    g = None
    for s in range(N_DEV):
        term = ref[s] if own is None else jnp.where(me == s, own, ref[s])
        g = term.astype(F32) if g is None else g + term.astype(F32)
    return g


def _reduce_adamw(name, parts, w, m, v, tr):
    rows, cols = w.shape
    n_parts = len(parts)
    per = rows // n_parts // tr

    def body(*refs):
        part_refs, slot_refs = refs[:n_parts], refs[n_parts:2 * n_parts]
        w_ref, m_ref, v_ref, g_out, d_out, m_out, v_out, own, sem = refs[2 * n_parts:]
        _, me = _position()
        i = pl.program_id(0)
        for p in range(n_parts):
            @pl.when(i // per == p)
            def _(p=p):
                cp = pltpu.make_async_copy(parts[p][1](part_refs[p], me, (i - p * per) * tr, tr), own, sem)
                cp.start()
                cp.wait()
                g_out[...] = _slot_sum(slot_refs[p], own[...], me)
        d_out[...], m_out[...], v_out[...] = _adamw(w_ref[...], g_out[...], m_ref[...], v_ref[...])

    tile = pl.BlockSpec((tr, cols), lambda i: (i, 0))
    slot_spec = lambda p: pl.BlockSpec((N_DEV, tr, cols), lambda i: (0, jnp.clip(i - p * per, 0, per - 1), 0))
    return _pcall(
        body, name=name, grid=(rows // tr,), out_shape=[jax.ShapeDtypeStruct((rows, cols), F32)] * 4,
        in_specs=[_ANY] * n_parts + [slot_spec(p) for p in range(n_parts)] + [tile, tile, tile],
        out_specs=[tile] * 4,
        scratch_shapes=[pltpu.VMEM((tr, cols), parts[0][0].dtype), pltpu.SemaphoreType.DMA(())],
        compiler_params=_params(("arbitrary",)),
    )(*[p[0] for p in parts], *[p[2] for p in parts], w, m, v)


SM_GMIX, SM_GV, SM_GFF, SM_GFIN, SM_LOSS, SM_BGATE, SM_BS, SM_ROWS = 0, 1, 2, 3, 4, 5, 7, 8


def _pack_small(st_ffn, st_mix, dbg, dbs, conv_rows):
    def body(ffn_ref, mix_ref, dbg_ref, dbs_ref, sm_ref, conv_ref):
        sm_ref[SM_GMIX:SM_GMIX + 1, :] = mix_ref[ST_GMIX:ST_GMIX + 1, :]
        sm_ref[SM_GV:SM_GV + 1, :] = mix_ref[ST_GV:ST_GV + 1, :]
        sm_ref[SM_GFF:SM_GFF + 1, :] = ffn_ref[ST_GFF:ST_GFF + 1, :]
        sm_ref[SM_GFIN:SM_GFIN + 1, :] = ffn_ref[ST_GFIN:ST_GFIN + 1, :]
        sm_ref[SM_LOSS:SM_LOSS + 1, :] = ffn_ref[ST_LOSS:ST_LOSS + 1, :]
        sm_ref[SM_BGATE:SM_BGATE + 1, :] = dbg_ref[0:1, :D]
        sm_ref[SM_BGATE + 1:SM_BGATE + 2, :] = dbg_ref[0:1, D:]
        for g in range(N_GROUPS):
            sm_ref[SM_BS:SM_BS + 1, g * SG:(g + 1) * SG] = jnp.sum(dbs_ref[g].T, axis=0, keepdims=True)
        conv_ref[...] = jnp.zeros_like(conv_ref)
        for p in range(N_DEV):
            conv_ref[p, 0:conv_rows, :] = mix_ref[ST_CONV:ST_CONV + conv_rows, p * LANE:(p + 1) * LANE]

    return _pcall(
        body, name="pack_small",
        out_shape=(jax.ShapeDtypeStruct((SM_ROWS, D), F32), jax.ShapeDtypeStruct((N_DEV, 8, LANE), F32)),
        in_specs=[_VMEM] * 4, out_specs=[_VMEM] * 2, compiler_params=_params(),
    )(st_ffn, st_mix, dbg, dbs)


def _small_update(sm_slots, ws_own, ws_slots, conv_slots, params):
    flat = [a for t in params for a in t]

    def body(sm_ref, wso_ref, ws_ref, conv_ref, *refs):
        ins, outs = refs[:len(flat)], refs[len(flat):]
        loss_ref, outs = outs[0], outs[1:]
        _, me = _position()
        sm = _slot_sum(sm_ref)
        loss_ref[...] = (0.5 / D) * jnp.sum(sm[SM_LOSS:SM_LOSS + 1, :], axis=1, keepdims=True)
        grads = [sm[SM_GMIX:SM_GMIX + 1, :], sm[SM_GV:SM_GV + 1, :], sm[SM_GFF:SM_GFF + 1, :],
                 sm[SM_GFIN:SM_GFIN + 1, :], sm[SM_BGATE:SM_BGATE + 2, :], sm[SM_BS:SM_BS + 1, :],
                 _masked_ws(_slot_sum(ws_ref, wso_ref[...], me)), _slot_sum(conv_ref)]
        for n, g in enumerate(grads):
            w_ref, m_ref, v_ref = ins[3 * n:3 * n + 3]
            g_out, d_out, m_out, v_out = outs[4 * n:4 * n + 4]
            g_out[...] = g
            d_out[...], m_out[...], v_out[...] = _adamw(w_ref[...], g, m_ref[...], v_ref[...])

    out_shape = [jax.ShapeDtypeStruct((1, 1), F32)]
    for w, _, _ in params:
        out_shape += [jax.ShapeDtypeStruct(w.shape, F32)] * 4
    return _pcall(
        body, name="small_update", out_shape=out_shape,
        in_specs=[_VMEM] * (4 + len(flat)), out_specs=[_VMEM] * len(out_shape), compiler_params=_params(),
    )(sm_slots, ws_own, ws_slots, conv_slots, *flat)


def kernel(x, norm_mix_g, w_in, b_gate, norm_v_g, w_s, b_s, conv_w, w_proj_a, w_proj_b, w_out, norm_ff_g, w_ff1, w_ff2, norm_final_g, loss_target, m_norm_mix_g, m_w_in, m_b_gate, m_norm_v_g, m_w_s, m_b_s, m_conv_w, m_w_proj_a, m_w_proj_b, m_w_out, m_norm_ff_g, m_w_ff1, m_w_ff2, m_norm_final_g, v_norm_mix_g, v_w_in, v_b_gate, v_norm_v_g, v_w_s, v_b_s, v_conv_w, v_w_proj_a, v_w_proj_b, v_w_out, v_norm_ff_g, v_w_ff1, v_w_ff2, v_norm_final_g):
    T = x.shape[1]
    tt = min(256, T)
    tk = min(4096, T)
    conv_rows = conv_w.shape[1]

    pad8 = lambda a: jnp.pad(a, ((0, 8 - a.shape[0]), (0, 0)))
    xs = x.reshape(T, D)
    tgt = loss_target.reshape(T, D)
    g_mix, g_v, g_ff, g_fin = norm_mix_g, norm_v_g, norm_ff_g, norm_final_g.reshape(1, D)
    ws = w_s[0]
    bs3 = b_s.reshape(N_GROUPS, SG, 1)

    c_in, c_ff1 = w_in.shape[2], w_ff1.shape[2]
    r_ff2, r_p = w_ff2.shape[1], w_proj_a.shape[1]
    lane_view = lambda width: (lambda ref, p: _lane_block(ref, p, width))
    row_view = lambda rows: (lambda ref, p: _row_block(ref, p, rows))
    whole = lambda ref, p: ref
    slots = lambda shape, dtype: lax.empty((N_DEV,) + shape, dtype)

    placed = _place_weights(w_in[0], w_ff1[0], w_ff2[0], w_proj_a[0], w_proj_b[0], w_out[0], pad8(conv_w[0]))
    mixer_views = [lane_view(c_in), row_view(r_p), row_view(r_p), row_view(r_p), lane_view(LANE)]
    ffn_views = [lane_view(c_ff1), row_view(r_ff2)]
    sems, placed, g_token = _start_copies("gather_start", placed, _gather_items(mixer_views + ffn_views), [5, 2])
    W_in, PA, PB, WO, conv = _wait_copies(
        "gather_wait_mixer", placed[:5], sems[0], sems[1], _gather_items(mixer_views), g_token)
    proj, h, a, c, m, pa, pb, x1 = _mixer_fwd(xs, g_mix, W_in, b_gate, g_v, ws, bs3, conv, PA, PB, WO, tt)
    W1, W2 = _wait_copies("gather_wait_ffn", placed[5:], sems[2], sems[3], _gather_items(ffn_views), x1)
    hf, s, dpre, dx2, dx1, st_ffn = _ffn_fwd_bwd(x1, tgt, g_ff, g_fin, W1, W2, tt)

    d_ff2 = _weight_grad("dw_ff2", s, dx2, D, tk)
    d_ff1 = _weight_grad("dw_ff1", hf, dpre, D, tk)
    ff_views = [lane_view(c_ff1), row_view(r_ff2)]
    ff_sems, ff_arrays, ff_token = _start_copies(
        "scatter_start_ffn", [d_ff1, d_ff2, slots((D, c_ff1), BF16), slots((r_ff2, D), BF16)],
        _scatter_items(ff_views), [2])

    dgate, dpa, dpb, da, dc, dx1b, dbg = _gate_bwd(dx1, proj, pa, pb, b_gate, PA, PB, WO, tt, deps=(ff_token,))
    d_o = _weight_grad("dw_out", m, dx1b, D, tk)
    d_pa = _weight_grad("dw_proj_a", a, dpa, D, tk)
    d_pb = _weight_grad("dw_proj_b", c, dpb, D, tk)
    p_views = [row_view(r_p), row_view(r_p), row_view(r_p)]
    p_sems, p_arrays, p_token = _start_copies(
        "scatter_start_proj", [d_pa, d_pb, d_o] + [slots((r_p, D), BF16)] * 3, _scatter_items(p_views), [3])

    dproj, grad_x, st_mix, dws, dbs = _branch_bwd(
        da, dc, dgate, proj, xs, dx1, g_mix, W_in, g_v, ws, bs3, conv, tt, deps=(p_token,))
    half = D // 2
    d_in_a = _weight_grad("dw_in_top", h, dproj, D, tk, rows=(0, half))
    a_sems, a_arrays, a_token = _start_copies(
        "scatter_start_in_top", [d_in_a, slots((half, c_in), BF16)], _scatter_items([lane_view(c_in)]), [1])
    d_in_b = _weight_grad("dw_in_bottom", h, dproj, D, tk, deps=(a_token,), rows=(half, half))
    b_views = [lane_view(c_in), whole]
    b_sems, b_arrays, b_token = _start_copies(
        "scatter_start_in_bottom", [d_in_b, dws, slots((half, c_in), BF16), slots(dws.shape, F32)],
        _scatter_items(b_views), [2])

    small, d_conv = _pack_small(st_ffn, st_mix, dbg, dbs, conv_rows)
    r_conv, r_small = _small_exchange(d_conv, small)
    d_ff1, d_ff2, s_ff1, s_ff2 = _wait_copies(
        "scatter_wait_ffn", ff_arrays, ff_sems[0], ff_sems[1], _scatter_items(ff_views), b_token)
    d_pa, d_pb, d_o, s_pa, s_pb, s_o = _wait_copies(
        "scatter_wait_proj", p_arrays, p_sems[0], p_sems[1], _scatter_items(p_views), b_token)

    own_cols = lambda width: (
        lambda ref, me, r0, n: ref.at[pl.ds(r0, n), pl.ds(pl.multiple_of(me * width, LANE), width)])
    own_rows = lambda rows: (lambda ref, me, r0, n: ref.at[pl.ds(me * rows + r0, n), :])
    big = {
        "w_ff1": _reduce_adamw(
            "adamw_w_ff1", [(d_ff1, own_cols(c_ff1), s_ff1)], w_ff1[0], m_w_ff1[0], v_w_ff1[0], 256),
        "w_ff2": _reduce_adamw(
            "adamw_w_ff2", [(d_ff2, own_rows(r_ff2), s_ff2)], w_ff2[0], m_w_ff2[0], v_w_ff2[0], 128),
        "w_proj_a": _reduce_adamw(
            "adamw_w_proj_a", [(d_pa, own_rows(r_p), s_pa)], w_proj_a[0], m_w_proj_a[0], v_w_proj_a[0], 128),
        "w_proj_b": _reduce_adamw(
            "adamw_w_proj_b", [(d_pb, own_rows(r_p), s_pb)], w_proj_b[0], m_w_proj_b[0], v_w_proj_b[0], 128),
        "w_out": _reduce_adamw("adamw_w_out", [(d_o, own_rows(r_p), s_o)], w_out[0], m_w_out[0], v_w_out[0], 128),
    }
    late = big["w_out"][0]
    d_in_a, s_in_a = _wait_copies(
        "scatter_wait_in_top", a_arrays, a_sems[0], a_sems[1], _scatter_items([lane_view(c_in)]), late)
    d_in_b, dws, s_in_b, s_ws = _wait_copies(
        "scatter_wait_in_bottom", b_arrays, b_sems[0], b_sems[1], _scatter_items(b_views), late)
    big["w_in"] = _reduce_adamw(
        "adamw_w_in", [(d_in_a, own_cols(c_in), s_in_a), (d_in_b, own_cols(c_in), s_in_b)],
        w_in[0], m_w_in[0], v_w_in[0], 256)

    two = lambda a: a.reshape(2, D)
    one = lambda a: a.reshape(1, D)
    small_params = [
        (norm_mix_g, m_norm_mix_g, v_norm_mix_g),
        (norm_v_g, m_norm_v_g, v_norm_v_g),
        (norm_ff_g, m_norm_ff_g, v_norm_ff_g),
        (one(norm_final_g), one(m_norm_final_g), one(v_norm_final_g)),
        (two(b_gate), two(m_b_gate), two(v_b_gate)),
        (one(b_s), one(m_b_s), one(v_b_s)),
        (w_s[0], m_w_s[0], v_w_s[0]),
        (pad8(conv_w[0]), pad8(m_conv_w[0]), pad8(v_conv_w[0])),
    ]
    res = _small_update(r_small, dws, s_ws, r_conv, small_params)
    loss = res[0].reshape(())
    names = ["norm_mix_g", "norm_v_g", "norm_ff_g", "norm_final_g", "b_gate", "b_s", "w_s", "conv_w"]
    shapes = {"norm_mix_g": norm_mix_g.shape, "norm_v_g": norm_v_g.shape, "norm_ff_g": norm_ff_g.shape,
              "norm_final_g": norm_final_g.shape, "b_gate": b_gate.shape, "b_s": b_s.shape, "w_s": w_s.shape}
    out = {}
    for n, name in enumerate(names):
        quad = res[1 + 4 * n:5 + 4 * n]
        if name == "conv_w":
            out[name] = [q[:conv_rows][None] for q in quad]
        else:
            out[name] = [q.reshape(shapes[name]) for q in quad]
    for name, quad in big.items():
        out[name] = [q[None] for q in quad]

    order = ["norm_mix_g", "w_in", "b_gate", "norm_v_g", "w_s", "b_s", "conv_w", "w_proj_a", "w_proj_b", "w_out",
             "norm_ff_g", "w_ff1", "w_ff2", "norm_final_g"]
    grads = [out[n][0] for n in order]
    deltas = [out[n][1] for n in order]
    new_m = [out[n][2] for n in order]
    new_v = [out[n][3] for n in order]
    return (loss, grad_x.reshape(x.shape), *grads, *deltas, *new_m, *new_v)
```

```python
import math

import jax
import jax.numpy as jnp
from jax import lax
from jax.experimental import pallas as pl
from jax.experimental.pallas import tpu as pltpu

F32 = jnp.float32
BF16 = jnp.bfloat16

N_DEV = 8
D = 1024
D_FF = 4096
IN_COLS = 7 * D
SG = 128
N_GROUPS = 8
CHUNK = 64
EPS = 1e-6
HALO = 16
LANE = 128
VMEM_LIMIT = 56 * 1024 * 1024

ADAM_LR = 0.001
ADAM_B1 = 0.9
ADAM_B2 = 0.999
ADAM_EPS = 1e-08
ADAM_WD = 0.01
ADAM_STEP = 10

SQRT_HALF = math.sqrt(0.5)
INV_SQRT_2PI = 1.0 / math.sqrt(2.0 * math.pi)

_REL = [(dx, dy, dc) for dx in (0, 1) for dy in (0, 1) for dc in (0, 1)]

_VMEM = pl.BlockSpec(memory_space=pltpu.VMEM)
_ANY = pl.BlockSpec(memory_space=pl.ANY)


def _pcall(body, **kw):
    return pl.pallas_call(body, **kw)


def _params(sem=None):
    if sem is None:
        return pltpu.CompilerParams(vmem_limit_bytes=VMEM_LIMIT)
    return pltpu.CompilerParams(dimension_semantics=sem, vmem_limit_bytes=VMEM_LIMIT)


def _const_spec(shape):
    nd = len(shape)
    return pl.BlockSpec(shape, lambda *_: (0,) * nd, pipeline_mode=pl.Buffered(1))


def _after(body, n_in, deps):
    def wrapped(*refs):
        return body(*refs[:n_in], *refs[n_in + len(deps):])
    return wrapped


def _mm(a, b):
    return jnp.dot(a, b, preferred_element_type=F32)


def _nt(a, b):
    return lax.dot_general(a, b, (((1,), (1,)), ((), ())), preferred_element_type=F32)


def _tn(a, b):
    return lax.dot_general(a, b, (((0,), (0,)), ((), ())), preferred_element_type=F32)


def _rms(x):
    r = lax.rsqrt(jnp.mean(x * x, axis=-1, keepdims=True) + EPS)
    return x * r, r


def _rms_bwd(dyg, xh, r):
    return r * (dyg - xh * jnp.mean(dyg * xh, axis=-1, keepdims=True))


def _gelu(x):
    cdf = 0.5 * (1.0 + lax.erf(x * SQRT_HALF))
    return x * cdf, cdf


def _gelu_grad(x, cdf):
    return cdf + x * (jnp.exp(-0.5 * x * x) * INV_SQRT_2PI)


def _masked_ws(ws):
    i = lax.broadcasted_iota(jnp.int32, (SG, SG), 0)
    j = lax.broadcasted_iota(jnp.int32, (SG, SG), 1)
    keep = jnp.logical_or(j < CHUNK, i >= CHUNK)
    return jnp.where(keep[None], ws, jnp.zeros_like(ws))


def _shift_down(halo, q, k):
    ext = jnp.concatenate([halo, q], axis=0)
    return pltpu.roll(ext, k, 0)[halo.shape[0]:]


def _shift_up(q, nxt, k):
    ext = jnp.concatenate([q, nxt], axis=0)
    return pltpu.roll(ext, ext.shape[0] - k, 0)[:q.shape[0]]


def _col(k):
    return slice(k * D, (k + 1) * D)


def _position():
    x, y, c = lax.axis_index("x"), lax.axis_index("y"), lax.axis_index("c")
    return (x, y, c), 4 * x + 2 * y + c


def _exchange(items, send_sems, recv_sems, local_sems):
    (x, y, c), me = _position()
    started = []
    for w, (src_of, dst_of) in enumerate(items):
        own = pltpu.make_async_copy(src_of(me), dst_of(me), local_sems.at[w])
        own.start()
        started.append(own)
        for k in range(1, N_DEV):
            dx, dy, dc = _REL[k]
            peer = (1 - x if dx else x, 1 - y if dy else y, 1 - c if dc else c)
            pid = 4 * peer[0] + 2 * peer[1] + peer[2]
            cp = pltpu.make_async_remote_copy(
                src_ref=src_of(pid), dst_ref=dst_of(me),
                send_sem=send_sems.at[w * N_DEV + k], recv_sem=recv_sems.at[w * N_DEV + k],
                device_id=peer, device_id_type=pl.DeviceIdType.MESH)
            cp.start()
            started.append(cp)
    for cp in started:
        cp.wait()


def _lane_block(ref, p, width):
    return ref.at[:, pl.ds(pl.multiple_of(p * width, LANE), width)]


def _row_block(ref, p, rows):
    return ref.at[pl.ds(p * rows, rows), :]


def _small_exchange(d_conv, small):
    def body(dconv, sm, rconv, rsm, send_sems, recv_sems, local_sems):
        items = [
            (lambda p: dconv.at[p], lambda p: rconv.at[p]),
            (lambda p: sm, lambda p: rsm.at[p]),
        ]
        _exchange(items, send_sems, recv_sems, local_sems)

    n_items = 2
    return _pcall(
        body, name="small_exchange",
        out_shape=(jax.ShapeDtypeStruct((N_DEV,) + d_conv.shape[1:], F32),
                   jax.ShapeDtypeStruct((N_DEV,) + small.shape, F32)),
        in_specs=[_ANY] * n_items, out_specs=[_ANY] * n_items,
        scratch_shapes=[pltpu.SemaphoreType.DMA((n_items * N_DEV,)), pltpu.SemaphoreType.DMA((n_items * N_DEV,)),
                        pltpu.SemaphoreType.DMA((n_items,))],
        compiler_params=_params(),
    )(d_conv, small)


_HBM = pl.BlockSpec(memory_space=pltpu.HBM)
_SEM = pl.BlockSpec(memory_space=pltpu.SEMAPHORE)
_EFFECT = pltpu.SideEffectType.DATAFLOW_SIDE_EFFECTING


REL_ORDER = (1, 2, 4, 3, 5, 6, 7)


def _to_peers(src_of, dst_of, order=REL_ORDER):
    return [(src_of, dst_of, k) for k in order]


def _remote_copies(entries, send_sems, recv_sems):
    (x, y, c), me = _position()
    copies = []
    for n, (src_of, dst_of, k) in enumerate(entries):
        dx, dy, dc = _REL[k]
        peer = (1 - x if dx else x, 1 - y if dy else y, 1 - c if dc else c)
        pid = 4 * peer[0] + 2 * peer[1] + peer[2]
        copies.append(pltpu.make_async_remote_copy(
            src_ref=src_of(pid, me), dst_ref=dst_of(me), send_sem=send_sems.at[n], recv_sem=recv_sems.at[n],
            device_id=peer, device_id_type=pl.DeviceIdType.MESH))
    return copies


def _start_copies(name, arrays, make_groups):
    n = len(arrays)
    sizes = [len(g) for g in make_groups([None] * n)]

    def body(*refs):
        sems, token = refs[n:n + 2 * len(sizes)], refs[-1]
        for g, entries in enumerate(make_groups(refs[:n])):
            for cp in _remote_copies(entries, sems[2 * g], sems[2 * g + 1]):
                cp.start()
        token[...] = jnp.zeros_like(token)

    out_shape = []
    for size in sizes:
        out_shape += [pltpu.SemaphoreType.DMA((size,))] * 2
    out_shape += [pltpu.HBM(a.shape, a.dtype) for a in arrays] + [jax.ShapeDtypeStruct((8, LANE), F32)]
    res = _pcall(
        body, name=name, out_shape=out_shape,
        in_specs=[_HBM] * n, out_specs=[_SEM] * (2 * len(sizes)) + [_HBM] * n + [_VMEM],
        input_output_aliases={i: 2 * len(sizes) + i for i in range(n)},
        compiler_params=pltpu.CompilerParams(has_side_effects=_EFFECT),
    )(*[pltpu.with_memory_space_constraint(a, pltpu.HBM) for a in arrays])
    return res[:2 * len(sizes)], res[2 * len(sizes):-1], res[-1]


def _wait_copies(name, arrays, send_sems, recv_sems, make_entries, after):
    n = len(arrays)

    def body(*refs):
        for cp in _remote_copies(make_entries(refs[:n]), refs[n], refs[n + 1]):
            cp.wait_send()
            cp.wait_recv()

    return _pcall(
        body, name=name, out_shape=[pltpu.HBM(a.shape, a.dtype) for a in arrays],
        in_specs=[_HBM] * n + [_SEM, _SEM, _ANY], out_specs=[_HBM] * n,
        input_output_aliases={i: i for i in range(n)},
        compiler_params=pltpu.CompilerParams(has_side_effects=_EFFECT),
    )(*arrays, send_sems, recv_sems, after)


def _place_weights(w_in, w_ff1, w_ff2, w_pa, w_pb, w_o, conv8):
    n_items = 7
    c_in, c_ff1 = w_in.shape[1], w_ff1.shape[1]
    r_ff2, r_p = w_ff2.shape[0], w_pa.shape[0]

    def body(win_ref, w1_ref, w2_ref, pa_ref, pb_ref, wo_ref, cw_ref,
             win_o, pa_o, pb_o, wo_o, cw_o, w1_o, w2_o,
             s_win, s_w1, s_w2, s_pa, s_pb, s_wo, sems):
        _, me = _position()
        for src, stage in ((win_ref, s_win), (w1_ref, s_w1), (w2_ref, s_w2),
                           (pa_ref, s_pa), (pb_ref, s_pb), (wo_ref, s_wo)):
            stage[...] = src[...].astype(BF16)
        pairs = [
            (s_win, _lane_block(win_o, me, c_in)), (s_pa, _row_block(pa_o, me, r_p)),
            (s_pb, _row_block(pb_o, me, r_p)), (s_wo, _row_block(wo_o, me, r_p)),
            (cw_ref, _lane_block(cw_o, me, conv8.shape[1])),
            (s_w1, _lane_block(w1_o, me, c_ff1)), (s_w2, _row_block(w2_o, me, r_ff2)),
        ]
        copies = [pltpu.make_async_copy(s, d, sems.at[n]) for n, (s, d) in enumerate(pairs)]
        for cp in copies:
            cp.start()
        for cp in copies:
            cp.wait()

    out_shape = (
        jax.ShapeDtypeStruct((D, N_DEV * c_in), BF16),
        jax.ShapeDtypeStruct((N_DEV * r_p, D), BF16),
        jax.ShapeDtypeStruct((N_DEV * r_p, D), BF16),
        jax.ShapeDtypeStruct((N_DEV * r_p, D), BF16),
        jax.ShapeDtypeStruct((conv8.shape[0], N_DEV * conv8.shape[1]), F32),
        jax.ShapeDtypeStruct((D, N_DEV * c_ff1), BF16),
        jax.ShapeDtypeStruct((N_DEV * r_ff2, D), BF16),
    )
    return _pcall(
        body, name="place_weights", out_shape=out_shape,
        in_specs=[_VMEM] * n_items, out_specs=[_ANY] * n_items,
        scratch_shapes=[pltpu.VMEM(w.shape, BF16) for w in (w_in, w_ff1, w_ff2, w_pa, w_pb, w_o)]
        + [pltpu.SemaphoreType.DMA((n_items,))],
        compiler_params=_params(),
    )(w_in, w_ff1, w_ff2, w_pa, w_pb, w_o, conv8)


def _gather_entries(refs, views, order=REL_ORDER):
    entries = []
    for r, v in zip(refs, views):
        entries += _to_peers(lambda pid, me, r=r, v=v: v(r, me), lambda me, r=r, v=v: v(r, me), order)
    return entries


def _scatter_entries(views):
    def make(refs):
        srcs, lands = refs[:len(views)], refs[len(views):]
        entries = []
        for r, v, l in zip(srcs, views, lands):
            entries += _to_peers(lambda pid, me, r=r, v=v: v(r, pid), lambda me, l=l: l.at[me])
        return entries
    return make


def _norm_in(x, g_mix, tt):
    T = x.shape[0]

    def body(x_ref, g_ref, h_ref):
        xh, _ = _rms(x_ref[...])
        h_ref[...] = (xh * g_ref[...]).astype(BF16)

    tile = pl.BlockSpec((tt, D), lambda i: (i, 0))
    return _pcall(
        body, name="norm_in", grid=(T // tt,), out_shape=jax.ShapeDtypeStruct((T, D), BF16),
        in_specs=[tile, _const_spec(g_mix.shape)], out_specs=tile, compiler_params=_params(("parallel",)),
    )(x, g_mix)


def _in_proj_block(name, blk, h, w_in, proj, width, tt):
    T = h.shape[0]

    def body(blk_ref, h_ref, w_ref, proj_in, o_ref):
        o_ref[...] = _mm(h_ref[...], w_ref[...]).astype(BF16)

    return _pcall(
        body, name=name, out_shape=jax.ShapeDtypeStruct(proj.shape, proj.dtype),
        grid_spec=pltpu.PrefetchScalarGridSpec(
            num_scalar_prefetch=1, grid=(T // tt,),
            in_specs=[pl.BlockSpec((tt, D), lambda i, b: (i, 0)),
                      pl.BlockSpec((D, width), lambda i, b: (0, b[0]), pipeline_mode=pl.Buffered(1)), _ANY],
            out_specs=pl.BlockSpec((tt, width), lambda i, b: (i, b[0]))),
        input_output_aliases={3: 0},
        compiler_params=_params(("arbitrary",)),
    )(blk, h, w_in, proj)


def _mixer_fwd(x, proj, b_gate, g_v, w_s, b_s3, conv, w_pa, w_pb, w_o, tt):
    T = x.shape[0]
    nt = T // tt
    nb = tt // SG

    def body(x_ref, proj_ref, bg_ref, gv_ref, ws_ref, bs_ref, cw_ref, pa_w, pb_w, wo_w,
             a_ref, c_ref, m_ref, pa_ref, pb_ref, x1_ref, q_carry, mix_s):
        @pl.when(pl.program_id(0) == 0)
        def _():
            q_carry[...] = jnp.zeros_like(q_carry)

        def proj(k):
            return proj_ref[:, _col(k)].astype(F32)

        vg, _ = _gelu(proj(1))
        vh, _ = _rms(vg)
        vp = (vh * gv_ref[...]).astype(BF16)
        wm = _masked_ws(ws_ref[...]).astype(BF16)
        for n in range(nb):
            rows = slice(n * SG, (n + 1) * SG)
            for g in range(N_GROUPS):
                cols = slice(g * SG, (g + 1) * SG)
                mix_s[rows, cols] = _mm(wm[g], vp[rows, cols]) + bs_ref[g]
        ug, _ = _gelu(proj(0))
        a = (ug * mix_s[...]).astype(BF16)
        a_ref[...] = a
        pa = _mm(a, pa_w[...])
        pa_ref[...] = pa.astype(BF16)
        m = jax.nn.sigmoid(proj(5) + bg_ref[:, :D]) * pa

        bgate = proj(2)
        q = proj(3) * proj(4)
        halo = q_carry[...]
        cv = cw_ref[0:1, :] * _shift_down(halo, q, 2) + cw_ref[1:2, :] * _shift_down(halo, q, 1) + cw_ref[2:3, :] * q
        q_carry[...] = q[tt - q_carry.shape[0]:, :]
        c = (bgate * cv).astype(BF16)
        c_ref[...] = c
        pb = _mm(c, pb_w[...])
        pb_ref[...] = pb.astype(BF16)
        m = (m + jax.nn.sigmoid(proj(6) + bg_ref[:, D:]) * pb).astype(BF16)
        m_ref[...] = m
        x1_ref[...] = x_ref[...] + _mm(m, wo_w[...])

    tile = lambda w: pl.BlockSpec((tt, w), lambda i: (i, 0))
    out_shape = [jax.ShapeDtypeStruct((T, D), BF16)] * 5 + [jax.ShapeDtypeStruct((T, D), F32)]
    return _pcall(
        body, name="mixer_fwd", grid=(nt,), out_shape=out_shape,
        in_specs=[tile(D), tile(IN_COLS), _const_spec(b_gate.shape),
                  _const_spec(g_v.shape), _const_spec(w_s.shape), _const_spec(b_s3.shape), _const_spec(conv.shape),
                  _const_spec(w_pa.shape), _const_spec(w_pb.shape), _const_spec(w_o.shape)],
        out_specs=[tile(D)] * 6,
        scratch_shapes=[pltpu.VMEM((8, D), F32), pltpu.VMEM((tt, D), F32)],
        compiler_params=_params(("arbitrary",)),
    )(x, proj, b_gate, g_v, w_s, b_s3, conv, w_pa, w_pb, w_o)


ST_GFIN, ST_GFF, ST_LOSS = 0, 1, 2


def _ffn_fwd_bwd(x1, tgt, g_ff, g_fin, w1, w2, tt):
    T = x1.shape[0]
    nt = T // tt
    nk = D_FF // D

    def body(x1_ref, tgt_ref, gff_ref, gfin_ref, w1_ref, w2_ref,
             hf_ref, s_ref, dpre_ref, dx2_ref, dx1_ref, st_ref, z_s):
        @pl.when(pl.program_id(0) == 0)
        def _():
            st_ref[...] = jnp.zeros_like(st_ref)

        x1 = x1_ref[...]
        xh1, r1 = _rms(x1)
        hf = (xh1 * gff_ref[...]).astype(BF16)
        hf_ref[...] = hf
        acc = jnp.zeros((tt, D), F32)
        for k in range(nk):
            z = jnp.maximum(_mm(hf, w1_ref[:, _col(k)]), 0.0)
            z_s[:, _col(k)] = z
            s = (z * z).astype(BF16)
            s_ref[:, _col(k)] = s
            acc = acc + _mm(s, w2_ref[_col(k), :])
        x2 = x1 + acc
        xh2, r2 = _rms(x2)
        diff = xh2 * gfin_ref[...] - tgt_ref[...]
        st_ref[ST_LOSS:ST_LOSS + 1, :] += jnp.sum(diff * diff, axis=0, keepdims=True)
        dy = diff * (1.0 / D)
        st_ref[ST_GFIN:ST_GFIN + 1, :] += jnp.sum(dy * xh2, axis=0, keepdims=True)
        dx2 = _rms_bwd(dy * gfin_ref[...], xh2, r2)
        dx2b = dx2.astype(BF16)
        dx2_ref[...] = dx2b
        dhf = jnp.zeros((tt, D), F32)
        for k in range(nk):
            dpre = (_nt(dx2b, w2_ref[_col(k), :]) * (2.0 * z_s[:, _col(k)])).astype(BF16)
            dpre_ref[:, _col(k)] = dpre
            dhf = dhf + _nt(dpre, w1_ref[:, _col(k)])
        st_ref[ST_GFF:ST_GFF + 1, :] += jnp.sum(dhf * xh1, axis=0, keepdims=True)
        dx1_ref[...] = dx2 + _rms_bwd(dhf * gff_ref[...], xh1, r1)

    tile = lambda w: pl.BlockSpec((tt, w), lambda i: (i, 0))
    out_shape = (jax.ShapeDtypeStruct((T, D), BF16), jax.ShapeDtypeStruct((T, D_FF), BF16),
                 jax.ShapeDtypeStruct((T, D_FF), BF16), jax.ShapeDtypeStruct((T, D), BF16),
                 jax.ShapeDtypeStruct((T, D), F32), jax.ShapeDtypeStruct((8, D), F32))
    return _pcall(
        body, name="ffn_fwd_bwd", grid=(nt,), out_shape=out_shape,
        in_specs=[tile(D), tile(D), _const_spec(g_ff.shape), _const_spec(g_fin.shape),
                  _const_spec(w1.shape), _const_spec(w2.shape)],
        out_specs=[tile(D), tile(D_FF), tile(D_FF), tile(D), tile(D), pl.BlockSpec((8, D), lambda i: (0, 0))],
        scratch_shapes=[pltpu.VMEM((tt, D_FF), F32)],
        compiler_params=_params(("arbitrary",)),
    )(x1, tgt, g_ff, g_fin, w1, w2)


ST_GV, ST_CONV = 0, 1


def _mixer_bwd(dx1, proj, pa, pb, b_gate, g_v, w_s, b_s3, conv, w_pa, w_pb, w_o, tt, deps=()):
    T = dx1.shape[0]
    nt = T // tt
    nb = tt // SG
    hb = tt // HALO

    def body(dx1_ref, proj_ref, cgh_ref, xsh_ref, pa_ref, pb_ref,
             bg_ref, gv_ref, ws_ref, bs_ref, cw_ref, pa_w, pb_w, wo_w,
             dproj_ref, dpa_ref, dpb_ref, dx1b_ref, st_ref, dbg_ref, dws_ref, dbs_ref, d_carry, mix_s, dvp_s):
        i = pl.program_id(0)

        @pl.when(i == 0)
        def _():
            st_ref[...] = jnp.zeros_like(st_ref)
            dbg_ref[...] = jnp.zeros_like(dbg_ref)
            dws_ref[...] = jnp.zeros_like(dws_ref)
            dbs_ref[...] = jnp.zeros_like(dbs_ref)
            d_carry[...] = jnp.zeros_like(d_carry)

        def pj(k):
            return proj_ref[:, _col(k)].astype(F32)

        def put(k, val):
            dproj_ref[:, _col(k)] = val.astype(BF16)

        dx1b = dx1_ref[...].astype(BF16)
        dx1b_ref[...] = dx1b
        dm = _nt(dx1b, wo_w[...])
        s_a = jax.nn.sigmoid(pj(5) + bg_ref[:, :D])
        s_b = jax.nn.sigmoid(pj(6) + bg_ref[:, D:])
        dpa = dm * s_a
        dpb = dm * s_b
        dpa_b = dpa.astype(BF16)
        dpb_b = dpb.astype(BF16)
        dpa_ref[...] = dpa_b
        dpb_ref[...] = dpb_b
        dga = dpa * pa_ref[...].astype(F32) * (1.0 - s_a)
        dgb = dpb * pb_ref[...].astype(F32) * (1.0 - s_b)
        dbg_ref[0:1, :D] += jnp.sum(dga, axis=0, keepdims=True)
        dbg_ref[0:1, D:] += jnp.sum(dgb, axis=0, keepdims=True)
        put(5, dga)
        put(6, dgb)
        da = _nt(dpa_b, pa_w[...])
        dc = _nt(dpb_b, pb_w[...])

        v = pj(1)
        vg, v_cdf = _gelu(v)
        vh, rv = _rms(vg)
        vp = (vh * gv_ref[...]).astype(BF16)
        wm = _masked_ws(ws_ref[...]).astype(BF16)
        for n in range(nb):
            rows = slice(n * SG, (n + 1) * SG)
            for g in range(N_GROUPS):
                cols = slice(g * SG, (g + 1) * SG)
                mix_s[rows, cols] = _mm(wm[g], vp[rows, cols]) + bs_ref[g]
        u = pj(0)
        ug, u_cdf = _gelu(u)
        put(0, da * mix_s[...] * _gelu_grad(u, u_cdf))
        dmix = da * ug
        dmix_b = dmix.astype(BF16)
        for n in range(nb):
            rows = slice(n * SG, (n + 1) * SG)
            for g in range(N_GROUPS):
                cols = slice(g * SG, (g + 1) * SG)
                blk = dmix_b[rows, cols]
                dws_ref[g] += _nt(blk, vp[rows, cols])
                dbs_ref[g] += dmix[rows, cols]
                dvp_s[rows, cols] = _tn(wm[g], blk)
        dvp = dvp_s[...]
        st_ref[ST_GV:ST_GV + 1, :] += jnp.sum(dvp * vh, axis=0, keepdims=True)
        put(1, _rms_bwd(dvp * gv_ref[...], vh, rv) * _gelu_grad(v, v_cdf))

        bgate, cg, xs = pj(2), pj(3), pj(4)
        q = cg * xs
        has_prev = (i < nt - 1).astype(F32)
        halo = cgh_ref[...].astype(F32) * xsh_ref[...].astype(F32) * has_prev
        q2 = _shift_down(halo, q, 2)
        q1 = _shift_down(halo, q, 1)
        w0, w1, w2 = cw_ref[0:1, :], cw_ref[1:2, :], cw_ref[2:3, :]
        put(2, dc * (w0 * q2 + w1 * q1 + w2 * q))
        dcv = dc * bgate
        st_ref[ST_CONV:ST_CONV + 1, :] += jnp.sum(dcv * q2, axis=0, keepdims=True)
        st_ref[ST_CONV + 1:ST_CONV + 2, :] += jnp.sum(dcv * q1, axis=0, keepdims=True)
        st_ref[ST_CONV + 2:ST_CONV + 3, :] += jnp.sum(dcv * q, axis=0, keepdims=True)
        nxt = d_carry[...]
        dq = w2 * dcv + w1 * _shift_up(dcv, nxt, 1) + w0 * _shift_up(dcv, nxt, 2)
        d_carry[...] = dcv[:d_carry.shape[0], :]
        put(3, dq * xs)
        put(4, dq * cg)

    rev = lambda i: nt - 1 - i
    tile = lambda w: pl.BlockSpec((tt, w), lambda i: (rev(i), 0))
    halo_spec = lambda k: pl.BlockSpec((HALO, D), lambda i: (jnp.maximum(rev(i) * hb - 1, 0), k))
    res = lambda shape: pl.BlockSpec(shape, lambda i: (0,) * len(shape))
    out_shape = (jax.ShapeDtypeStruct((T, IN_COLS), BF16), jax.ShapeDtypeStruct((T, D), BF16),
                 jax.ShapeDtypeStruct((T, D), BF16), jax.ShapeDtypeStruct((T, D), BF16),
                 jax.ShapeDtypeStruct((8, D), F32), jax.ShapeDtypeStruct((8, 2 * D), F32),
                 jax.ShapeDtypeStruct((N_GROUPS, SG, SG), F32), jax.ShapeDtypeStruct((N_GROUPS, SG, SG), F32))
    return _pcall(
        _after(body, 14, deps), name="mixer_bwd", grid=(nt,), out_shape=out_shape,
        in_specs=[tile(D), tile(IN_COLS), halo_spec(3), halo_spec(4), tile(D), tile(D),
                  _const_spec(b_gate.shape), _const_spec(g_v.shape), _const_spec(w_s.shape),
                  _const_spec(b_s3.shape), _const_spec(conv.shape),
                  _const_spec(w_pa.shape), _const_spec(w_pb.shape), _const_spec(w_o.shape)] + [_ANY] * len(deps),
        out_specs=[tile(IN_COLS), tile(D), tile(D), tile(D), res((8, D)), res((8, 2 * D)),
                   res((N_GROUPS, SG, SG)), res((N_GROUPS, SG, SG))],
        scratch_shapes=[pltpu.VMEM((8, D), F32), pltpu.VMEM((tt, D), F32), pltpu.VMEM((tt, D), F32)],
        compiler_params=_params(("arbitrary",)),
    )(dx1, proj, proj, proj, pa, pb, b_gate, g_v, w_s, b_s3, conv, w_pa, w_pb, w_o, *deps)


def _in_proj_bwd(dproj, x, dx1, g_mix, w_in, tt, deps=()):
    T = x.shape[0]

    def body(dproj_ref, x_ref, dx1_ref, gmix_ref, win_ref, gx_ref, st_ref):
        @pl.when(pl.program_id(0) == 0)
        def _():
            st_ref[...] = jnp.zeros_like(st_ref)

        dh = _nt(dproj_ref[...], win_ref[...])
        xh, r = _rms(x_ref[...])
        st_ref[0:1, :] += jnp.sum(dh * xh, axis=0, keepdims=True)
        gx_ref[...] = dx1_ref[...] + _rms_bwd(dh * gmix_ref[...], xh, r)

    tile = lambda w: pl.BlockSpec((tt, w), lambda i: (i, 0))
    return _pcall(
        _after(body, 5, deps), name="in_proj_bwd", grid=(T // tt,),
        out_shape=(jax.ShapeDtypeStruct((T, D), F32), jax.ShapeDtypeStruct((8, D), F32)),
        in_specs=[tile(IN_COLS), tile(D), tile(D), _const_spec(g_mix.shape), _const_spec(w_in.shape)]
        + [_ANY] * len(deps),
        out_specs=[tile(D), pl.BlockSpec((8, D), lambda i: (0, 0))],
        compiler_params=_params(("arbitrary",)),
    )(dproj, x, dx1, g_mix, w_in, *deps)


def _weight_grad(name, act, dout, bc, tk, deps=()):
    T, n_in = act.shape
    n_out = dout.shape[1]
    nk = T // tk
    bi = min(n_in, D)

    def body(a_ref, d_ref, o_ref, acc):
        k = pl.program_id(2)
        p = _tn(a_ref[...], d_ref[...])

        @pl.when(k == 0)
        def _():
            acc[...] = p

        @pl.when(k > 0)
        def _():
            acc[...] += p

        @pl.when(k == nk - 1)
        def _():
            o_ref[...] = acc[...].astype(o_ref.dtype)

    return _pcall(
        _after(body, 2, deps), name=name, grid=(n_in // bi, n_out // bc, nk),
        out_shape=jax.ShapeDtypeStruct((n_in, n_out), BF16),
        in_specs=[pl.BlockSpec((tk, bi), lambda i, j, k: (k, i)), pl.BlockSpec((tk, bc), lambda i, j, k: (k, j))]
        + [_ANY] * len(deps),
        out_specs=pl.BlockSpec((bi, bc), lambda i, j, k: (i, j)),
        scratch_shapes=[pltpu.VMEM((bi, bc), F32)],
        compiler_params=_params(("arbitrary", "arbitrary", "arbitrary")),
    )(act, dout, *deps)


def _adamw(w, g, m, v):
    m = ADAM_B1 * m + (1.0 - ADAM_B1) * g
    v = ADAM_B2 * v + (1.0 - ADAM_B2) * (g * g)
    m_hat = m / (1.0 - ADAM_B1 ** ADAM_STEP)
    v_hat = v / (1.0 - ADAM_B2 ** ADAM_STEP)
    delta = -ADAM_LR * (m_hat / (jnp.sqrt(v_hat) + ADAM_EPS) + ADAM_WD * w)
    return delta, m, v


def _slot_sum(ref, own=None, me=None):
    g = None
    for s in range(N_DEV):
        term = ref[s] if own is None else jnp.where(me == s, own, ref[s])
        g = term.astype(F32) if g is None else g + term.astype(F32)
    return g


def _reduce_adamw(name, partial, own_rows, slots, w, m, v, tr):
    rows, cols = w.shape

    def body(part_ref, slot_ref, w_ref, m_ref, v_ref, g_out, d_out, m_out, v_out, own, sem):
        _, me = _position()
        cp = pltpu.make_async_copy(own_rows(part_ref, me, pl.program_id(0) * tr, tr), own, sem)
        cp.start()
        cp.wait()
        g = _slot_sum(slot_ref, own[...], me)
        g_out[...] = g
        d_out[...], m_out[...], v_out[...] = _adamw(w_ref[...], g, m_ref[...], v_ref[...])

    tile = pl.BlockSpec((tr, cols), lambda i: (i, 0))
    return _pcall(
        body, name=name, grid=(rows // tr,), out_shape=[jax.ShapeDtypeStruct((rows, cols), F32)] * 4,
        in_specs=[_ANY, pl.BlockSpec((N_DEV, tr, cols), lambda i: (0, i, 0)), tile, tile, tile],
        out_specs=[tile] * 4,
        scratch_shapes=[pltpu.VMEM((tr, cols), partial.dtype), pltpu.SemaphoreType.DMA(())],
        compiler_params=_params(("arbitrary",)),
    )(partial, slots, w, m, v)


SM_GMIX, SM_GV, SM_GFF, SM_GFIN, SM_LOSS, SM_BGATE, SM_BS, SM_ROWS = 0, 1, 2, 3, 4, 5, 7, 8


def _pack_small(st_ffn, st_mix, st_in, dbg, dbs, conv_rows):
    def body(ffn_ref, mix_ref, in_ref, dbg_ref, dbs_ref, sm_ref, conv_ref):
        sm_ref[SM_GMIX:SM_GMIX + 1, :] = in_ref[0:1, :]
        sm_ref[SM_GV:SM_GV + 1, :] = mix_ref[ST_GV:ST_GV + 1, :]
        sm_ref[SM_GFF:SM_GFF + 1, :] = ffn_ref[ST_GFF:ST_GFF + 1, :]
        sm_ref[SM_GFIN:SM_GFIN + 1, :] = ffn_ref[ST_GFIN:ST_GFIN + 1, :]
        sm_ref[SM_LOSS:SM_LOSS + 1, :] = ffn_ref[ST_LOSS:ST_LOSS + 1, :]
        sm_ref[SM_BGATE:SM_BGATE + 1, :] = dbg_ref[0:1, :D]
        sm_ref[SM_BGATE + 1:SM_BGATE + 2, :] = dbg_ref[0:1, D:]
        for g in range(N_GROUPS):
            sm_ref[SM_BS:SM_BS + 1, g * SG:(g + 1) * SG] = jnp.sum(dbs_ref[g].T, axis=0, keepdims=True)
        conv_ref[...] = jnp.zeros_like(conv_ref)
        for p in range(N_DEV):
            conv_ref[p, 0:conv_rows, :] = mix_ref[ST_CONV:ST_CONV + conv_rows, p * LANE:(p + 1) * LANE]

    return _pcall(
        body, name="pack_small",
        out_shape=(jax.ShapeDtypeStruct((SM_ROWS, D), F32), jax.ShapeDtypeStruct((N_DEV, 8, LANE), F32)),
        in_specs=[_VMEM] * 5, out_specs=[_VMEM] * 2, compiler_params=_params(),
    )(st_ffn, st_mix, st_in, dbg, dbs)


def _small_update(sm_slots, ws_own, ws_slots, conv_slots, params):
    flat = [a for t in params for a in t]

    def body(sm_ref, wso_ref, ws_ref, conv_ref, *refs):
        ins, outs = refs[:len(flat)], refs[len(flat):]
        loss_ref, outs = outs[0], outs[1:]
        _, me = _position()
        sm = _slot_sum(sm_ref)
        loss_ref[...] = (0.5 / D) * jnp.sum(sm[SM_LOSS:SM_LOSS + 1, :], axis=1, keepdims=True)
        grads = [sm[SM_GMIX:SM_GMIX + 1, :], sm[SM_GV:SM_GV + 1, :], sm[SM_GFF:SM_GFF + 1, :],
                 sm[SM_GFIN:SM_GFIN + 1, :], sm[SM_BGATE:SM_BGATE + 2, :], sm[SM_BS:SM_BS + 1, :],
                 _masked_ws(_slot_sum(ws_ref, wso_ref[...], me)), _slot_sum(conv_ref)]
        for n, g in enumerate(grads):
            w_ref, m_ref, v_ref = ins[3 * n:3 * n + 3]
            g_out, d_out, m_out, v_out = outs[4 * n:4 * n + 4]
            g_out[...] = g
            d_out[...], m_out[...], v_out[...] = _adamw(w_ref[...], g, m_ref[...], v_ref[...])

    out_shape = [jax.ShapeDtypeStruct((1, 1), F32)]
    for w, _, _ in params:
        out_shape += [jax.ShapeDtypeStruct(w.shape, F32)] * 4
    return _pcall(
        body, name="small_update", out_shape=out_shape,
        in_specs=[_VMEM] * (4 + len(flat)), out_specs=[_VMEM] * len(out_shape), compiler_params=_params(),
    )(sm_slots, ws_own, ws_slots, conv_slots, *flat)


def kernel(x, norm_mix_g, w_in, b_gate, norm_v_g, w_s, b_s, conv_w, w_proj_a, w_proj_b, w_out, norm_ff_g, w_ff1, w_ff2, norm_final_g, loss_target, m_norm_mix_g, m_w_in, m_b_gate, m_norm_v_g, m_w_s, m_b_s, m_conv_w, m_w_proj_a, m_w_proj_b, m_w_out, m_norm_ff_g, m_w_ff1, m_w_ff2, m_norm_final_g, v_norm_mix_g, v_w_in, v_b_gate, v_norm_v_g, v_w_s, v_b_s, v_conv_w, v_w_proj_a, v_w_proj_b, v_w_out, v_norm_ff_g, v_w_ff1, v_w_ff2, v_norm_final_g):
    T = x.shape[1]
    tt = min(256, T)
    tk = min(4096, T)
    conv_rows = conv_w.shape[1]

    pad8 = lambda a: jnp.pad(a, ((0, 8 - a.shape[0]), (0, 0)))
    xs = x.reshape(T, D)
    tgt = loss_target.reshape(T, D)
    g_mix, g_v, g_ff, g_fin = norm_mix_g, norm_v_g, norm_ff_g, norm_final_g.reshape(1, D)
    ws = w_s[0]
    bs3 = b_s.reshape(N_GROUPS, SG, 1)

    c_in, c_ff1 = w_in.shape[2], w_ff1.shape[2]
    r_ff2, r_p = w_ff2.shape[1], w_proj_a.shape[1]
    lane_view = lambda width: (lambda ref, p: _lane_block(ref, p, width))
    row_view = lambda rows: (lambda ref, p: _row_block(ref, p, rows))
    whole = lambda ref, p: ref
    slots = lambda shape, dtype: lax.empty((N_DEV,) + shape, dtype)

    placed = _place_weights(w_in[0], w_ff1[0], w_ff2[0], w_proj_a[0], w_proj_b[0], w_out[0], pad8(conv_w[0]))
    mixer_views = [row_view(r_p), row_view(r_p), row_view(r_p), lane_view(LANE)]
    ffn_views = [lane_view(c_ff1), row_view(r_ff2)]
    in_block = lambda refs, k: _gather_entries(refs[:1], [lane_view(c_in)], (k,))

    def gather_groups(refs):
        return ([in_block(refs, k) for k in REL_ORDER]
                + [_gather_entries(refs[1:5], mixer_views), _gather_entries(refs[5:], ffn_views)])

    sems, placed, g_token = _start_copies("gather_start", placed, gather_groups)
    h = _norm_in(xs, g_mix, min(512, T))
    _, me = _position()
    W_in = placed[0]
    tp = min(1024, T)
    proj = _in_proj_block("in_proj_0", me.reshape(1), h, W_in, lax.empty((T, IN_COLS), BF16), c_in, tp)
    for n, k in enumerate(REL_ORDER):
        W_in, = _wait_copies(f"gather_wait_in_{k}", [W_in], sems[2 * n], sems[2 * n + 1],
                             lambda refs, k=k: in_block(refs, k), proj)
        proj = _in_proj_block(f"in_proj_{k}", (me ^ k).reshape(1), h, W_in, proj, c_in, tp)
    n = len(REL_ORDER)
    PA, PB, WO, conv = _wait_copies(
        "gather_wait_mixer", placed[1:5], sems[2 * n], sems[2 * n + 1],
        lambda refs: _gather_entries(refs, mixer_views), proj)
    a, c, m, pa, pb, x1 = _mixer_fwd(xs, proj, b_gate, g_v, ws, bs3, conv, PA, PB, WO, tt)
    W1, W2 = _wait_copies(
        "gather_wait_ffn", placed[5:], sems[2 * n + 2], sems[2 * n + 3],
        lambda refs: _gather_entries(refs, ffn_views), x1)
    hf, s, dpre, dx2, dx1, st_ffn = _ffn_fwd_bwd(x1, tgt, g_ff, g_fin, W1, W2, tt)

    d_ff2 = _weight_grad("dw_ff2", s, dx2, D, tk)
    d_ff1 = _weight_grad("dw_ff1", hf, dpre, D, tk)
    ff_views = [lane_view(c_ff1), row_view(r_ff2)]
    ff_sems, ff_arrays, ff_token = _start_copies(
        "scatter_start_ffn", [d_ff1, d_ff2, slots((D, c_ff1), BF16), slots((r_ff2, D), BF16)],
        lambda refs: [_scatter_entries(ff_views)(refs)])

    dproj, dpa, dpb, dx1b, st_mix, dbg, dws, dbs = _mixer_bwd(
        dx1, proj, pa, pb, b_gate, g_v, ws, bs3, conv, PA, PB, WO, tt, deps=(ff_token,))
    d_o = _weight_grad("dw_out", m, dx1b, D, tk)
    d_pa = _weight_grad("dw_proj_a", a, dpa, D, tk)
    d_pb = _weight_grad("dw_proj_b", c, dpb, D, tk)
    p_views = [row_view(r_p), row_view(r_p), row_view(r_p), whole]
    p_sems, p_arrays, p_token = _start_copies(
        "scatter_start_proj", [d_pa, d_pb, d_o, dws] + [slots((r_p, D), BF16)] * 3 + [slots(dws.shape, F32)],
        lambda refs: [_scatter_entries(p_views)(refs)])

    d_in = _weight_grad("dw_in", h, dproj, D, tk, deps=(p_token,))
    in_views = [lane_view(c_in)]
    in_sems, in_arrays, in_token = _start_copies(
        "scatter_start_in", [d_in, slots((D, c_in), BF16)], lambda refs: [_scatter_entries(in_views)(refs)])
    grad_x, st_in = _in_proj_bwd(dproj, xs, dx1, g_mix, W_in, min(512, T), deps=(in_token,))

    small, d_conv = _pack_small(st_ffn, st_mix, st_in, dbg, dbs, conv_rows)
    r_conv, r_small = _small_exchange(d_conv, small)
    d_ff1, d_ff2, s_ff1, s_ff2 = _wait_copies(
        "scatter_wait_ffn", ff_arrays, ff_sems[0], ff_sems[1], _scatter_entries(ff_views), r_small)
    d_pa, d_pb, d_o, dws, s_pa, s_pb, s_o, s_ws = _wait_copies(
        "scatter_wait_proj", p_arrays, p_sems[0], p_sems[1], _scatter_entries(p_views), r_small)
    d_in, s_in = _wait_copies("scatter_wait_in", in_arrays, in_sems[0], in_sems[1], _scatter_entries(in_views), r_small)

    own_cols = lambda width: (
        lambda ref, me, r0, n: ref.at[pl.ds(r0, n), pl.ds(pl.multiple_of(me * width, LANE), width)])
    own_rows = lambda rows: (lambda ref, me, r0, n: ref.at[pl.ds(me * rows + r0, n), :])
    big = {
        "w_in": _reduce_adamw("adamw_w_in", d_in, own_cols(c_in), s_in, w_in[0], m_w_in[0], v_w_in[0], 256),
        "w_ff1": _reduce_adamw("adamw_w_ff1", d_ff1, own_cols(c_ff1), s_ff1, w_ff1[0], m_w_ff1[0], v_w_ff1[0], 256),
        "w_ff2": _reduce_adamw("adamw_w_ff2", d_ff2, own_rows(r_ff2), s_ff2, w_ff2[0], m_w_ff2[0], v_w_ff2[0], 128),
        "w_proj_a": _reduce_adamw(
            "adamw_w_proj_a", d_pa, own_rows(r_p), s_pa, w_proj_a[0], m_w_proj_a[0], v_w_proj_a[0], 128),
        "w_proj_b": _reduce_adamw(
            "adamw_w_proj_b", d_pb, own_rows(r_p), s_pb, w_proj_b[0], m_w_proj_b[0], v_w_proj_b[0], 128),
        "w_out": _reduce_adamw("adamw_w_out", d_o, own_rows(r_p), s_o, w_out[0], m_w_out[0], v_w_out[0], 128),
    }

    two = lambda a: a.reshape(2, D)
    one = lambda a: a.reshape(1, D)
    small_params = [
        (norm_mix_g, m_norm_mix_g, v_norm_mix_g),
        (norm_v_g, m_norm_v_g, v_norm_v_g),
        (norm_ff_g, m_norm_ff_g, v_norm_ff_g),
        (one(norm_final_g), one(m_norm_final_g), one(v_norm_final_g)),
        (two(b_gate), two(m_b_gate), two(v_b_gate)),
        (one(b_s), one(m_b_s), one(v_b_s)),
        (w_s[0], m_w_s[0], v_w_s[0]),
        (pad8(conv_w[0]), pad8(m_conv_w[0]), pad8(v_conv_w[0])),
    ]
    res = _small_update(r_small, dws, s_ws, r_conv, small_params)
    loss = res[0].reshape(())
    names = ["norm_mix_g", "norm_v_g", "norm_ff_g", "norm_final_g", "b_gate", "b_s", "w_s", "conv_w"]
    shapes = {"norm_mix_g": norm_mix_g.shape, "norm_v_g": norm_v_g.shape, "norm_ff_g": norm_ff_g.shape,
              "norm_final_g": norm_final_g.shape, "b_gate": b_gate.shape, "b_s": b_s.shape, "w_s": w_s.shape}
    out = {}
    for n, name in enumerate(names):
        quad = res[1 + 4 * n:5 + 4 * n]
        if name == "conv_w":
            out[name] = [q[:conv_rows][None] for q in quad]
        else:
            out[name] = [q.reshape(shapes[name]) for q in quad]
    for name, quad in big.items():
        out[name] = [q[None] for q in quad]

    order = ["norm_mix_g", "w_in", "b_gate", "norm_v_g", "w_s", "b_s", "conv_w", "w_proj_a", "w_proj_b", "w_out",
             "norm_ff_g", "w_ff1", "w_ff2", "norm_final_g"]
    grads = [out[n][0] for n in order]
    deltas = [out[n][1] for n in order]
    new_m = [out[n][2] for n in order]
    new_v = [out[n][3] for n in order]
    return (loss, grad_x.reshape(x.shape), *grads, *deltas, *new_m, *new_v)
```

```python
import math

import jax
import jax.numpy as jnp
from jax import lax
from jax.experimental import pallas as pl
from jax.experimental.pallas import tpu as pltpu

F32 = jnp.float32
BF16 = jnp.bfloat16

N_DEV = 8
D = 1024
D_FF = 4096
IN_COLS = 7 * D
SG = 128
N_GROUPS = 8
CHUNK = 64
EPS = 1e-6
HALO = 16
LANE = 128
VMEM_LIMIT = 56 * 1024 * 1024

ADAM_LR = 0.001
ADAM_B1 = 0.9
ADAM_B2 = 0.999
ADAM_EPS = 1e-08
ADAM_WD = 0.01
ADAM_STEP = 10

SQRT_HALF = math.sqrt(0.5)
INV_SQRT_2PI = 1.0 / math.sqrt(2.0 * math.pi)

_REL = [(dx, dy, dc) for dx in (0, 1) for dy in (0, 1) for dc in (0, 1)]

_VMEM = pl.BlockSpec(memory_space=pltpu.VMEM)
_ANY = pl.BlockSpec(memory_space=pl.ANY)


def _pcall(body, **kw):
    return pl.pallas_call(body, **kw)


def _params(sem=None):
    if sem is None:
        return pltpu.CompilerParams(vmem_limit_bytes=VMEM_LIMIT)
    return pltpu.CompilerParams(dimension_semantics=sem, vmem_limit_bytes=VMEM_LIMIT)


def _const_spec(shape):
    nd = len(shape)
    return pl.BlockSpec(shape, lambda *_: (0,) * nd, pipeline_mode=pl.Buffered(1))


def _after(body, n_in, deps):
    def wrapped(*refs):
        return body(*refs[:n_in], *refs[n_in + len(deps):])
    return wrapped


def _mm(a, b):
    return jnp.dot(a, b, preferred_element_type=F32)


def _nt(a, b):
    return lax.dot_general(a, b, (((1,), (1,)), ((), ())), preferred_element_type=F32)


def _tn(a, b):
    return lax.dot_general(a, b, (((0,), (0,)), ((), ())), preferred_element_type=F32)


def _rms(x):
    r = lax.rsqrt(jnp.mean(x * x, axis=-1, keepdims=True) + EPS)
    return x * r, r


def _rms_bwd(dyg, xh, r):
    return r * (dyg - xh * jnp.mean(dyg * xh, axis=-1, keepdims=True))


def _gelu(x):
    cdf = 0.5 * (1.0 + lax.erf(x * SQRT_HALF))
    return x * cdf, cdf


def _gelu_grad(x, cdf):
    return cdf + x * (jnp.exp(-0.5 * x * x) * INV_SQRT_2PI)


def _masked_ws(ws):
    i = lax.broadcasted_iota(jnp.int32, (SG, SG), 0)
    j = lax.broadcasted_iota(jnp.int32, (SG, SG), 1)
    keep = jnp.logical_or(j < CHUNK, i >= CHUNK)
    return jnp.where(keep[None], ws, jnp.zeros_like(ws))


def _shift_down(halo, q, k):
    ext = jnp.concatenate([halo, q], axis=0)
    return pltpu.roll(ext, k, 0)[halo.shape[0]:]


def _shift_up(q, nxt, k):
    ext = jnp.concatenate([q, nxt], axis=0)
    return pltpu.roll(ext, ext.shape[0] - k, 0)[:q.shape[0]]


def _col(k):
    return slice(k * D, (k + 1) * D)


def _position():
    x, y, c = lax.axis_index("x"), lax.axis_index("y"), lax.axis_index("c")
    return (x, y, c), 4 * x + 2 * y + c


def _exchange(items, send_sems, recv_sems, local_sems):
    (x, y, c), me = _position()
    started = []
    for w, (src_of, dst_of) in enumerate(items):
        own = pltpu.make_async_copy(src_of(me), dst_of(me), local_sems.at[w])
        own.start()
        started.append(own)
        for k in range(1, N_DEV):
            dx, dy, dc = _REL[k]
            peer = (1 - x if dx else x, 1 - y if dy else y, 1 - c if dc else c)
            pid = 4 * peer[0] + 2 * peer[1] + peer[2]
            cp = pltpu.make_async_remote_copy(
                src_ref=src_of(pid), dst_ref=dst_of(me),
                send_sem=send_sems.at[w * N_DEV + k], recv_sem=recv_sems.at[w * N_DEV + k],
                device_id=peer, device_id_type=pl.DeviceIdType.MESH)
            cp.start()
            started.append(cp)
    for cp in started:
        cp.wait()


def _lane_block(ref, p, width):
    return ref.at[:, pl.ds(pl.multiple_of(p * width, LANE), width)]


def _row_block(ref, p, rows):
    return ref.at[pl.ds(p * rows, rows), :]


def _small_exchange(d_conv, small):
    def body(dconv, sm, rconv, rsm, send_sems, recv_sems, local_sems):
        items = [
            (lambda p: dconv.at[p], lambda p: rconv.at[p]),
            (lambda p: sm, lambda p: rsm.at[p]),
        ]
        _exchange(items, send_sems, recv_sems, local_sems)

    n_items = 2
    return _pcall(
        body, name="small_exchange",
        out_shape=(jax.ShapeDtypeStruct((N_DEV,) + d_conv.shape[1:], F32),
                   jax.ShapeDtypeStruct((N_DEV,) + small.shape, F32)),
        in_specs=[_ANY] * n_items, out_specs=[_ANY] * n_items,
        scratch_shapes=[pltpu.SemaphoreType.DMA((n_items * N_DEV,)), pltpu.SemaphoreType.DMA((n_items * N_DEV,)),
                        pltpu.SemaphoreType.DMA((n_items,))],
        compiler_params=_params(),
    )(d_conv, small)


_HBM = pl.BlockSpec(memory_space=pltpu.HBM)
_SEM = pl.BlockSpec(memory_space=pltpu.SEMAPHORE)
_EFFECT = pltpu.SideEffectType.DATAFLOW_SIDE_EFFECTING


REL_ORDER = (1, 2, 4, 3, 5, 6, 7)


def _to_peers(src_of, dst_of, order=REL_ORDER):
    return [(src_of, dst_of, k) for k in order]


def _remote_copies(entries, send_sems, recv_sems):
    (x, y, c), me = _position()
    copies = []
    for n, (src_of, dst_of, k) in enumerate(entries):
        dx, dy, dc = _REL[k]
        peer = (1 - x if dx else x, 1 - y if dy else y, 1 - c if dc else c)
        pid = 4 * peer[0] + 2 * peer[1] + peer[2]
        copies.append(pltpu.make_async_remote_copy(
            src_ref=src_of(pid, me), dst_ref=dst_of(me), send_sem=send_sems.at[n], recv_sem=recv_sems.at[n],
            device_id=peer, device_id_type=pl.DeviceIdType.MESH))
    return copies


def _start_copies(name, arrays, make_groups):
    n = len(arrays)
    sizes = [len(g) for g in make_groups([None] * n)]

    def body(*refs):
        sems, token = refs[n:n + 2 * len(sizes)], refs[-1]
        for g, entries in enumerate(make_groups(refs[:n])):
            for cp in _remote_copies(entries, sems[2 * g], sems[2 * g + 1]):
                cp.start()
        token[...] = jnp.zeros_like(token)

    out_shape = []
    for size in sizes:
        out_shape += [pltpu.SemaphoreType.DMA((size,))] * 2
    out_shape += [pltpu.HBM(a.shape, a.dtype) for a in arrays] + [jax.ShapeDtypeStruct((8, LANE), F32)]
    res = _pcall(
        body, name=name, out_shape=out_shape,
        in_specs=[_HBM] * n, out_specs=[_SEM] * (2 * len(sizes)) + [_HBM] * n + [_VMEM],
        input_output_aliases={i: 2 * len(sizes) + i for i in range(n)},
        compiler_params=pltpu.CompilerParams(has_side_effects=_EFFECT),
    )(*[pltpu.with_memory_space_constraint(a, pltpu.HBM) for a in arrays])
    return res[:2 * len(sizes)], res[2 * len(sizes):-1], res[-1]


def _wait_copies(name, arrays, send_sems, recv_sems, make_entries, after):
    n = len(arrays)

    def body(*refs):
        for cp in _remote_copies(make_entries(refs[:n]), refs[n], refs[n + 1]):
            cp.wait_send()
            cp.wait_recv()

    return _pcall(
        body, name=name, out_shape=[pltpu.HBM(a.shape, a.dtype) for a in arrays],
        in_specs=[_HBM] * n + [_SEM, _SEM, _ANY], out_specs=[_HBM] * n,
        input_output_aliases={i: i for i in range(n)},
        compiler_params=pltpu.CompilerParams(has_side_effects=_EFFECT),
    )(*arrays, send_sems, recv_sems, after)


def _place_weights(w_in, w_ff1, w_ff2, w_pa, w_pb, w_o, conv8):
    n_items = 7
    c_in, c_ff1 = w_in.shape[1], w_ff1.shape[1]
    r_ff2, r_p = w_ff2.shape[0], w_pa.shape[0]

    def body(win_ref, w1_ref, w2_ref, pa_ref, pb_ref, wo_ref, cw_ref,
             win_o, pa_o, pb_o, wo_o, cw_o, w1_o, w2_o,
             s_win, s_w1, s_w2, s_pa, s_pb, s_wo, sems):
        _, me = _position()
        for src, stage in ((win_ref, s_win), (w1_ref, s_w1), (w2_ref, s_w2),
                           (pa_ref, s_pa), (pb_ref, s_pb), (wo_ref, s_wo)):
            stage[...] = src[...].astype(BF16)
        pairs = [
            (s_win, _lane_block(win_o, me, c_in)), (s_pa, _row_block(pa_o, me, r_p)),
            (s_pb, _row_block(pb_o, me, r_p)), (s_wo, _row_block(wo_o, me, r_p)),
            (cw_ref, _lane_block(cw_o, me, conv8.shape[1])),
            (s_w1, _lane_block(w1_o, me, c_ff1)), (s_w2, _row_block(w2_o, me, r_ff2)),
        ]
        copies = [pltpu.make_async_copy(s, d, sems.at[n]) for n, (s, d) in enumerate(pairs)]
        for cp in copies:
            cp.start()
        for cp in copies:
            cp.wait()

    out_shape = (
        jax.ShapeDtypeStruct((D, N_DEV * c_in), BF16),
        jax.ShapeDtypeStruct((N_DEV * r_p, D), BF16),
        jax.ShapeDtypeStruct((N_DEV * r_p, D), BF16),
        jax.ShapeDtypeStruct((N_DEV * r_p, D), BF16),
        jax.ShapeDtypeStruct((conv8.shape[0], N_DEV * conv8.shape[1]), F32),
        jax.ShapeDtypeStruct((D, N_DEV * c_ff1), BF16),
        jax.ShapeDtypeStruct((N_DEV * r_ff2, D), BF16),
    )
    return _pcall(
        body, name="place_weights", out_shape=out_shape,
        in_specs=[_VMEM] * n_items, out_specs=[_ANY] * n_items,
        scratch_shapes=[pltpu.VMEM(w.shape, BF16) for w in (w_in, w_ff1, w_ff2, w_pa, w_pb, w_o)]
        + [pltpu.SemaphoreType.DMA((n_items,))],
        compiler_params=_params(),
    )(w_in, w_ff1, w_ff2, w_pa, w_pb, w_o, conv8)


def _gather_entries(refs, views, order=REL_ORDER):
    entries = []
    for r, v in zip(refs, views):
        entries += _to_peers(lambda pid, me, r=r, v=v: v(r, me), lambda me, r=r, v=v: v(r, me), order)
    return entries


def _scatter_entries(views):
    def make(refs):
        srcs, lands = refs[:len(views)], refs[len(views):]
        entries = []
        for r, v, l in zip(srcs, views, lands):
            entries += _to_peers(lambda pid, me, r=r, v=v: v(r, pid), lambda me, l=l: l.at[me])
        return entries
    return make


def _norm_in(x, g_mix, tt):
    T = x.shape[0]

    def body(x_ref, g_ref, h_ref):
        xh, _ = _rms(x_ref[...])
        h_ref[...] = (xh * g_ref[...]).astype(BF16)

    tile = pl.BlockSpec((tt, D), lambda i: (i, 0))
    return _pcall(
        body, name="norm_in", grid=(T // tt,), out_shape=jax.ShapeDtypeStruct((T, D), BF16),
        in_specs=[tile, _const_spec(g_mix.shape)], out_specs=tile, compiler_params=_params(("parallel",)),
    )(x, g_mix)


def _in_proj_pair(name, blk, h, w_in, proj, width, tt):
    T = h.shape[0]
    nt = T // tt

    def body(blk_ref, h_ref, wa_ref, wb_ref, proj_in, proj_out, w_s, o_s, sems):
        i = pl.program_id(0)

        @pl.when(i == 0)
        def _():
            w_s[:, :width] = wa_ref[...]
            w_s[:, width:] = wb_ref[...]

        def copies(slot, step):
            rows = pl.ds(step * tt, tt)
            return [pltpu.make_async_copy(
                o_s.at[slot, :, half * width:(half + 1) * width],
                proj_out.at[rows, pl.ds(pl.multiple_of(blk_ref[half] * width, LANE), width)],
                sems.at[slot, half]) for half in range(2)]

        slot = i % 2

        @pl.when(i >= 2)
        def _():
            for cp in copies(slot, i - 2):
                cp.wait()

        o_s[slot] = _mm(h_ref[...], w_s[...]).astype(BF16)
        for cp in copies(slot, i):
            cp.start()

        @pl.when(i == nt - 1)
        def _():
            for cp in copies(slot, i) + (copies(1 - slot, i - 1) if nt > 1 else []):
                cp.wait()

    w_spec = lambda half: pl.BlockSpec((D, width), lambda i, b: (0, b[half]), pipeline_mode=pl.Buffered(1))
    return _pcall(
        body, name=name, out_shape=jax.ShapeDtypeStruct(proj.shape, proj.dtype),
        grid_spec=pltpu.PrefetchScalarGridSpec(
            num_scalar_prefetch=1, grid=(nt,),
            in_specs=[pl.BlockSpec((tt, D), lambda i, b: (i, 0)), w_spec(0), w_spec(1), _ANY],
            out_specs=_ANY,
            scratch_shapes=[pltpu.VMEM((D, 2 * width), BF16), pltpu.VMEM((2, tt, 2 * width), BF16),
                            pltpu.SemaphoreType.DMA((2, 2))]),
        input_output_aliases={4: 0},
        compiler_params=_params(("arbitrary",)),
    )(blk, h, w_in, w_in, proj)


def _mixer_fwd(x, proj, b_gate, g_v, w_s, b_s3, conv, w_pa, w_pb, w_o, tt):
    T = x.shape[0]
    nt = T // tt
    nb = tt // SG

    def body(x_ref, proj_ref, bg_ref, gv_ref, ws_ref, bs_ref, cw_ref, pa_w, pb_w, wo_w,
             a_ref, c_ref, m_ref, pa_ref, pb_ref, x1_ref, q_carry, mix_s):
        @pl.when(pl.program_id(0) == 0)
        def _():
            q_carry[...] = jnp.zeros_like(q_carry)

        def proj(k):
            return proj_ref[:, _col(k)].astype(F32)

        vg, _ = _gelu(proj(1))
        vh, _ = _rms(vg)
        vp = (vh * gv_ref[...]).astype(BF16)
        wm = _masked_ws(ws_ref[...]).astype(BF16)
        for n in range(nb):
            rows = slice(n * SG, (n + 1) * SG)
            for g in range(N_GROUPS):
                cols = slice(g * SG, (g + 1) * SG)
                mix_s[rows, cols] = _mm(wm[g], vp[rows, cols]) + bs_ref[g]
        ug, _ = _gelu(proj(0))
        a = (ug * mix_s[...]).astype(BF16)
        a_ref[...] = a
        pa = _mm(a, pa_w[...])
        pa_ref[...] = pa.astype(BF16)
        m = jax.nn.sigmoid(proj(5) + bg_ref[:, :D]) * pa

        bgate = proj(2)
        q = proj(3) * proj(4)
        halo = q_carry[...]
        cv = cw_ref[0:1, :] * _shift_down(halo, q, 2) + cw_ref[1:2, :] * _shift_down(halo, q, 1) + cw_ref[2:3, :] * q
        q_carry[...] = q[tt - q_carry.shape[0]:, :]
        c = (bgate * cv).astype(BF16)
        c_ref[...] = c
        pb = _mm(c, pb_w[...])
        pb_ref[...] = pb.astype(BF16)
        m = (m + jax.nn.sigmoid(proj(6) + bg_ref[:, D:]) * pb).astype(BF16)
        m_ref[...] = m
        x1_ref[...] = x_ref[...] + _mm(m, wo_w[...])

    tile = lambda w: pl.BlockSpec((tt, w), lambda i: (i, 0))
    out_shape = [jax.ShapeDtypeStruct((T, D), BF16)] * 5 + [jax.ShapeDtypeStruct((T, D), F32)]
    return _pcall(
        body, name="mixer_fwd", grid=(nt,), out_shape=out_shape,
        in_specs=[tile(D), tile(IN_COLS), _const_spec(b_gate.shape),
                  _const_spec(g_v.shape), _const_spec(w_s.shape), _const_spec(b_s3.shape), _const_spec(conv.shape),
                  _const_spec(w_pa.shape), _const_spec(w_pb.shape), _const_spec(w_o.shape)],
        out_specs=[tile(D)] * 6,
        scratch_shapes=[pltpu.VMEM((8, D), F32), pltpu.VMEM((tt, D), F32)],
        compiler_params=_params(("arbitrary",)),
    )(x, proj, b_gate, g_v, w_s, b_s3, conv, w_pa, w_pb, w_o)


ST_GFIN, ST_GFF, ST_LOSS = 0, 1, 2


def _ffn_fwd_bwd(x1, tgt, g_ff, g_fin, w1, w2, tt):
    T = x1.shape[0]
    nt = T // tt
    nk = D_FF // D

    def body(x1_ref, tgt_ref, gff_ref, gfin_ref, w1_ref, w2_ref,
             hf_ref, s_ref, dpre_ref, dx2_ref, dx1_ref, st_ref, z_s):
        @pl.when(pl.program_id(0) == 0)
        def _():
            st_ref[...] = jnp.zeros_like(st_ref)

        x1 = x1_ref[...]
        xh1, r1 = _rms(x1)
        hf = (xh1 * gff_ref[...]).astype(BF16)
        hf_ref[...] = hf
        acc = jnp.zeros((tt, D), F32)
        for k in range(nk):
            z = jnp.maximum(_mm(hf, w1_ref[:, _col(k)]), 0.0)
            z_s[:, _col(k)] = z
            s = (z * z).astype(BF16)
            s_ref[:, _col(k)] = s
            acc = acc + _mm(s, w2_ref[_col(k), :])
        x2 = x1 + acc
        xh2, r2 = _rms(x2)
        diff = xh2 * gfin_ref[...] - tgt_ref[...]
        st_ref[ST_LOSS:ST_LOSS + 1, :] += jnp.sum(diff * diff, axis=0, keepdims=True)
        dy = diff * (1.0 / D)
        st_ref[ST_GFIN:ST_GFIN + 1, :] += jnp.sum(dy * xh2, axis=0, keepdims=True)
        dx2 = _rms_bwd(dy * gfin_ref[...], xh2, r2)
        dx2b = dx2.astype(BF16)
        dx2_ref[...] = dx2b
        dhf = jnp.zeros((tt, D), F32)
        for k in range(nk):
            dpre = (_nt(dx2b, w2_ref[_col(k), :]) * (2.0 * z_s[:, _col(k)])).astype(BF16)
            dpre_ref[:, _col(k)] = dpre
            dhf = dhf + _nt(dpre, w1_ref[:, _col(k)])
        st_ref[ST_GFF:ST_GFF + 1, :] += jnp.sum(dhf * xh1, axis=0, keepdims=True)
        dx1_ref[...] = dx2 + _rms_bwd(dhf * gff_ref[...], xh1, r1)

    tile = lambda w: pl.BlockSpec((tt, w), lambda i: (i, 0))
    out_shape = (jax.ShapeDtypeStruct((T, D), BF16), jax.ShapeDtypeStruct((T, D_FF), BF16),
                 jax.ShapeDtypeStruct((T, D_FF), BF16), jax.ShapeDtypeStruct((T, D), BF16),
                 jax.ShapeDtypeStruct((T, D), F32), jax.ShapeDtypeStruct((8, D), F32))
    return _pcall(
        body, name="ffn_fwd_bwd", grid=(nt,), out_shape=out_shape,
        in_specs=[tile(D), tile(D), _const_spec(g_ff.shape), _const_spec(g_fin.shape),
                  _const_spec(w1.shape), _const_spec(w2.shape)],
        out_specs=[tile(D), tile(D_FF), tile(D_FF), tile(D), tile(D), pl.BlockSpec((8, D), lambda i: (0, 0))],
        scratch_shapes=[pltpu.VMEM((tt, D_FF), F32)],
        compiler_params=_params(("arbitrary",)),
    )(x1, tgt, g_ff, g_fin, w1, w2)


ST_GV, ST_CONV = 0, 1


def _mixer_bwd(dx1, proj, pa, pb, b_gate, g_v, w_s, b_s3, conv, w_pa, w_pb, w_o, tt, deps=()):
    T = dx1.shape[0]
    nt = T // tt
    nb = tt // SG
    hb = tt // HALO

    def body(dx1_ref, proj_ref, cgh_ref, xsh_ref, pa_ref, pb_ref,
             bg_ref, gv_ref, ws_ref, bs_ref, cw_ref, pa_w, pb_w, wo_w,
             dproj_ref, dpa_ref, dpb_ref, dx1b_ref, st_ref, dbg_ref, dws_ref, dbs_ref, d_carry, mix_s, dvp_s):
        i = pl.program_id(0)

        @pl.when(i == 0)
        def _():
            st_ref[...] = jnp.zeros_like(st_ref)
            dbg_ref[...] = jnp.zeros_like(dbg_ref)
            dws_ref[...] = jnp.zeros_like(dws_ref)
            dbs_ref[...] = jnp.zeros_like(dbs_ref)
            d_carry[...] = jnp.zeros_like(d_carry)

        def pj(k):
            return proj_ref[:, _col(k)].astype(F32)

        def put(k, val):
            dproj_ref[:, _col(k)] = val.astype(BF16)

        dx1b = dx1_ref[...].astype(BF16)
        dx1b_ref[...] = dx1b
        dm = _nt(dx1b, wo_w[...])
        s_a = jax.nn.sigmoid(pj(5) + bg_ref[:, :D])
        s_b = jax.nn.sigmoid(pj(6) + bg_ref[:, D:])
        dpa = dm * s_a
        dpb = dm * s_b
        dpa_b = dpa.astype(BF16)
        dpb_b = dpb.astype(BF16)
        dpa_ref[...] = dpa_b
        dpb_ref[...] = dpb_b
        dga = dpa * pa_ref[...].astype(F32) * (1.0 - s_a)
        dgb = dpb * pb_ref[...].astype(F32) * (1.0 - s_b)
        dbg_ref[0:1, :D] += jnp.sum(dga, axis=0, keepdims=True)
        dbg_ref[0:1, D:] += jnp.sum(dgb, axis=0, keepdims=True)
        put(5, dga)
        put(6, dgb)
        da = _nt(dpa_b, pa_w[...])
        dc = _nt(dpb_b, pb_w[...])

        v = pj(1)
        vg, v_cdf = _gelu(v)
        vh, rv = _rms(vg)
        vp = (vh * gv_ref[...]).astype(BF16)
        wm = _masked_ws(ws_ref[...]).astype(BF16)
        for n in range(nb):
            rows = slice(n * SG, (n + 1) * SG)
            for g in range(N_GROUPS):
                cols = slice(g * SG, (g + 1) * SG)
                mix_s[rows, cols] = _mm(wm[g], vp[rows, cols]) + bs_ref[g]
        u = pj(0)
        ug, u_cdf = _gelu(u)
        put(0, da * mix_s[...] * _gelu_grad(u, u_cdf))
        dmix = da * ug
        dmix_b = dmix.astype(BF16)
        for n in range(nb):
            rows = slice(n * SG, (n + 1) * SG)
            for g in range(N_GROUPS):
                cols = slice(g * SG, (g + 1) * SG)
                blk = dmix_b[rows, cols]
                dws_ref[g] += _nt(blk, vp[rows, cols])
                dbs_ref[g] += dmix[rows, cols]
                dvp_s[rows, cols] = _tn(wm[g], blk)
        dvp = dvp_s[...]
        st_ref[ST_GV:ST_GV + 1, :] += jnp.sum(dvp * vh, axis=0, keepdims=True)
        put(1, _rms_bwd(dvp * gv_ref[...], vh, rv) * _gelu_grad(v, v_cdf))

        bgate, cg, xs = pj(2), pj(3), pj(4)
        q = cg * xs
        has_prev = (i < nt - 1).astype(F32)
        halo = cgh_ref[...].astype(F32) * xsh_ref[...].astype(F32) * has_prev
        q2 = _shift_down(halo, q, 2)
        q1 = _shift_down(halo, q, 1)
        w0, w1, w2 = cw_ref[0:1, :], cw_ref[1:2, :], cw_ref[2:3, :]
        put(2, dc * (w0 * q2 + w1 * q1 + w2 * q))
        dcv = dc * bgate
        st_ref[ST_CONV:ST_CONV + 1, :] += jnp.sum(dcv * q2, axis=0, keepdims=True)
        st_ref[ST_CONV + 1:ST_CONV + 2, :] += jnp.sum(dcv * q1, axis=0, keepdims=True)
        st_ref[ST_CONV + 2:ST_CONV + 3, :] += jnp.sum(dcv * q, axis=0, keepdims=True)
        nxt = d_carry[...]
        dq = w2 * dcv + w1 * _shift_up(dcv, nxt, 1) + w0 * _shift_up(dcv, nxt, 2)
        d_carry[...] = dcv[:d_carry.shape[0], :]
        put(3, dq * xs)
        put(4, dq * cg)

    rev = lambda i: nt - 1 - i
    tile = lambda w: pl.BlockSpec((tt, w), lambda i: (rev(i), 0))
    halo_spec = lambda k: pl.BlockSpec((HALO, D), lambda i: (jnp.maximum(rev(i) * hb - 1, 0), k))
    res = lambda shape: pl.BlockSpec(shape, lambda i: (0,) * len(shape))
    out_shape = (jax.ShapeDtypeStruct((T, IN_COLS), BF16), jax.ShapeDtypeStruct((T, D), BF16),
                 jax.ShapeDtypeStruct((T, D), BF16), jax.ShapeDtypeStruct((T, D), BF16),
                 jax.ShapeDtypeStruct((8, D), F32), jax.ShapeDtypeStruct((8, 2 * D), F32),
                 jax.ShapeDtypeStruct((N_GROUPS, SG, SG), F32), jax.ShapeDtypeStruct((N_GROUPS, SG, SG), F32))
    return _pcall(
        _after(body, 14, deps), name="mixer_bwd", grid=(nt,), out_shape=out_shape,
        in_specs=[tile(D), tile(IN_COLS), halo_spec(3), halo_spec(4), tile(D), tile(D),
                  _const_spec(b_gate.shape), _const_spec(g_v.shape), _const_spec(w_s.shape),
                  _const_spec(b_s3.shape), _const_spec(conv.shape),
                  _const_spec(w_pa.shape), _const_spec(w_pb.shape), _const_spec(w_o.shape)] + [_ANY] * len(deps),
        out_specs=[tile(IN_COLS), tile(D), tile(D), tile(D), res((8, D)), res((8, 2 * D)),
                   res((N_GROUPS, SG, SG)), res((N_GROUPS, SG, SG))],
        scratch_shapes=[pltpu.VMEM((8, D), F32), pltpu.VMEM((tt, D), F32), pltpu.VMEM((tt, D), F32)],
        compiler_params=_params(("arbitrary",)),
    )(dx1, proj, proj, proj, pa, pb, b_gate, g_v, w_s, b_s3, conv, w_pa, w_pb, w_o, *deps)


def _in_proj_bwd(dproj, x, dx1, g_mix, w_in, tt, deps=()):
    T = x.shape[0]

    def body(dproj_ref, x_ref, dx1_ref, gmix_ref, win_ref, gx_ref, st_ref):
        @pl.when(pl.program_id(0) == 0)
        def _():
            st_ref[...] = jnp.zeros_like(st_ref)

        dh = _nt(dproj_ref[...], win_ref[...])
        xh, r = _rms(x_ref[...])
        st_ref[0:1, :] += jnp.sum(dh * xh, axis=0, keepdims=True)
        gx_ref[...] = dx1_ref[...] + _rms_bwd(dh * gmix_ref[...], xh, r)

    tile = lambda w: pl.BlockSpec((tt, w), lambda i: (i, 0))
    return _pcall(
        _after(body, 5, deps), name="in_proj_bwd", grid=(T // tt,),
        out_shape=(jax.ShapeDtypeStruct((T, D), F32), jax.ShapeDtypeStruct((8, D), F32)),
        in_specs=[tile(IN_COLS), tile(D), tile(D), _const_spec(g_mix.shape), _const_spec(w_in.shape)]
        + [_ANY] * len(deps),
        out_specs=[tile(D), pl.BlockSpec((8, D), lambda i: (0, 0))],
        compiler_params=_params(("arbitrary",)),
    )(dproj, x, dx1, g_mix, w_in, *deps)


def _weight_grad(name, act, dout, bc, tk, deps=()):
    T, n_in = act.shape
    n_out = dout.shape[1]
    nk = T // tk
    bi = min(n_in, D)

    def body(a_ref, d_ref, o_ref, acc):
        k = pl.program_id(2)

        @pl.when(k == 0)
        def _():
            acc[...] = jnp.zeros_like(acc)

        acc[...] += _tn(a_ref[...], d_ref[...])

        @pl.when(k == nk - 1)
        def _():
            o_ref[...] = acc[...].astype(o_ref.dtype)

    return _pcall(
        _after(body, 2, deps), name=name, grid=(n_in // bi, n_out // bc, nk),
        out_shape=jax.ShapeDtypeStruct((n_in, n_out), BF16),
        in_specs=[pl.BlockSpec((tk, bi), lambda i, j, k: (k, i)), pl.BlockSpec((tk, bc), lambda i, j, k: (k, j))]
        + [_ANY] * len(deps),
        out_specs=pl.BlockSpec((bi, bc), lambda i, j, k: (i, j)),
        scratch_shapes=[pltpu.VMEM((bi, bc), F32)],
        compiler_params=_params(("arbitrary", "arbitrary", "arbitrary")),
    )(act, dout, *deps)


def _adamw(w, g, m, v):
    m = ADAM_B1 * m + (1.0 - ADAM_B1) * g
    v = ADAM_B2 * v + (1.0 - ADAM_B2) * (g * g)
    m_hat = m / (1.0 - ADAM_B1 ** ADAM_STEP)
    v_hat = v / (1.0 - ADAM_B2 ** ADAM_STEP)
    delta = -ADAM_LR * (m_hat / (jnp.sqrt(v_hat) + ADAM_EPS) + ADAM_WD * w)
    return delta, m, v


def _slot_sum(ref, own=None, me=None):
    g = None
    for s in range(N_DEV):
        term = ref[s] if own is None else jnp.where(me == s, own, ref[s])
        g = term.astype(F32) if g is None else g + term.astype(F32)
    return g


def _reduce_adamw(name, partial, own_rows, slots, w, m, v, tr):
    rows, cols = w.shape

    def body(part_ref, slot_ref, w_ref, m_ref, v_ref, g_out, d_out, m_out, v_out, own, sem):
        _, me = _position()
        cp = pltpu.make_async_copy(own_rows(part_ref, me, pl.program_id(0) * tr, tr), own, sem)
        cp.start()
        cp.wait()
        g = _slot_sum(slot_ref, own[...], me)
        g_out[...] = g
        d_out[...], m_out[...], v_out[...] = _adamw(w_ref[...], g, m_ref[...], v_ref[...])

    tile = pl.BlockSpec((tr, cols), lambda i: (i, 0))
    return _pcall(
        body, name=name, grid=(rows // tr,), out_shape=[jax.ShapeDtypeStruct((rows, cols), F32)] * 4,
        in_specs=[_ANY, pl.BlockSpec((N_DEV, tr, cols), lambda i: (0, i, 0)), tile, tile, tile],
        out_specs=[tile] * 4,
        scratch_shapes=[pltpu.VMEM((tr, cols), partial.dtype), pltpu.SemaphoreType.DMA(())],
        compiler_params=_params(("arbitrary",)),
    )(partial, slots, w, m, v)


SM_GMIX, SM_GV, SM_GFF, SM_GFIN, SM_LOSS, SM_BGATE, SM_BS, SM_ROWS = 0, 1, 2, 3, 4, 5, 7, 8


def _pack_small(st_ffn, st_mix, st_in, dbg, dbs, conv_rows):
    def body(ffn_ref, mix_ref, in_ref, dbg_ref, dbs_ref, sm_ref, conv_ref):
        sm_ref[SM_GMIX:SM_GMIX + 1, :] = in_ref[0:1, :]
        sm_ref[SM_GV:SM_GV + 1, :] = mix_ref[ST_GV:ST_GV + 1, :]
        sm_ref[SM_GFF:SM_GFF + 1, :] = ffn_ref[ST_GFF:ST_GFF + 1, :]
        sm_ref[SM_GFIN:SM_GFIN + 1, :] = ffn_ref[ST_GFIN:ST_GFIN + 1, :]
        sm_ref[SM_LOSS:SM_LOSS + 1, :] = ffn_ref[ST_LOSS:ST_LOSS + 1, :]
        sm_ref[SM_BGATE:SM_BGATE + 1, :] = dbg_ref[0:1, :D]
        sm_ref[SM_BGATE + 1:SM_BGATE + 2, :] = dbg_ref[0:1, D:]
        for g in range(N_GROUPS):
            sm_ref[SM_BS:SM_BS + 1, g * SG:(g + 1) * SG] = jnp.sum(dbs_ref[g].T, axis=0, keepdims=True)
        conv_ref[...] = jnp.zeros_like(conv_ref)
        for p in range(N_DEV):
            conv_ref[p, 0:conv_rows, :] = mix_ref[ST_CONV:ST_CONV + conv_rows, p * LANE:(p + 1) * LANE]

    return _pcall(
        body, name="pack_small",
        out_shape=(jax.ShapeDtypeStruct((SM_ROWS, D), F32), jax.ShapeDtypeStruct((N_DEV, 8, LANE), F32)),
        in_specs=[_VMEM] * 5, out_specs=[_VMEM] * 2, compiler_params=_params(),
    )(st_ffn, st_mix, st_in, dbg, dbs)


def _small_update(sm_slots, ws_own, ws_slots, conv_slots, params):
    flat = [a for t in params for a in t]

    def body(sm_ref, wso_ref, ws_ref, conv_ref, *refs):
        ins, outs = refs[:len(flat)], refs[len(flat):]
        loss_ref, outs = outs[0], outs[1:]
        _, me = _position()
        sm = _slot_sum(sm_ref)
        loss_ref[...] = (0.5 / D) * jnp.sum(sm[SM_LOSS:SM_LOSS + 1, :], axis=1, keepdims=True)
        grads = [sm[SM_GMIX:SM_GMIX + 1, :], sm[SM_GV:SM_GV + 1, :], sm[SM_GFF:SM_GFF + 1, :],
                 sm[SM_GFIN:SM_GFIN + 1, :], sm[SM_BGATE:SM_BGATE + 2, :], sm[SM_BS:SM_BS + 1, :],
                 _masked_ws(_slot_sum(ws_ref, wso_ref[...], me)), _slot_sum(conv_ref)]
        for n, g in enumerate(grads):
            w_ref, m_ref, v_ref = ins[3 * n:3 * n + 3]
            g_out, d_out, m_out, v_out = outs[4 * n:4 * n + 4]
            g_out[...] = g
            d_out[...], m_out[...], v_out[...] = _adamw(w_ref[...], g, m_ref[...], v_ref[...])

    out_shape = [jax.ShapeDtypeStruct((1, 1), F32)]
    for w, _, _ in params:
        out_shape += [jax.ShapeDtypeStruct(w.shape, F32)] * 4
    return _pcall(
        body, name="small_update", out_shape=out_shape,
        in_specs=[_VMEM] * (4 + len(flat)), out_specs=[_VMEM] * len(out_shape), compiler_params=_params(),
    )(sm_slots, ws_own, ws_slots, conv_slots, *flat)


def kernel(x, norm_mix_g, w_in, b_gate, norm_v_g, w_s, b_s, conv_w, w_proj_a, w_proj_b, w_out, norm_ff_g, w_ff1, w_ff2, norm_final_g, loss_target, m_norm_mix_g, m_w_in, m_b_gate, m_norm_v_g, m_w_s, m_b_s, m_conv_w, m_w_proj_a, m_w_proj_b, m_w_out, m_norm_ff_g, m_w_ff1, m_w_ff2, m_norm_final_g, v_norm_mix_g, v_w_in, v_b_gate, v_norm_v_g, v_w_s, v_b_s, v_conv_w, v_w_proj_a, v_w_proj_b, v_w_out, v_norm_ff_g, v_w_ff1, v_w_ff2, v_norm_final_g):
    T = x.shape[1]
    tt = min(256, T)
    tk = min(4096, T)
    conv_rows = conv_w.shape[1]

    pad8 = lambda a: jnp.pad(a, ((0, 8 - a.shape[0]), (0, 0)))
    xs = x.reshape(T, D)
    tgt = loss_target.reshape(T, D)
    g_mix, g_v, g_ff, g_fin = norm_mix_g, norm_v_g, norm_ff_g, norm_final_g.reshape(1, D)
    ws = w_s[0]
    bs3 = b_s.reshape(N_GROUPS, SG, 1)

    c_in, c_ff1 = w_in.shape[2], w_ff1.shape[2]
    r_ff2, r_p = w_ff2.shape[1], w_proj_a.shape[1]
    lane_view = lambda width: (lambda ref, p: _lane_block(ref, p, width))
    row_view = lambda rows: (lambda ref, p: _row_block(ref, p, rows))
    whole = lambda ref, p: ref
    slots = lambda shape, dtype: lax.empty((N_DEV,) + shape, dtype)

    placed = _place_weights(w_in[0], w_ff1[0], w_ff2[0], w_proj_a[0], w_proj_b[0], w_out[0], pad8(conv_w[0]))
    mixer_views = [row_view(r_p), row_view(r_p), row_view(r_p), lane_view(LANE)]
    ffn_views = [lane_view(c_ff1), row_view(r_ff2)]
    pairs = ((0, 1), (2, 4), (3, 5), (6, 7))
    in_group = lambda refs, ks: _gather_entries(refs[:1], [lane_view(c_in)], tuple(k for k in ks if k))

    def gather_groups(refs):
        return ([in_group(refs, ks) for ks in pairs]
                + [_gather_entries(refs[1:5], mixer_views), _gather_entries(refs[5:], ffn_views)])

    sems, placed, _ = _start_copies("gather_start", placed, gather_groups)
    h = _norm_in(xs, g_mix, min(512, T))
    _, me = _position()
    W_in = placed[0]
    tp = min(1024, T)
    proj = lax.empty((T, IN_COLS), BF16)
    for n, ks in enumerate(pairs):
        W_in, = _wait_copies(f"gather_wait_in_{n}", [W_in], sems[2 * n], sems[2 * n + 1],
                             lambda refs, ks=ks: in_group(refs, ks), proj if n else h)
        proj = _in_proj_pair(f"in_proj_{n}", jnp.stack([me ^ ks[0], me ^ ks[1]]), h, W_in, proj, c_in, tp)
    n = len(pairs)
    PA, PB, WO, conv = _wait_copies(
        "gather_wait_mixer", placed[1:5], sems[2 * n], sems[2 * n + 1],
        lambda refs: _gather_entries(refs, mixer_views), proj)
    a, c, m, pa, pb, x1 = _mixer_fwd(xs, proj, b_gate, g_v, ws, bs3, conv, PA, PB, WO, tt)
    W1, W2 = _wait_copies(
        "gather_wait_ffn", placed[5:], sems[2 * n + 2], sems[2 * n + 3],
        lambda refs: _gather_entries(refs, ffn_views), x1)
    hf, s, dpre, dx2, dx1, st_ffn = _ffn_fwd_bwd(x1, tgt, g_ff, g_fin, W1, W2, tt)

    d_ff2 = _weight_grad("dw_ff2", s, dx2, D, tk)
    d_ff1 = _weight_grad("dw_ff1", hf, dpre, D, tk)
    ff_views = [lane_view(c_ff1), row_view(r_ff2)]
    ff_sems, ff_arrays, ff_token = _start_copies(
        "scatter_start_ffn", [d_ff1, d_ff2, slots((D, c_ff1), BF16), slots((r_ff2, D), BF16)],
        lambda refs: [_scatter_entries(ff_views)(refs)])

    dproj, dpa, dpb, dx1b, st_mix, dbg, dws, dbs = _mixer_bwd(
        dx1, proj, pa, pb, b_gate, g_v, ws, bs3, conv, PA, PB, WO, tt, deps=(ff_token,))
    d_o = _weight_grad("dw_out", m, dx1b, D, tk)
    d_pa = _weight_grad("dw_proj_a", a, dpa, D, tk)
    d_pb = _weight_grad("dw_proj_b", c, dpb, D, tk)
    p_views = [row_view(r_p), row_view(r_p), row_view(r_p), whole]
    p_sems, p_arrays, p_token = _start_copies(
        "scatter_start_proj", [d_pa, d_pb, d_o, dws] + [slots((r_p, D), BF16)] * 3 + [slots(dws.shape, F32)],
        lambda refs: [_scatter_entries(p_views)(refs)])

    d_in = _weight_grad("dw_in", h, dproj, D, tk, deps=(p_token,))
    in_views = [lane_view(c_in)]
    in_sems, in_arrays, in_token = _start_copies(
        "scatter_start_in", [d_in, slots((D, c_in), BF16)], lambda refs: [_scatter_entries(in_views)(refs)])
    grad_x, st_in = _in_proj_bwd(dproj, xs, dx1, g_mix, W_in, min(512, T), deps=(in_token,))

    small, d_conv = _pack_small(st_ffn, st_mix, st_in, dbg, dbs, conv_rows)
    r_conv, r_small = _small_exchange(d_conv, small)
    d_ff1, d_ff2, s_ff1, s_ff2 = _wait_copies(
        "scatter_wait_ffn", ff_arrays, ff_sems[0], ff_sems[1], _scatter_entries(ff_views), r_small)
    d_pa, d_pb, d_o, dws, s_pa, s_pb, s_o, s_ws = _wait_copies(
        "scatter_wait_proj", p_arrays, p_sems[0], p_sems[1], _scatter_entries(p_views), r_small)
    d_in, s_in = _wait_copies("scatter_wait_in", in_arrays, in_sems[0], in_sems[1], _scatter_entries(in_views), r_small)

    own_cols = lambda width: (
        lambda ref, me, r0, n: ref.at[pl.ds(r0, n), pl.ds(pl.multiple_of(me * width, LANE), width)])
    own_rows = lambda rows: (lambda ref, me, r0, n: ref.at[pl.ds(me * rows + r0, n), :])
    big = {
        "w_in": _reduce_adamw("adamw_w_in", d_in, own_cols(c_in), s_in, w_in[0], m_w_in[0], v_w_in[0], 256),
        "w_ff1": _reduce_adamw("adamw_w_ff1", d_ff1, own_cols(c_ff1), s_ff1, w_ff1[0], m_w_ff1[0], v_w_ff1[0], 256),
        "w_ff2": _reduce_adamw("adamw_w_ff2", d_ff2, own_rows(r_ff2), s_ff2, w_ff2[0], m_w_ff2[0], v_w_ff2[0], 128),
        "w_proj_a": _reduce_adamw(
            "adamw_w_proj_a", d_pa, own_rows(r_p), s_pa, w_proj_a[0], m_w_proj_a[0], v_w_proj_a[0], 128),
        "w_proj_b": _reduce_adamw(
            "adamw_w_proj_b", d_pb, own_rows(r_p), s_pb, w_proj_b[0], m_w_proj_b[0], v_w_proj_b[0], 128),
        "w_out": _reduce_adamw("adamw_w_out", d_o, own_rows(r_p), s_o, w_out[0], m_w_out[0], v_w_out[0], 128),
    }

    two = lambda a: a.reshape(2, D)
    one = lambda a: a.reshape(1, D)
    small_params = [
        (norm_mix_g, m_norm_mix_g, v_norm_mix_g),
        (norm_v_g, m_norm_v_g, v_norm_v_g),
        (norm_ff_g, m_norm_ff_g, v_norm_ff_g),
        (one(norm_final_g), one(m_norm_final_g), one(v_norm_final_g)),
        (two(b_gate), two(m_b_gate), two(v_b_gate)),
        (one(b_s), one(m_b_s), one(v_b_s)),
        (w_s[0], m_w_s[0], v_w_s[0]),
        (pad8(conv_w[0]), pad8(m_conv_w[0]), pad8(v_conv_w[0])),
    ]
    res = _small_update(r_small, dws, s_ws, r_conv, small_params)
    loss = res[0].reshape(())
    names = ["norm_mix_g", "norm_v_g", "norm_ff_g", "norm_final_g", "b_gate", "b_s", "w_s", "conv_w"]
    shapes = {"norm_mix_g": norm_mix_g.shape, "norm_v_g": norm_v_g.shape, "norm_ff_g": norm_ff_g.shape,
              "norm_final_g": norm_final_g.shape, "b_gate": b_gate.shape, "b_s": b_s.shape, "w_s": w_s.shape}
    out = {}
    for n, name in enumerate(names):
        quad = res[1 + 4 * n:5 + 4 * n]
        if name == "conv_w":
            out[name] = [q[:conv_rows][None] for q in quad]
        else:
            out[name] = [q.reshape(shapes[name]) for q in quad]
    for name, quad in big.items():
        out[name] = [q[None] for q in quad]

    order = ["norm_mix_g", "w_in", "b_gate", "norm_v_g", "w_s", "b_s", "conv_w", "w_proj_a", "w_proj_b", "w_out",
             "norm_ff_g", "w_ff1", "w_ff2", "norm_final_g"]
    grads = [out[n][0] for n in order]
    deltas = [out[n][1] for n in order]
    new_m = [out[n][2] for n in order]
    new_v = [out[n][3] for n in order]
    return (loss, grad_x.reshape(x.shape), *grads, *deltas, *new_m, *new_v)
```

```python
import math

import jax
import jax.numpy as jnp
from jax import lax
from jax.experimental import pallas as pl
from jax.experimental.pallas import tpu as pltpu

F32 = jnp.float32
BF16 = jnp.bfloat16

N_DEV = 8
D = 1024
D_FF = 4096
IN_COLS = 7 * D
SG = 128
N_GROUPS = 8
CHUNK = 64
EPS = 1e-6
HALO = 16
LANE = 128
VMEM_LIMIT = 56 * 1024 * 1024

ADAM_LR = 0.001
ADAM_B1 = 0.9
ADAM_B2 = 0.999
ADAM_EPS = 1e-08
ADAM_WD = 0.01
ADAM_STEP = 10

SQRT_HALF = math.sqrt(0.5)
INV_SQRT_2PI = 1.0 / math.sqrt(2.0 * math.pi)

_REL = [(dx, dy, dc) for dx in (0, 1) for dy in (0, 1) for dc in (0, 1)]

_VMEM = pl.BlockSpec(memory_space=pltpu.VMEM)
_ANY = pl.BlockSpec(memory_space=pl.ANY)


def _pcall(body, **kw):
    return pl.pallas_call(body, **kw)


def _params(sem=None):
    if sem is None:
        return pltpu.CompilerParams(vmem_limit_bytes=VMEM_LIMIT)
    return pltpu.CompilerParams(dimension_semantics=sem, vmem_limit_bytes=VMEM_LIMIT)


def _const_spec(shape):
    nd = len(shape)
    return pl.BlockSpec(shape, lambda *_: (0,) * nd, pipeline_mode=pl.Buffered(1))


def _after(body, n_in, deps):
    def wrapped(*refs):
        return body(*refs[:n_in], *refs[n_in + len(deps):])
    return wrapped


def _mm(a, b):
    return jnp.dot(a, b, preferred_element_type=F32)


def _nt(a, b):
    return lax.dot_general(a, b, (((1,), (1,)), ((), ())), preferred_element_type=F32)


def _tn(a, b):
    return lax.dot_general(a, b, (((0,), (0,)), ((), ())), preferred_element_type=F32)


def _rms(x):
    r = lax.rsqrt(jnp.mean(x * x, axis=-1, keepdims=True) + EPS)
    return x * r, r


def _rms_bwd(dyg, xh, r):
    return r * (dyg - xh * jnp.mean(dyg * xh, axis=-1, keepdims=True))


def _gelu(x):
    cdf = 0.5 * (1.0 + lax.erf(x * SQRT_HALF))
    return x * cdf, cdf


def _gelu_grad(x, cdf):
    return cdf + x * (jnp.exp(-0.5 * x * x) * INV_SQRT_2PI)


def _masked_ws(ws):
    i = lax.broadcasted_iota(jnp.int32, (SG, SG), 0)
    j = lax.broadcasted_iota(jnp.int32, (SG, SG), 1)
    keep = jnp.logical_or(j < CHUNK, i >= CHUNK)
    return jnp.where(keep[None], ws, jnp.zeros_like(ws))


def _shift_down(halo, q, k):
    ext = jnp.concatenate([halo, q], axis=0)
    return pltpu.roll(ext, k, 0)[halo.shape[0]:]


def _shift_up(q, nxt, k):
    ext = jnp.concatenate([q, nxt], axis=0)
    return pltpu.roll(ext, ext.shape[0] - k, 0)[:q.shape[0]]


def _col(k):
    return slice(k * D, (k + 1) * D)


def _position():
    x, y, c = lax.axis_index("x"), lax.axis_index("y"), lax.axis_index("c")
    return (x, y, c), 4 * x + 2 * y + c


def _exchange(items, send_sems, recv_sems, local_sems):
    (x, y, c), me = _position()
    started = []
    for w, (src_of, dst_of) in enumerate(items):
        own = pltpu.make_async_copy(src_of(me), dst_of(me), local_sems.at[w])
        own.start()
        started.append(own)
        for k in range(1, N_DEV):
            dx, dy, dc = _REL[k]
            peer = (1 - x if dx else x, 1 - y if dy else y, 1 - c if dc else c)
            pid = 4 * peer[0] + 2 * peer[1] + peer[2]
            cp = pltpu.make_async_remote_copy(
                src_ref=src_of(pid), dst_ref=dst_of(me),
                send_sem=send_sems.at[w * N_DEV + k], recv_sem=recv_sems.at[w * N_DEV + k],
                device_id=peer, device_id_type=pl.DeviceIdType.MESH)
            cp.start()
            started.append(cp)
    for cp in started:
        cp.wait()


def _lane_block(ref, p, width):
    return ref.at[:, pl.ds(pl.multiple_of(p * width, LANE), width)]


def _row_block(ref, p, rows):
    return ref.at[pl.ds(p * rows, rows), :]


def _small_exchange(d_conv, small):
    def body(dconv, sm, rconv, rsm, send_sems, recv_sems, local_sems):
        items = [
            (lambda p: dconv.at[p], lambda p: rconv.at[p]),
            (lambda p: sm, lambda p: rsm.at[p]),
        ]
        _exchange(items, send_sems, recv_sems, local_sems)

    n_items = 2
    return _pcall(
        body, name="small_exchange",
        out_shape=(jax.ShapeDtypeStruct((N_DEV,) + d_conv.shape[1:], F32),
                   jax.ShapeDtypeStruct((N_DEV,) + small.shape, F32)),
        in_specs=[_ANY] * n_items, out_specs=[_ANY] * n_items,
        scratch_shapes=[pltpu.SemaphoreType.DMA((n_items * N_DEV,)), pltpu.SemaphoreType.DMA((n_items * N_DEV,)),
                        pltpu.SemaphoreType.DMA((n_items,))],
        compiler_params=_params(),
    )(d_conv, small)


_HBM = pl.BlockSpec(memory_space=pltpu.HBM)
_SEM = pl.BlockSpec(memory_space=pltpu.SEMAPHORE)
_EFFECT = pltpu.SideEffectType.DATAFLOW_SIDE_EFFECTING


REL_ORDER = (1, 2, 4, 3, 5, 6, 7)


def _to_peers(src_of, dst_of, order=REL_ORDER):
    return [(src_of, dst_of, k) for k in order]


def _remote_copies(entries, send_sems, recv_sems):
    (x, y, c), me = _position()
    copies = []
    for n, (src_of, dst_of, k) in enumerate(entries):
        dx, dy, dc = _REL[k]
        peer = (1 - x if dx else x, 1 - y if dy else y, 1 - c if dc else c)
        pid = 4 * peer[0] + 2 * peer[1] + peer[2]
        copies.append(pltpu.make_async_remote_copy(
            src_ref=src_of(pid, me), dst_ref=dst_of(me), send_sem=send_sems.at[n], recv_sem=recv_sems.at[n],
            device_id=peer, device_id_type=pl.DeviceIdType.MESH))
    return copies


def _start_copies(name, arrays, make_groups):
    n = len(arrays)
    sizes = [len(g) for g in make_groups([None] * n)]

    def body(*refs):
        sems, token = refs[n:n + 2 * len(sizes)], refs[-1]
        for g, entries in enumerate(make_groups(refs[:n])):
            for cp in _remote_copies(entries, sems[2 * g], sems[2 * g + 1]):
                cp.start()
        token[...] = jnp.zeros_like(token)

    out_shape = []
    for size in sizes:
        out_shape += [pltpu.SemaphoreType.DMA((size,))] * 2
    out_shape += [pltpu.HBM(a.shape, a.dtype) for a in arrays] + [jax.ShapeDtypeStruct((8, LANE), F32)]
    res = _pcall(
        body, name=name, out_shape=out_shape,
        in_specs=[_HBM] * n, out_specs=[_SEM] * (2 * len(sizes)) + [_HBM] * n + [_VMEM],
        input_output_aliases={i: 2 * len(sizes) + i for i in range(n)},
        compiler_params=pltpu.CompilerParams(has_side_effects=_EFFECT),
    )(*[pltpu.with_memory_space_constraint(a, pltpu.HBM) for a in arrays])
    return res[:2 * len(sizes)], res[2 * len(sizes):-1], res[-1]


def _wait_copies(name, arrays, send_sems, recv_sems, make_entries, after):
    n = len(arrays)

    def body(*refs):
        for cp in _remote_copies(make_entries(refs[:n]), refs[n], refs[n + 1]):
            cp.wait_send()
            cp.wait_recv()

    return _pcall(
        body, name=name, out_shape=[pltpu.HBM(a.shape, a.dtype) for a in arrays],
        in_specs=[_HBM] * n + [_SEM, _SEM, _ANY], out_specs=[_HBM] * n,
        input_output_aliases={i: i for i in range(n)},
        compiler_params=pltpu.CompilerParams(has_side_effects=_EFFECT),
    )(*arrays, send_sems, recv_sems, after)


def _place_weights(w_in, w_ff1, w_ff2, w_pa, w_pb, w_o, conv8):
    n_items = 7
    c_in, c_ff1 = w_in.shape[1], w_ff1.shape[1]
    r_ff2, r_p = w_ff2.shape[0], w_pa.shape[0]

    def body(win_ref, w1_ref, w2_ref, pa_ref, pb_ref, wo_ref, cw_ref,
             win_o, pa_o, pb_o, wo_o, cw_o, w1_o, w2_o,
             s_win, s_w1, s_w2, s_pa, s_pb, s_wo, sems):
        _, me = _position()
        for src, stage in ((win_ref, s_win), (w1_ref, s_w1), (w2_ref, s_w2),
                           (pa_ref, s_pa), (pb_ref, s_pb), (wo_ref, s_wo)):
            stage[...] = src[...].astype(BF16)
        pairs = [
            (s_win, _lane_block(win_o, me, c_in)), (s_pa, _row_block(pa_o, me, r_p)),
            (s_pb, _row_block(pb_o, me, r_p)), (s_wo, _row_block(wo_o, me, r_p)),
            (cw_ref, _lane_block(cw_o, me, conv8.shape[1])),
            (s_w1, _lane_block(w1_o, me, c_ff1)), (s_w2, _row_block(w2_o, me, r_ff2)),
        ]
        copies = [pltpu.make_async_copy(s, d, sems.at[n]) for n, (s, d) in enumerate(pairs)]
        for cp in copies:
            cp.start()
        for cp in copies:
            cp.wait()

    out_shape = (
        jax.ShapeDtypeStruct((D, N_DEV * c_in), BF16),
        jax.ShapeDtypeStruct((N_DEV * r_p, D), BF16),
        jax.ShapeDtypeStruct((N_DEV * r_p, D), BF16),
        jax.ShapeDtypeStruct((N_DEV * r_p, D), BF16),
        jax.ShapeDtypeStruct((conv8.shape[0], N_DEV * conv8.shape[1]), F32),
        jax.ShapeDtypeStruct((D, N_DEV * c_ff1), BF16),
        jax.ShapeDtypeStruct((N_DEV * r_ff2, D), BF16),
    )
    return _pcall(
        body, name="place_weights", out_shape=out_shape,
        in_specs=[_VMEM] * n_items, out_specs=[_ANY] * n_items,
        scratch_shapes=[pltpu.VMEM(w.shape, BF16) for w in (w_in, w_ff1, w_ff2, w_pa, w_pb, w_o)]
        + [pltpu.SemaphoreType.DMA((n_items,))],
        compiler_params=_params(),
    )(w_in, w_ff1, w_ff2, w_pa, w_pb, w_o, conv8)


def _gather_entries(refs, views, order=REL_ORDER):
    entries = []
    for r, v in zip(refs, views):
        entries += _to_peers(lambda pid, me, r=r, v=v: v(r, me), lambda me, r=r, v=v: v(r, me), order)
    return entries


def _scatter_entries(views):
    def make(refs):
        srcs, lands = refs[:len(views)], refs[len(views):]
        entries = []
        for r, v, l in zip(srcs, views, lands):
            entries += _to_peers(lambda pid, me, r=r, v=v: v(r, pid), lambda me, l=l: l.at[me])
        return entries
    return make


def _in_proj_pair(name, blk, act, w_in, proj, width, tt, g_mix=None):
    T = act.shape[0]
    nt = T // tt
    normed = g_mix is not None

    def body(blk_ref, act_ref, *refs):
        if normed:
            g_ref, wa_ref, wb_ref, _, proj_out, h_out, w_s, o_s, sems = refs
        else:
            wa_ref, wb_ref, _, proj_out, w_s, o_s, sems = refs
        i = pl.program_id(0)

        @pl.when(i == 0)
        def _():
            w_s[:, :width] = wa_ref[...]
            w_s[:, width:] = wb_ref[...]

        def copies(slot, step):
            rows = pl.ds(step * tt, tt)
            return [pltpu.make_async_copy(
                o_s.at[slot, :, half * width:(half + 1) * width],
                proj_out.at[rows, pl.ds(pl.multiple_of(blk_ref[half] * width, LANE), width)],
                sems.at[slot, half]) for half in range(2)]

        slot = i % 2

        @pl.when(i >= 2)
        def _():
            for cp in copies(slot, i - 2):
                cp.wait()

        if normed:
            xh, _ = _rms(act_ref[...])
            h = (xh * g_ref[...]).astype(BF16)
            h_out[...] = h
        else:
            h = act_ref[...]
        o_s[slot] = _mm(h, w_s[...]).astype(BF16)
        for cp in copies(slot, i):
            cp.start()

        @pl.when(i == nt - 1)
        def _():
            for cp in copies(slot, i) + (copies(1 - slot, i - 1) if nt > 1 else []):
                cp.wait()

    tile = pl.BlockSpec((tt, D), lambda i, b: (i, 0))
    w_spec = lambda half: pl.BlockSpec((D, width), lambda i, b: (0, b[half]), pipeline_mode=pl.Buffered(1))
    extra_in = [_const_spec(g_mix.shape)] if normed else []
    extra_args = [g_mix] if normed else []
    out_shape = [jax.ShapeDtypeStruct(proj.shape, proj.dtype)] + ([jax.ShapeDtypeStruct((T, D), BF16)] if normed else [])
    res = _pcall(
        body, name=name, out_shape=out_shape,
        grid_spec=pltpu.PrefetchScalarGridSpec(
            num_scalar_prefetch=1, grid=(nt,),
            in_specs=[tile] + extra_in + [w_spec(0), w_spec(1), _ANY],
            out_specs=[_ANY] + ([tile] if normed else []),
            scratch_shapes=[pltpu.VMEM((D, 2 * width), BF16), pltpu.VMEM((2, tt, 2 * width), BF16),
                            pltpu.SemaphoreType.DMA((2, 2))]),
        input_output_aliases={4 + len(extra_in): 0},
        compiler_params=_params(("arbitrary",)),
    )(blk, act, *extra_args, w_in, w_in, proj)
    return res if normed else res[0]


def _mixer_fwd(x, proj, b_gate, g_v, w_s, b_s3, conv, w_pa, w_pb, w_o, tt):
    T = x.shape[0]
    nt = T // tt
    nb = tt // SG

    def body(x_ref, proj_ref, bg_ref, gv_ref, ws_ref, bs_ref, cw_ref, pa_w, pb_w, wo_w,
             a_ref, c_ref, m_ref, pa_ref, pb_ref, x1_ref, q_carry, mix_s):
        @pl.when(pl.program_id(0) == 0)
        def _():
            q_carry[...] = jnp.zeros_like(q_carry)

        def proj(k):
            return proj_ref[:, _col(k)].astype(F32)

        vg, _ = _gelu(proj(1))
        vh, _ = _rms(vg)
        vp = (vh * gv_ref[...]).astype(BF16)
        wm = _masked_ws(ws_ref[...]).astype(BF16)
        for n in range(nb):
            rows = slice(n * SG, (n + 1) * SG)
            for g in range(N_GROUPS):
                cols = slice(g * SG, (g + 1) * SG)
                mix_s[rows, cols] = _mm(wm[g], vp[rows, cols]) + bs_ref[g]
        ug, _ = _gelu(proj(0))
        a = (ug * mix_s[...]).astype(BF16)
        a_ref[...] = a
        pa = _mm(a, pa_w[...])
        pa_ref[...] = pa.astype(BF16)
        m = jax.nn.sigmoid(proj(5) + bg_ref[:, :D]) * pa

        bgate = proj(2)
        q = proj(3) * proj(4)
        halo = q_carry[...]
        cv = cw_ref[0:1, :] * _shift_down(halo, q, 2) + cw_ref[1:2, :] * _shift_down(halo, q, 1) + cw_ref[2:3, :] * q
        q_carry[...] = q[tt - q_carry.shape[0]:, :]
        c = (bgate * cv).astype(BF16)
        c_ref[...] = c
        pb = _mm(c, pb_w[...])
        pb_ref[...] = pb.astype(BF16)
        m = (m + jax.nn.sigmoid(proj(6) + bg_ref[:, D:]) * pb).astype(BF16)
        m_ref[...] = m
        x1_ref[...] = x_ref[...] + _mm(m, wo_w[...])

    tile = lambda w: pl.BlockSpec((tt, w), lambda i: (i, 0))
    out_shape = [jax.ShapeDtypeStruct((T, D), BF16)] * 5 + [jax.ShapeDtypeStruct((T, D), F32)]
    return _pcall(
        body, name="mixer_fwd", grid=(nt,), out_shape=out_shape,
        in_specs=[tile(D), tile(IN_COLS), _const_spec(b_gate.shape),
                  _const_spec(g_v.shape), _const_spec(w_s.shape), _const_spec(b_s3.shape), _const_spec(conv.shape),
                  _const_spec(w_pa.shape), _const_spec(w_pb.shape), _const_spec(w_o.shape)],
        out_specs=[tile(D)] * 6,
        scratch_shapes=[pltpu.VMEM((8, D), F32), pltpu.VMEM((tt, D), F32)],
        compiler_params=_params(("arbitrary",)),
    )(x, proj, b_gate, g_v, w_s, b_s3, conv, w_pa, w_pb, w_o)


ST_GFIN, ST_GFF, ST_LOSS = 0, 1, 2


def _ffn_fwd_bwd(x1, tgt, g_ff, g_fin, w1, w2, tt):
    T = x1.shape[0]
    nt = T // tt
    nk = D_FF // D

    def body(x1_ref, tgt_ref, gff_ref, gfin_ref, w1_ref, w2_ref,
             hf_ref, s_ref, dpre_ref, dx2_ref, dx1_ref, st_ref, z_s):
        @pl.when(pl.program_id(0) == 0)
        def _():
            st_ref[...] = jnp.zeros_like(st_ref)

        x1 = x1_ref[...]
        xh1, r1 = _rms(x1)
        hf = (xh1 * gff_ref[...]).astype(BF16)
        hf_ref[...] = hf
        acc = jnp.zeros((tt, D), F32)
        for k in range(nk):
            z = jnp.maximum(_mm(hf, w1_ref[:, _col(k)]), 0.0)
            z_s[:, _col(k)] = z
            s = (z * z).astype(BF16)
            s_ref[:, _col(k)] = s
            acc = acc + _mm(s, w2_ref[_col(k), :])
        x2 = x1 + acc
        xh2, r2 = _rms(x2)
        diff = xh2 * gfin_ref[...] - tgt_ref[...]
        st_ref[ST_LOSS:ST_LOSS + 1, :] += jnp.sum(diff * diff, axis=0, keepdims=True)
        dy = diff * (1.0 / D)
        st_ref[ST_GFIN:ST_GFIN + 1, :] += jnp.sum(dy * xh2, axis=0, keepdims=True)
        dx2 = _rms_bwd(dy * gfin_ref[...], xh2, r2)
        dx2b = dx2.astype(BF16)
        dx2_ref[...] = dx2b
        dhf = jnp.zeros((tt, D), F32)
        for k in range(nk):
            dpre = (_nt(dx2b, w2_ref[_col(k), :]) * (2.0 * z_s[:, _col(k)])).astype(BF16)
            dpre_ref[:, _col(k)] = dpre
            dhf = dhf + _nt(dpre, w1_ref[:, _col(k)])
        st_ref[ST_GFF:ST_GFF + 1, :] += jnp.sum(dhf * xh1, axis=0, keepdims=True)
        dx1_ref[...] = dx2 + _rms_bwd(dhf * gff_ref[...], xh1, r1)

    tile = lambda w: pl.BlockSpec((tt, w), lambda i: (i, 0))
    out_shape = (jax.ShapeDtypeStruct((T, D), BF16), jax.ShapeDtypeStruct((T, D_FF), BF16),
                 jax.ShapeDtypeStruct((T, D_FF), BF16), jax.ShapeDtypeStruct((T, D), BF16),
                 jax.ShapeDtypeStruct((T, D), F32), jax.ShapeDtypeStruct((8, D), F32))
    return _pcall(
        body, name="ffn_fwd_bwd", grid=(nt,), out_shape=out_shape,
        in_specs=[tile(D), tile(D), _const_spec(g_ff.shape), _const_spec(g_fin.shape),
                  _const_spec(w1.shape), _const_spec(w2.shape)],
        out_specs=[tile(D), tile(D_FF), tile(D_FF), tile(D), tile(D), pl.BlockSpec((8, D), lambda i: (0, 0))],
        scratch_shapes=[pltpu.VMEM((tt, D_FF), F32)],
        compiler_params=_params(("arbitrary",)),
    )(x1, tgt, g_ff, g_fin, w1, w2)


ST_GV, ST_CONV = 0, 1


def _mixer_bwd(dx1, proj, pa, pb, b_gate, g_v, w_s, b_s3, conv, w_pa, w_pb, w_o, tt, deps=()):
    T = dx1.shape[0]
    nt = T // tt
    nb = tt // SG
    hb = tt // HALO

    def body(dx1_ref, proj_ref, cgh_ref, xsh_ref, pa_ref, pb_ref,
             bg_ref, gv_ref, ws_ref, bs_ref, cw_ref, pa_w, pb_w, wo_w,
             dproj_ref, dpa_ref, dpb_ref, dx1b_ref, st_ref, dbg_ref, dws_ref, dbs_ref, d_carry, mix_s, dvp_s):
        i = pl.program_id(0)

        @pl.when(i == 0)
        def _():
            st_ref[...] = jnp.zeros_like(st_ref)
            dbg_ref[...] = jnp.zeros_like(dbg_ref)
            dws_ref[...] = jnp.zeros_like(dws_ref)
            dbs_ref[...] = jnp.zeros_like(dbs_ref)
            d_carry[...] = jnp.zeros_like(d_carry)

        def pj(k):
            return proj_ref[:, _col(k)].astype(F32)

        def put(k, val):
            dproj_ref[:, _col(k)] = val.astype(BF16)

        dx1b = dx1_ref[...].astype(BF16)
        dx1b_ref[...] = dx1b
        dm = _nt(dx1b, wo_w[...])
        s_a = jax.nn.sigmoid(pj(5) + bg_ref[:, :D])
        s_b = jax.nn.sigmoid(pj(6) + bg_ref[:, D:])
        dpa = dm * s_a
        dpb = dm * s_b
        dpa_b = dpa.astype(BF16)
        dpb_b = dpb.astype(BF16)
        dpa_ref[...] = dpa_b
        dpb_ref[...] = dpb_b
        dga = dpa * pa_ref[...].astype(F32) * (1.0 - s_a)
        dgb = dpb * pb_ref[...].astype(F32) * (1.0 - s_b)
        dbg_ref[0:1, :D] += jnp.sum(dga, axis=0, keepdims=True)
        dbg_ref[0:1, D:] += jnp.sum(dgb, axis=0, keepdims=True)
        put(5, dga)
        put(6, dgb)
        da = _nt(dpa_b, pa_w[...])
        dc = _nt(dpb_b, pb_w[...])

        v = pj(1)
        vg, v_cdf = _gelu(v)
        vh, rv = _rms(vg)
        vp = (vh * gv_ref[...]).astype(BF16)
        wm = _masked_ws(ws_ref[...]).astype(BF16)
        for n in range(nb):
            rows = slice(n * SG, (n + 1) * SG)
            for g in range(N_GROUPS):
                cols = slice(g * SG, (g + 1) * SG)
                mix_s[rows, cols] = _mm(wm[g], vp[rows, cols]) + bs_ref[g]
        u = pj(0)
        ug, u_cdf = _gelu(u)
        put(0, da * mix_s[...] * _gelu_grad(u, u_cdf))
        dmix = da * ug
        dmix_b = dmix.astype(BF16)
        for n in range(nb):
            rows = slice(n * SG, (n + 1) * SG)
            for g in range(N_GROUPS):
                cols = slice(g * SG, (g + 1) * SG)
                blk = dmix_b[rows, cols]
                dws_ref[g] += _nt(blk, vp[rows, cols])
                dbs_ref[g] += dmix[rows, cols]
                dvp_s[rows, cols] = _tn(wm[g], blk)
        dvp = dvp_s[...]
        st_ref[ST_GV:ST_GV + 1, :] += jnp.sum(dvp * vh, axis=0, keepdims=True)
        put(1, _rms_bwd(dvp * gv_ref[...], vh, rv) * _gelu_grad(v, v_cdf))

        bgate, cg, xs = pj(2), pj(3), pj(4)
        q = cg * xs
        has_prev = (i < nt - 1).astype(F32)
        halo = cgh_ref[...].astype(F32) * xsh_ref[...].astype(F32) * has_prev
        q2 = _shift_down(halo, q, 2)
        q1 = _shift_down(halo, q, 1)
        w0, w1, w2 = cw_ref[0:1, :], cw_ref[1:2, :], cw_ref[2:3, :]
        put(2, dc * (w0 * q2 + w1 * q1 + w2 * q))
        dcv = dc * bgate
        st_ref[ST_CONV:ST_CONV + 1, :] += jnp.sum(dcv * q2, axis=0, keepdims=True)
        st_ref[ST_CONV + 1:ST_CONV + 2, :] += jnp.sum(dcv * q1, axis=0, keepdims=True)
        st_ref[ST_CONV + 2:ST_CONV + 3, :] += jnp.sum(dcv * q, axis=0, keepdims=True)
        nxt = d_carry[...]
        dq = w2 * dcv + w1 * _shift_up(dcv, nxt, 1) + w0 * _shift_up(dcv, nxt, 2)
        d_carry[...] = dcv[:d_carry.shape[0], :]
        put(3, dq * xs)
        put(4, dq * cg)

    rev = lambda i: nt - 1 - i
    tile = lambda w: pl.BlockSpec((tt, w), lambda i: (rev(i), 0))
    halo_spec = lambda k: pl.BlockSpec((HALO, D), lambda i: (jnp.maximum(rev(i) * hb - 1, 0), k))
    res = lambda shape: pl.BlockSpec(shape, lambda i: (0,) * len(shape))
    out_shape = (jax.ShapeDtypeStruct((T, IN_COLS), BF16), jax.ShapeDtypeStruct((T, D), BF16),
                 jax.ShapeDtypeStruct((T, D), BF16), jax.ShapeDtypeStruct((T, D), BF16),
                 jax.ShapeDtypeStruct((8, D), F32), jax.ShapeDtypeStruct((8, 2 * D), F32),
                 jax.ShapeDtypeStruct((N_GROUPS, SG, SG), F32), jax.ShapeDtypeStruct((N_GROUPS, SG, SG), F32))
    return _pcall(
        _after(body, 14, deps), name="mixer_bwd", grid=(nt,), out_shape=out_shape,
        in_specs=[tile(D), tile(IN_COLS), halo_spec(3), halo_spec(4), tile(D), tile(D),
                  _const_spec(b_gate.shape), _const_spec(g_v.shape), _const_spec(w_s.shape),
                  _const_spec(b_s3.shape), _const_spec(conv.shape),
                  _const_spec(w_pa.shape), _const_spec(w_pb.shape), _const_spec(w_o.shape)] + [_ANY] * len(deps),
        out_specs=[tile(IN_COLS), tile(D), tile(D), tile(D), res((8, D)), res((8, 2 * D)),
                   res((N_GROUPS, SG, SG)), res((N_GROUPS, SG, SG))],
        scratch_shapes=[pltpu.VMEM((8, D), F32), pltpu.VMEM((tt, D), F32), pltpu.VMEM((tt, D), F32)],
        compiler_params=_params(("arbitrary",)),
    )(dx1, proj, proj, proj, pa, pb, b_gate, g_v, w_s, b_s3, conv, w_pa, w_pb, w_o, *deps)


def _in_proj_bwd(dproj, x, dx1, g_mix, w_in, tt, deps=()):
    T = x.shape[0]

    def body(dproj_ref, x_ref, dx1_ref, gmix_ref, win_ref, gx_ref, st_ref):
        @pl.when(pl.program_id(0) == 0)
        def _():
            st_ref[...] = jnp.zeros_like(st_ref)

        dh = _nt(dproj_ref[...], win_ref[...])
        xh, r = _rms(x_ref[...])
        st_ref[0:1, :] += jnp.sum(dh * xh, axis=0, keepdims=True)
        gx_ref[...] = dx1_ref[...] + _rms_bwd(dh * gmix_ref[...], xh, r)

    tile = lambda w: pl.BlockSpec((tt, w), lambda i: (i, 0))
    return _pcall(
        _after(body, 5, deps), name="in_proj_bwd", grid=(T // tt,),
        out_shape=(jax.ShapeDtypeStruct((T, D), F32), jax.ShapeDtypeStruct((8, D), F32)),
        in_specs=[tile(IN_COLS), tile(D), tile(D), _const_spec(g_mix.shape), _const_spec(w_in.shape)]
        + [_ANY] * len(deps),
        out_specs=[tile(D), pl.BlockSpec((8, D), lambda i: (0, 0))],
        compiler_params=_params(("arbitrary",)),
    )(dproj, x, dx1, g_mix, w_in, *deps)


def _weight_grad(name, act, dout, bc, tk, deps=()):
    T, n_in = act.shape
    n_out = dout.shape[1]
    nk = T // tk
    bi = min(n_in, D)

    def body(a_ref, d_ref, o_ref, acc):
        k = pl.program_id(2)

        @pl.when(k == 0)
        def _():
            acc[...] = jnp.zeros_like(acc)

        acc[...] += _tn(a_ref[...], d_ref[...])

        @pl.when(k == nk - 1)
        def _():
            o_ref[...] = acc[...].astype(o_ref.dtype)

    return _pcall(
        _after(body, 2, deps), name=name, grid=(n_in // bi, n_out // bc, nk),
        out_shape=jax.ShapeDtypeStruct((n_in, n_out), BF16),
        in_specs=[pl.BlockSpec((tk, bi), lambda i, j, k: (k, i)), pl.BlockSpec((tk, bc), lambda i, j, k: (k, j))]
        + [_ANY] * len(deps),
        out_specs=pl.BlockSpec((bi, bc), lambda i, j, k: (i, j)),
        scratch_shapes=[pltpu.VMEM((bi, bc), F32)],
        compiler_params=_params(("arbitrary", "arbitrary", "arbitrary")),
    )(act, dout, *deps)


def _adamw(w, g, m, v):
    m = ADAM_B1 * m + (1.0 - ADAM_B1) * g
    v = ADAM_B2 * v + (1.0 - ADAM_B2) * (g * g)
    m_hat = m / (1.0 - ADAM_B1 ** ADAM_STEP)
    v_hat = v / (1.0 - ADAM_B2 ** ADAM_STEP)
    delta = -ADAM_LR * (m_hat / (jnp.sqrt(v_hat) + ADAM_EPS) + ADAM_WD * w)
    return delta, m, v


def _slot_sum(ref, own=None, me=None):
    g = None
    for s in range(N_DEV):
        term = ref[s] if own is None else jnp.where(me == s, own, ref[s])
        g = term.astype(F32) if g is None else g + term.astype(F32)
    return g


def _reduce_adamw(name, me, partial, own_block, slots, w, m, v, tr):
    rows, cols = w.shape

    def body(me_ref, own_ref, slot_ref, w_ref, m_ref, v_ref, g_out, d_out, m_out, v_out):
        g = _slot_sum(slot_ref, own_ref[...], me_ref[0])
        g_out[...] = g
        d_out[...], m_out[...], v_out[...] = _adamw(w_ref[...], g, m_ref[...], v_ref[...])

    tile = pl.BlockSpec((tr, cols), lambda i, me_ref: (i, 0))
    return _pcall(
        body, name=name, out_shape=[jax.ShapeDtypeStruct((rows, cols), F32)] * 4,
        grid_spec=pltpu.PrefetchScalarGridSpec(
            num_scalar_prefetch=1, grid=(rows // tr,),
            in_specs=[pl.BlockSpec((tr, cols), lambda i, me_ref: own_block(i, me_ref[0])),
                      pl.BlockSpec((N_DEV, tr, cols), lambda i, me_ref: (0, i, 0)), tile, tile, tile],
            out_specs=[tile] * 4),
        compiler_params=_params(("arbitrary",)),
    )(me, partial, slots, w, m, v)


SM_GMIX, SM_GV, SM_GFF, SM_GFIN, SM_LOSS, SM_BGATE, SM_BS, SM_ROWS = 0, 1, 2, 3, 4, 5, 7, 8


def _pack_small(st_ffn, st_mix, st_in, dbg, dbs, conv_rows):
    def body(ffn_ref, mix_ref, in_ref, dbg_ref, dbs_ref, sm_ref, conv_ref):
        sm_ref[SM_GMIX:SM_GMIX + 1, :] = in_ref[0:1, :]
        sm_ref[SM_GV:SM_GV + 1, :] = mix_ref[ST_GV:ST_GV + 1, :]
        sm_ref[SM_GFF:SM_GFF + 1, :] = ffn_ref[ST_GFF:ST_GFF + 1, :]
        sm_ref[SM_GFIN:SM_GFIN + 1, :] = ffn_ref[ST_GFIN:ST_GFIN + 1, :]
        sm_ref[SM_LOSS:SM_LOSS + 1, :] = ffn_ref[ST_LOSS:ST_LOSS + 1, :]
        sm_ref[SM_BGATE:SM_BGATE + 1, :] = dbg_ref[0:1, :D]
        sm_ref[SM_BGATE + 1:SM_BGATE + 2, :] = dbg_ref[0:1, D:]
        for g in range(N_GROUPS):
            sm_ref[SM_BS:SM_BS + 1, g * SG:(g + 1) * SG] = jnp.sum(dbs_ref[g].T, axis=0, keepdims=True)
        conv_ref[...] = jnp.zeros_like(conv_ref)
        for p in range(N_DEV):
            conv_ref[p, 0:conv_rows, :] = mix_ref[ST_CONV:ST_CONV + conv_rows, p * LANE:(p + 1) * LANE]

    return _pcall(
        body, name="pack_small",
        out_shape=(jax.ShapeDtypeStruct((SM_ROWS, D), F32), jax.ShapeDtypeStruct((N_DEV, 8, LANE), F32)),
        in_specs=[_VMEM] * 5, out_specs=[_VMEM] * 2, compiler_params=_params(),
    )(st_ffn, st_mix, st_in, dbg, dbs)


def _small_update(sm_slots, ws_own, ws_slots, conv_slots, params):
    flat = [a for t in params for a in t]

    def body(sm_ref, wso_ref, ws_ref, conv_ref, *refs):
        ins, outs = refs[:len(flat)], refs[len(flat):]
        loss_ref, outs = outs[0], outs[1:]
        _, me = _position()
        sm = _slot_sum(sm_ref)
        loss_ref[...] = (0.5 / D) * jnp.sum(sm[SM_LOSS:SM_LOSS + 1, :], axis=1, keepdims=True)
        grads = [sm[SM_GMIX:SM_GMIX + 1, :], sm[SM_GV:SM_GV + 1, :], sm[SM_GFF:SM_GFF + 1, :],
                 sm[SM_GFIN:SM_GFIN + 1, :], sm[SM_BGATE:SM_BGATE + 2, :], sm[SM_BS:SM_BS + 1, :],
                 _masked_ws(_slot_sum(ws_ref, wso_ref[...], me)), _slot_sum(conv_ref)]
        for n, g in enumerate(grads):
            w_ref, m_ref, v_ref = ins[3 * n:3 * n + 3]
            g_out, d_out, m_out, v_out = outs[4 * n:4 * n + 4]
            g_out[...] = g
            d_out[...], m_out[...], v_out[...] = _adamw(w_ref[...], g, m_ref[...], v_ref[...])

    out_shape = [jax.ShapeDtypeStruct((1, 1), F32)]
    for w, _, _ in params:
        out_shape += [jax.ShapeDtypeStruct(w.shape, F32)] * 4
    return _pcall(
        body, name="small_update", out_shape=out_shape,
        in_specs=[_VMEM] * (4 + len(flat)), out_specs=[_VMEM] * len(out_shape), compiler_params=_params(),
    )(sm_slots, ws_own, ws_slots, conv_slots, *flat)


def kernel(x, norm_mix_g, w_in, b_gate, norm_v_g, w_s, b_s, conv_w, w_proj_a, w_proj_b, w_out, norm_ff_g, w_ff1, w_ff2, norm_final_g, loss_target, m_norm_mix_g, m_w_in, m_b_gate, m_norm_v_g, m_w_s, m_b_s, m_conv_w, m_w_proj_a, m_w_proj_b, m_w_out, m_norm_ff_g, m_w_ff1, m_w_ff2, m_norm_final_g, v_norm_mix_g, v_w_in, v_b_gate, v_norm_v_g, v_w_s, v_b_s, v_conv_w, v_w_proj_a, v_w_proj_b, v_w_out, v_norm_ff_g, v_w_ff1, v_w_ff2, v_norm_final_g):
    T = x.shape[1]
    tt = min(256, T)
    tk = min(4096, T)
    conv_rows = conv_w.shape[1]

    pad8 = lambda a: jnp.pad(a, ((0, 8 - a.shape[0]), (0, 0)))
    xs = x.reshape(T, D)
    tgt = loss_target.reshape(T, D)
    g_mix, g_v, g_ff, g_fin = norm_mix_g, norm_v_g, norm_ff_g, norm_final_g.reshape(1, D)
    ws = w_s[0]
    bs3 = b_s.reshape(N_GROUPS, SG, 1)

    c_in, c_ff1 = w_in.shape[2], w_ff1.shape[2]
    r_ff2, r_p = w_ff2.shape[1], w_proj_a.shape[1]
    lane_view = lambda width: (lambda ref, p: _lane_block(ref, p, width))
    row_view = lambda rows: (lambda ref, p: _row_block(ref, p, rows))
    whole = lambda ref, p: ref
    slots = lambda shape, dtype: lax.empty((N_DEV,) + shape, dtype)

    placed = _place_weights(w_in[0], w_ff1[0], w_ff2[0], w_proj_a[0], w_proj_b[0], w_out[0], pad8(conv_w[0]))
    mixer_views = [row_view(r_p), row_view(r_p), row_view(r_p), lane_view(LANE)]
    ffn_views = [lane_view(c_ff1), row_view(r_ff2)]
    pairs = ((0, 1), (2, 4), (3, 5), (6, 7))
    in_group = lambda refs, ks: _gather_entries(refs[:1], [lane_view(c_in)], tuple(k for k in ks if k))

    def gather_groups(refs):
        return ([in_group(refs, ks) for ks in pairs]
                + [_gather_entries(refs[1:5], mixer_views), _gather_entries(refs[5:], ffn_views)])

    sems, placed, g_token = _start_copies("gather_start", placed, gather_groups)
    _, me = _position()
    W_in = placed[0]
    tp = min(1024, T)
    proj = lax.empty((T, IN_COLS), BF16)
    for n, ks in enumerate(pairs):
        W_in, = _wait_copies(f"gather_wait_in_{n}", [W_in], sems[2 * n], sems[2 * n + 1],
                             lambda refs, ks=ks: in_group(refs, ks), proj if n else g_token)
        blk = jnp.stack([me ^ ks[0], me ^ ks[1]])
        if n == 0:
            proj, h = _in_proj_pair(f"in_proj_{n}", blk, xs, W_in, proj, c_in, tp, g_mix)
        else:
            proj = _in_proj_pair(f"in_proj_{n}", blk, h, W_in, proj, c_in, tp)
    n = len(pairs)
    PA, PB, WO, conv = _wait_copies(
        "gather_wait_mixer", placed[1:5], sems[2 * n], sems[2 * n + 1],
        lambda refs: _gather_entries(refs, mixer_views), proj)
    a, c, m, pa, pb, x1 = _mixer_fwd(xs, proj, b_gate, g_v, ws, bs3, conv, PA, PB, WO, min(512, T))
    W1, W2 = _wait_copies(
        "gather_wait_ffn", placed[5:], sems[2 * n + 2], sems[2 * n + 3],
        lambda refs: _gather_entries(refs, ffn_views), x1)
    hf, s, dpre, dx2, dx1, st_ffn = _ffn_fwd_bwd(x1, tgt, g_ff, g_fin, W1, W2, tt)

    d_ff2 = _weight_grad("dw_ff2", s, dx2, D, tk)
    d_ff1 = _weight_grad("dw_ff1", hf, dpre, D, tk)
    ff_views = [lane_view(c_ff1), row_view(r_ff2)]
    ff_sems, ff_arrays, ff_token = _start_copies(
        "scatter_start_ffn", [d_ff1, d_ff2, slots((D, c_ff1), BF16), slots((r_ff2, D), BF16)],
        lambda refs: [_scatter_entries(ff_views)(refs)])

    dproj, dpa, dpb, dx1b, st_mix, dbg, dws, dbs = _mixer_bwd(
        dx1, proj, pa, pb, b_gate, g_v, ws, bs3, conv, PA, PB, WO, tt, deps=(ff_token,))
    d_o = _weight_grad("dw_out", m, dx1b, D, tk)
    d_pa = _weight_grad("dw_proj_a", a, dpa, D, tk)
    d_pb = _weight_grad("dw_proj_b", c, dpb, D, tk)
    p_views = [row_view(r_p), row_view(r_p), row_view(r_p), whole]
    p_sems, p_arrays, p_token = _start_copies(
        "scatter_start_proj", [d_pa, d_pb, d_o, dws] + [slots((r_p, D), BF16)] * 3 + [slots(dws.shape, F32)],
        lambda refs: [_scatter_entries(p_views)(refs)])

    d_in = _weight_grad("dw_in", h, dproj, D, tk, deps=(p_token,))
    in_views = [lane_view(c_in)]
    in_sems, in_arrays, in_token = _start_copies(
        "scatter_start_in", [d_in, slots((D, c_in), BF16)], lambda refs: [_scatter_entries(in_views)(refs)])
    grad_x, st_in = _in_proj_bwd(dproj, xs, dx1, g_mix, W_in, min(512, T), deps=(in_token,))

    small, d_conv = _pack_small(st_ffn, st_mix, st_in, dbg, dbs, conv_rows)
    r_conv, r_small = _small_exchange(d_conv, small)
    d_ff1, d_ff2, s_ff1, s_ff2 = _wait_copies(
        "scatter_wait_ffn", ff_arrays, ff_sems[0], ff_sems[1], _scatter_entries(ff_views), r_small)
    d_pa, d_pb, d_o, dws, s_pa, s_pb, s_o, s_ws = _wait_copies(
        "scatter_wait_proj", p_arrays, p_sems[0], p_sems[1], _scatter_entries(p_views), r_small)
    d_in, s_in = _wait_copies("scatter_wait_in", in_arrays, in_sems[0], in_sems[1], _scatter_entries(in_views), r_small)

    own_cols = lambda i, me: (i, me)
    own_rows = lambda rows, tr: (lambda i, me: (me * (rows // tr) + i, 0))
    me1 = me.reshape(1)
    big = {
        "w_in": _reduce_adamw("adamw_w_in", me1, d_in, own_cols, s_in, w_in[0], m_w_in[0], v_w_in[0], 128),
        "w_ff1": _reduce_adamw("adamw_w_ff1", me1, d_ff1, own_cols, s_ff1, w_ff1[0], m_w_ff1[0], v_w_ff1[0], 128),
        "w_ff2": _reduce_adamw(
            "adamw_w_ff2", me1, d_ff2, own_rows(r_ff2, 64), s_ff2, w_ff2[0], m_w_ff2[0], v_w_ff2[0], 64),
        "w_proj_a": _reduce_adamw(
            "adamw_w_proj_a", me1, d_pa, own_rows(r_p, 32), s_pa, w_proj_a[0], m_w_proj_a[0], v_w_proj_a[0], 32),
        "w_proj_b": _reduce_adamw(
            "adamw_w_proj_b", me1, d_pb, own_rows(r_p, 32), s_pb, w_proj_b[0], m_w_proj_b[0], v_w_proj_b[0], 32),
        "w_out": _reduce_adamw(
            "adamw_w_out", me1, d_o, own_rows(r_p, 32), s_o, w_out[0], m_w_out[0], v_w_out[0], 32),
    }

    two = lambda a: a.reshape(2, D)
    one = lambda a: a.reshape(1, D)
    small_params = [
        (norm_mix_g, m_norm_mix_g, v_norm_mix_g),
        (norm_v_g, m_norm_v_g, v_norm_v_g),
        (norm_ff_g, m_norm_ff_g, v_norm_ff_g),
        (one(norm_final_g), one(m_norm_final_g), one(v_norm_final_g)),
        (two(b_gate), two(m_b_gate), two(v_b_gate)),
        (one(b_s), one(m_b_s), one(v_b_s)),
        (w_s[0], m_w_s[0], v_w_s[0]),
        (pad8(conv_w[0]), pad8(m_conv_w[0]), pad8(v_conv_w[0])),
    ]
    res = _small_update(r_small, dws, s_ws, r_conv, small_params)
    loss = res[0].reshape(())
    names = ["norm_mix_g", "norm_v_g", "norm_ff_g", "norm_final_g", "b_gate", "b_s", "w_s", "conv_w"]
    shapes = {"norm_mix_g": norm_mix_g.shape, "norm_v_g": norm_v_g.shape, "norm_ff_g": norm_ff_g.shape,
              "norm_final_g": norm_final_g.shape, "b_gate": b_gate.shape, "b_s": b_s.shape, "w_s": w_s.shape}
    out = {}
    for n, name in enumerate(names):
        quad = res[1 + 4 * n:5 + 4 * n]
        if name == "conv_w":
            out[name] = [q[:conv_rows][None] for q in quad]
        else:
            out[name] = [q.reshape(shapes[name]) for q in quad]
    for name, quad in big.items():
        out[name] = [q[None] for q in quad]

    order = ["norm_mix_g", "w_in", "b_gate", "norm_v_g", "w_s", "b_s", "conv_w", "w_proj_a", "w_proj_b", "w_out",
             "norm_ff_g", "w_ff1", "w_ff2", "norm_final_g"]
    grads = [out[n][0] for n in order]
    deltas = [out[n][1] for n in order]
    new_m = [out[n][2] for n in order]
    new_v = [out[n][3] for n in order]
    return (loss, grad_x.reshape(x.shape), *grads, *deltas, *new_m, *new_v)
```

```python
import math

import jax
import jax.numpy as jnp
from jax import lax
from jax.experimental import pallas as pl
from jax.experimental.pallas import tpu as pltpu

F32 = jnp.float32
BF16 = jnp.bfloat16

N_DEV = 8
D = 1024
D_FF = 4096
IN_COLS = 7 * D
SG = 128
N_GROUPS = 8
CHUNK = 64
EPS = 1e-6
HALO = 16
LANE = 128
VMEM_LIMIT = 62 * 1024 * 1024

ADAM_LR = 0.001
ADAM_B1 = 0.9
ADAM_B2 = 0.999
ADAM_EPS = 1e-08
ADAM_WD = 0.01
ADAM_STEP = 10

SQRT_HALF = math.sqrt(0.5)
INV_SQRT_2PI = 1.0 / math.sqrt(2.0 * math.pi)

_REL = [(dx, dy, dc) for dx in (0, 1) for dy in (0, 1) for dc in (0, 1)]

_VMEM = pl.BlockSpec(memory_space=pltpu.VMEM)
_ANY = pl.BlockSpec(memory_space=pl.ANY)


def _pcall(body, **kw):
    return pl.pallas_call(body, **kw)


def _params(sem=None):
    if sem is None:
        return pltpu.CompilerParams(vmem_limit_bytes=VMEM_LIMIT)
    return pltpu.CompilerParams(dimension_semantics=sem, vmem_limit_bytes=VMEM_LIMIT)


def _const_spec(shape):
    nd = len(shape)
    return pl.BlockSpec(shape, lambda *_: (0,) * nd, pipeline_mode=pl.Buffered(1))


def _after(body, n_in, deps):
    def wrapped(*refs):
        return body(*refs[:n_in], *refs[n_in + len(deps):])
    return wrapped


def _mm(a, b):
    return jnp.dot(a, b, preferred_element_type=F32)


def _nt(a, b):
    return lax.dot_general(a, b, (((1,), (1,)), ((), ())), preferred_element_type=F32)


def _tn(a, b):
    return lax.dot_general(a, b, (((0,), (0,)), ((), ())), preferred_element_type=F32)


def _rms(x):
    r = lax.rsqrt(jnp.mean(x * x, axis=-1, keepdims=True) + EPS)
    return x * r, r


def _rms_bwd(dyg, xh, r):
    return r * (dyg - xh * jnp.mean(dyg * xh, axis=-1, keepdims=True))


def _gelu(x):
    cdf = 0.5 * (1.0 + lax.erf(x * SQRT_HALF))
    return x * cdf, cdf


def _gelu_grad(x, cdf):
    return cdf + x * (jnp.exp(-0.5 * x * x) * INV_SQRT_2PI)


def _masked_ws(ws):
    i = lax.broadcasted_iota(jnp.int32, (SG, SG), 0)
    j = lax.broadcasted_iota(jnp.int32, (SG, SG), 1)
    keep = jnp.logical_or(j < CHUNK, i >= CHUNK)
    return jnp.where(keep[None], ws, jnp.zeros_like(ws))


def _shift_down(halo, q, k):
    ext = jnp.concatenate([halo, q], axis=0)
    return pltpu.roll(ext, k, 0)[halo.shape[0]:]


def _shift_up(q, nxt, k):
    ext = jnp.concatenate([q, nxt], axis=0)
    return pltpu.roll(ext, ext.shape[0] - k, 0)[:q.shape[0]]


def _col(k):
    return slice(k * D, (k + 1) * D)


def _position():
    x, y, c = lax.axis_index("x"), lax.axis_index("y"), lax.axis_index("c")
    return (x, y, c), 4 * x + 2 * y + c


def _exchange(items, send_sems, recv_sems, local_sems):
    (x, y, c), me = _position()
    started = []
    for w, (src_of, dst_of) in enumerate(items):
        own = pltpu.make_async_copy(src_of(me), dst_of(me), local_sems.at[w])
        own.start()
        started.append(own)
        for k in range(1, N_DEV):
            dx, dy, dc = _REL[k]
            peer = (1 - x if dx else x, 1 - y if dy else y, 1 - c if dc else c)
            pid = 4 * peer[0] + 2 * peer[1] + peer[2]
            cp = pltpu.make_async_remote_copy(
                src_ref=src_of(pid), dst_ref=dst_of(me),
                send_sem=send_sems.at[w * N_DEV + k], recv_sem=recv_sems.at[w * N_DEV + k],
                device_id=peer, device_id_type=pl.DeviceIdType.MESH)
            cp.start()
            started.append(cp)
    for cp in started:
        cp.wait()


def _lane_block(ref, p, width):
    return ref.at[:, pl.ds(pl.multiple_of(p * width, LANE), width)]


def _row_block(ref, p, rows):
    return ref.at[pl.ds(p * rows, rows), :]


def _small_exchange(d_conv, small):
    def body(dconv, sm, rconv, rsm, send_sems, recv_sems, local_sems):
        items = [
            (lambda p: dconv.at[p], lambda p: rconv.at[p]),
            (lambda p: sm, lambda p: rsm.at[p]),
        ]
        _exchange(items, send_sems, recv_sems, local_sems)

    n_items = 2
    return _pcall(
        body, name="small_exchange",
        out_shape=(jax.ShapeDtypeStruct((N_DEV,) + d_conv.shape[1:], F32),
                   jax.ShapeDtypeStruct((N_DEV,) + small.shape, F32)),
        in_specs=[_ANY] * n_items, out_specs=[_ANY] * n_items,
        scratch_shapes=[pltpu.SemaphoreType.DMA((n_items * N_DEV,)), pltpu.SemaphoreType.DMA((n_items * N_DEV,)),
                        pltpu.SemaphoreType.DMA((n_items,))],
        compiler_params=_params(),
    )(d_conv, small)


_HBM = pl.BlockSpec(memory_space=pltpu.HBM)
_SEM = pl.BlockSpec(memory_space=pltpu.SEMAPHORE)
_EFFECT = pltpu.SideEffectType.DATAFLOW_SIDE_EFFECTING


REL_ORDER = (1, 2, 4, 3, 5, 6, 7)


def _to_peers(src_of, dst_of, order=REL_ORDER):
    return [(src_of, dst_of, k) for k in order]


def _remote_copies(entries, send_sems, recv_sems):
    (x, y, c), me = _position()
    copies = []
    for n, (src_of, dst_of, k) in enumerate(entries):
        dx, dy, dc = _REL[k]
        peer = (1 - x if dx else x, 1 - y if dy else y, 1 - c if dc else c)
        pid = 4 * peer[0] + 2 * peer[1] + peer[2]
        copies.append(pltpu.make_async_remote_copy(
            src_ref=src_of(pid, me), dst_ref=dst_of(me), send_sem=send_sems.at[n], recv_sem=recv_sems.at[n],
            device_id=peer, device_id_type=pl.DeviceIdType.MESH))
    return copies


def _start_copies(name, arrays, make_groups):
    n = len(arrays)
    sizes = [len(g) for g in make_groups([None] * n)]

    def body(*refs):
        sems, token = refs[n:n + 2 * len(sizes)], refs[-1]
        for g, entries in enumerate(make_groups(refs[:n])):
            for cp in _remote_copies(entries, sems[2 * g], sems[2 * g + 1]):
                cp.start()
        token[...] = jnp.zeros_like(token)

    out_shape = []
    for size in sizes:
        out_shape += [pltpu.SemaphoreType.DMA((size,))] * 2
    out_shape += [pltpu.HBM(a.shape, a.dtype) for a in arrays] + [jax.ShapeDtypeStruct((8, LANE), F32)]
    res = _pcall(
        body, name=name, out_shape=out_shape,
        in_specs=[_HBM] * n, out_specs=[_SEM] * (2 * len(sizes)) + [_HBM] * n + [_VMEM],
        input_output_aliases={i: 2 * len(sizes) + i for i in range(n)},
        compiler_params=pltpu.CompilerParams(has_side_effects=_EFFECT),
    )(*[pltpu.with_memory_space_constraint(a, pltpu.HBM) for a in arrays])
    return res[:2 * len(sizes)], res[2 * len(sizes):-1], res[-1]


def _wait_copies(name, arrays, send_sems, recv_sems, make_entries, after):
    n = len(arrays)

    def body(*refs):
        for cp in _remote_copies(make_entries(refs[:n]), refs[n], refs[n + 1]):
            cp.wait_send()
            cp.wait_recv()

    return _pcall(
        body, name=name, out_shape=[pltpu.HBM(a.shape, a.dtype) for a in arrays],
        in_specs=[_HBM] * n + [_SEM, _SEM, _ANY], out_specs=[_HBM] * n,
        input_output_aliases={i: i for i in range(n)},
        compiler_params=pltpu.CompilerParams(has_side_effects=_EFFECT),
    )(*arrays, send_sems, recv_sems, after)


def _place_weights(w_in, w_ff1, w_ff2, w_pa, w_pb, w_o, conv8):
    n_items = 7
    c_in, c_ff1 = w_in.shape[1], w_ff1.shape[1]
    r_ff2, r_p = w_ff2.shape[0], w_pa.shape[0]

    def body(win_ref, w1_ref, w2_ref, pa_ref, pb_ref, wo_ref, cw_ref,
             win_o, pa_o, pb_o, wo_o, cw_o, w1_o, w2_o,
             s_win, s_w1, s_w2, s_pa, s_pb, s_wo, sems):
        _, me = _position()
        for src, stage in ((win_ref, s_win), (w1_ref, s_w1), (w2_ref, s_w2),
                           (pa_ref, s_pa), (pb_ref, s_pb), (wo_ref, s_wo)):
            stage[...] = src[...].astype(BF16)
        pairs = [
            (s_win, _lane_block(win_o, me, c_in)), (s_pa, _row_block(pa_o, me, r_p)),
            (s_pb, _row_block(pb_o, me, r_p)), (s_wo, _row_block(wo_o, me, r_p)),
            (cw_ref, _lane_block(cw_o, me, conv8.shape[1])),
            (s_w1, _lane_block(w1_o, me, c_ff1)), (s_w2, _row_block(w2_o, me, r_ff2)),
        ]
        copies = [pltpu.make_async_copy(s, d, sems.at[n]) for n, (s, d) in enumerate(pairs)]
        for cp in copies:
            cp.start()
        for cp in copies:
            cp.wait()

    out_shape = (
        jax.ShapeDtypeStruct((D, N_DEV * c_in), BF16),
        jax.ShapeDtypeStruct((N_DEV * r_p, D), BF16),
        jax.ShapeDtypeStruct((N_DEV * r_p, D), BF16),
        jax.ShapeDtypeStruct((N_DEV * r_p, D), BF16),
        jax.ShapeDtypeStruct((conv8.shape[0], N_DEV * conv8.shape[1]), F32),
        jax.ShapeDtypeStruct((D, N_DEV * c_ff1), BF16),
        jax.ShapeDtypeStruct((N_DEV * r_ff2, D), BF16),
    )
    return _pcall(
        body, name="place_weights", out_shape=out_shape,
        in_specs=[_VMEM] * n_items, out_specs=[_ANY] * n_items,
        scratch_shapes=[pltpu.VMEM(w.shape, BF16) for w in (w_in, w_ff1, w_ff2, w_pa, w_pb, w_o)]
        + [pltpu.SemaphoreType.DMA((n_items,))],
        compiler_params=_params(),
    )(w_in, w_ff1, w_ff2, w_pa, w_pb, w_o, conv8)


def _gather_entries(refs, views, order=REL_ORDER):
    entries = []
    for r, v in zip(refs, views):
        entries += _to_peers(lambda pid, me, r=r, v=v: v(r, me), lambda me, r=r, v=v: v(r, me), order)
    return entries


def _scatter_entries(views):
    def make(refs):
        srcs, lands = refs[:len(views)], refs[len(views):]
        entries = []
        for r, v, l in zip(srcs, views, lands):
            entries += _to_peers(lambda pid, me, r=r, v=v: v(r, pid), lambda me, l=l: l.at[me])
        return entries
    return make


def _in_proj_pair(name, blk, act, w_in, proj, width, tt, g_mix=None):
    T = act.shape[0]
    nt = T // tt
    normed = g_mix is not None

    def body(blk_ref, act_ref, *refs):
        if normed:
            g_ref, wa_ref, wb_ref, _, proj_out, h_out, w_s, o_s, sems = refs
        else:
            wa_ref, wb_ref, _, proj_out, w_s, o_s, sems = refs
        i = pl.program_id(0)

        @pl.when(i == 0)
        def _():
            w_s[:, :width] = wa_ref[...]
            w_s[:, width:] = wb_ref[...]

        def copies(slot, step):
            rows = pl.ds(step * tt, tt)
            return [pltpu.make_async_copy(
                o_s.at[slot, :, half * width:(half + 1) * width],
                proj_out.at[rows, pl.ds(pl.multiple_of(blk_ref[half] * width, LANE), width)],
                sems.at[slot, half]) for half in range(2)]

        slot = i % 2

        @pl.when(i >= 2)
        def _():
            for cp in copies(slot, i - 2):
                cp.wait()

        if normed:
            xh, _ = _rms(act_ref[...])
            h = (xh * g_ref[...]).astype(BF16)
            h_out[...] = h
        else:
            h = act_ref[...]
        o_s[slot] = _mm(h, w_s[...]).astype(BF16)
        for cp in copies(slot, i):
            cp.start()

        @pl.when(i == nt - 1)
        def _():
            for cp in copies(slot, i) + (copies(1 - slot, i - 1) if nt > 1 else []):
                cp.wait()

    tile = pl.BlockSpec((tt, D), lambda i, b: (i, 0))
    w_spec = lambda half: pl.BlockSpec((D, width), lambda i, b: (0, b[half]), pipeline_mode=pl.Buffered(1))
    extra_in = [_const_spec(g_mix.shape)] if normed else []
    extra_args = [g_mix] if normed else []
    out_shape = [jax.ShapeDtypeStruct(proj.shape, proj.dtype)] + ([jax.ShapeDtypeStruct((T, D), BF16)] if normed else [])
    res = _pcall(
        body, name=name, out_shape=out_shape,
        grid_spec=pltpu.PrefetchScalarGridSpec(
            num_scalar_prefetch=1, grid=(nt,),
            in_specs=[tile] + extra_in + [w_spec(0), w_spec(1), _ANY],
            out_specs=[_ANY] + ([tile] if normed else []),
            scratch_shapes=[pltpu.VMEM((D, 2 * width), BF16), pltpu.VMEM((2, tt, 2 * width), BF16),
                            pltpu.SemaphoreType.DMA((2, 2))]),
        input_output_aliases={4 + len(extra_in): 0},
        compiler_params=_params(("arbitrary",)),
    )(blk, act, *extra_args, w_in, w_in, proj)
    return res if normed else res[0]


def _mixer_fwd(x, proj, b_gate, g_v, w_s, b_s3, conv, w_pa, w_pb, w_o, tt):
    T = x.shape[0]
    nt = T // tt
    nb = tt // SG

    def body(x_ref, proj_ref, bg_ref, gv_ref, ws_ref, bs_ref, cw_ref, pa_w, pb_w, wo_w,
             a_ref, c_ref, m_ref, pa_ref, pb_ref, x1_ref, q_carry, mix_s):
        @pl.when(pl.program_id(0) == 0)
        def _():
            q_carry[...] = jnp.zeros_like(q_carry)

        def proj(k):
            return proj_ref[:, _col(k)].astype(F32)

        vg, _ = _gelu(proj(1))
        vh, _ = _rms(vg)
        vp = (vh * gv_ref[...]).astype(BF16)
        wm = _masked_ws(ws_ref[...]).astype(BF16)
        for n in range(nb):
            rows = slice(n * SG, (n + 1) * SG)
            for g in range(N_GROUPS):
                cols = slice(g * SG, (g + 1) * SG)
                mix_s[rows, cols] = _mm(wm[g], vp[rows, cols]) + bs_ref[g]
        ug, _ = _gelu(proj(0))
        a = (ug * mix_s[...]).astype(BF16)
        a_ref[...] = a
        pa = _mm(a, pa_w[...])
        pa_ref[...] = pa.astype(BF16)
        m = jax.nn.sigmoid(proj(5) + bg_ref[:, :D]) * pa

        bgate = proj(2)
        q = proj(3) * proj(4)
        halo = q_carry[...]
        cv = cw_ref[0:1, :] * _shift_down(halo, q, 2) + cw_ref[1:2, :] * _shift_down(halo, q, 1) + cw_ref[2:3, :] * q
        q_carry[...] = q[tt - q_carry.shape[0]:, :]
        c = (bgate * cv).astype(BF16)
        c_ref[...] = c
        pb = _mm(c, pb_w[...])
        pb_ref[...] = pb.astype(BF16)
        m = (m + jax.nn.sigmoid(proj(6) + bg_ref[:, D:]) * pb).astype(BF16)
        m_ref[...] = m
        x1_ref[...] = x_ref[...] + _mm(m, wo_w[...])

    tile = lambda w: pl.BlockSpec((tt, w), lambda i: (i, 0))
    out_shape = [jax.ShapeDtypeStruct((T, D), BF16)] * 5 + [jax.ShapeDtypeStruct((T, D), F32)]
    return _pcall(
        body, name="mixer_fwd", grid=(nt,), out_shape=out_shape,
        in_specs=[tile(D), tile(IN_COLS), _const_spec(b_gate.shape),
                  _const_spec(g_v.shape), _const_spec(w_s.shape), _const_spec(b_s3.shape), _const_spec(conv.shape),
                  _const_spec(w_pa.shape), _const_spec(w_pb.shape), _const_spec(w_o.shape)],
        out_specs=[tile(D)] * 6,
        scratch_shapes=[pltpu.VMEM((8, D), F32), pltpu.VMEM((tt, D), F32)],
        compiler_params=_params(("arbitrary",)),
    )(x, proj, b_gate, g_v, w_s, b_s3, conv, w_pa, w_pb, w_o)


ST_GFIN, ST_GFF, ST_LOSS = 0, 1, 2


def _ffn_fwd_bwd(x1, tgt, g_ff, g_fin, w1, w2, tt):
    T = x1.shape[0]
    nt = T // tt
    nk = D_FF // D

    def body(x1_ref, tgt_ref, gff_ref, gfin_ref, w1_ref, w2_ref,
             hf_ref, s_ref, dpre_ref, dx2_ref, dx1_ref, st_ref, z_s):
        @pl.when(pl.program_id(0) == 0)
        def _():
            st_ref[...] = jnp.zeros_like(st_ref)

        x1 = x1_ref[...]
        xh1, r1 = _rms(x1)
        hf = (xh1 * gff_ref[...]).astype(BF16)
        hf_ref[...] = hf
        acc = jnp.zeros((tt, D), F32)
        for k in range(nk):
            z = jnp.maximum(_mm(hf, w1_ref[:, _col(k)]), 0.0)
            z_s[:, _col(k)] = z.astype(BF16)
            s = (z * z).astype(BF16)
            s_ref[:, _col(k)] = s
            acc = acc + _mm(s, w2_ref[_col(k), :])
        x2 = x1 + acc
        xh2, r2 = _rms(x2)
        diff = xh2 * gfin_ref[...] - tgt_ref[...]
        st_ref[ST_LOSS:ST_LOSS + 1, :] += jnp.sum(diff * diff, axis=0, keepdims=True)
        dy = diff * (1.0 / D)
        st_ref[ST_GFIN:ST_GFIN + 1, :] += jnp.sum(dy * xh2, axis=0, keepdims=True)
        dx2 = _rms_bwd(dy * gfin_ref[...], xh2, r2)
        dx2b = dx2.astype(BF16)
        dx2_ref[...] = dx2b
        dhf = jnp.zeros((tt, D), F32)
        for k in range(nk):
            dpre = (_nt(dx2b, w2_ref[_col(k), :]) * (2.0 * z_s[:, _col(k)].astype(F32))).astype(BF16)
            dpre_ref[:, _col(k)] = dpre
            dhf = dhf + _nt(dpre, w1_ref[:, _col(k)])
        st_ref[ST_GFF:ST_GFF + 1, :] += jnp.sum(dhf * xh1, axis=0, keepdims=True)
        dx1_ref[...] = dx2 + _rms_bwd(dhf * gff_ref[...], xh1, r1)

    tile = lambda w: pl.BlockSpec((tt, w), lambda i: (i, 0))
    out_shape = (jax.ShapeDtypeStruct((T, D), BF16), jax.ShapeDtypeStruct((T, D_FF), BF16),
                 jax.ShapeDtypeStruct((T, D_FF), BF16), jax.ShapeDtypeStruct((T, D), BF16),
                 jax.ShapeDtypeStruct((T, D), F32), jax.ShapeDtypeStruct((8, D), F32))
    return _pcall(
        body, name="ffn_fwd_bwd", grid=(nt,), out_shape=out_shape,
        in_specs=[tile(D), tile(D), _const_spec(g_ff.shape), _const_spec(g_fin.shape),
                  _const_spec(w1.shape), _const_spec(w2.shape)],
        out_specs=[tile(D), tile(D_FF), tile(D_FF), tile(D), tile(D), pl.BlockSpec((8, D), lambda i: (0, 0))],
        scratch_shapes=[pltpu.VMEM((tt, D_FF), BF16)],
        compiler_params=_params(("arbitrary",)),
    )(x1, tgt, g_ff, g_fin, w1, w2)


ST_GV, ST_CONV = 0, 1


def _mixer_bwd(dx1, proj, pa, pb, b_gate, g_v, w_s, b_s3, conv, w_pa, w_pb, w_o, tt, deps=()):
    T = dx1.shape[0]
    nt = T // tt
    nb = tt // SG
    hb = tt // HALO

    def body(dx1_ref, proj_ref, cgh_ref, xsh_ref, pa_ref, pb_ref,
             bg_ref, gv_ref, ws_ref, bs_ref, cw_ref, pa_w, pb_w, wo_w,
             dproj_ref, dpa_ref, dpb_ref, dx1b_ref, st_ref, dbg_ref, dws_ref, dbs_ref, d_carry, mix_s, dvp_s):
        i = pl.program_id(0)

        @pl.when(i == 0)
        def _():
            st_ref[...] = jnp.zeros_like(st_ref)
            dbg_ref[...] = jnp.zeros_like(dbg_ref)
            dws_ref[...] = jnp.zeros_like(dws_ref)
            dbs_ref[...] = jnp.zeros_like(dbs_ref)
            d_carry[...] = jnp.zeros_like(d_carry)

        def pj(k):
            return proj_ref[:, _col(k)].astype(F32)

        def put(k, val):
            dproj_ref[:, _col(k)] = val.astype(BF16)

        dx1b = dx1_ref[...].astype(BF16)
        dx1b_ref[...] = dx1b
        dm = _nt(dx1b, wo_w[...])
        s_a = jax.nn.sigmoid(pj(5) + bg_ref[:, :D])
        s_b = jax.nn.sigmoid(pj(6) + bg_ref[:, D:])
        dpa = dm * s_a
        dpb = dm * s_b
        dpa_b = dpa.astype(BF16)
        dpb_b = dpb.astype(BF16)
        dpa_ref[...] = dpa_b
        dpb_ref[...] = dpb_b
        dga = dpa * pa_ref[...].astype(F32) * (1.0 - s_a)
        dgb = dpb * pb_ref[...].astype(F32) * (1.0 - s_b)
        dbg_ref[0:1, :D] += jnp.sum(dga, axis=0, keepdims=True)
        dbg_ref[0:1, D:] += jnp.sum(dgb, axis=0, keepdims=True)
        put(5, dga)
        put(6, dgb)
        da = _nt(dpa_b, pa_w[...])
        dc = _nt(dpb_b, pb_w[...])

        v = pj(1)
        vg, v_cdf = _gelu(v)
        vh, rv = _rms(vg)
        vp = (vh * gv_ref[...]).astype(BF16)
        wm = _masked_ws(ws_ref[...]).astype(BF16)
        for n in range(nb):
            rows = slice(n * SG, (n + 1) * SG)
            for g in range(N_GROUPS):
                cols = slice(g * SG, (g + 1) * SG)
                mix_s[rows, cols] = _mm(wm[g], vp[rows, cols]) + bs_ref[g]
        u = pj(0)
        ug, u_cdf = _gelu(u)
        put(0, da * mix_s[...] * _gelu_grad(u, u_cdf))
        dmix = da * ug
        dmix_b = dmix.astype(BF16)
        for n in range(nb):
            rows = slice(n * SG, (n + 1) * SG)
            for g in range(N_GROUPS):
                cols = slice(g * SG, (g + 1) * SG)
                blk = dmix_b[rows, cols]
                dws_ref[g] += _nt(blk, vp[rows, cols])
                dbs_ref[g] += dmix[rows, cols]
                dvp_s[rows, cols] = _tn(wm[g], blk)
        dvp = dvp_s[...]
        st_ref[ST_GV:ST_GV + 1, :] += jnp.sum(dvp * vh, axis=0, keepdims=True)
        put(1, _rms_bwd(dvp * gv_ref[...], vh, rv) * _gelu_grad(v, v_cdf))

        bgate, cg, xs = pj(2), pj(3), pj(4)
        q = cg * xs
        has_prev = (i < nt - 1).astype(F32)
        halo = cgh_ref[...].astype(F32) * xsh_ref[...].astype(F32) * has_prev
        q2 = _shift_down(halo, q, 2)
        q1 = _shift_down(halo, q, 1)
        w0, w1, w2 = cw_ref[0:1, :], cw_ref[1:2, :], cw_ref[2:3, :]
        put(2, dc * (w0 * q2 + w1 * q1 + w2 * q))
        dcv = dc * bgate
        st_ref[ST_CONV:ST_CONV + 1, :] += jnp.sum(dcv * q2, axis=0, keepdims=True)
        st_ref[ST_CONV + 1:ST_CONV + 2, :] += jnp.sum(dcv * q1, axis=0, keepdims=True)
        st_ref[ST_CONV + 2:ST_CONV + 3, :] += jnp.sum(dcv * q, axis=0, keepdims=True)
        nxt = d_carry[...]
        dq = w2 * dcv + w1 * _shift_up(dcv, nxt, 1) + w0 * _shift_up(dcv, nxt, 2)
        d_carry[...] = dcv[:d_carry.shape[0], :]
        put(3, dq * xs)
        put(4, dq * cg)

    rev = lambda i: nt - 1 - i
    tile = lambda w: pl.BlockSpec((tt, w), lambda i: (rev(i), 0))
    halo_spec = lambda k: pl.BlockSpec((HALO, D), lambda i: (jnp.maximum(rev(i) * hb - 1, 0), k))
    res = lambda shape: pl.BlockSpec(shape, lambda i: (0,) * len(shape))
    out_shape = (jax.ShapeDtypeStruct((T, IN_COLS), BF16), jax.ShapeDtypeStruct((T, D), BF16),
                 jax.ShapeDtypeStruct((T, D), BF16), jax.ShapeDtypeStruct((T, D), BF16),
                 jax.ShapeDtypeStruct((8, D), F32), jax.ShapeDtypeStruct((8, 2 * D), F32),
                 jax.ShapeDtypeStruct((N_GROUPS, SG, SG), F32), jax.ShapeDtypeStruct((N_GROUPS, SG, SG), F32))
    return _pcall(
        _after(body, 14, deps), name="mixer_bwd", grid=(nt,), out_shape=out_shape,
        in_specs=[tile(D), tile(IN_COLS), halo_spec(3), halo_spec(4), tile(D), tile(D),
                  _const_spec(b_gate.shape), _const_spec(g_v.shape), _const_spec(w_s.shape),
                  _const_spec(b_s3.shape), _const_spec(conv.shape),
                  _const_spec(w_pa.shape), _const_spec(w_pb.shape), _const_spec(w_o.shape)] + [_ANY] * len(deps),
        out_specs=[tile(IN_COLS), tile(D), tile(D), tile(D), res((8, D)), res((8, 2 * D)),
                   res((N_GROUPS, SG, SG)), res((N_GROUPS, SG, SG))],
        scratch_shapes=[pltpu.VMEM((8, D), F32), pltpu.VMEM((tt, D), F32), pltpu.VMEM((tt, D), F32)],
        compiler_params=_params(("arbitrary",)),
    )(dx1, proj, proj, proj, pa, pb, b_gate, g_v, w_s, b_s3, conv, w_pa, w_pb, w_o, *deps)


def _in_proj_bwd(dproj, x, dx1, g_mix, w_in, tt, deps=()):
    T = x.shape[0]

    def body(dproj_ref, x_ref, dx1_ref, gmix_ref, win_ref, gx_ref, st_ref):
        @pl.when(pl.program_id(0) == 0)
        def _():
            st_ref[...] = jnp.zeros_like(st_ref)

        dh = _nt(dproj_ref[...], win_ref[...])
        xh, r = _rms(x_ref[...])
        st_ref[0:1, :] += jnp.sum(dh * xh, axis=0, keepdims=True)
        gx_ref[...] = dx1_ref[...] + _rms_bwd(dh * gmix_ref[...], xh, r)

    tile = lambda w: pl.BlockSpec((tt, w), lambda i: (i, 0))
    return _pcall(
        _after(body, 5, deps), name="in_proj_bwd", grid=(T // tt,),
        out_shape=(jax.ShapeDtypeStruct((T, D), F32), jax.ShapeDtypeStruct((8, D), F32)),
        in_specs=[tile(IN_COLS), tile(D), tile(D), _const_spec(g_mix.shape), _const_spec(w_in.shape)]
        + [_ANY] * len(deps),
        out_specs=[tile(D), pl.BlockSpec((8, D), lambda i: (0, 0))],
        compiler_params=_params(("arbitrary",)),
    )(dproj, x, dx1, g_mix, w_in, *deps)


def _weight_grad(name, act, dout, bc, tk, deps=()):
    T, n_in = act.shape
    n_out = dout.shape[1]
    nk = T // tk
    bi = min(n_in, D)

    def body(a_ref, d_ref, o_ref, acc):
        k = pl.program_id(2)

        @pl.when(k == 0)
        def _():
            acc[...] = jnp.zeros_like(acc)

        acc[...] += _tn(a_ref[...], d_ref[...])

        @pl.when(k == nk - 1)
        def _():
            o_ref[...] = acc[...].astype(o_ref.dtype)

    return _pcall(
        _after(body, 2, deps), name=name, grid=(n_in // bi, n_out // bc, nk),
        out_shape=jax.ShapeDtypeStruct((n_in, n_out), BF16),
        in_specs=[pl.BlockSpec((tk, bi), lambda i, j, k: (k, i)), pl.BlockSpec((tk, bc), lambda i, j, k: (k, j))]
        + [_ANY] * len(deps),
        out_specs=pl.BlockSpec((bi, bc), lambda i, j, k: (i, j)),
        scratch_shapes=[pltpu.VMEM((bi, bc), F32)],
        compiler_params=_params(("arbitrary", "arbitrary", "arbitrary")),
    )(act, dout, *deps)


def _adamw(w, g, m, v):
    m = ADAM_B1 * m + (1.0 - ADAM_B1) * g
    v = ADAM_B2 * v + (1.0 - ADAM_B2) * (g * g)
    m_hat = m / (1.0 - ADAM_B1 ** ADAM_STEP)
    v_hat = v / (1.0 - ADAM_B2 ** ADAM_STEP)
    delta = -ADAM_LR * (m_hat / (jnp.sqrt(v_hat) + ADAM_EPS) + ADAM_WD * w)
    return delta, m, v


def _slot_sum(ref, own=None, me=None):
    g = None
    for s in range(N_DEV):
        term = ref[s] if own is None else jnp.where(me == s, own, ref[s])
        g = term.astype(F32) if g is None else g + term.astype(F32)
    return g


def _reduce_adamw(name, me, partial, own_block, slots, w, m, v, tr):
    rows, cols = w.shape

    def body(me_ref, own_ref, slot_ref, w_ref, m_ref, v_ref, g_out, d_out, m_out, v_out):
        g = _slot_sum(slot_ref, own_ref[...], me_ref[0])
        g_out[...] = g
        d_out[...], m_out[...], v_out[...] = _adamw(w_ref[...], g, m_ref[...], v_ref[...])

    tile = pl.BlockSpec((tr, cols), lambda i, me_ref: (i, 0))
    return _pcall(
        body, name=name, out_shape=[jax.ShapeDtypeStruct((rows, cols), F32)] * 4,
        grid_spec=pltpu.PrefetchScalarGridSpec(
            num_scalar_prefetch=1, grid=(rows // tr,),
            in_specs=[pl.BlockSpec((tr, cols), lambda i, me_ref: own_block(i, me_ref[0])),
                      pl.BlockSpec((N_DEV, tr, cols), lambda i, me_ref: (0, i, 0)), tile, tile, tile],
            out_specs=[tile] * 4),
        compiler_params=_params(("arbitrary",)),
    )(me, partial, slots, w, m, v)


SM_GMIX, SM_GV, SM_GFF, SM_GFIN, SM_LOSS, SM_BGATE, SM_BS, SM_ROWS = 0, 1, 2, 3, 4, 5, 7, 8


def _pack_small(st_ffn, st_mix, st_in, dbg, dbs, conv_rows):
    def body(ffn_ref, mix_ref, in_ref, dbg_ref, dbs_ref, sm_ref, conv_ref):
        sm_ref[SM_GMIX:SM_GMIX + 1, :] = in_ref[0:1, :]
        sm_ref[SM_GV:SM_GV + 1, :] = mix_ref[ST_GV:ST_GV + 1, :]
        sm_ref[SM_GFF:SM_GFF + 1, :] = ffn_ref[ST_GFF:ST_GFF + 1, :]
        sm_ref[SM_GFIN:SM_GFIN + 1, :] = ffn_ref[ST_GFIN:ST_GFIN + 1, :]
        sm_ref[SM_LOSS:SM_LOSS + 1, :] = ffn_ref[ST_LOSS:ST_LOSS + 1, :]
        sm_ref[SM_BGATE:SM_BGATE + 1, :] = dbg_ref[0:1, :D]
        sm_ref[SM_BGATE + 1:SM_BGATE + 2, :] = dbg_ref[0:1, D:]
        for g in range(N_GROUPS):
            sm_ref[SM_BS:SM_BS + 1, g * SG:(g + 1) * SG] = jnp.sum(dbs_ref[g].T, axis=0, keepdims=True)
        conv_ref[...] = jnp.zeros_like(conv_ref)
        for p in range(N_DEV):
            conv_ref[p, 0:conv_rows, :] = mix_ref[ST_CONV:ST_CONV + conv_rows, p * LANE:(p + 1) * LANE]

    return _pcall(
        body, name="pack_small",
        out_shape=(jax.ShapeDtypeStruct((SM_ROWS, D), F32), jax.ShapeDtypeStruct((N_DEV, 8, LANE), F32)),
        in_specs=[_VMEM] * 5, out_specs=[_VMEM] * 2, compiler_params=_params(),
    )(st_ffn, st_mix, st_in, dbg, dbs)


def _small_update(sm_slots, ws_own, ws_slots, conv_slots, params):
    flat = [a for t in params for a in t]

    def body(sm_ref, wso_ref, ws_ref, conv_ref, *refs):
        ins, outs = refs[:len(flat)], refs[len(flat):]
        loss_ref, outs = outs[0], outs[1:]
        _, me = _position()
        sm = _slot_sum(sm_ref)
        loss_ref[...] = (0.5 / D) * jnp.sum(sm[SM_LOSS:SM_LOSS + 1, :], axis=1, keepdims=True)
        grads = [sm[SM_GMIX:SM_GMIX + 1, :], sm[SM_GV:SM_GV + 1, :], sm[SM_GFF:SM_GFF + 1, :],
                 sm[SM_GFIN:SM_GFIN + 1, :], sm[SM_BGATE:SM_BGATE + 2, :], sm[SM_BS:SM_BS + 1, :],
                 _masked_ws(_slot_sum(ws_ref, wso_ref[...], me)), _slot_sum(conv_ref)]
        for n, g in enumerate(grads):
            w_ref, m_ref, v_ref = ins[3 * n:3 * n + 3]
            g_out, d_out, m_out, v_out = outs[4 * n:4 * n + 4]
            g_out[...] = g
            d_out[...], m_out[...], v_out[...] = _adamw(w_ref[...], g, m_ref[...], v_ref[...])

    out_shape = [jax.ShapeDtypeStruct((1, 1), F32)]
    for w, _, _ in params:
        out_shape += [jax.ShapeDtypeStruct(w.shape, F32)] * 4
    return _pcall(
        body, name="small_update", out_shape=out_shape,
        in_specs=[_VMEM] * (4 + len(flat)), out_specs=[_VMEM] * len(out_shape), compiler_params=_params(),
    )(sm_slots, ws_own, ws_slots, conv_slots, *flat)


def kernel(x, norm_mix_g, w_in, b_gate, norm_v_g, w_s, b_s, conv_w, w_proj_a, w_proj_b, w_out, norm_ff_g, w_ff1, w_ff2, norm_final_g, loss_target, m_norm_mix_g, m_w_in, m_b_gate, m_norm_v_g, m_w_s, m_b_s, m_conv_w, m_w_proj_a, m_w_proj_b, m_w_out, m_norm_ff_g, m_w_ff1, m_w_ff2, m_norm_final_g, v_norm_mix_g, v_w_in, v_b_gate, v_norm_v_g, v_w_s, v_b_s, v_conv_w, v_w_proj_a, v_w_proj_b, v_w_out, v_norm_ff_g, v_w_ff1, v_w_ff2, v_norm_final_g):
    T = x.shape[1]
    tt = min(256, T)
    tk = min(4096, T)
    conv_rows = conv_w.shape[1]

    pad8 = lambda a: jnp.pad(a, ((0, 8 - a.shape[0]), (0, 0)))
    xs = x.reshape(T, D)
    tgt = loss_target.reshape(T, D)
    g_mix, g_v, g_ff, g_fin = norm_mix_g, norm_v_g, norm_ff_g, norm_final_g.reshape(1, D)
    ws = w_s[0]
    bs3 = b_s.reshape(N_GROUPS, SG, 1)

    c_in, c_ff1 = w_in.shape[2], w_ff1.shape[2]
    r_ff2, r_p = w_ff2.shape[1], w_proj_a.shape[1]
    lane_view = lambda width: (lambda ref, p: _lane_block(ref, p, width))
    row_view = lambda rows: (lambda ref, p: _row_block(ref, p, rows))
    whole = lambda ref, p: ref
    slots = lambda shape, dtype: lax.empty((N_DEV,) + shape, dtype)

    placed = _place_weights(w_in[0], w_ff1[0], w_ff2[0], w_proj_a[0], w_proj_b[0], w_out[0], pad8(conv_w[0]))
    mixer_views = [row_view(r_p), row_view(r_p), row_view(r_p), lane_view(LANE)]
    ffn_views = [lane_view(c_ff1), row_view(r_ff2)]
    pairs = ((0, 1), (2, 4), (3, 5), (6, 7))
    in_group = lambda refs, ks: _gather_entries(refs[:1], [lane_view(c_in)], tuple(k for k in ks if k))

    def gather_groups(refs):
        return ([in_group(refs, ks) for ks in pairs]
                + [_gather_entries(refs[1:5], mixer_views), _gather_entries(refs[5:], ffn_views)])

    sems, placed, g_token = _start_copies("gather_start", placed, gather_groups)
    _, me = _position()
    W_in = placed[0]
    tp = min(2048, T)
    proj = lax.empty((T, IN_COLS), BF16)
    for n, ks in enumerate(pairs):
        W_in, = _wait_copies(f"gather_wait_in_{n}", [W_in], sems[2 * n], sems[2 * n + 1],
                             lambda refs, ks=ks: in_group(refs, ks), proj if n else g_token)
        blk = jnp.stack([me ^ ks[0], me ^ ks[1]])
        if n == 0:
            proj, h = _in_proj_pair(f"in_proj_{n}", blk, xs, W_in, proj, c_in, tp, g_mix)
        else:
            proj = _in_proj_pair(f"in_proj_{n}", blk, h, W_in, proj, c_in, tp)
    n = len(pairs)
    PA, PB, WO, conv = _wait_copies(
        "gather_wait_mixer", placed[1:5], sems[2 * n], sems[2 * n + 1],
        lambda refs: _gather_entries(refs, mixer_views), proj)
    a, c, m, pa, pb, x1 = _mixer_fwd(xs, proj, b_gate, g_v, ws, bs3, conv, PA, PB, WO, min(512, T))
    W1, W2 = _wait_copies(
        "gather_wait_ffn", placed[5:], sems[2 * n + 2], sems[2 * n + 3],
        lambda refs: _gather_entries(refs, ffn_views), x1)
    hf, s, dpre, dx2, dx1, st_ffn = _ffn_fwd_bwd(x1, tgt, g_ff, g_fin, W1, W2, min(512, T))

    d_ff2 = _weight_grad("dw_ff2", s, dx2, D, tk)
    d_ff1 = _weight_grad("dw_ff1", hf, dpre, D, tk)
    ff_views = [lane_view(c_ff1), row_view(r_ff2)]
    ff_sems, ff_arrays, ff_token = _start_copies(
        "scatter_start_ffn", [d_ff1, d_ff2, slots((D, c_ff1), BF16), slots((r_ff2, D), BF16)],
        lambda refs: [_scatter_entries(ff_views)(refs)])

    dproj, dpa, dpb, dx1b, st_mix, dbg, dws, dbs = _mixer_bwd(
        dx1, proj, pa, pb, b_gate, g_v, ws, bs3, conv, PA, PB, WO, tt, deps=(ff_token,))
    d_o = _weight_grad("dw_out", m, dx1b, D, tk)
    d_pa = _weight_grad("dw_proj_a", a, dpa, D, tk)
    d_pb = _weight_grad("dw_proj_b", c, dpb, D, tk)
    p_views = [row_view(r_p), row_view(r_p), row_view(r_p), whole]
    p_sems, p_arrays, p_token = _start_copies(
        "scatter_start_proj", [d_pa, d_pb, d_o, dws] + [slots((r_p, D), BF16)] * 3 + [slots(dws.shape, F32)],
        lambda refs: [_scatter_entries(p_views)(refs)])

    d_in = _weight_grad("dw_in", h, dproj, D, tk, deps=(p_token,))
    in_views = [lane_view(c_in)]
    in_sems, in_arrays, in_token = _start_copies(
        "scatter_start_in", [d_in, slots((D, c_in), BF16)], lambda refs: [_scatter_entries(in_views)(refs)])
    grad_x, st_in = _in_proj_bwd(dproj, xs, dx1, g_mix, W_in, min(512, T), deps=(in_token,))

    small, d_conv = _pack_small(st_ffn, st_mix, st_in, dbg, dbs, conv_rows)
    r_conv, r_small = _small_exchange(d_conv, small)
    d_ff1, d_ff2, s_ff1, s_ff2 = _wait_copies(
        "scatter_wait_ffn", ff_arrays, ff_sems[0], ff_sems[1], _scatter_entries(ff_views), r_small)
    d_pa, d_pb, d_o, dws, s_pa, s_pb, s_o, s_ws = _wait_copies(
        "scatter_wait_proj", p_arrays, p_sems[0], p_sems[1], _scatter_entries(p_views), r_small)
    d_in, s_in = _wait_copies("scatter_wait_in", in_arrays, in_sems[0], in_sems[1], _scatter_entries(in_views), r_small)

    own_cols = lambda i, me: (i, me)
    own_rows = lambda rows, tr: (lambda i, me: (me * (rows // tr) + i, 0))
    me1 = me.reshape(1)
    big = {
        "w_in": _reduce_adamw("adamw_w_in", me1, d_in, own_cols, s_in, w_in[0], m_w_in[0], v_w_in[0], 128),
        "w_ff1": _reduce_adamw("adamw_w_ff1", me1, d_ff1, own_cols, s_ff1, w_ff1[0], m_w_ff1[0], v_w_ff1[0], 128),
        "w_ff2": _reduce_adamw(
            "adamw_w_ff2", me1, d_ff2, own_rows(r_ff2, 64), s_ff2, w_ff2[0], m_w_ff2[0], v_w_ff2[0], 64),
        "w_proj_a": _reduce_adamw(
            "adamw_w_proj_a", me1, d_pa, own_rows(r_p, 32), s_pa, w_proj_a[0], m_w_proj_a[0], v_w_proj_a[0], 32),
        "w_proj_b": _reduce_adamw(
            "adamw_w_proj_b", me1, d_pb, own_rows(r_p, 32), s_pb, w_proj_b[0], m_w_proj_b[0], v_w_proj_b[0], 32),
        "w_out": _reduce_adamw(
            "adamw_w_out", me1, d_o, own_rows(r_p, 32), s_o, w_out[0], m_w_out[0], v_w_out[0], 32),
    }

    two = lambda a: a.reshape(2, D)
    one = lambda a: a.reshape(1, D)
    small_params = [
        (norm_mix_g, m_norm_mix_g, v_norm_mix_g),
        (norm_v_g, m_norm_v_g, v_norm_v_g),
        (norm_ff_g, m_norm_ff_g, v_norm_ff_g),
        (one(norm_final_g), one(m_norm_final_g), one(v_norm_final_g)),
        (two(b_gate), two(m_b_gate), two(v_b_gate)),
        (one(b_s), one(m_b_s), one(v_b_s)),
        (w_s[0], m_w_s[0], v_w_s[0]),
        (pad8(conv_w[0]), pad8(m_conv_w[0]), pad8(v_conv_w[0])),
    ]
    res = _small_update(r_small, dws, s_ws, r_conv, small_params)
    loss = res[0].reshape(())
    names = ["norm_mix_g", "norm_v_g", "norm_ff_g", "norm_final_g", "b_gate", "b_s", "w_s", "conv_w"]
    shapes = {"norm_mix_g": norm_mix_g.shape, "norm_v_g": norm_v_g.shape, "norm_ff_g": norm_ff_g.shape,
              "norm_final_g": norm_final_g.shape, "b_gate": b_gate.shape, "b_s": b_s.shape, "w_s": w_s.shape}
    out = {}
    for n, name in enumerate(names):
        quad = res[1 + 4 * n:5 + 4 * n]
        if name == "conv_w":
            out[name] = [q[:conv_rows][None] for q in quad]
        else:
            out[name] = [q.reshape(shapes[name]) for q in quad]
    for name, quad in big.items():
        out[name] = [q[None] for q in quad]

    order = ["norm_mix_g", "w_in", "b_gate", "norm_v_g", "w_s", "b_s", "conv_w", "w_proj_a", "w_proj_b", "w_out",
             "norm_ff_g", "w_ff1", "w_ff2", "norm_final_g"]
    grads = [out[n][0] for n in order]
    deltas = [out[n][1] for n in order]
    new_m = [out[n][2] for n in order]
    new_v = [out[n][3] for n in order]
    return (loss, grad_x.reshape(x.shape), *grads, *deltas, *new_m, *new_v)
```

```python
import math

import jax
import jax.numpy as jnp
from jax import lax
from jax.experimental import pallas as pl
from jax.experimental.pallas import tpu as pltpu

F32 = jnp.float32
BF16 = jnp.bfloat16

N_DEV = 8
D = 1024
D_FF = 4096
IN_COLS = 7 * D
SG = 128
N_GROUPS = 8
CHUNK = 64
EPS = 1e-6
HALO = 16
LANE = 128
VMEM_LIMIT = 62 * 1024 * 1024

ADAM_LR = 0.001
ADAM_B1 = 0.9
ADAM_B2 = 0.999
ADAM_EPS = 1e-08
ADAM_WD = 0.01
ADAM_STEP = 10

SQRT_HALF = math.sqrt(0.5)
INV_SQRT_2PI = 1.0 / math.sqrt(2.0 * math.pi)

_REL = [(dx, dy, dc) for dx in (0, 1) for dy in (0, 1) for dc in (0, 1)]

_VMEM = pl.BlockSpec(memory_space=pltpu.VMEM)
_ANY = pl.BlockSpec(memory_space=pl.ANY)


def _pcall(body, **kw):
    return pl.pallas_call(body, **kw)


def _params(sem=None):
    if sem is None:
        return pltpu.CompilerParams(vmem_limit_bytes=VMEM_LIMIT)
    return pltpu.CompilerParams(dimension_semantics=sem, vmem_limit_bytes=VMEM_LIMIT)


def _const_spec(shape):
    nd = len(shape)
    return pl.BlockSpec(shape, lambda *_: (0,) * nd, pipeline_mode=pl.Buffered(1))


def _after(body, n_in, deps):
    def wrapped(*refs):
        return body(*refs[:n_in], *refs[n_in + len(deps):])
    return wrapped


def _mm(a, b):
    return jnp.dot(a, b, preferred_element_type=F32)


def _nt(a, b):
    return lax.dot_general(a, b, (((1,), (1,)), ((), ())), preferred_element_type=F32)


def _tn(a, b):
    return lax.dot_general(a, b, (((0,), (0,)), ((), ())), preferred_element_type=F32)


def _rms(x):
    r = lax.rsqrt(jnp.mean(x * x, axis=-1, keepdims=True) + EPS)
    return x * r, r


def _rms_bwd(dyg, xh, r):
    return r * (dyg - xh * jnp.mean(dyg * xh, axis=-1, keepdims=True))


def _gelu(x):
    cdf = 0.5 * (1.0 + lax.erf(x * SQRT_HALF))
    return x * cdf, cdf


def _gelu_grad(x, cdf):
    return cdf + x * (jnp.exp(-0.5 * x * x) * INV_SQRT_2PI)


def _masked_ws(ws):
    i = lax.broadcasted_iota(jnp.int32, (SG, SG), 0)
    j = lax.broadcasted_iota(jnp.int32, (SG, SG), 1)
    keep = jnp.logical_or(j < CHUNK, i >= CHUNK)
    return jnp.where(keep[None], ws, jnp.zeros_like(ws))


def _shift_down(halo, q, k):
    ext = jnp.concatenate([halo, q], axis=0)
    return pltpu.roll(ext, k, 0)[halo.shape[0]:]


def _shift_up(q, nxt, k):
    ext = jnp.concatenate([q, nxt], axis=0)
    return pltpu.roll(ext, ext.shape[0] - k, 0)[:q.shape[0]]


def _col(k):
    return slice(k * D, (k + 1) * D)


def _position():
    x, y, c = lax.axis_index("x"), lax.axis_index("y"), lax.axis_index("c")
    return (x, y, c), 4 * x + 2 * y + c


def _exchange(items, send_sems, recv_sems, local_sems):
    (x, y, c), me = _position()
    started = []
    for w, (src_of, dst_of) in enumerate(items):
        own = pltpu.make_async_copy(src_of(me), dst_of(me), local_sems.at[w])
        own.start()
        started.append(own)
        for k in range(1, N_DEV):
            dx, dy, dc = _REL[k]
            peer = (1 - x if dx else x, 1 - y if dy else y, 1 - c if dc else c)
            pid = 4 * peer[0] + 2 * peer[1] + peer[2]
            cp = pltpu.make_async_remote_copy(
                src_ref=src_of(pid), dst_ref=dst_of(me),
                send_sem=send_sems.at[w * N_DEV + k], recv_sem=recv_sems.at[w * N_DEV + k],
                device_id=peer, device_id_type=pl.DeviceIdType.MESH)
            cp.start()
            started.append(cp)
    for cp in started:
        cp.wait()


def _lane_block(ref, p, width):
    return ref.at[:, pl.ds(pl.multiple_of(p * width, LANE), width)]


def _row_block(ref, p, rows):
    return ref.at[pl.ds(p * rows, rows), :]


def _small_exchange(d_conv, small):
    def body(dconv, sm, rconv, rsm, send_sems, recv_sems, local_sems):
        items = [
            (lambda p: dconv.at[p], lambda p: rconv.at[p]),
            (lambda p: sm, lambda p: rsm.at[p]),
        ]
        _exchange(items, send_sems, recv_sems, local_sems)

    n_items = 2
    return _pcall(
        body, name="small_exchange",
        out_shape=(jax.ShapeDtypeStruct((N_DEV,) + d_conv.shape[1:], F32),
                   jax.ShapeDtypeStruct((N_DEV,) + small.shape, F32)),
        in_specs=[_ANY] * n_items, out_specs=[_ANY] * n_items,
        scratch_shapes=[pltpu.SemaphoreType.DMA((n_items * N_DEV,)), pltpu.SemaphoreType.DMA((n_items * N_DEV,)),
                        pltpu.SemaphoreType.DMA((n_items,))],
        compiler_params=_params(),
    )(d_conv, small)


_HBM = pl.BlockSpec(memory_space=pltpu.HBM)
_SEM = pl.BlockSpec(memory_space=pltpu.SEMAPHORE)
_EFFECT = pltpu.SideEffectType.DATAFLOW_SIDE_EFFECTING


REL_ORDER = (1, 2, 4, 3, 5, 6, 7)


def _to_peers(src_of, dst_of, order=REL_ORDER):
    return [(src_of, dst_of, k) for k in order]


def _remote_copies(entries, send_sems, recv_sems):
    (x, y, c), me = _position()
    copies = []
    for n, (src_of, dst_of, k) in enumerate(entries):
        dx, dy, dc = _REL[k]
        peer = (1 - x if dx else x, 1 - y if dy else y, 1 - c if dc else c)
        pid = 4 * peer[0] + 2 * peer[1] + peer[2]
        copies.append(pltpu.make_async_remote_copy(
            src_ref=src_of(pid, me), dst_ref=dst_of(me), send_sem=send_sems.at[n], recv_sem=recv_sems.at[n],
            device_id=peer, device_id_type=pl.DeviceIdType.MESH))
    return copies


def _start_copies(name, arrays, make_groups):
    n = len(arrays)
    sizes = [len(g) for g in make_groups([None] * n)]

    def body(*refs):
        sems, token = refs[n:n + 2 * len(sizes)], refs[-1]
        for g, entries in enumerate(make_groups(refs[:n])):
            for cp in _remote_copies(entries, sems[2 * g], sems[2 * g + 1]):
                cp.start()
        token[...] = jnp.zeros_like(token)

    out_shape = []
    for size in sizes:
        out_shape += [pltpu.SemaphoreType.DMA((size,))] * 2
    out_shape += [pltpu.HBM(a.shape, a.dtype) for a in arrays] + [jax.ShapeDtypeStruct((8, LANE), F32)]
    res = _pcall(
        body, name=name, out_shape=out_shape,
        in_specs=[_HBM] * n, out_specs=[_SEM] * (2 * len(sizes)) + [_HBM] * n + [_VMEM],
        input_output_aliases={i: 2 * len(sizes) + i for i in range(n)},
        compiler_params=pltpu.CompilerParams(has_side_effects=_EFFECT),
    )(*[pltpu.with_memory_space_constraint(a, pltpu.HBM) for a in arrays])
    return res[:2 * len(sizes)], res[2 * len(sizes):-1], res[-1]


def _wait_copies(name, arrays, send_sems, recv_sems, make_entries, after):
    n = len(arrays)

    def body(*refs):
        for cp in _remote_copies(make_entries(refs[:n]), refs[n], refs[n + 1]):
            cp.wait_send()
            cp.wait_recv()

    return _pcall(
        body, name=name, out_shape=[pltpu.HBM(a.shape, a.dtype) for a in arrays],
        in_specs=[_HBM] * n + [_SEM, _SEM, _ANY], out_specs=[_HBM] * n,
        input_output_aliases={i: i for i in range(n)},
        compiler_params=pltpu.CompilerParams(has_side_effects=_EFFECT),
    )(*arrays, send_sems, recv_sems, after)


def _place_weights(w_in, w_ff1, w_ff2, w_pa, w_pb, w_o, conv8):
    n_items = 7
    c_in, c_ff1 = w_in.shape[1], w_ff1.shape[1]
    r_ff2, r_p = w_ff2.shape[0], w_pa.shape[0]

    def body(win_ref, w1_ref, w2_ref, pa_ref, pb_ref, wo_ref, cw_ref,
             win_o, pa_o, pb_o, wo_o, cw_o, w1_o, w2_o,
             s_win, s_w1, s_w2, s_pa, s_pb, s_wo, sems):
        _, me = _position()
        for src, stage in ((win_ref, s_win), (w1_ref, s_w1), (w2_ref, s_w2),
                           (pa_ref, s_pa), (pb_ref, s_pb), (wo_ref, s_wo)):
            stage[...] = src[...].astype(BF16)
        pairs = [
            (s_win, _lane_block(win_o, me, c_in)), (s_pa, _row_block(pa_o, me, r_p)),
            (s_pb, _row_block(pb_o, me, r_p)), (s_wo, _row_block(wo_o, me, r_p)),
            (cw_ref, _lane_block(cw_o, me, conv8.shape[1])),
            (s_w1, _lane_block(w1_o, me, c_ff1)), (s_w2, _row_block(w2_o, me, r_ff2)),
        ]
        copies = [pltpu.make_async_copy(s, d, sems.at[n]) for n, (s, d) in enumerate(pairs)]
        for cp in copies:
            cp.start()
        for cp in copies:
            cp.wait()

    out_shape = (
        jax.ShapeDtypeStruct((D, N_DEV * c_in), BF16),
        jax.ShapeDtypeStruct((N_DEV * r_p, D), BF16),
        jax.ShapeDtypeStruct((N_DEV * r_p, D), BF16),
        jax.ShapeDtypeStruct((N_DEV * r_p, D), BF16),
        jax.ShapeDtypeStruct((conv8.shape[0], N_DEV * conv8.shape[1]), F32),
        jax.ShapeDtypeStruct((D, N_DEV * c_ff1), BF16),
        jax.ShapeDtypeStruct((N_DEV * r_ff2, D), BF16),
    )
    return _pcall(
        body, name="place_weights", out_shape=out_shape,
        in_specs=[_VMEM] * n_items, out_specs=[_ANY] * n_items,
        scratch_shapes=[pltpu.VMEM(w.shape, BF16) for w in (w_in, w_ff1, w_ff2, w_pa, w_pb, w_o)]
        + [pltpu.SemaphoreType.DMA((n_items,))],
        compiler_params=_params(),
    )(w_in, w_ff1, w_ff2, w_pa, w_pb, w_o, conv8)


def _gather_entries(refs, views, order=REL_ORDER):
    entries = []
    for r, v in zip(refs, views):
        entries += _to_peers(lambda pid, me, r=r, v=v: v(r, me), lambda me, r=r, v=v: v(r, me), order)
    return entries


def _scatter_entries(views):
    def make(refs):
        srcs, lands = refs[:len(views)], refs[len(views):]
        entries = []
        for r, v, l in zip(srcs, views, lands):
            entries += _to_peers(lambda pid, me, r=r, v=v: v(r, pid), lambda me, l=l: l.at[me])
        return entries
    return make


def _in_proj_pair(name, blk, act, w_in, proj, width, tt, g_mix=None):
    T = act.shape[0]
    nt = T // tt
    normed = g_mix is not None

    def body(blk_ref, act_ref, *refs):
        if normed:
            g_ref, wa_ref, wb_ref, _, proj_out, h_out, w_s, o_s, sems = refs
        else:
            wa_ref, wb_ref, _, proj_out, w_s, o_s, sems = refs
        i = pl.program_id(0)

        @pl.when(i == 0)
        def _():
            w_s[:, :width] = wa_ref[...]
            w_s[:, width:] = wb_ref[...]

        def copies(slot, step):
            rows = pl.ds(step * tt, tt)
            return [pltpu.make_async_copy(
                o_s.at[slot, :, half * width:(half + 1) * width],
                proj_out.at[rows, pl.ds(pl.multiple_of(blk_ref[half] * width, LANE), width)],
                sems.at[slot, half]) for half in range(2)]

        slot = i % 2

        @pl.when(i >= 2)
        def _():
            for cp in copies(slot, i - 2):
                cp.wait()

        if normed:
            xh, _ = _rms(act_ref[...])
            h = (xh * g_ref[...]).astype(BF16)
            h_out[...] = h
        else:
            h = act_ref[...]
        o_s[slot] = _mm(h, w_s[...]).astype(BF16)
        for cp in copies(slot, i):
            cp.start()

        @pl.when(i == nt - 1)
        def _():
            for cp in copies(slot, i) + (copies(1 - slot, i - 1) if nt > 1 else []):
                cp.wait()

    tile = pl.BlockSpec((tt, D), lambda i, b: (i, 0))
    w_spec = lambda half: pl.BlockSpec((D, width), lambda i, b: (0, b[half]), pipeline_mode=pl.Buffered(1))
    extra_in = [_const_spec(g_mix.shape)] if normed else []
    extra_args = [g_mix] if normed else []
    out_shape = [jax.ShapeDtypeStruct(proj.shape, proj.dtype)] + ([jax.ShapeDtypeStruct((T, D), BF16)] if normed else [])
    res = _pcall(
        body, name=name, out_shape=out_shape,
        grid_spec=pltpu.PrefetchScalarGridSpec(
            num_scalar_prefetch=1, grid=(nt,),
            in_specs=[tile] + extra_in + [w_spec(0), w_spec(1), _ANY],
            out_specs=[_ANY] + ([tile] if normed else []),
            scratch_shapes=[pltpu.VMEM((D, 2 * width), BF16), pltpu.VMEM((2, tt, 2 * width), BF16),
                            pltpu.SemaphoreType.DMA((2, 2))]),
        input_output_aliases={4 + len(extra_in): 0},
        compiler_params=_params(("arbitrary",)),
    )(blk, act, *extra_args, w_in, w_in, proj)
    return res if normed else res[0]


def _mixer_fwd(x, proj, b_gate, g_v, w_s, b_s3, conv, w_pa, w_pb, w_o, tt):
    T = x.shape[0]
    nt = T // tt
    nb = tt // SG

    def body(x_ref, proj_ref, bg_ref, gv_ref, ws_ref, bs_ref, cw_ref, pa_w, pb_w, wo_w,
             a_ref, c_ref, m_ref, x1_ref, q_carry, mix_s):
        @pl.when(pl.program_id(0) == 0)
        def _():
            q_carry[...] = jnp.zeros_like(q_carry)

        def proj(k):
            return proj_ref[:, _col(k)].astype(F32)

        vg, _ = _gelu(proj(1))
        vh, _ = _rms(vg)
        vp = (vh * gv_ref[...]).astype(BF16)
        wm = _masked_ws(ws_ref[...]).astype(BF16)
        for n in range(nb):
            rows = slice(n * SG, (n + 1) * SG)
            for g in range(N_GROUPS):
                cols = slice(g * SG, (g + 1) * SG)
                mix_s[rows, cols] = _mm(wm[g], vp[rows, cols]) + bs_ref[g]
        ug, _ = _gelu(proj(0))
        a = (ug * mix_s[...]).astype(BF16)
        a_ref[...] = a
        m = jax.nn.sigmoid(proj(5) + bg_ref[:, :D]) * _mm(a, pa_w[...])

        bgate = proj(2)
        q = proj(3) * proj(4)
        halo = q_carry[...]
        cv = cw_ref[0:1, :] * _shift_down(halo, q, 2) + cw_ref[1:2, :] * _shift_down(halo, q, 1) + cw_ref[2:3, :] * q
        q_carry[...] = q[tt - q_carry.shape[0]:, :]
        c = (bgate * cv).astype(BF16)
        c_ref[...] = c
        m = (m + jax.nn.sigmoid(proj(6) + bg_ref[:, D:]) * _mm(c, pb_w[...])).astype(BF16)
        m_ref[...] = m
        x1_ref[...] = x_ref[...] + _mm(m, wo_w[...])

    tile = lambda w: pl.BlockSpec((tt, w), lambda i: (i, 0))
    out_shape = [jax.ShapeDtypeStruct((T, D), BF16)] * 3 + [jax.ShapeDtypeStruct((T, D), F32)]
    return _pcall(
        body, name="mixer_fwd", grid=(nt,), out_shape=out_shape,
        in_specs=[tile(D), tile(IN_COLS), _const_spec(b_gate.shape),
                  _const_spec(g_v.shape), _const_spec(w_s.shape), _const_spec(b_s3.shape), _const_spec(conv.shape),
                  _const_spec(w_pa.shape), _const_spec(w_pb.shape), _const_spec(w_o.shape)],
        out_specs=[tile(D)] * 4,
        scratch_shapes=[pltpu.VMEM((8, D), F32), pltpu.VMEM((tt, D), F32)],
        compiler_params=_params(("arbitrary",)),
    )(x, proj, b_gate, g_v, w_s, b_s3, conv, w_pa, w_pb, w_o)


ST_GFIN, ST_GFF, ST_LOSS = 0, 1, 2


def _ffn_fwd_bwd(x1, tgt, g_ff, g_fin, w1, w2, tt):
    T = x1.shape[0]
    nt = T // tt
    nk = D_FF // D

    def body(x1_ref, tgt_ref, gff_ref, gfin_ref, w1_ref, w2_ref,
             hf_ref, s_ref, dpre_ref, dx2_ref, dx1_ref, st_ref, z_s):
        @pl.when(pl.program_id(0) == 0)
        def _():
            st_ref[...] = jnp.zeros_like(st_ref)

        x1 = x1_ref[...]
        xh1, r1 = _rms(x1)
        hf = (xh1 * gff_ref[...]).astype(BF16)
        hf_ref[...] = hf
        acc = jnp.zeros((tt, D), F32)
        for k in range(nk):
            z = jnp.maximum(_mm(hf, w1_ref[:, _col(k)]), 0.0)
            z_s[:, _col(k)] = z.astype(BF16)
            s = (z * z).astype(BF16)
            s_ref[:, _col(k)] = s
            acc = acc + _mm(s, w2_ref[_col(k), :])
        x2 = x1 + acc
        xh2, r2 = _rms(x2)
        diff = xh2 * gfin_ref[...] - tgt_ref[...]
        st_ref[ST_LOSS:ST_LOSS + 1, :] += jnp.sum(diff * diff, axis=0, keepdims=True)
        dy = diff * (1.0 / D)
        st_ref[ST_GFIN:ST_GFIN + 1, :] += jnp.sum(dy * xh2, axis=0, keepdims=True)
        dx2 = _rms_bwd(dy * gfin_ref[...], xh2, r2)
        dx2b = dx2.astype(BF16)
        dx2_ref[...] = dx2b
        dhf = jnp.zeros((tt, D), F32)
        for k in range(nk):
            dpre = (_nt(dx2b, w2_ref[_col(k), :]) * (2.0 * z_s[:, _col(k)].astype(F32))).astype(BF16)
            dpre_ref[:, _col(k)] = dpre
            dhf = dhf + _nt(dpre, w1_ref[:, _col(k)])
        st_ref[ST_GFF:ST_GFF + 1, :] += jnp.sum(dhf * xh1, axis=0, keepdims=True)
        dx1_ref[...] = dx2 + _rms_bwd(dhf * gff_ref[...], xh1, r1)

    tile = lambda w: pl.BlockSpec((tt, w), lambda i: (i, 0))
    out_shape = (jax.ShapeDtypeStruct((T, D), BF16), jax.ShapeDtypeStruct((T, D_FF), BF16),
                 jax.ShapeDtypeStruct((T, D_FF), BF16), jax.ShapeDtypeStruct((T, D), BF16),
                 jax.ShapeDtypeStruct((T, D), F32), jax.ShapeDtypeStruct((8, D), F32))
    return _pcall(
        body, name="ffn_fwd_bwd", grid=(nt,), out_shape=out_shape,
        in_specs=[tile(D), tile(D), _const_spec(g_ff.shape), _const_spec(g_fin.shape),
                  _const_spec(w1.shape), _const_spec(w2.shape)],
        out_specs=[tile(D), tile(D_FF), tile(D_FF), tile(D), tile(D), pl.BlockSpec((8, D), lambda i: (0, 0))],
        scratch_shapes=[pltpu.VMEM((tt, D_FF), BF16)],
        compiler_params=_params(("arbitrary",)),
    )(x1, tgt, g_ff, g_fin, w1, w2)


ST_GV, ST_CONV = 0, 1


def _mixer_bwd(dx1, proj, a, c, b_gate, g_v, w_s, b_s3, conv, w_pa, w_pb, w_o, tt, deps=()):
    T = dx1.shape[0]
    nt = T // tt
    nb = tt // SG
    hb = tt // HALO

    def body(dx1_ref, proj_ref, cgh_ref, xsh_ref, a_ref, c_ref,
             bg_ref, gv_ref, ws_ref, bs_ref, cw_ref, pa_w, pb_w, wo_w,
             dproj_ref, dpa_ref, dpb_ref, dx1b_ref, st_ref, dbg_ref, dws_ref, dbs_ref, d_carry, mix_s, dvp_s):
        i = pl.program_id(0)

        @pl.when(i == 0)
        def _():
            st_ref[...] = jnp.zeros_like(st_ref)
            dbg_ref[...] = jnp.zeros_like(dbg_ref)
            dws_ref[...] = jnp.zeros_like(dws_ref)
            dbs_ref[...] = jnp.zeros_like(dbs_ref)
            d_carry[...] = jnp.zeros_like(d_carry)

        def pj(k):
            return proj_ref[:, _col(k)].astype(F32)

        def put(k, val):
            dproj_ref[:, _col(k)] = val.astype(BF16)

        dx1b = dx1_ref[...].astype(BF16)
        dx1b_ref[...] = dx1b
        dm = _nt(dx1b, wo_w[...])
        s_a = jax.nn.sigmoid(pj(5) + bg_ref[:, :D])
        s_b = jax.nn.sigmoid(pj(6) + bg_ref[:, D:])
        dpa = dm * s_a
        dpb = dm * s_b
        dpa_b = dpa.astype(BF16)
        dpb_b = dpb.astype(BF16)
        dpa_ref[...] = dpa_b
        dpb_ref[...] = dpb_b
        dga = dpa * _mm(a_ref[...], pa_w[...]) * (1.0 - s_a)
        dgb = dpb * _mm(c_ref[...], pb_w[...]) * (1.0 - s_b)
        dbg_ref[0:1, :D] += jnp.sum(dga, axis=0, keepdims=True)
        dbg_ref[0:1, D:] += jnp.sum(dgb, axis=0, keepdims=True)
        put(5, dga)
        put(6, dgb)
        da = _nt(dpa_b, pa_w[...])
        dc = _nt(dpb_b, pb_w[...])

        v = pj(1)
        vg, v_cdf = _gelu(v)
        vh, rv = _rms(vg)
        vp = (vh * gv_ref[...]).astype(BF16)
        wm = _masked_ws(ws_ref[...]).astype(BF16)
        for n in range(nb):
            rows = slice(n * SG, (n + 1) * SG)
            for g in range(N_GROUPS):
                cols = slice(g * SG, (g + 1) * SG)
                mix_s[rows, cols] = _mm(wm[g], vp[rows, cols]) + bs_ref[g]
        u = pj(0)
        ug, u_cdf = _gelu(u)
        put(0, da * mix_s[...] * _gelu_grad(u, u_cdf))
        dmix = da * ug
        dmix_b = dmix.astype(BF16)
        for n in range(nb):
            rows = slice(n * SG, (n + 1) * SG)
            for g in range(N_GROUPS):
                cols = slice(g * SG, (g + 1) * SG)
                blk = dmix_b[rows, cols]
                dws_ref[g] += _nt(blk, vp[rows, cols])
                dbs_ref[g] += dmix[rows, cols]
                dvp_s[rows, cols] = _tn(wm[g], blk)
        dvp = dvp_s[...]
        st_ref[ST_GV:ST_GV + 1, :] += jnp.sum(dvp * vh, axis=0, keepdims=True)
        put(1, _rms_bwd(dvp * gv_ref[...], vh, rv) * _gelu_grad(v, v_cdf))

        bgate, cg, xs = pj(2), pj(3), pj(4)
        q = cg * xs
        has_prev = (i < nt - 1).astype(F32)
        halo = cgh_ref[...].astype(F32) * xsh_ref[...].astype(F32) * has_prev
        q2 = _shift_down(halo, q, 2)
        q1 = _shift_down(halo, q, 1)
        w0, w1, w2 = cw_ref[0:1, :], cw_ref[1:2, :], cw_ref[2:3, :]
        put(2, dc * (w0 * q2 + w1 * q1 + w2 * q))
        dcv = dc * bgate
        st_ref[ST_CONV:ST_CONV + 1, :] += jnp.sum(dcv * q2, axis=0, keepdims=True)
        st_ref[ST_CONV + 1:ST_CONV + 2, :] += jnp.sum(dcv * q1, axis=0, keepdims=True)
        st_ref[ST_CONV + 2:ST_CONV + 3, :] += jnp.sum(dcv * q, axis=0, keepdims=True)
        nxt = d_carry[...]
        dq = w2 * dcv + w1 * _shift_up(dcv, nxt, 1) + w0 * _shift_up(dcv, nxt, 2)
        d_carry[...] = dcv[:d_carry.shape[0], :]
        put(3, dq * xs)
        put(4, dq * cg)

    rev = lambda i: nt - 1 - i
    tile = lambda w: pl.BlockSpec((tt, w), lambda i: (rev(i), 0))
    halo_spec = lambda k: pl.BlockSpec((HALO, D), lambda i: (jnp.maximum(rev(i) * hb - 1, 0), k))
    res = lambda shape: pl.BlockSpec(shape, lambda i: (0,) * len(shape))
    out_shape = (jax.ShapeDtypeStruct((T, IN_COLS), BF16), jax.ShapeDtypeStruct((T, D), BF16),
                 jax.ShapeDtypeStruct((T, D), BF16), jax.ShapeDtypeStruct((T, D), BF16),
                 jax.ShapeDtypeStruct((8, D), F32), jax.ShapeDtypeStruct((8, 2 * D), F32),
                 jax.ShapeDtypeStruct((N_GROUPS, SG, SG), F32), jax.ShapeDtypeStruct((N_GROUPS, SG, SG), F32))
    return _pcall(
        _after(body, 14, deps), name="mixer_bwd", grid=(nt,), out_shape=out_shape,
        in_specs=[tile(D), tile(IN_COLS), halo_spec(3), halo_spec(4), tile(D), tile(D),
                  _const_spec(b_gate.shape), _const_spec(g_v.shape), _const_spec(w_s.shape),
                  _const_spec(b_s3.shape), _const_spec(conv.shape),
                  _const_spec(w_pa.shape), _const_spec(w_pb.shape), _const_spec(w_o.shape)] + [_ANY] * len(deps),
        out_specs=[tile(IN_COLS), tile(D), tile(D), tile(D), res((8, D)), res((8, 2 * D)),
                   res((N_GROUPS, SG, SG)), res((N_GROUPS, SG, SG))],
        scratch_shapes=[pltpu.VMEM((8, D), F32), pltpu.VMEM((tt, D), F32), pltpu.VMEM((tt, D), F32)],
        compiler_params=_params(("arbitrary",)),
    )(dx1, proj, proj, proj, a, c, b_gate, g_v, w_s, b_s3, conv, w_pa, w_pb, w_o, *deps)


def _in_proj_bwd(dproj, x, dx1, g_mix, w_in, tt, deps=()):
    T = x.shape[0]

    def body(dproj_ref, x_ref, dx1_ref, gmix_ref, win_ref, gx_ref, st_ref):
        @pl.when(pl.program_id(0) == 0)
        def _():
            st_ref[...] = jnp.zeros_like(st_ref)

        dh = _nt(dproj_ref[...], win_ref[...])
        xh, r = _rms(x_ref[...])
        st_ref[0:1, :] += jnp.sum(dh * xh, axis=0, keepdims=True)
        gx_ref[...] = dx1_ref[...] + _rms_bwd(dh * gmix_ref[...], xh, r)

    tile = lambda w: pl.BlockSpec((tt, w), lambda i: (i, 0))
    return _pcall(
        _after(body, 5, deps), name="in_proj_bwd", grid=(T // tt,),
        out_shape=(jax.ShapeDtypeStruct((T, D), F32), jax.ShapeDtypeStruct((8, D), F32)),
        in_specs=[tile(IN_COLS), tile(D), tile(D), _const_spec(g_mix.shape), _const_spec(w_in.shape)]
        + [_ANY] * len(deps),
        out_specs=[tile(D), pl.BlockSpec((8, D), lambda i: (0, 0))],
        compiler_params=_params(("arbitrary",)),
    )(dproj, x, dx1, g_mix, w_in, *deps)


def _weight_grad(name, act, dout, bc, tk, deps=()):
    T, n_in = act.shape
    n_out = dout.shape[1]
    nk = T // tk
    bi = min(n_in, D)

    def body(a_ref, d_ref, o_ref, acc):
        k = pl.program_id(2)

        @pl.when(k == 0)
        def _():
            acc[...] = jnp.zeros_like(acc)

        acc[...] += _tn(a_ref[...], d_ref[...])

        @pl.when(k == nk - 1)
        def _():
            o_ref[...] = acc[...].astype(o_ref.dtype)

    return _pcall(
        _after(body, 2, deps), name=name, grid=(n_in // bi, n_out // bc, nk),
        out_shape=jax.ShapeDtypeStruct((n_in, n_out), BF16),
        in_specs=[pl.BlockSpec((tk, bi), lambda i, j, k: (k, i)), pl.BlockSpec((tk, bc), lambda i, j, k: (k, j))]
        + [_ANY] * len(deps),
        out_specs=pl.BlockSpec((bi, bc), lambda i, j, k: (i, j)),
        scratch_shapes=[pltpu.VMEM((bi, bc), F32)],
        compiler_params=_params(("arbitrary", "arbitrary", "arbitrary")),
    )(act, dout, *deps)


def _adamw(w, g, m, v):
    m = ADAM_B1 * m + (1.0 - ADAM_B1) * g
    v = ADAM_B2 * v + (1.0 - ADAM_B2) * (g * g)
    m_hat = m / (1.0 - ADAM_B1 ** ADAM_STEP)
    v_hat = v / (1.0 - ADAM_B2 ** ADAM_STEP)
    delta = -ADAM_LR * (m_hat / (jnp.sqrt(v_hat) + ADAM_EPS) + ADAM_WD * w)
    return delta, m, v


def _slot_sum(ref, own=None, me=None):
    g = None
    for s in range(N_DEV):
        term = ref[s] if own is None else jnp.where(me == s, own, ref[s])
        g = term.astype(F32) if g is None else g + term.astype(F32)
    return g


def _reduce_adamw(name, me, partial, own_block, slots, w, m, v, tr):
    rows, cols = w.shape

    def body(me_ref, own_ref, slot_ref, w_ref, m_ref, v_ref, g_out, d_out, m_out, v_out):
        g = _slot_sum(slot_ref, own_ref[...], me_ref[0])
        g_out[...] = g
        d_out[...], m_out[...], v_out[...] = _adamw(w_ref[...], g, m_ref[...], v_ref[...])

    tile = pl.BlockSpec((tr, cols), lambda i, me_ref: (i, 0))
    return _pcall(
        body, name=name, out_shape=[jax.ShapeDtypeStruct((rows, cols), F32)] * 4,
        grid_spec=pltpu.PrefetchScalarGridSpec(
            num_scalar_prefetch=1, grid=(rows // tr,),
            in_specs=[pl.BlockSpec((tr, cols), lambda i, me_ref: own_block(i, me_ref[0])),
                      pl.BlockSpec((N_DEV, tr, cols), lambda i, me_ref: (0, i, 0)), tile, tile, tile],
            out_specs=[tile] * 4),
        compiler_params=_params(("arbitrary",)),
    )(me, partial, slots, w, m, v)


SM_GMIX, SM_GV, SM_GFF, SM_GFIN, SM_LOSS, SM_BGATE, SM_BS, SM_ROWS = 0, 1, 2, 3, 4, 5, 7, 8


def _pack_small(st_ffn, st_mix, st_in, dbg, dbs, conv_rows):
    def body(ffn_ref, mix_ref, in_ref, dbg_ref, dbs_ref, sm_ref, conv_ref):
        sm_ref[SM_GMIX:SM_GMIX + 1, :] = in_ref[0:1, :]
        sm_ref[SM_GV:SM_GV + 1, :] = mix_ref[ST_GV:ST_GV + 1, :]
        sm_ref[SM_GFF:SM_GFF + 1, :] = ffn_ref[ST_GFF:ST_GFF + 1, :]
        sm_ref[SM_GFIN:SM_GFIN + 1, :] = ffn_ref[ST_GFIN:ST_GFIN + 1, :]
        sm_ref[SM_LOSS:SM_LOSS + 1, :] = ffn_ref[ST_LOSS:ST_LOSS + 1, :]
        sm_ref[SM_BGATE:SM_BGATE + 1, :] = dbg_ref[0:1, :D]
        sm_ref[SM_BGATE + 1:SM_BGATE + 2, :] = dbg_ref[0:1, D:]
        for g in range(N_GROUPS):
            sm_ref[SM_BS:SM_BS + 1, g * SG:(g + 1) * SG] = jnp.sum(dbs_ref[g].T, axis=0, keepdims=True)
        conv_ref[...] = jnp.zeros_like(conv_ref)
        for p in range(N_DEV):
            conv_ref[p, 0:conv_rows, :] = mix_ref[ST_CONV:ST_CONV + conv_rows, p * LANE:(p + 1) * LANE]

    return _pcall(
        body, name="pack_small",
        out_shape=(jax.ShapeDtypeStruct((SM_ROWS, D), F32), jax.ShapeDtypeStruct((N_DEV, 8, LANE), F32)),
        in_specs=[_VMEM] * 5, out_specs=[_VMEM] * 2, compiler_params=_params(),
    )(st_ffn, st_mix, st_in, dbg, dbs)


def _small_update(sm_slots, ws_own, ws_slots, conv_slots, params):
    flat = [a for t in params for a in t]

    def body(sm_ref, wso_ref, ws_ref, conv_ref, *refs):
        ins, outs = refs[:len(flat)], refs[len(flat):]
        loss_ref, outs = outs[0], outs[1:]
        _, me = _position()
        sm = _slot_sum(sm_ref)
        loss_ref[...] = (0.5 / D) * jnp.sum(sm[SM_LOSS:SM_LOSS + 1, :], axis=1, keepdims=True)
        grads = [sm[SM_GMIX:SM_GMIX + 1, :], sm[SM_GV:SM_GV + 1, :], sm[SM_GFF:SM_GFF + 1, :],
                 sm[SM_GFIN:SM_GFIN + 1, :], sm[SM_BGATE:SM_BGATE + 2, :], sm[SM_BS:SM_BS + 1, :],
                 _masked_ws(_slot_sum(ws_ref, wso_ref[...], me)), _slot_sum(conv_ref)]
        for n, g in enumerate(grads):
            w_ref, m_ref, v_ref = ins[3 * n:3 * n + 3]
            g_out, d_out, m_out, v_out = outs[4 * n:4 * n + 4]
            g_out[...] = g
            d_out[...], m_out[...], v_out[...] = _adamw(w_ref[...], g, m_ref[...], v_ref[...])

    out_shape = [jax.ShapeDtypeStruct((1, 1), F32)]
    for w, _, _ in params:
        out_shape += [jax.ShapeDtypeStruct(w.shape, F32)] * 4
    return _pcall(
        body, name="small_update", out_shape=out_shape,
        in_specs=[_VMEM] * (4 + len(flat)), out_specs=[_VMEM] * len(out_shape), compiler_params=_params(),
    )(sm_slots, ws_own, ws_slots, conv_slots, *flat)


def kernel(x, norm_mix_g, w_in, b_gate, norm_v_g, w_s, b_s, conv_w, w_proj_a, w_proj_b, w_out, norm_ff_g, w_ff1, w_ff2, norm_final_g, loss_target, m_norm_mix_g, m_w_in, m_b_gate, m_norm_v_g, m_w_s, m_b_s, m_conv_w, m_w_proj_a, m_w_proj_b, m_w_out, m_norm_ff_g, m_w_ff1, m_w_ff2, m_norm_final_g, v_norm_mix_g, v_w_in, v_b_gate, v_norm_v_g, v_w_s, v_b_s, v_conv_w, v_w_proj_a, v_w_proj_b, v_w_out, v_norm_ff_g, v_w_ff1, v_w_ff2, v_norm_final_g):
    T = x.shape[1]
    tt = min(256, T)
    tk = min(4096, T)
    conv_rows = conv_w.shape[1]

    pad8 = lambda a: jnp.pad(a, ((0, 8 - a.shape[0]), (0, 0)))
    xs = x.reshape(T, D)
    tgt = loss_target.reshape(T, D)
    g_mix, g_v, g_ff, g_fin = norm_mix_g, norm_v_g, norm_ff_g, norm_final_g.reshape(1, D)
    ws = w_s[0]
    bs3 = b_s.reshape(N_GROUPS, SG, 1)

    c_in, c_ff1 = w_in.shape[2], w_ff1.shape[2]
    r_ff2, r_p = w_ff2.shape[1], w_proj_a.shape[1]
    lane_view = lambda width: (lambda ref, p: _lane_block(ref, p, width))
    row_view = lambda rows: (lambda ref, p: _row_block(ref, p, rows))
    whole = lambda ref, p: ref
    slots = lambda shape, dtype: lax.empty((N_DEV,) + shape, dtype)

    placed = _place_weights(w_in[0], w_ff1[0], w_ff2[0], w_proj_a[0], w_proj_b[0], w_out[0], pad8(conv_w[0]))
    mixer_views = [row_view(r_p), row_view(r_p), row_view(r_p), lane_view(LANE)]
    ffn_views = [lane_view(c_ff1), row_view(r_ff2)]
    pairs = ((0, 1), (2, 4), (3, 5), (6, 7))
    in_group = lambda refs, ks: _gather_entries(refs[:1], [lane_view(c_in)], tuple(k for k in ks if k))

    def gather_groups(refs):
        return ([in_group(refs, ks) for ks in pairs]
                + [_gather_entries(refs[1:5], mixer_views), _gather_entries(refs[5:], ffn_views)])

    sems, placed, g_token = _start_copies("gather_start", placed, gather_groups)
    _, me = _position()
    W_in = placed[0]
    tp = min(2048, T)
    proj = lax.empty((T, IN_COLS), BF16)
    for n, ks in enumerate(pairs):
        W_in, = _wait_copies(f"gather_wait_in_{n}", [W_in], sems[2 * n], sems[2 * n + 1],
                             lambda refs, ks=ks: in_group(refs, ks), proj if n else g_token)
        blk = jnp.stack([me ^ ks[0], me ^ ks[1]])
        if n == 0:
            proj, h = _in_proj_pair(f"in_proj_{n}", blk, xs, W_in, proj, c_in, tp, g_mix)
        else:
            proj = _in_proj_pair(f"in_proj_{n}", blk, h, W_in, proj, c_in, tp)
    n = len(pairs)
    PA, PB, WO, conv = _wait_copies(
        "gather_wait_mixer", placed[1:5], sems[2 * n], sems[2 * n + 1],
        lambda refs: _gather_entries(refs, mixer_views), proj)
    a, c, m, x1 = _mixer_fwd(xs, proj, b_gate, g_v, ws, bs3, conv, PA, PB, WO, min(512, T))
    W1, W2 = _wait_copies(
        "gather_wait_ffn", placed[5:], sems[2 * n + 2], sems[2 * n + 3],
        lambda refs: _gather_entries(refs, ffn_views), x1)
    hf, s, dpre, dx2, dx1, st_ffn = _ffn_fwd_bwd(x1, tgt, g_ff, g_fin, W1, W2, min(512, T))

    d_ff2 = _weight_grad("dw_ff2", s, dx2, D, tk)
    d_ff1 = _weight_grad("dw_ff1", hf, dpre, D, tk)
    ff_views = [lane_view(c_ff1), row_view(r_ff2)]
    ff_sems, ff_arrays, ff_token = _start_copies(
        "scatter_start_ffn", [d_ff1, d_ff2, slots((D, c_ff1), BF16), slots((r_ff2, D), BF16)],
        lambda refs: [_scatter_entries(ff_views)(refs)])

    dproj, dpa, dpb, dx1b, st_mix, dbg, dws, dbs = _mixer_bwd(
        dx1, proj, a, c, b_gate, g_v, ws, bs3, conv, PA, PB, WO, tt, deps=(ff_token,))
    d_o = _weight_grad("dw_out", m, dx1b, D, tk)
    d_pa = _weight_grad("dw_proj_a", a, dpa, D, tk)
    d_pb = _weight_grad("dw_proj_b", c, dpb, D, tk)
    p_views = [row_view(r_p), row_view(r_p), row_view(r_p), whole]
    p_sems, p_arrays, p_token = _start_copies(
        "scatter_start_proj", [d_pa, d_pb, d_o, dws] + [slots((r_p, D), BF16)] * 3 + [slots(dws.shape, F32)],
        lambda refs: [_scatter_entries(p_views)(refs)])

    d_in = _weight_grad("dw_in", h, dproj, D, tk, deps=(p_token,))
    in_views = [lane_view(c_in)]
    in_sems, in_arrays, in_token = _start_copies(
        "scatter_start_in", [d_in, slots((D, c_in), BF16)], lambda refs: [_scatter_entries(in_views)(refs)])
    grad_x, st_in = _in_proj_bwd(dproj, xs, dx1, g_mix, W_in, min(512, T), deps=(in_token,))

    small, d_conv = _pack_small(st_ffn, st_mix, st_in, dbg, dbs, conv_rows)
    r_conv, r_small = _small_exchange(d_conv, small)
    d_ff1, d_ff2, s_ff1, s_ff2 = _wait_copies(
        "scatter_wait_ffn", ff_arrays, ff_sems[0], ff_sems[1], _scatter_entries(ff_views), r_small)
    d_pa, d_pb, d_o, dws, s_pa, s_pb, s_o, s_ws = _wait_copies(
        "scatter_wait_proj", p_arrays, p_sems[0], p_sems[1], _scatter_entries(p_views), r_small)
    d_in, s_in = _wait_copies("scatter_wait_in", in_arrays, in_sems[0], in_sems[1], _scatter_entries(in_views), r_small)

    own_cols = lambda i, me: (i, me)
    own_rows = lambda rows, tr: (lambda i, me: (me * (rows // tr) + i, 0))
    me1 = me.reshape(1)
    big = {
        "w_in": _reduce_adamw("adamw_w_in", me1, d_in, own_cols, s_in, w_in[0], m_w_in[0], v_w_in[0], 128),
        "w_ff1": _reduce_adamw("adamw_w_ff1", me1, d_ff1, own_cols, s_ff1, w_ff1[0], m_w_ff1[0], v_w_ff1[0], 128),
        "w_ff2": _reduce_adamw(
            "adamw_w_ff2", me1, d_ff2, own_rows(r_ff2, 64), s_ff2, w_ff2[0], m_w_ff2[0], v_w_ff2[0], 64),
        "w_proj_a": _reduce_adamw(
            "adamw_w_proj_a", me1, d_pa, own_rows(r_p, 32), s_pa, w_proj_a[0], m_w_proj_a[0], v_w_proj_a[0], 32),
        "w_proj_b": _reduce_adamw(
            "adamw_w_proj_b", me1, d_pb, own_rows(r_p, 32), s_pb, w_proj_b[0], m_w_proj_b[0], v_w_proj_b[0], 32),
        "w_out": _reduce_adamw(
            "adamw_w_out", me1, d_o, own_rows(r_p, 32), s_o, w_out[0], m_w_out[0], v_w_out[0], 32),
    }

    two = lambda a: a.reshape(2, D)
    one = lambda a: a.reshape(1, D)
    small_params = [
        (norm_mix_g, m_norm_mix_g, v_norm_mix_g),
        (norm_v_g, m_norm_v_g, v_norm_v_g),
        (norm_ff_g, m_norm_ff_g, v_norm_ff_g),
        (one(norm_final_g), one(m_norm_final_g), one(v_norm_final_g)),
        (two(b_gate), two(m_b_gate), two(v_b_gate)),
        (one(b_s), one(m_b_s), one(v_b_s)),
        (w_s[0], m_w_s[0], v_w_s[0]),
        (pad8(conv_w[0]), pad8(m_conv_w[0]), pad8(v_conv_w[0])),
    ]
    res = _small_update(r_small, dws, s_ws, r_conv, small_params)
    loss = res[0].reshape(())
    names = ["norm_mix_g", "norm_v_g", "norm_ff_g", "norm_final_g", "b_gate", "b_s", "w_s", "conv_w"]
    shapes = {"norm_mix_g": norm_mix_g.shape, "norm_v_g": norm_v_g.shape, "norm_ff_g": norm_ff_g.shape,
              "norm_final_g": norm_final_g.shape, "b_gate": b_gate.shape, "b_s": b_s.shape, "w_s": w_s.shape}
    out = {}
    for n, name in enumerate(names):
        quad = res[1 + 4 * n:5 + 4 * n]
        if name == "conv_w":
            out[name] = [q[:conv_rows][None] for q in quad]
        else:
            out[name] = [q.reshape(shapes[name]) for q in quad]
    for name, quad in big.items():
        out[name] = [q[None] for q in quad]

    order = ["norm_mix_g", "w_in", "b_gate", "norm_v_g", "w_s", "b_s", "conv_w", "w_proj_a", "w_proj_b", "w_out",
             "norm_ff_g", "w_ff1", "w_ff2", "norm_final_g"]
    grads = [out[n][0] for n in order]
    deltas = [out[n][1] for n in order]
    new_m = [out[n][2] for n in order]
    new_v = [out[n][3] for n in order]
    return (loss, grad_x.reshape(x.shape), *grads, *deltas, *new_m, *new_v)
```

```python
import math

import jax
import jax.numpy as jnp
from jax import lax
from jax.experimental import pallas as pl
from jax.experimental.pallas import tpu as pltpu

F32 = jnp.float32
BF16 = jnp.bfloat16

N_DEV = 8
D = 1024
D_FF = 4096
IN_COLS = 7 * D
SG = 128
N_GROUPS = 8
CHUNK = 64
EPS = 1e-6
HALO = 16
LANE = 128
VMEM_LIMIT = 62 * 1024 * 1024

ADAM_LR = 0.001
ADAM_B1 = 0.9
ADAM_B2 = 0.999
ADAM_EPS = 1e-08
ADAM_WD = 0.01
ADAM_STEP = 10

SQRT_HALF = math.sqrt(0.5)
INV_SQRT_2PI = 1.0 / math.sqrt(2.0 * math.pi)

_REL = [(dx, dy, dc) for dx in (0, 1) for dy in (0, 1) for dc in (0, 1)]

_VMEM = pl.BlockSpec(memory_space=pltpu.VMEM)
_ANY = pl.BlockSpec(memory_space=pl.ANY)


def _pcall(body, **kw):
    return pl.pallas_call(body, **kw)


def _params(sem=None):
    if sem is None:
        return pltpu.CompilerParams(vmem_limit_bytes=VMEM_LIMIT)
    return pltpu.CompilerParams(dimension_semantics=sem, vmem_limit_bytes=VMEM_LIMIT)


def _const_spec(shape):
    nd = len(shape)
    return pl.BlockSpec(shape, lambda *_: (0,) * nd, pipeline_mode=pl.Buffered(1))


def _after(body, n_in, deps):
    def wrapped(*refs):
        return body(*refs[:n_in], *refs[n_in + len(deps):])
    return wrapped


def _mm(a, b):
    return jnp.dot(a, b, preferred_element_type=F32)


def _nt(a, b):
    return lax.dot_general(a, b, (((1,), (1,)), ((), ())), preferred_element_type=F32)


def _tn(a, b):
    return lax.dot_general(a, b, (((0,), (0,)), ((), ())), preferred_element_type=F32)


def _rms(x):
    r = lax.rsqrt(jnp.mean(x * x, axis=-1, keepdims=True) + EPS)
    return x * r, r


def _rms_bwd(dyg, xh, r):
    return r * (dyg - xh * jnp.mean(dyg * xh, axis=-1, keepdims=True))


def _gelu(x):
    cdf = 0.5 * (1.0 + lax.erf(x * SQRT_HALF))
    return x * cdf, cdf


def _gelu_grad(x, cdf):
    return cdf + x * (jnp.exp(-0.5 * x * x) * INV_SQRT_2PI)


def _masked_ws(ws):
    i = lax.broadcasted_iota(jnp.int32, (SG, SG), 0)
    j = lax.broadcasted_iota(jnp.int32, (SG, SG), 1)
    keep = jnp.logical_or(j < CHUNK, i >= CHUNK)
    return jnp.where(keep[None], ws, jnp.zeros_like(ws))


def _shift_down(halo, q, k):
    ext = jnp.concatenate([halo, q], axis=0)
    return pltpu.roll(ext, k, 0)[halo.shape[0]:]


def _shift_up(q, nxt, k):
    ext = jnp.concatenate([q, nxt], axis=0)
    return pltpu.roll(ext, ext.shape[0] - k, 0)[:q.shape[0]]


def _col(k):
    return slice(k * D, (k + 1) * D)


def _position():
    x, y, c = lax.axis_index("x"), lax.axis_index("y"), lax.axis_index("c")
    return (x, y, c), 4 * x + 2 * y + c


def _exchange(items, send_sems, recv_sems, local_sems):
    (x, y, c), me = _position()
    started = []
    for w, (src_of, dst_of) in enumerate(items):
        own = pltpu.make_async_copy(src_of(me), dst_of(me), local_sems.at[w])
        own.start()
        started.append(own)
        for k in range(1, N_DEV):
            dx, dy, dc = _REL[k]
            peer = (1 - x if dx else x, 1 - y if dy else y, 1 - c if dc else c)
            pid = 4 * peer[0] + 2 * peer[1] + peer[2]
            cp = pltpu.make_async_remote_copy(
                src_ref=src_of(pid), dst_ref=dst_of(me),
                send_sem=send_sems.at[w * N_DEV + k], recv_sem=recv_sems.at[w * N_DEV + k],
                device_id=peer, device_id_type=pl.DeviceIdType.MESH)
            cp.start()
            started.append(cp)
    for cp in started:
        cp.wait()


def _lane_block(ref, p, width):
    return ref.at[:, pl.ds(pl.multiple_of(p * width, LANE), width)]


def _row_block(ref, p, rows):
    return ref.at[pl.ds(p * rows, rows), :]


def _small_exchange(d_conv, small):
    def body(dconv, sm, rconv, rsm, send_sems, recv_sems, local_sems):
        items = [
            (lambda p: dconv.at[p], lambda p: rconv.at[p]),
            (lambda p: sm, lambda p: rsm.at[p]),
        ]
        _exchange(items, send_sems, recv_sems, local_sems)

    n_items = 2
    return _pcall(
        body, name="small_exchange",
        out_shape=(jax.ShapeDtypeStruct((N_DEV,) + d_conv.shape[1:], F32),
                   jax.ShapeDtypeStruct((N_DEV,) + small.shape, F32)),
        in_specs=[_ANY] * n_items, out_specs=[_ANY] * n_items,
        scratch_shapes=[pltpu.SemaphoreType.DMA((n_items * N_DEV,)), pltpu.SemaphoreType.DMA((n_items * N_DEV,)),
                        pltpu.SemaphoreType.DMA((n_items,))],
        compiler_params=_params(),
    )(d_conv, small)


_HBM = pl.BlockSpec(memory_space=pltpu.HBM)
_SEM = pl.BlockSpec(memory_space=pltpu.SEMAPHORE)
_EFFECT = pltpu.SideEffectType.DATAFLOW_SIDE_EFFECTING


REL_ORDER = (1, 2, 4, 3, 5, 6, 7)


def _to_peers(src_of, dst_of, order=REL_ORDER):
    return [(src_of, dst_of, k) for k in order]


def _remote_copies(entries, send_sems, recv_sems):
    (x, y, c), me = _position()
    copies = []
    for n, (src_of, dst_of, k) in enumerate(entries):
        dx, dy, dc = _REL[k]
        peer = (1 - x if dx else x, 1 - y if dy else y, 1 - c if dc else c)
        pid = 4 * peer[0] + 2 * peer[1] + peer[2]
        copies.append(pltpu.make_async_remote_copy(
            src_ref=src_of(pid, me), dst_ref=dst_of(me), send_sem=send_sems.at[n], recv_sem=recv_sems.at[n],
            device_id=peer, device_id_type=pl.DeviceIdType.MESH))
    return copies


def _start_copies(name, arrays, make_groups):
    n = len(arrays)
    sizes = [len(g) for g in make_groups([None] * n)]

    def body(*refs):
        sems, token = refs[n:n + 2 * len(sizes)], refs[-1]
        for g, entries in enumerate(make_groups(refs[:n])):
            for cp in _remote_copies(entries, sems[2 * g], sems[2 * g + 1]):
                cp.start()
        token[...] = jnp.zeros_like(token)

    out_shape = []
    for size in sizes:
        out_shape += [pltpu.SemaphoreType.DMA((size,))] * 2
    out_shape += [pltpu.HBM(a.shape, a.dtype) for a in arrays] + [jax.ShapeDtypeStruct((8, LANE), F32)]
    res = _pcall(
        body, name=name, out_shape=out_shape,
        in_specs=[_HBM] * n, out_specs=[_SEM] * (2 * len(sizes)) + [_HBM] * n + [_VMEM],
        input_output_aliases={i: 2 * len(sizes) + i for i in range(n)},
        compiler_params=pltpu.CompilerParams(has_side_effects=_EFFECT),
    )(*[pltpu.with_memory_space_constraint(a, pltpu.HBM) for a in arrays])
    return res[:2 * len(sizes)], res[2 * len(sizes):-1], res[-1]


def _wait_copies(name, arrays, send_sems, recv_sems, make_entries, after):
    n = len(arrays)

    def body(*refs):
        for cp in _remote_copies(make_entries(refs[:n]), refs[n], refs[n + 1]):
            cp.wait_send()
            cp.wait_recv()

    return _pcall(
        body, name=name, out_shape=[pltpu.HBM(a.shape, a.dtype) for a in arrays],
        in_specs=[_HBM] * n + [_SEM, _SEM, _ANY], out_specs=[_HBM] * n,
        input_output_aliases={i: i for i in range(n)},
        compiler_params=pltpu.CompilerParams(has_side_effects=_EFFECT),
    )(*arrays, send_sems, recv_sems, after)


def _place_weights(w_in, w_ff1, w_ff2, w_pa, w_pb, w_o, conv8):
    n_items = 7
    c_in, c_ff1 = w_in.shape[1], w_ff1.shape[1]
    r_ff2, r_p = w_ff2.shape[0], w_pa.shape[0]

    def body(win_ref, w1_ref, w2_ref, pa_ref, pb_ref, wo_ref, cw_ref,
             win_o, pa_o, pb_o, wo_o, cw_o, w1_o, w2_o,
             s_win, s_w1, s_w2, s_pa, s_pb, s_wo, sems):
        _, me = _position()
        for src, stage in ((win_ref, s_win), (w1_ref, s_w1), (w2_ref, s_w2),
                           (pa_ref, s_pa), (pb_ref, s_pb), (wo_ref, s_wo)):
            stage[...] = src[...].astype(BF16)
        pairs = [
            (s_win, _lane_block(win_o, me, c_in)), (s_pa, _row_block(pa_o, me, r_p)),
            (s_pb, _row_block(pb_o, me, r_p)), (s_wo, _row_block(wo_o, me, r_p)),
            (cw_ref, _lane_block(cw_o, me, conv8.shape[1])),
            (s_w1, _lane_block(w1_o, me, c_ff1)), (s_w2, _row_block(w2_o, me, r_ff2)),
        ]
        copies = [pltpu.make_async_copy(s, d, sems.at[n]) for n, (s, d) in enumerate(pairs)]
        for cp in copies:
            cp.start()
        for cp in copies:
            cp.wait()

    out_shape = (
        jax.ShapeDtypeStruct((D, N_DEV * c_in), BF16),
        jax.ShapeDtypeStruct((N_DEV * r_p, D), BF16),
        jax.ShapeDtypeStruct((N_DEV * r_p, D), BF16),
        jax.ShapeDtypeStruct((N_DEV * r_p, D), BF16),
        jax.ShapeDtypeStruct((conv8.shape[0], N_DEV * conv8.shape[1]), F32),
        jax.ShapeDtypeStruct((D, N_DEV * c_ff1), BF16),
        jax.ShapeDtypeStruct((N_DEV * r_ff2, D), BF16),
    )
    return _pcall(
        body, name="place_weights", out_shape=out_shape,
        in_specs=[_VMEM] * n_items, out_specs=[_ANY] * n_items,
        scratch_shapes=[pltpu.VMEM(w.shape, BF16) for w in (w_in, w_ff1, w_ff2, w_pa, w_pb, w_o)]
        + [pltpu.SemaphoreType.DMA((n_items,))],
        compiler_params=_params(),
    )(w_in, w_ff1, w_ff2, w_pa, w_pb, w_o, conv8)


def _gather_entries(refs, views, order=REL_ORDER):
    entries = []
    for r, v in zip(refs, views):
        entries += _to_peers(lambda pid, me, r=r, v=v: v(r, me), lambda me, r=r, v=v: v(r, me), order)
    return entries


def _scatter_entries(views):
    def make(refs):
        srcs, lands = refs[:len(views)], refs[len(views):]
        entries = []
        for r, v, l in zip(srcs, views, lands):
            entries += _to_peers(lambda pid, me, r=r, v=v: v(r, pid), lambda me, l=l: l.at[me])
        return entries
    return make


def _in_proj_pair(name, blk, act, w_in, proj, width, tt, g_mix=None):
    T = act.shape[0]
    nt = T // tt
    normed = g_mix is not None

    def body(blk_ref, act_ref, *refs):
        if normed:
            g_ref, wa_ref, wb_ref, _, proj_out, h_out, w_s, o_s, sems = refs
        else:
            wa_ref, wb_ref, _, proj_out, w_s, o_s, sems = refs
        i = pl.program_id(0)

        @pl.when(i == 0)
        def _():
            w_s[:, :width] = wa_ref[...]
            w_s[:, width:] = wb_ref[...]

        def copies(slot, step):
            rows = pl.ds(step * tt, tt)
            return [pltpu.make_async_copy(
                o_s.at[slot, :, half * width:(half + 1) * width],
                proj_out.at[rows, pl.ds(pl.multiple_of(blk_ref[half] * width, LANE), width)],
                sems.at[slot, half]) for half in range(2)]

        slot = i % 2

        @pl.when(i >= 2)
        def _():
            for cp in copies(slot, i - 2):
                cp.wait()

        if normed:
            xh, _ = _rms(act_ref[...])
            h = (xh * g_ref[...]).astype(BF16)
            h_out[...] = h
        else:
            h = act_ref[...]
        o_s[slot] = _mm(h, w_s[...]).astype(BF16)
        for cp in copies(slot, i):
            cp.start()

        @pl.when(i == nt - 1)
        def _():
            for cp in copies(slot, i) + (copies(1 - slot, i - 1) if nt > 1 else []):
                cp.wait()

    tile = pl.BlockSpec((tt, D), lambda i, b: (i, 0))
    w_spec = lambda half: pl.BlockSpec((D, width), lambda i, b: (0, b[half]), pipeline_mode=pl.Buffered(1))
    extra_in = [_const_spec(g_mix.shape)] if normed else []
    extra_args = [g_mix] if normed else []
    out_shape = [jax.ShapeDtypeStruct(proj.shape, proj.dtype)] + ([jax.ShapeDtypeStruct((T, D), BF16)] if normed else [])
    res = _pcall(
        body, name=name, out_shape=out_shape,
        grid_spec=pltpu.PrefetchScalarGridSpec(
            num_scalar_prefetch=1, grid=(nt,),
            in_specs=[tile] + extra_in + [w_spec(0), w_spec(1), _ANY],
            out_specs=[_ANY] + ([tile] if normed else []),
            scratch_shapes=[pltpu.VMEM((D, 2 * width), BF16), pltpu.VMEM((2, tt, 2 * width), BF16),
                            pltpu.SemaphoreType.DMA((2, 2))]),
        input_output_aliases={4 + len(extra_in): 0},
        compiler_params=_params(("arbitrary",)),
    )(blk, act, *extra_args, w_in, w_in, proj)
    return res if normed else res[0]


def _mixer_fwd(x, proj, b_gate, g_v, w_s, b_s3, conv, w_pa, w_pb, w_o, tt):
    T = x.shape[0]
    nt = T // tt
    nb = tt // SG

    def body(x_ref, proj_ref, bg_ref, gv_ref, ws_ref, bs_ref, cw_ref, pa_w, pb_w, wo_w,
             acm_ref, pa_ref, pb_ref, x1_ref, q_carry, mix_s):
        @pl.when(pl.program_id(0) == 0)
        def _():
            q_carry[...] = jnp.zeros_like(q_carry)

        def proj(k):
            return proj_ref[:, _col(k)].astype(F32)

        vg, _ = _gelu(proj(1))
        vh, _ = _rms(vg)
        vp = (vh * gv_ref[...]).astype(BF16)
        wm = _masked_ws(ws_ref[...]).astype(BF16)
        for n in range(nb):
            rows = slice(n * SG, (n + 1) * SG)
            for g in range(N_GROUPS):
                cols = slice(g * SG, (g + 1) * SG)
                mix_s[rows, cols] = _mm(wm[g], vp[rows, cols]) + bs_ref[g]
        ug, _ = _gelu(proj(0))
        a = (ug * mix_s[...]).astype(BF16)
        acm_ref[0] = a
        pa = _mm(a, pa_w[...])
        pa_ref[...] = pa.astype(BF16)
        m = jax.nn.sigmoid(proj(5) + bg_ref[:, :D]) * pa

        bgate = proj(2)
        q = proj(3) * proj(4)
        halo = q_carry[...]
        cv = cw_ref[0:1, :] * _shift_down(halo, q, 2) + cw_ref[1:2, :] * _shift_down(halo, q, 1) + cw_ref[2:3, :] * q
        q_carry[...] = q[tt - q_carry.shape[0]:, :]
        c = (bgate * cv).astype(BF16)
        acm_ref[1] = c
        pb = _mm(c, pb_w[...])
        pb_ref[...] = pb.astype(BF16)
        m = (m + jax.nn.sigmoid(proj(6) + bg_ref[:, D:]) * pb).astype(BF16)
        acm_ref[2] = m
        x1_ref[...] = x_ref[...] + _mm(m, wo_w[...])

    tile = lambda w: pl.BlockSpec((tt, w), lambda i: (i, 0))
    out_shape = ([jax.ShapeDtypeStruct((3, T, D), BF16)] + [jax.ShapeDtypeStruct((T, D), BF16)] * 2
                 + [jax.ShapeDtypeStruct((T, D), F32)])
    return _pcall(
        body, name="mixer_fwd", grid=(nt,), out_shape=out_shape,
        in_specs=[tile(D), tile(IN_COLS), _const_spec(b_gate.shape),
                  _const_spec(g_v.shape), _const_spec(w_s.shape), _const_spec(b_s3.shape), _const_spec(conv.shape),
                  _const_spec(w_pa.shape), _const_spec(w_pb.shape), _const_spec(w_o.shape)],
        out_specs=[pl.BlockSpec((3, tt, D), lambda i: (0, i, 0))] + [tile(D)] * 3,
        scratch_shapes=[pltpu.VMEM((8, D), F32), pltpu.VMEM((tt, D), F32)],
        compiler_params=_params(("arbitrary",)),
    )(x, proj, b_gate, g_v, w_s, b_s3, conv, w_pa, w_pb, w_o)


ST_GFIN, ST_GFF, ST_LOSS = 0, 1, 2


def _ffn_fwd_bwd(x1, tgt, g_ff, g_fin, w1, w2, tt):
    T = x1.shape[0]
    nt = T // tt
    nk = D_FF // D

    def body(x1_ref, tgt_ref, gff_ref, gfin_ref, w1_ref, w2_ref,
             hf_ref, s_ref, dpre_ref, dx2_ref, dx1_ref, st_ref, z_s):
        @pl.when(pl.program_id(0) == 0)
        def _():
            st_ref[...] = jnp.zeros_like(st_ref)

        x1 = x1_ref[...]
        xh1, r1 = _rms(x1)
        hf = (xh1 * gff_ref[...]).astype(BF16)
        hf_ref[...] = hf
        acc = jnp.zeros((tt, D), F32)
        for k in range(nk):
            z = jnp.maximum(_mm(hf, w1_ref[:, _col(k)]), 0.0)
            z_s[:, _col(k)] = z.astype(BF16)
            s = (z * z).astype(BF16)
            s_ref[:, _col(k)] = s
            acc = acc + _mm(s, w2_ref[_col(k), :])
        x2 = x1 + acc
        xh2, r2 = _rms(x2)
        diff = xh2 * gfin_ref[...] - tgt_ref[...]
        st_ref[ST_LOSS:ST_LOSS + 1, :] += jnp.sum(diff * diff, axis=0, keepdims=True)
        dy = diff * (1.0 / D)
        st_ref[ST_GFIN:ST_GFIN + 1, :] += jnp.sum(dy * xh2, axis=0, keepdims=True)
        dx2 = _rms_bwd(dy * gfin_ref[...], xh2, r2)
        dx2b = dx2.astype(BF16)
        dx2_ref[...] = dx2b
        dhf = jnp.zeros((tt, D), F32)
        for k in range(nk):
            dpre = (_nt(dx2b, w2_ref[_col(k), :]) * (2.0 * z_s[:, _col(k)].astype(F32))).astype(BF16)
            dpre_ref[:, _col(k)] = dpre
            dhf = dhf + _nt(dpre, w1_ref[:, _col(k)])
        st_ref[ST_GFF:ST_GFF + 1, :] += jnp.sum(dhf * xh1, axis=0, keepdims=True)
        dx1_ref[...] = dx2 + _rms_bwd(dhf * gff_ref[...], xh1, r1)

    tile = lambda w: pl.BlockSpec((tt, w), lambda i: (i, 0))
    out_shape = (jax.ShapeDtypeStruct((T, D), BF16), jax.ShapeDtypeStruct((T, D_FF), BF16),
                 jax.ShapeDtypeStruct((T, D_FF), BF16), jax.ShapeDtypeStruct((T, D), BF16),
                 jax.ShapeDtypeStruct((T, D), F32), jax.ShapeDtypeStruct((8, D), F32))
    return _pcall(
        body, name="ffn_fwd_bwd", grid=(nt,), out_shape=out_shape,
        in_specs=[tile(D), tile(D), _const_spec(g_ff.shape), _const_spec(g_fin.shape),
                  _const_spec(w1.shape), _const_spec(w2.shape)],
        out_specs=[tile(D), tile(D_FF), tile(D_FF), tile(D), tile(D), pl.BlockSpec((8, D), lambda i: (0, 0))],
        scratch_shapes=[pltpu.VMEM((tt, D_FF), BF16)],
        compiler_params=_params(("arbitrary",)),
    )(x1, tgt, g_ff, g_fin, w1, w2)


ST_GV, ST_CONV = 0, 1


def _mixer_bwd(dx1, proj, pa, pb, b_gate, g_v, w_s, b_s3, conv, w_pa, w_pb, w_o, tt, deps=()):
    T = dx1.shape[0]
    nt = T // tt
    nb = tt // SG
    hb = tt // HALO

    def body(dx1_ref, proj_ref, cgh_ref, xsh_ref, pa_ref, pb_ref,
             bg_ref, gv_ref, ws_ref, bs_ref, cw_ref, pa_w, pb_w, wo_w,
             dproj_ref, dstk_ref, st_ref, dbg_ref, dws_ref, dbs_ref, d_carry, mix_s, dvp_s):
        i = pl.program_id(0)

        @pl.when(i == 0)
        def _():
            st_ref[...] = jnp.zeros_like(st_ref)
            dbg_ref[...] = jnp.zeros_like(dbg_ref)
            dws_ref[...] = jnp.zeros_like(dws_ref)
            dbs_ref[...] = jnp.zeros_like(dbs_ref)
            d_carry[...] = jnp.zeros_like(d_carry)

        def pj(k):
            return proj_ref[:, _col(k)].astype(F32)

        def put(k, val):
            dproj_ref[:, _col(k)] = val.astype(BF16)

        dx1b = dx1_ref[...].astype(BF16)
        dstk_ref[2] = dx1b
        dm = _nt(dx1b, wo_w[...])
        s_a = jax.nn.sigmoid(pj(5) + bg_ref[:, :D])
        s_b = jax.nn.sigmoid(pj(6) + bg_ref[:, D:])
        dpa = dm * s_a
        dpb = dm * s_b
        dpa_b = dpa.astype(BF16)
        dpb_b = dpb.astype(BF16)
        dstk_ref[0] = dpa_b
        dstk_ref[1] = dpb_b
        dga = dpa * pa_ref[...].astype(F32) * (1.0 - s_a)
        dgb = dpb * pb_ref[...].astype(F32) * (1.0 - s_b)
        dbg_ref[0:1, :D] += jnp.sum(dga, axis=0, keepdims=True)
        dbg_ref[0:1, D:] += jnp.sum(dgb, axis=0, keepdims=True)
        put(5, dga)
        put(6, dgb)
        da = _nt(dpa_b, pa_w[...])
        dc = _nt(dpb_b, pb_w[...])

        v = pj(1)
        vg, v_cdf = _gelu(v)
        vh, rv = _rms(vg)
        vp = (vh * gv_ref[...]).astype(BF16)
        wm = _masked_ws(ws_ref[...]).astype(BF16)
        for n in range(nb):
            rows = slice(n * SG, (n + 1) * SG)
            for g in range(N_GROUPS):
                cols = slice(g * SG, (g + 1) * SG)
                mix_s[rows, cols] = _mm(wm[g], vp[rows, cols]) + bs_ref[g]
        u = pj(0)
        ug, u_cdf = _gelu(u)
        put(0, da * mix_s[...] * _gelu_grad(u, u_cdf))
        dmix = da * ug
        dmix_b = dmix.astype(BF16)
        for n in range(nb):
            rows = slice(n * SG, (n + 1) * SG)
            for g in range(N_GROUPS):
                cols = slice(g * SG, (g + 1) * SG)
                blk = dmix_b[rows, cols]
                dws_ref[g] += _nt(blk, vp[rows, cols])
                dbs_ref[g] += dmix[rows, cols]
                dvp_s[rows, cols] = _tn(wm[g], blk)
        dvp = dvp_s[...]
        st_ref[ST_GV:ST_GV + 1, :] += jnp.sum(dvp * vh, axis=0, keepdims=True)
        put(1, _rms_bwd(dvp * gv_ref[...], vh, rv) * _gelu_grad(v, v_cdf))

        bgate, cg, xs = pj(2), pj(3), pj(4)
        q = cg * xs
        has_prev = (i < nt - 1).astype(F32)
        halo = cgh_ref[...].astype(F32) * xsh_ref[...].astype(F32) * has_prev
        q2 = _shift_down(halo, q, 2)
        q1 = _shift_down(halo, q, 1)
        w0, w1, w2 = cw_ref[0:1, :], cw_ref[1:2, :], cw_ref[2:3, :]
        put(2, dc * (w0 * q2 + w1 * q1 + w2 * q))
        dcv = dc * bgate
        st_ref[ST_CONV:ST_CONV + 1, :] += jnp.sum(dcv * q2, axis=0, keepdims=True)
        st_ref[ST_CONV + 1:ST_CONV + 2, :] += jnp.sum(dcv * q1, axis=0, keepdims=True)
        st_ref[ST_CONV + 2:ST_CONV + 3, :] += jnp.sum(dcv * q, axis=0, keepdims=True)
        nxt = d_carry[...]
        dq = w2 * dcv + w1 * _shift_up(dcv, nxt, 1) + w0 * _shift_up(dcv, nxt, 2)
        d_carry[...] = dcv[:d_carry.shape[0], :]
        put(3, dq * xs)
        put(4, dq * cg)

    rev = lambda i: nt - 1 - i
    tile = lambda w: pl.BlockSpec((tt, w), lambda i: (rev(i), 0))
    halo_spec = lambda k: pl.BlockSpec((HALO, D), lambda i: (jnp.maximum(rev(i) * hb - 1, 0), k))
    res = lambda shape: pl.BlockSpec(shape, lambda i: (0,) * len(shape))
    out_shape = (jax.ShapeDtypeStruct((T, IN_COLS), BF16), jax.ShapeDtypeStruct((3, T, D), BF16),
                 jax.ShapeDtypeStruct((8, D), F32), jax.ShapeDtypeStruct((8, 2 * D), F32),
                 jax.ShapeDtypeStruct((N_GROUPS, SG, SG), F32), jax.ShapeDtypeStruct((N_GROUPS, SG, SG), F32))
    return _pcall(
        _after(body, 14, deps), name="mixer_bwd", grid=(nt,), out_shape=out_shape,
        in_specs=[tile(D), tile(IN_COLS), halo_spec(3), halo_spec(4), tile(D), tile(D),
                  _const_spec(b_gate.shape), _const_spec(g_v.shape), _const_spec(w_s.shape),
                  _const_spec(b_s3.shape), _const_spec(conv.shape),
                  _const_spec(w_pa.shape), _const_spec(w_pb.shape), _const_spec(w_o.shape)] + [_ANY] * len(deps),
        out_specs=[tile(IN_COLS), pl.BlockSpec((3, tt, D), lambda i: (0, rev(i), 0)), res((8, D)), res((8, 2 * D)),
                   res((N_GROUPS, SG, SG)), res((N_GROUPS, SG, SG))],
        scratch_shapes=[pltpu.VMEM((8, D), F32), pltpu.VMEM((tt, D), F32), pltpu.VMEM((tt, D), F32)],
        compiler_params=_params(("arbitrary",)),
    )(dx1, proj, proj, proj, pa, pb, b_gate, g_v, w_s, b_s3, conv, w_pa, w_pb, w_o, *deps)


def _in_proj_bwd(dproj, x, dx1, g_mix, w_in, tt, deps=()):
    T = x.shape[0]

    def body(dproj_ref, x_ref, dx1_ref, gmix_ref, win_ref, gx_ref, st_ref):
        @pl.when(pl.program_id(0) == 0)
        def _():
            st_ref[...] = jnp.zeros_like(st_ref)

        dh = _nt(dproj_ref[...], win_ref[...])
        xh, r = _rms(x_ref[...])
        st_ref[0:1, :] += jnp.sum(dh * xh, axis=0, keepdims=True)
        gx_ref[...] = dx1_ref[...] + _rms_bwd(dh * gmix_ref[...], xh, r)

    tile = lambda w: pl.BlockSpec((tt, w), lambda i: (i, 0))
    return _pcall(
        _after(body, 5, deps), name="in_proj_bwd", grid=(T // tt,),
        out_shape=(jax.ShapeDtypeStruct((T, D), F32), jax.ShapeDtypeStruct((8, D), F32)),
        in_specs=[tile(IN_COLS), tile(D), tile(D), _const_spec(g_mix.shape), _const_spec(w_in.shape)]
        + [_ANY] * len(deps),
        out_specs=[tile(D), pl.BlockSpec((8, D), lambda i: (0, 0))],
        compiler_params=_params(("arbitrary",)),
    )(dproj, x, dx1, g_mix, w_in, *deps)


def _weight_grad(name, act, dout, bc, tk, deps=()):
    T, n_in = act.shape
    n_out = dout.shape[1]
    nk = T // tk
    bi = min(n_in, D)

    def body(a_ref, d_ref, o_ref, acc):
        k = pl.program_id(2)

        @pl.when(k == 0)
        def _():
            acc[...] = jnp.zeros_like(acc)

        acc[...] += _tn(a_ref[...], d_ref[...])

        @pl.when(k == nk - 1)
        def _():
            o_ref[...] = acc[...].astype(o_ref.dtype)

    return _pcall(
        _after(body, 2, deps), name=name, grid=(n_in // bi, n_out // bc, nk),
        out_shape=jax.ShapeDtypeStruct((n_in, n_out), BF16),
        in_specs=[pl.BlockSpec((tk, bi), lambda i, j, k: (k, i)), pl.BlockSpec((tk, bc), lambda i, j, k: (k, j))]
        + [_ANY] * len(deps),
        out_specs=pl.BlockSpec((bi, bc), lambda i, j, k: (i, j)),
        scratch_shapes=[pltpu.VMEM((bi, bc), F32)],
        compiler_params=_params(("arbitrary", "arbitrary", "arbitrary")),
    )(act, dout, *deps)


def _weight_grad_stack(name, acts, douts, tk):
    n, T, _ = acts.shape
    nk = T // tk

    def body(a_ref, d_ref, o_ref, acc):
        k = pl.program_id(1)

        @pl.when(k == 0)
        def _():
            acc[...] = jnp.zeros_like(acc)

        acc[...] += _tn(a_ref[...], d_ref[...])

        @pl.when(k == nk - 1)
        def _():
            o_ref[...] = acc[...].astype(o_ref.dtype)

    tile = pl.BlockSpec((None, tk, D), lambda w, k: (w, k, 0))
    return _pcall(
        body, name=name, grid=(n, nk), out_shape=jax.ShapeDtypeStruct((n, D, D), BF16),
        in_specs=[tile, tile], out_specs=pl.BlockSpec((None, D, D), lambda w, k: (w, 0, 0)),
        scratch_shapes=[pltpu.VMEM((D, D), F32)],
        compiler_params=_params(("arbitrary", "arbitrary")),
    )(acts, douts)


def _adamw(w, g, m, v):
    m = ADAM_B1 * m + (1.0 - ADAM_B1) * g
    v = ADAM_B2 * v + (1.0 - ADAM_B2) * (g * g)
    m_hat = m / (1.0 - ADAM_B1 ** ADAM_STEP)
    v_hat = v / (1.0 - ADAM_B2 ** ADAM_STEP)
    delta = -ADAM_LR * (m_hat / (jnp.sqrt(v_hat) + ADAM_EPS) + ADAM_WD * w)
    return delta, m, v


def _slot_sum(ref, own=None, me=None):
    g = None
    for s in range(N_DEV):
        term = ref[s] if own is None else jnp.where(me == s, own, ref[s])
        g = term.astype(F32) if g is None else g + term.astype(F32)
    return g


def _reduce_adamw(me, items, steps):
    n = len(items)

    def body(me_ref, *refs):
        ins, outs = refs[:5 * n], refs[5 * n:]
        for j in range(n):
            own_ref, slot_ref, w_ref, m_ref, v_ref = ins[5 * j:5 * j + 5]
            g_out, d_out, m_out, v_out = outs[4 * j:4 * j + 4]
            g = _slot_sum(slot_ref, own_ref[...], me_ref[0])
            g_out[...] = g
            d_out[...], m_out[...], v_out[...] = _adamw(w_ref[...], g, m_ref[...], v_ref[...])

    in_specs, out_specs, out_shape, args = [], [], [], []
    for partial, own_block, slots, w, m, v, lead in items:
        rows, cols = w.shape
        tr = rows // steps
        tile = pl.BlockSpec((tr, cols), lambda i, me_ref: (i, 0))
        if lead is None:
            in_specs += [
                pl.BlockSpec((tr, cols), lambda i, me_ref, own_block=own_block, tr=tr: own_block(i, me_ref[0], tr)),
                pl.BlockSpec((N_DEV, tr, cols), lambda i, me_ref: (0, i, 0))]
        else:
            in_specs += [
                pl.BlockSpec((None, tr, cols),
                             lambda i, me_ref, own_block=own_block, tr=tr, lead=lead: (lead, *own_block(i, me_ref[0], tr))),
                pl.BlockSpec((N_DEV, None, tr, cols), lambda i, me_ref, lead=lead: (0, lead, i, 0))]
        in_specs += [tile, tile, tile]
        out_specs += [tile] * 4
        out_shape += [jax.ShapeDtypeStruct((rows, cols), F32)] * 4
        args += [partial, slots, w, m, v]
    res = _pcall(
        body, name="reduce_adamw", out_shape=out_shape,
        grid_spec=pltpu.PrefetchScalarGridSpec(
            num_scalar_prefetch=1, grid=(steps,), in_specs=in_specs, out_specs=out_specs),
        compiler_params=_params(("arbitrary",)),
    )(me, *args)
    return [res[4 * j:4 * j + 4] for j in range(n)]


SM_GMIX, SM_GV, SM_GFF, SM_GFIN, SM_LOSS, SM_BGATE, SM_BS, SM_ROWS = 0, 1, 2, 3, 4, 5, 7, 8


def _pack_small(st_ffn, st_mix, st_in, dbg, dbs, conv_rows):
    def body(ffn_ref, mix_ref, in_ref, dbg_ref, dbs_ref, sm_ref, conv_ref):
        sm_ref[SM_GMIX:SM_GMIX + 1, :] = in_ref[0:1, :]
        sm_ref[SM_GV:SM_GV + 1, :] = mix_ref[ST_GV:ST_GV + 1, :]
        sm_ref[SM_GFF:SM_GFF + 1, :] = ffn_ref[ST_GFF:ST_GFF + 1, :]
        sm_ref[SM_GFIN:SM_GFIN + 1, :] = ffn_ref[ST_GFIN:ST_GFIN + 1, :]
        sm_ref[SM_LOSS:SM_LOSS + 1, :] = ffn_ref[ST_LOSS:ST_LOSS + 1, :]
        sm_ref[SM_BGATE:SM_BGATE + 1, :] = dbg_ref[0:1, :D]
        sm_ref[SM_BGATE + 1:SM_BGATE + 2, :] = dbg_ref[0:1, D:]
        for g in range(N_GROUPS):
            sm_ref[SM_BS:SM_BS + 1, g * SG:(g + 1) * SG] = jnp.sum(dbs_ref[g].T, axis=0, keepdims=True)
        conv_ref[...] = jnp.zeros_like(conv_ref)
        for p in range(N_DEV):
            conv_ref[p, 0:conv_rows, :] = mix_ref[ST_CONV:ST_CONV + conv_rows, p * LANE:(p + 1) * LANE]

    return _pcall(
        body, name="pack_small",
        out_shape=(jax.ShapeDtypeStruct((SM_ROWS, D), F32), jax.ShapeDtypeStruct((N_DEV, 8, LANE), F32)),
        in_specs=[_VMEM] * 5, out_specs=[_VMEM] * 2, compiler_params=_params(),
    )(st_ffn, st_mix, st_in, dbg, dbs)


def _small_update(sm_slots, ws_own, ws_slots, conv_slots, params):
    flat = [a for t in params for a in t]

    def body(sm_ref, wso_ref, ws_ref, conv_ref, *refs):
        ins, outs = refs[:len(flat)], refs[len(flat):]
        loss_ref, outs = outs[0], outs[1:]
        _, me = _position()
        sm = _slot_sum(sm_ref)
        loss_ref[...] = (0.5 / D) * jnp.sum(sm[SM_LOSS:SM_LOSS + 1, :], axis=1, keepdims=True)
        grads = [sm[SM_GMIX:SM_GMIX + 1, :], sm[SM_GV:SM_GV + 1, :], sm[SM_GFF:SM_GFF + 1, :],
                 sm[SM_GFIN:SM_GFIN + 1, :], sm[SM_BGATE:SM_BGATE + 2, :], sm[SM_BS:SM_BS + 1, :],
                 _masked_ws(_slot_sum(ws_ref, wso_ref[...], me)), _slot_sum(conv_ref)]
        for n, g in enumerate(grads):
            w_ref, m_ref, v_ref = ins[3 * n:3 * n + 3]
            g_out, d_out, m_out, v_out = outs[4 * n:4 * n + 4]
            g_out[...] = g
            d_out[...], m_out[...], v_out[...] = _adamw(w_ref[...], g, m_ref[...], v_ref[...])

    out_shape = [jax.ShapeDtypeStruct((1, 1), F32)]
    for w, _, _ in params:
        out_shape += [jax.ShapeDtypeStruct(w.shape, F32)] * 4
    return _pcall(
        body, name="small_update", out_shape=out_shape,
        in_specs=[_VMEM] * (4 + len(flat)), out_specs=[_VMEM] * len(out_shape), compiler_params=_params(),
    )(sm_slots, ws_own, ws_slots, conv_slots, *flat)


def kernel(x, norm_mix_g, w_in, b_gate, norm_v_g, w_s, b_s, conv_w, w_proj_a, w_proj_b, w_out, norm_ff_g, w_ff1, w_ff2, norm_final_g, loss_target, m_norm_mix_g, m_w_in, m_b_gate, m_norm_v_g, m_w_s, m_b_s, m_conv_w, m_w_proj_a, m_w_proj_b, m_w_out, m_norm_ff_g, m_w_ff1, m_w_ff2, m_norm_final_g, v_norm_mix_g, v_w_in, v_b_gate, v_norm_v_g, v_w_s, v_b_s, v_conv_w, v_w_proj_a, v_w_proj_b, v_w_out, v_norm_ff_g, v_w_ff1, v_w_ff2, v_norm_final_g):
    T = x.shape[1]
    tt = min(256, T)
    tk = min(4096, T)
    conv_rows = conv_w.shape[1]

    pad8 = lambda a: jnp.pad(a, ((0, 8 - a.shape[0]), (0, 0)))
    xs = x.reshape(T, D)
    tgt = loss_target.reshape(T, D)
    g_mix, g_v, g_ff, g_fin = norm_mix_g, norm_v_g, norm_ff_g, norm_final_g.reshape(1, D)
    ws = w_s[0]
    bs3 = b_s.reshape(N_GROUPS, SG, 1)

    c_in, c_ff1 = w_in.shape[2], w_ff1.shape[2]
    r_ff2, r_p = w_ff2.shape[1], w_proj_a.shape[1]
    lane_view = lambda width: (lambda ref, p: _lane_block(ref, p, width))
    row_view = lambda rows: (lambda ref, p: _row_block(ref, p, rows))
    slots = lambda shape, dtype: lax.empty((N_DEV,) + shape, dtype)

    placed = _place_weights(w_in[0], w_ff1[0], w_ff2[0], w_proj_a[0], w_proj_b[0], w_out[0], pad8(conv_w[0]))
    mixer_views = [row_view(r_p), row_view(r_p), row_view(r_p), lane_view(LANE)]
    ffn_views = [lane_view(c_ff1), row_view(r_ff2)]
    pairs = ((0, 1), (2, 4), (3, 5), (6, 7))
    in_group = lambda refs, ks: _gather_entries(refs[:1], [lane_view(c_in)], tuple(k for k in ks if k))

    def gather_groups(refs):
        return ([in_group(refs, ks) for ks in pairs]
                + [_gather_entries(refs[1:5], mixer_views), _gather_entries(refs[5:], ffn_views)])

    sems, placed, g_token = _start_copies("gather_start", placed, gather_groups)
    _, me = _position()
    W_in = placed[0]
    tp = min(2048, T)
    proj = lax.empty((T, IN_COLS), BF16)
    for n, ks in enumerate(pairs):
        W_in, = _wait_copies(f"gather_wait_in_{n}", [W_in], sems[2 * n], sems[2 * n + 1],
                             lambda refs, ks=ks: in_group(refs, ks), proj if n else g_token)
        blk = jnp.stack([me ^ ks[0], me ^ ks[1]])
        if n == 0:
            proj, h = _in_proj_pair(f"in_proj_{n}", blk, xs, W_in, proj, c_in, tp, g_mix)
        else:
            proj = _in_proj_pair(f"in_proj_{n}", blk, h, W_in, proj, c_in, tp)
    n = len(pairs)
    PA, PB, WO, conv = _wait_copies(
        "gather_wait_mixer", placed[1:5], sems[2 * n], sems[2 * n + 1],
        lambda refs: _gather_entries(refs, mixer_views), proj)
    acm, pa, pb, x1 = _mixer_fwd(xs, proj, b_gate, g_v, ws, bs3, conv, PA, PB, WO, min(512, T))
    W1, W2 = _wait_copies(
        "gather_wait_ffn", placed[5:], sems[2 * n + 2], sems[2 * n + 3],
        lambda refs: _gather_entries(refs, ffn_views), x1)
    hf, s, dpre, dx2, dx1, st_ffn = _ffn_fwd_bwd(x1, tgt, g_ff, g_fin, W1, W2, min(512, T))

    d_ff2 = _weight_grad("dw_ff2", s, dx2, D, tk)
    d_ff1 = _weight_grad("dw_ff1", hf, dpre, D, tk)
    ff_views = [lane_view(c_ff1), row_view(r_ff2)]
    ff_sems, ff_arrays, ff_token = _start_copies(
        "scatter_start_ffn", [d_ff1, d_ff2, slots((D, c_ff1), BF16), slots((r_ff2, D), BF16)],
        lambda refs: [_scatter_entries(ff_views)(refs)])

    dproj, dstk, st_mix, dbg, dws, dbs = _mixer_bwd(
        dx1, proj, pa, pb, b_gate, g_v, ws, bs3, conv, PA, PB, WO, tt, deps=(ff_token,))
    d_p3 = _weight_grad_stack("dw_proj", acm, dstk, tk)

    def proj_entries(refs):
        d3, dws_ref, land3, land_ws = refs
        entries = []
        for w in range(3):
            entries += _to_peers(lambda pid, me, w=w: d3.at[w, pl.ds(pid * r_p, r_p), :],
                                 lambda me, w=w: land3.at[me, w])
        return entries + _to_peers(lambda pid, me: dws_ref, lambda me: land_ws.at[me])

    p_sems, p_arrays, p_token = _start_copies(
        "scatter_start_proj", [d_p3, dws, slots((3, r_p, D), BF16), slots(dws.shape, F32)],
        lambda refs: [proj_entries(refs)])

    d_in = _weight_grad("dw_in", h, dproj, D, tk, deps=(p_token,))
    in_views = [lane_view(c_in)]
    in_sems, in_arrays, in_token = _start_copies(
        "scatter_start_in", [d_in, slots((D, c_in), BF16)], lambda refs: [_scatter_entries(in_views)(refs)])
    grad_x, st_in = _in_proj_bwd(dproj, xs, dx1, g_mix, W_in, min(512, T), deps=(in_token,))

    small, d_conv = _pack_small(st_ffn, st_mix, st_in, dbg, dbs, conv_rows)
    r_conv, r_small = _small_exchange(d_conv, small)
    d_ff1, d_ff2, s_ff1, s_ff2 = _wait_copies(
        "scatter_wait_ffn", ff_arrays, ff_sems[0], ff_sems[1], _scatter_entries(ff_views), r_small)
    d_p3, dws, s_p3, s_ws = _wait_copies(
        "scatter_wait_proj", p_arrays, p_sems[0], p_sems[1], proj_entries, r_small)
    d_in, s_in = _wait_copies("scatter_wait_in", in_arrays, in_sems[0], in_sems[1], _scatter_entries(in_views), r_small)

    own_cols = lambda i, me, tr: (i, me)
    own_rows = lambda rows: (lambda i, me, tr: (me * (rows // tr) + i, 0))
    names = ["w_in", "w_ff1", "w_ff2", "w_proj_a", "w_proj_b", "w_out"]
    quads = _reduce_adamw(me.reshape(1), [
        (d_in, own_cols, s_in, w_in[0], m_w_in[0], v_w_in[0], None),
        (d_ff1, own_cols, s_ff1, w_ff1[0], m_w_ff1[0], v_w_ff1[0], None),
        (d_ff2, own_rows(r_ff2), s_ff2, w_ff2[0], m_w_ff2[0], v_w_ff2[0], None),
        (d_p3, own_rows(r_p), s_p3, w_proj_a[0], m_w_proj_a[0], v_w_proj_a[0], 0),
        (d_p3, own_rows(r_p), s_p3, w_proj_b[0], m_w_proj_b[0], v_w_proj_b[0], 1),
        (d_p3, own_rows(r_p), s_p3, w_out[0], m_w_out[0], v_w_out[0], 2),
    ], 8)
    big = dict(zip(names, quads))

    two = lambda a: a.reshape(2, D)
    one = lambda a: a.reshape(1, D)
    small_params = [
        (norm_mix_g, m_norm_mix_g, v_norm_mix_g),
        (norm_v_g, m_norm_v_g, v_norm_v_g),
        (norm_ff_g, m_norm_ff_g, v_norm_ff_g),
        (one(norm_final_g), one(m_norm_final_g), one(v_norm_final_g)),
        (two(b_gate), two(m_b_gate), two(v_b_gate)),
        (one(b_s), one(m_b_s), one(v_b_s)),
        (w_s[0], m_w_s[0], v_w_s[0]),
        (pad8(conv_w[0]), pad8(m_conv_w[0]), pad8(v_conv_w[0])),
    ]
    res = _small_update(r_small, dws, s_ws, r_conv, small_params)
    loss = res[0].reshape(())
    names = ["norm_mix_g", "norm_v_g", "norm_ff_g", "norm_final_g", "b_gate", "b_s", "w_s", "conv_w"]
    shapes = {"norm_mix_g": norm_mix_g.shape, "norm_v_g": norm_v_g.shape, "norm_ff_g": norm_ff_g.shape,
              "norm_final_g": norm_final_g.shape, "b_gate": b_gate.shape, "b_s": b_s.shape, "w_s": w_s.shape}
    out = {}
    for n, name in enumerate(names):
        quad = res[1 + 4 * n:5 + 4 * n]
        if name == "conv_w":
            out[name] = [q[:conv_rows][None] for q in quad]
        else:
            out[name] = [q.reshape(shapes[name]) for q in quad]
    for name, quad in big.items():
        out[name] = [q[None] for q in quad]

    order = ["norm_mix_g", "w_in", "b_gate", "norm_v_g", "w_s", "b_s", "conv_w", "w_proj_a", "w_proj_b", "w_out",
             "norm_ff_g", "w_ff1", "w_ff2", "norm_final_g"]
    grads = [out[n][0] for n in order]
    deltas = [out[n][1] for n in order]
    new_m = [out[n][2] for n in order]
    new_v = [out[n][3] for n in order]
    return (loss, grad_x.reshape(x.shape), *grads, *deltas, *new_m, *new_v)
```

```python
import math

import jax
import jax.numpy as jnp
from jax import lax
from jax.experimental import pallas as pl
from jax.experimental.pallas import tpu as pltpu

F32 = jnp.float32
BF16 = jnp.bfloat16

N_DEV = 8
D = 1024
D_FF = 4096
IN_COLS = 7 * D
SG = 128
N_GROUPS = 8
CHUNK = 64
EPS = 1e-6
HALO = 16
LANE = 128
VMEM_LIMIT = 62 * 1024 * 1024

ADAM_LR = 0.001
ADAM_B1 = 0.9
ADAM_B2 = 0.999
ADAM_EPS = 1e-08
ADAM_WD = 0.01
ADAM_STEP = 10

SQRT_HALF = math.sqrt(0.5)
PDF_EXP2_SCALE = -0.5 * math.log2(math.e)
PDF_EXP2_SHIFT = math.log2(1.0 / math.sqrt(2.0 * math.pi))

_REL = [(dx, dy, dc) for dx in (0, 1) for dy in (0, 1) for dc in (0, 1)]

_VMEM = pl.BlockSpec(memory_space=pltpu.VMEM)
_ANY = pl.BlockSpec(memory_space=pl.ANY)


def _pcall(body, **kw):
    return pl.pallas_call(body, **kw)


def _params(sem=None):
    if sem is None:
        return pltpu.CompilerParams(vmem_limit_bytes=VMEM_LIMIT)
    return pltpu.CompilerParams(dimension_semantics=sem, vmem_limit_bytes=VMEM_LIMIT)


def _const_spec(shape):
    nd = len(shape)
    return pl.BlockSpec(shape, lambda *_: (0,) * nd, pipeline_mode=pl.Buffered(1))


def _after(body, n_in, deps):
    def wrapped(*refs):
        return body(*refs[:n_in], *refs[n_in + len(deps):])
    return wrapped


def _mm(a, b):
    return jnp.dot(a, b, preferred_element_type=F32)


def _nt(a, b):
    return lax.dot_general(a, b, (((1,), (1,)), ((), ())), preferred_element_type=F32)


def _tn(a, b):
    return lax.dot_general(a, b, (((0,), (0,)), ((), ())), preferred_element_type=F32)


def _rms(x):
    r = lax.rsqrt(jnp.mean(x * x, axis=-1, keepdims=True) + EPS)
    return x * r, r


def _rms_bwd(dyg, xh, r):
    return r * (dyg - xh * jnp.mean(dyg * xh, axis=-1, keepdims=True))


def _gelu(x):
    cdf = 0.5 * (1.0 + lax.erf(x * SQRT_HALF))
    return x * cdf, cdf


def _gelu_grad(x, cdf):
    return cdf + x * jnp.exp2(x * x * PDF_EXP2_SCALE + PDF_EXP2_SHIFT)


def _masked_ws(ws):
    i = lax.broadcasted_iota(jnp.int32, (SG, SG), 0)
    j = lax.broadcasted_iota(jnp.int32, (SG, SG), 1)
    keep = jnp.logical_or(j < CHUNK, i >= CHUNK)
    return jnp.where(keep[None], ws, jnp.zeros_like(ws))


def _shift_down(halo, q, k):
    ext = jnp.concatenate([halo, q], axis=0)
    return pltpu.roll(ext, k, 0)[halo.shape[0]:]


def _shift_up(q, nxt, k):
    ext = jnp.concatenate([q, nxt], axis=0)
    return pltpu.roll(ext, ext.shape[0] - k, 0)[:q.shape[0]]


def _col(k):
    return slice(k * D, (k + 1) * D)


def _position():
    x, y, c = lax.axis_index("x"), lax.axis_index("y"), lax.axis_index("c")
    return (x, y, c), 4 * x + 2 * y + c


def _exchange(items, send_sems, recv_sems, local_sems):
    (x, y, c), me = _position()
    started = []
    for w, (src_of, dst_of) in enumerate(items):
        own = pltpu.make_async_copy(src_of(me), dst_of(me), local_sems.at[w])
        own.start()
        started.append(own)
        for k in range(1, N_DEV):
            dx, dy, dc = _REL[k]
            peer = (1 - x if dx else x, 1 - y if dy else y, 1 - c if dc else c)
            pid = 4 * peer[0] + 2 * peer[1] + peer[2]
            cp = pltpu.make_async_remote_copy(
                src_ref=src_of(pid), dst_ref=dst_of(me),
                send_sem=send_sems.at[w * N_DEV + k], recv_sem=recv_sems.at[w * N_DEV + k],
                device_id=peer, device_id_type=pl.DeviceIdType.MESH)
            cp.start()
            started.append(cp)
    for cp in started:
        cp.wait()


def _lane_block(ref, p, width):
    return ref.at[:, pl.ds(pl.multiple_of(p * width, LANE), width)]


def _row_block(ref, p, rows):
    return ref.at[pl.ds(p * rows, rows), :]


def _small_exchange(d_conv, small):
    def body(dconv, sm, rconv, rsm, send_sems, recv_sems, local_sems):
        items = [
            (lambda p: dconv.at[p], lambda p: rconv.at[p]),
            (lambda p: sm, lambda p: rsm.at[p]),
        ]
        _exchange(items, send_sems, recv_sems, local_sems)

    n_items = 2
    return _pcall(
        body, name="small_exchange",
        out_shape=(jax.ShapeDtypeStruct((N_DEV,) + d_conv.shape[1:], F32),
                   jax.ShapeDtypeStruct((N_DEV,) + small.shape, F32)),
        in_specs=[_ANY] * n_items, out_specs=[_ANY] * n_items,
        scratch_shapes=[pltpu.SemaphoreType.DMA((n_items * N_DEV,)), pltpu.SemaphoreType.DMA((n_items * N_DEV,)),
                        pltpu.SemaphoreType.DMA((n_items,))],
        compiler_params=_params(),
    )(d_conv, small)


_HBM = pl.BlockSpec(memory_space=pltpu.HBM)
_SEM = pl.BlockSpec(memory_space=pltpu.SEMAPHORE)
_EFFECT = pltpu.SideEffectType.DATAFLOW_SIDE_EFFECTING


REL_ORDER = (1, 2, 4, 3, 5, 6, 7)


def _to_peers(src_of, dst_of, order=REL_ORDER):
    return [(src_of, dst_of, k) for k in order]


def _remote_copies(entries, send_sems, recv_sems):
    (x, y, c), me = _position()
    copies = []
    for n, (src_of, dst_of, k) in enumerate(entries):
        dx, dy, dc = _REL[k]
        peer = (1 - x if dx else x, 1 - y if dy else y, 1 - c if dc else c)
        pid = 4 * peer[0] + 2 * peer[1] + peer[2]
        copies.append(pltpu.make_async_remote_copy(
            src_ref=src_of(pid, me), dst_ref=dst_of(me), send_sem=send_sems.at[n], recv_sem=recv_sems.at[n],
            device_id=peer, device_id_type=pl.DeviceIdType.MESH))
    return copies


def _start_copies(name, arrays, make_groups):
    n = len(arrays)
    sizes = [len(g) for g in make_groups([None] * n)]

    def body(*refs):
        sems, token = refs[n:n + 2 * len(sizes)], refs[-1]
        for g, entries in enumerate(make_groups(refs[:n])):
            for cp in _remote_copies(entries, sems[2 * g], sems[2 * g + 1]):
                cp.start()
        token[...] = jnp.zeros_like(token)

    out_shape = []
    for size in sizes:
        out_shape += [pltpu.SemaphoreType.DMA((size,))] * 2
    out_shape += [pltpu.HBM(a.shape, a.dtype) for a in arrays] + [jax.ShapeDtypeStruct((8, LANE), F32)]
    res = _pcall(
        body, name=name, out_shape=out_shape,
        in_specs=[_HBM] * n, out_specs=[_SEM] * (2 * len(sizes)) + [_HBM] * n + [_VMEM],
        input_output_aliases={i: 2 * len(sizes) + i for i in range(n)},
        compiler_params=pltpu.CompilerParams(has_side_effects=_EFFECT),
    )(*[pltpu.with_memory_space_constraint(a, pltpu.HBM) for a in arrays])
    return res[:2 * len(sizes)], res[2 * len(sizes):-1], res[-1]


def _wait_copies(name, arrays, send_sems, recv_sems, make_entries, after):
    n = len(arrays)

    def body(*refs):
        for cp in _remote_copies(make_entries(refs[:n]), refs[n], refs[n + 1]):
            cp.wait_send()
            cp.wait_recv()

    return _pcall(
        body, name=name, out_shape=[pltpu.HBM(a.shape, a.dtype) for a in arrays],
        in_specs=[_HBM] * n + [_SEM, _SEM, _ANY], out_specs=[_HBM] * n,
        input_output_aliases={i: i for i in range(n)},
        compiler_params=pltpu.CompilerParams(has_side_effects=_EFFECT),
    )(*arrays, send_sems, recv_sems, after)


def _place_weights(w_in, w_ff1, w_ff2, w_pa, w_pb, w_o, conv8):
    n_items = 7
    c_in, c_ff1 = w_in.shape[1], w_ff1.shape[1]
    r_ff2, r_p = w_ff2.shape[0], w_pa.shape[0]

    def body(win_ref, w1_ref, w2_ref, pa_ref, pb_ref, wo_ref, cw_ref,
             win_o, pa_o, pb_o, wo_o, cw_o, w1_o, w2_o,
             s_win, s_w1, s_w2, s_pa, s_pb, s_wo, sems):
        _, me = _position()
        for src, stage in ((win_ref, s_win), (w1_ref, s_w1), (w2_ref, s_w2),
                           (pa_ref, s_pa), (pb_ref, s_pb), (wo_ref, s_wo)):
            stage[...] = src[...].astype(BF16)
        pairs = [
            (s_win, _lane_block(win_o, me, c_in)), (s_pa, _row_block(pa_o, me, r_p)),
            (s_pb, _row_block(pb_o, me, r_p)), (s_wo, _row_block(wo_o, me, r_p)),
            (cw_ref, _lane_block(cw_o, me, conv8.shape[1])),
            (s_w1, _lane_block(w1_o, me, c_ff1)), (s_w2, _row_block(w2_o, me, r_ff2)),
        ]
        copies = [pltpu.make_async_copy(s, d, sems.at[n]) for n, (s, d) in enumerate(pairs)]
        for cp in copies:
            cp.start()
        for cp in copies:
            cp.wait()

    out_shape = (
        jax.ShapeDtypeStruct((D, N_DEV * c_in), BF16),
        jax.ShapeDtypeStruct((N_DEV * r_p, D), BF16),
        jax.ShapeDtypeStruct((N_DEV * r_p, D), BF16),
        jax.ShapeDtypeStruct((N_DEV * r_p, D), BF16),
        jax.ShapeDtypeStruct((conv8.shape[0], N_DEV * conv8.shape[1]), F32),
        jax.ShapeDtypeStruct((D, N_DEV * c_ff1), BF16),
        jax.ShapeDtypeStruct((N_DEV * r_ff2, D), BF16),
    )
    return _pcall(
        body, name="place_weights", out_shape=out_shape,
        in_specs=[_VMEM] * n_items, out_specs=[_ANY] * n_items,
        scratch_shapes=[pltpu.VMEM(w.shape, BF16) for w in (w_in, w_ff1, w_ff2, w_pa, w_pb, w_o)]
        + [pltpu.SemaphoreType.DMA((n_items,))],
        compiler_params=_params(),
    )(w_in, w_ff1, w_ff2, w_pa, w_pb, w_o, conv8)


def _gather_entries(refs, views, order=REL_ORDER):
    entries = []
    for r, v in zip(refs, views):
        entries += _to_peers(lambda pid, me, r=r, v=v: v(r, me), lambda me, r=r, v=v: v(r, me), order)
    return entries


def _scatter_entries(views):
    def make(refs):
        srcs, lands = refs[:len(views)], refs[len(views):]
        entries = []
        for r, v, l in zip(srcs, views, lands):
            entries += _to_peers(lambda pid, me, r=r, v=v: v(r, pid), lambda me, l=l: l.at[me])
        return entries
    return make


def _in_proj_pair(name, blk, act, w_in, proj, width, tt, g_mix=None):
    T = act.shape[0]
    nt = T // tt
    normed = g_mix is not None

    def body(blk_ref, act_ref, *refs):
        if normed:
            g_ref, wa_ref, wb_ref, _, proj_out, h_out, w_s, o_s, sems = refs
        else:
            wa_ref, wb_ref, _, proj_out, w_s, o_s, sems = refs
        i = pl.program_id(0)

        @pl.when(i == 0)
        def _():
            w_s[:, :width] = wa_ref[...]
            w_s[:, width:] = wb_ref[...]

        def copies(slot, step):
            rows = pl.ds(step * tt, tt)
            return [pltpu.make_async_copy(
                o_s.at[slot, :, half * width:(half + 1) * width],
                proj_out.at[rows, pl.ds(pl.multiple_of(blk_ref[half] * width, LANE), width)],
                sems.at[slot, half]) for half in range(2)]

        slot = i % 2

        @pl.when(i >= 2)
        def _():
            for cp in copies(slot, i - 2):
                cp.wait()

        if normed:
            xh, _ = _rms(act_ref[...])
            h = (xh * g_ref[...]).astype(BF16)
            h_out[...] = h
        else:
            h = act_ref[...]
        o_s[slot] = _mm(h, w_s[...]).astype(BF16)
        for cp in copies(slot, i):
            cp.start()

        @pl.when(i == nt - 1)
        def _():
            for cp in copies(slot, i) + (copies(1 - slot, i - 1) if nt > 1 else []):
                cp.wait()

    tile = pl.BlockSpec((tt, D), lambda i, b: (i, 0))
    w_spec = lambda half: pl.BlockSpec((D, width), lambda i, b: (0, b[half]), pipeline_mode=pl.Buffered(1))
    extra_in = [_const_spec(g_mix.shape)] if normed else []
    extra_args = [g_mix] if normed else []
    out_shape = [jax.ShapeDtypeStruct(proj.shape, proj.dtype)] + ([jax.ShapeDtypeStruct((T, D), BF16)] if normed else [])
    res = _pcall(
        body, name=name, out_shape=out_shape,
        grid_spec=pltpu.PrefetchScalarGridSpec(
            num_scalar_prefetch=1, grid=(nt,),
            in_specs=[tile] + extra_in + [w_spec(0), w_spec(1), _ANY],
            out_specs=[_ANY] + ([tile] if normed else []),
            scratch_shapes=[pltpu.VMEM((D, 2 * width), BF16), pltpu.VMEM((2, tt, 2 * width), BF16),
                            pltpu.SemaphoreType.DMA((2, 2))]),
        input_output_aliases={4 + len(extra_in): 0},
        compiler_params=_params(("arbitrary",)),
    )(blk, act, *extra_args, w_in, w_in, proj)
    return res if normed else res[0]


def _mixer_fwd(x, proj, b_gate, g_v, w_s, b_s3, conv, w_pa, w_pb, w_o, tt):
    T = x.shape[0]
    nt = T // tt
    nb = tt // SG

    def body(x_ref, proj_ref, bg_ref, gv_ref, ws_ref, bs_ref, cw_ref, pa_w, pb_w, wo_w,
             acm_ref, pa_ref, pb_ref, x1_ref, q_carry, mix_s):
        @pl.when(pl.program_id(0) == 0)
        def _():
            q_carry[...] = jnp.zeros_like(q_carry)

        def proj(k):
            return proj_ref[:, _col(k)].astype(F32)

        vg, _ = _gelu(proj(1))
        vh, _ = _rms(vg)
        vp = (vh * gv_ref[...]).astype(BF16)
        wm = _masked_ws(ws_ref[...]).astype(BF16)
        for n in range(nb):
            rows = slice(n * SG, (n + 1) * SG)
            for g in range(N_GROUPS):
                cols = slice(g * SG, (g + 1) * SG)
                mix_s[rows, cols] = _mm(wm[g], vp[rows, cols]) + bs_ref[g]
        ug, _ = _gelu(proj(0))
        a = (ug * mix_s[...]).astype(BF16)
        acm_ref[:, _col(0)] = a
        pa = _mm(a, pa_w[...])
        pa_ref[...] = pa.astype(BF16)
        m = jax.nn.sigmoid(proj(5) + bg_ref[:, :D]) * pa

        bgate = proj(2)
        q = proj(3) * proj(4)
        halo = q_carry[...]
        cv = cw_ref[0:1, :] * _shift_down(halo, q, 2) + cw_ref[1:2, :] * _shift_down(halo, q, 1) + cw_ref[2:3, :] * q
        q_carry[...] = q[tt - q_carry.shape[0]:, :]
        c = (bgate * cv).astype(BF16)
        acm_ref[:, _col(1)] = c
        pb = _mm(c, pb_w[...])
        pb_ref[...] = pb.astype(BF16)
        m = (m + jax.nn.sigmoid(proj(6) + bg_ref[:, D:]) * pb).astype(BF16)
        acm_ref[:, _col(2)] = m
        x1_ref[...] = x_ref[...] + _mm(m, wo_w[...])

    tile = lambda w: pl.BlockSpec((tt, w), lambda i: (i, 0))
    out_shape = ([jax.ShapeDtypeStruct((T, 3 * D), BF16)] + [jax.ShapeDtypeStruct((T, D), BF16)] * 2
                 + [jax.ShapeDtypeStruct((T, D), F32)])
    return _pcall(
        body, name="mixer_fwd", grid=(nt,), out_shape=out_shape,
        in_specs=[tile(D), tile(IN_COLS), _const_spec(b_gate.shape),
                  _const_spec(g_v.shape), _const_spec(w_s.shape), _const_spec(b_s3.shape), _const_spec(conv.shape),
                  _const_spec(w_pa.shape), _const_spec(w_pb.shape), _const_spec(w_o.shape)],
        out_specs=[tile(3 * D)] + [tile(D)] * 3,
        scratch_shapes=[pltpu.VMEM((8, D), F32), pltpu.VMEM((tt, D), F32)],
        compiler_params=_params(("arbitrary",)),
    )(x, proj, b_gate, g_v, w_s, b_s3, conv, w_pa, w_pb, w_o)


ST_GFIN, ST_GFF, ST_LOSS = 0, 1, 2


def _ffn_fwd_bwd(x1, tgt, g_ff, g_fin, w1, w2, tt):
    T = x1.shape[0]
    nt = T // tt
    nk = D_FF // D

    def body(x1_ref, tgt_ref, gff_ref, gfin_ref, w1_ref, w2_ref,
             hf_ref, s_ref, dpre_ref, dx2_ref, dx1_ref, st_ref, z_s):
        @pl.when(pl.program_id(0) == 0)
        def _():
            st_ref[...] = jnp.zeros_like(st_ref)

        x1 = x1_ref[...]
        xh1, r1 = _rms(x1)
        hf = (xh1 * gff_ref[...]).astype(BF16)
        hf_ref[...] = hf
        acc = jnp.zeros((tt, D), F32)
        for k in range(nk):
            z = jnp.maximum(_mm(hf, w1_ref[:, _col(k)]), 0.0)
            z_s[:, _col(k)] = z.astype(BF16)
            s = (z * z).astype(BF16)
            s_ref[:, _col(k)] = s
            acc = acc + _mm(s, w2_ref[_col(k), :])
        x2 = x1 + acc
        xh2, r2 = _rms(x2)
        diff = xh2 * gfin_ref[...] - tgt_ref[...]
        st_ref[ST_LOSS:ST_LOSS + 1, :] += jnp.sum(diff * diff, axis=0, keepdims=True)
        dy = diff * (1.0 / D)
        st_ref[ST_GFIN:ST_GFIN + 1, :] += jnp.sum(dy * xh2, axis=0, keepdims=True)
        dx2 = _rms_bwd(dy * gfin_ref[...], xh2, r2)
        dx2b = dx2.astype(BF16)
        dx2_ref[...] = dx2b
        dhf = jnp.zeros((tt, D), F32)
        for k in range(nk):
            dpre = (_nt(dx2b, w2_ref[_col(k), :]) * (2.0 * z_s[:, _col(k)].astype(F32))).astype(BF16)
            dpre_ref[:, _col(k)] = dpre
            dhf = dhf + _nt(dpre, w1_ref[:, _col(k)])
        st_ref[ST_GFF:ST_GFF + 1, :] += jnp.sum(dhf * xh1, axis=0, keepdims=True)
        dx1_ref[...] = dx2 + _rms_bwd(dhf * gff_ref[...], xh1, r1)

    tile = lambda w: pl.BlockSpec((tt, w), lambda i: (i, 0))
    out_shape = (jax.ShapeDtypeStruct((T, D), BF16), jax.ShapeDtypeStruct((T, D_FF), BF16),
                 jax.ShapeDtypeStruct((T, D_FF), BF16), jax.ShapeDtypeStruct((T, D), BF16),
                 jax.ShapeDtypeStruct((T, D), F32), jax.ShapeDtypeStruct((8, D), F32))
    return _pcall(
        body, name="ffn_fwd_bwd", grid=(nt,), out_shape=out_shape,
        in_specs=[tile(D), tile(D), _const_spec(g_ff.shape), _const_spec(g_fin.shape),
                  _const_spec(w1.shape), _const_spec(w2.shape)],
        out_specs=[tile(D), tile(D_FF), tile(D_FF), tile(D), tile(D), pl.BlockSpec((8, D), lambda i: (0, 0))],
        scratch_shapes=[pltpu.VMEM((tt, D_FF), BF16)],
        compiler_params=_params(("arbitrary",)),
    )(x1, tgt, g_ff, g_fin, w1, w2)


ST_GV, ST_CONV = 0, 1


def _mixer_bwd(dx1, proj, pa, pb, b_gate, g_v, w_s, b_s3, conv, w_pa, w_pb, w_o, tt, deps=()):
    T = dx1.shape[0]
    nt = T // tt
    nb = tt // SG
    hb = tt // HALO

    def body(dx1_ref, proj_ref, cgh_ref, xsh_ref, pa_ref, pb_ref,
             bg_ref, gv_ref, ws_ref, bs_ref, cw_ref, pa_w, pb_w, wo_w,
             dproj_ref, dstk_ref, st_ref, dbg_ref, dws_ref, dbs_ref, d_carry, mix_s, dvp_s):
        i = pl.program_id(0)

        @pl.when(i == 0)
        def _():
            st_ref[...] = jnp.zeros_like(st_ref)
            dbg_ref[...] = jnp.zeros_like(dbg_ref)
            dws_ref[...] = jnp.zeros_like(dws_ref)
            dbs_ref[...] = jnp.zeros_like(dbs_ref)
            d_carry[...] = jnp.zeros_like(d_carry)

        def pj(k):
            return proj_ref[:, _col(k)].astype(F32)

        def put(k, val):
            dproj_ref[:, _col(k)] = val.astype(BF16)

        dx1b = dx1_ref[...].astype(BF16)
        dstk_ref[:, _col(2)] = dx1b
        dm = _nt(dx1b, wo_w[...])
        s_a = jax.nn.sigmoid(pj(5) + bg_ref[:, :D])
        s_b = jax.nn.sigmoid(pj(6) + bg_ref[:, D:])
        dpa = dm * s_a
        dpb = dm * s_b
        dpa_b = dpa.astype(BF16)
        dpb_b = dpb.astype(BF16)
        dstk_ref[:, _col(0)] = dpa_b
        dstk_ref[:, _col(1)] = dpb_b
        dga = dpa * pa_ref[...].astype(F32) * (1.0 - s_a)
        dgb = dpb * pb_ref[...].astype(F32) * (1.0 - s_b)
        dbg_ref[0:1, :D] += jnp.sum(dga, axis=0, keepdims=True)
        dbg_ref[0:1, D:] += jnp.sum(dgb, axis=0, keepdims=True)
        put(5, dga)
        put(6, dgb)
        da = _nt(dpa_b, pa_w[...])
        dc = _nt(dpb_b, pb_w[...])

        v = pj(1)
        vg, v_cdf = _gelu(v)
        vh, rv = _rms(vg)
        vp = (vh * gv_ref[...]).astype(BF16)
        wm = _masked_ws(ws_ref[...]).astype(BF16)
        for n in range(nb):
            rows = slice(n * SG, (n + 1) * SG)
            for g in range(N_GROUPS):
                cols = slice(g * SG, (g + 1) * SG)
                mix_s[rows, cols] = _mm(wm[g], vp[rows, cols]) + bs_ref[g]
        u = pj(0)
        ug, u_cdf = _gelu(u)
        put(0, da * mix_s[...] * _gelu_grad(u, u_cdf))
        dmix = da * ug
        dmix_b = dmix.astype(BF16)
        for n in range(nb):
            rows = slice(n * SG, (n + 1) * SG)
            for g in range(N_GROUPS):
                cols = slice(g * SG, (g + 1) * SG)
                blk = dmix_b[rows, cols]
                dws_ref[g] += _nt(blk, vp[rows, cols])
                dbs_ref[g] += dmix[rows, cols]
                dvp_s[rows, cols] = _tn(wm[g], blk)
        dvp = dvp_s[...]
        st_ref[ST_GV:ST_GV + 1, :] += jnp.sum(dvp * vh, axis=0, keepdims=True)
        put(1, _rms_bwd(dvp * gv_ref[...], vh, rv) * _gelu_grad(v, v_cdf))

        bgate, cg, xs = pj(2), pj(3), pj(4)
        q = cg * xs
        has_prev = (i < nt - 1).astype(F32)
        halo = cgh_ref[...].astype(F32) * xsh_ref[...].astype(F32) * has_prev
        q2 = _shift_down(halo, q, 2)
        q1 = _shift_down(halo, q, 1)
        w0, w1, w2 = cw_ref[0:1, :], cw_ref[1:2, :], cw_ref[2:3, :]
        put(2, dc * (w0 * q2 + w1 * q1 + w2 * q))
        dcv = dc * bgate
        st_ref[ST_CONV:ST_CONV + 1, :] += jnp.sum(dcv * q2, axis=0, keepdims=True)
        st_ref[ST_CONV + 1:ST_CONV + 2, :] += jnp.sum(dcv * q1, axis=0, keepdims=True)
        st_ref[ST_CONV + 2:ST_CONV + 3, :] += jnp.sum(dcv * q, axis=0, keepdims=True)
        nxt = d_carry[...]
        dq = w2 * dcv + w1 * _shift_up(dcv, nxt, 1) + w0 * _shift_up(dcv, nxt, 2)
        d_carry[...] = dcv[:d_carry.shape[0], :]
        put(3, dq * xs)
        put(4, dq * cg)

    rev = lambda i: nt - 1 - i
    tile = lambda w: pl.BlockSpec((tt, w), lambda i: (rev(i), 0))
    halo_spec = lambda k: pl.BlockSpec((HALO, D), lambda i: (jnp.maximum(rev(i) * hb - 1, 0), k))
    res = lambda shape: pl.BlockSpec(shape, lambda i: (0,) * len(shape))
    out_shape = (jax.ShapeDtypeStruct((T, IN_COLS), BF16), jax.ShapeDtypeStruct((T, 3 * D), BF16),
                 jax.ShapeDtypeStruct((8, D), F32), jax.ShapeDtypeStruct((8, 2 * D), F32),
                 jax.ShapeDtypeStruct((N_GROUPS, SG, SG), F32), jax.ShapeDtypeStruct((N_GROUPS, SG, SG), F32))
    return _pcall(
        _after(body, 14, deps), name="mixer_bwd", grid=(nt,), out_shape=out_shape,
        in_specs=[tile(D), tile(IN_COLS), halo_spec(3), halo_spec(4), tile(D), tile(D),
                  _const_spec(b_gate.shape), _const_spec(g_v.shape), _const_spec(w_s.shape),
                  _const_spec(b_s3.shape), _const_spec(conv.shape),
                  _const_spec(w_pa.shape), _const_spec(w_pb.shape), _const_spec(w_o.shape)] + [_ANY] * len(deps),
        out_specs=[tile(IN_COLS), tile(3 * D), res((8, D)), res((8, 2 * D)),
                   res((N_GROUPS, SG, SG)), res((N_GROUPS, SG, SG))],
        scratch_shapes=[pltpu.VMEM((8, D), F32), pltpu.VMEM((tt, D), F32), pltpu.VMEM((tt, D), F32)],
        compiler_params=_params(("arbitrary",)),
    )(dx1, proj, proj, proj, pa, pb, b_gate, g_v, w_s, b_s3, conv, w_pa, w_pb, w_o, *deps)


def _in_proj_bwd(dproj, x, dx1, g_mix, w_in, tt, deps=()):
    T = x.shape[0]

    def body(dproj_ref, x_ref, dx1_ref, gmix_ref, win_ref, gx_ref, st_ref):
        @pl.when(pl.program_id(0) == 0)
        def _():
            st_ref[...] = jnp.zeros_like(st_ref)

        dh = _nt(dproj_ref[...], win_ref[...])
        xh, r = _rms(x_ref[...])
        st_ref[0:1, :] += jnp.sum(dh * xh, axis=0, keepdims=True)
        gx_ref[...] = dx1_ref[...] + _rms_bwd(dh * gmix_ref[...], xh, r)

    tile = lambda w: pl.BlockSpec((tt, w), lambda i: (i, 0))
    return _pcall(
        _after(body, 5, deps), name="in_proj_bwd", grid=(T // tt,),
        out_shape=(jax.ShapeDtypeStruct((T, D), F32), jax.ShapeDtypeStruct((8, D), F32)),
        in_specs=[tile(IN_COLS), tile(D), tile(D), _const_spec(g_mix.shape), _const_spec(w_in.shape)]
        + [_ANY] * len(deps),
        out_specs=[tile(D), pl.BlockSpec((8, D), lambda i: (0, 0))],
        compiler_params=_params(("arbitrary",)),
    )(dproj, x, dx1, g_mix, w_in, *deps)


def _weight_grad(name, act, dout, bc, tk, deps=()):
    T, n_in = act.shape
    n_out = dout.shape[1]
    nk = T // tk
    bi = min(n_in, D)

    def body(a_ref, d_ref, o_ref, acc):
        k = pl.program_id(2)

        @pl.when(k == 0)
        def _():
            acc[...] = jnp.zeros_like(acc)

        acc[...] += _tn(a_ref[...], d_ref[...])

        @pl.when(k == nk - 1)
        def _():
            o_ref[...] = acc[...].astype(o_ref.dtype)

    return _pcall(
        _after(body, 2, deps), name=name, grid=(n_in // bi, n_out // bc, nk),
        out_shape=jax.ShapeDtypeStruct((n_in, n_out), BF16),
        in_specs=[pl.BlockSpec((tk, bi), lambda i, j, k: (k, i)), pl.BlockSpec((tk, bc), lambda i, j, k: (k, j))]
        + [_ANY] * len(deps),
        out_specs=pl.BlockSpec((bi, bc), lambda i, j, k: (i, j)),
        scratch_shapes=[pltpu.VMEM((bi, bc), F32)],
        compiler_params=_params(("arbitrary", "arbitrary", "arbitrary")),
    )(act, dout, *deps)


def _weight_grad_stack(name, acts, douts, tk):
    T, n = acts.shape[0], acts.shape[1] // D
    nk = T // tk

    def body(a_ref, d_ref, o_ref, acc):
        k = pl.program_id(1)

        @pl.when(k == 0)
        def _():
            acc[...] = jnp.zeros_like(acc)

        acc[...] += _tn(a_ref[...], d_ref[...])

        @pl.when(k == nk - 1)
        def _():
            o_ref[...] = acc[...].astype(o_ref.dtype)

    tile = pl.BlockSpec((tk, D), lambda w, k: (k, w))
    return _pcall(
        body, name=name, grid=(n, nk), out_shape=jax.ShapeDtypeStruct((n, D, D), BF16),
        in_specs=[tile, tile], out_specs=pl.BlockSpec((None, D, D), lambda w, k: (w, 0, 0)),
        scratch_shapes=[pltpu.VMEM((D, D), F32)],
        compiler_params=_params(("arbitrary", "arbitrary")),
    )(acts, douts)


def _adamw(w, g, m, v):
    m = ADAM_B1 * m + (1.0 - ADAM_B1) * g
    v = ADAM_B2 * v + (1.0 - ADAM_B2) * (g * g)
    m_hat = m / (1.0 - ADAM_B1 ** ADAM_STEP)
    v_hat = v / (1.0 - ADAM_B2 ** ADAM_STEP)
    delta = -ADAM_LR * (m_hat / (jnp.sqrt(v_hat) + ADAM_EPS) + ADAM_WD * w)
    return delta, m, v


def _slot_sum(ref, own=None, me=None):
    g = None
    for s in range(N_DEV):
        term = ref[s] if own is None else jnp.where(me == s, own, ref[s])
        g = term.astype(F32) if g is None else g + term.astype(F32)
    return g


def _reduce_adamw(me, items, steps):
    n = len(items)

    def body(me_ref, *refs):
        ins, outs = refs[:5 * n], refs[5 * n:]
        for j in range(n):
            own_ref, slot_ref, w_ref, m_ref, v_ref = ins[5 * j:5 * j + 5]
            g_out, d_out, m_out, v_out = outs[4 * j:4 * j + 4]
            g = _slot_sum(slot_ref, own_ref[...], me_ref[0])
            g_out[...] = g
            d_out[...], m_out[...], v_out[...] = _adamw(w_ref[...], g, m_ref[...], v_ref[...])

    in_specs, out_specs, out_shape, args = [], [], [], []
    for partial, own_block, slots, w, m, v, lead in items:
        rows, cols = w.shape
        tr = rows // steps
        tile = pl.BlockSpec((tr, cols), lambda i, me_ref: (i, 0))
        if lead is None:
            in_specs += [
                pl.BlockSpec((tr, cols), lambda i, me_ref, own_block=own_block, tr=tr: own_block(i, me_ref[0], tr)),
                pl.BlockSpec((N_DEV, tr, cols), lambda i, me_ref: (0, i, 0))]
        else:
            in_specs += [
                pl.BlockSpec((None, tr, cols),
                             lambda i, me_ref, own_block=own_block, tr=tr, lead=lead: (lead, *own_block(i, me_ref[0], tr))),
                pl.BlockSpec((N_DEV, None, tr, cols), lambda i, me_ref, lead=lead: (0, lead, i, 0))]
        in_specs += [tile, tile, tile]
        out_specs += [tile] * 4
        out_shape += [jax.ShapeDtypeStruct((rows, cols), F32)] * 4
        args += [partial, slots, w, m, v]
    res = _pcall(
        body, name="reduce_adamw", out_shape=out_shape,
        grid_spec=pltpu.PrefetchScalarGridSpec(
            num_scalar_prefetch=1, grid=(steps,), in_specs=in_specs, out_specs=out_specs),
        compiler_params=_params(("arbitrary",)),
    )(me, *args)
    return [res[4 * j:4 * j + 4] for j in range(n)]


SM_GMIX, SM_GV, SM_GFF, SM_GFIN, SM_LOSS, SM_BGATE, SM_BS, SM_ROWS = 0, 1, 2, 3, 4, 5, 7, 8


def _pack_small(st_ffn, st_mix, st_in, dbg, dbs, conv_rows):
    def body(ffn_ref, mix_ref, in_ref, dbg_ref, dbs_ref, sm_ref, conv_ref):
        sm_ref[SM_GMIX:SM_GMIX + 1, :] = in_ref[0:1, :]
        sm_ref[SM_GV:SM_GV + 1, :] = mix_ref[ST_GV:ST_GV + 1, :]
        sm_ref[SM_GFF:SM_GFF + 1, :] = ffn_ref[ST_GFF:ST_GFF + 1, :]
        sm_ref[SM_GFIN:SM_GFIN + 1, :] = ffn_ref[ST_GFIN:ST_GFIN + 1, :]
        sm_ref[SM_LOSS:SM_LOSS + 1, :] = ffn_ref[ST_LOSS:ST_LOSS + 1, :]
        sm_ref[SM_BGATE:SM_BGATE + 1, :] = dbg_ref[0:1, :D]
        sm_ref[SM_BGATE + 1:SM_BGATE + 2, :] = dbg_ref[0:1, D:]
        for g in range(N_GROUPS):
            sm_ref[SM_BS:SM_BS + 1, g * SG:(g + 1) * SG] = jnp.sum(dbs_ref[g].T, axis=0, keepdims=True)
        conv_ref[...] = jnp.zeros_like(conv_ref)
        for p in range(N_DEV):
            conv_ref[p, 0:conv_rows, :] = mix_ref[ST_CONV:ST_CONV + conv_rows, p * LANE:(p + 1) * LANE]

    return _pcall(
        body, name="pack_small",
        out_shape=(jax.ShapeDtypeStruct((SM_ROWS, D), F32), jax.ShapeDtypeStruct((N_DEV, 8, LANE), F32)),
        in_specs=[_VMEM] * 5, out_specs=[_VMEM] * 2, compiler_params=_params(),
    )(st_ffn, st_mix, st_in, dbg, dbs)


def _small_update(sm_slots, ws_own, ws_slots, conv_slots, params):
    flat = [a for t in params for a in t]

    def body(sm_ref, wso_ref, ws_ref, conv_ref, *refs):
        ins, outs = refs[:len(flat)], refs[len(flat):]
        loss_ref, outs = outs[0], outs[1:]
        _, me = _position()
        sm = _slot_sum(sm_ref)
        loss_ref[...] = (0.5 / D) * jnp.sum(sm[SM_LOSS:SM_LOSS + 1, :], axis=1, keepdims=True)
        grads = [sm[SM_GMIX:SM_GMIX + 1, :], sm[SM_GV:SM_GV + 1, :], sm[SM_GFF:SM_GFF + 1, :],
                 sm[SM_GFIN:SM_GFIN + 1, :], sm[SM_BGATE:SM_BGATE + 2, :], sm[SM_BS:SM_BS + 1, :],
                 _masked_ws(_slot_sum(ws_ref, wso_ref[...], me)), _slot_sum(conv_ref)]
        for n, g in enumerate(grads):
            w_ref, m_ref, v_ref = ins[3 * n:3 * n + 3]
            g_out, d_out, m_out, v_out = outs[4 * n:4 * n + 4]
            g_out[...] = g
            d_out[...], m_out[...], v_out[...] = _adamw(w_ref[...], g, m_ref[...], v_ref[...])

    out_shape = [jax.ShapeDtypeStruct((1, 1), F32)]
    for w, _, _ in params:
        out_shape += [jax.ShapeDtypeStruct(w.shape, F32)] * 4
    return _pcall(
        body, name="small_update", out_shape=out_shape,
        in_specs=[_VMEM] * (4 + len(flat)), out_specs=[_VMEM] * len(out_shape), compiler_params=_params(),
    )(sm_slots, ws_own, ws_slots, conv_slots, *flat)


def kernel(x, norm_mix_g, w_in, b_gate, norm_v_g, w_s, b_s, conv_w, w_proj_a, w_proj_b, w_out, norm_ff_g, w_ff1, w_ff2, norm_final_g, loss_target, m_norm_mix_g, m_w_in, m_b_gate, m_norm_v_g, m_w_s, m_b_s, m_conv_w, m_w_proj_a, m_w_proj_b, m_w_out, m_norm_ff_g, m_w_ff1, m_w_ff2, m_norm_final_g, v_norm_mix_g, v_w_in, v_b_gate, v_norm_v_g, v_w_s, v_b_s, v_conv_w, v_w_proj_a, v_w_proj_b, v_w_out, v_norm_ff_g, v_w_ff1, v_w_ff2, v_norm_final_g):
    T = x.shape[1]
    tt = min(256, T)
    tk = min(4096, T)
    conv_rows = conv_w.shape[1]

    pad8 = lambda a: jnp.pad(a, ((0, 8 - a.shape[0]), (0, 0)))
    xs = x.reshape(T, D)
    tgt = loss_target.reshape(T, D)
    g_mix, g_v, g_ff, g_fin = norm_mix_g, norm_v_g, norm_ff_g, norm_final_g.reshape(1, D)
    ws = w_s[0]
    bs3 = b_s.reshape(N_GROUPS, SG, 1)

    c_in, c_ff1 = w_in.shape[2], w_ff1.shape[2]
    r_ff2, r_p = w_ff2.shape[1], w_proj_a.shape[1]
    lane_view = lambda width: (lambda ref, p: _lane_block(ref, p, width))
    row_view = lambda rows: (lambda ref, p: _row_block(ref, p, rows))
    slots = lambda shape, dtype: lax.empty((N_DEV,) + shape, dtype)

    placed = _place_weights(w_in[0], w_ff1[0], w_ff2[0], w_proj_a[0], w_proj_b[0], w_out[0], pad8(conv_w[0]))
    mixer_views = [row_view(r_p), row_view(r_p), row_view(r_p), lane_view(LANE)]
    ffn_views = [lane_view(c_ff1), row_view(r_ff2)]
    pairs = ((0, 1), (2, 4), (3, 5), (6, 7))
    in_group = lambda refs, ks: _gather_entries(refs[:1], [lane_view(c_in)], tuple(k for k in ks if k))

    def gather_groups(refs):
        return ([in_group(refs, ks) for ks in pairs]
                + [_gather_entries(refs[1:5], mixer_views), _gather_entries(refs[5:], ffn_views)])

    sems, placed, g_token = _start_copies("gather_start", placed, gather_groups)
    _, me = _position()
    W_in = placed[0]
    tp = min(2048, T)
    proj = lax.empty((T, IN_COLS), BF16)
    for n, ks in enumerate(pairs):
        W_in, = _wait_copies(f"gather_wait_in_{n}", [W_in], sems[2 * n], sems[2 * n + 1],
                             lambda refs, ks=ks: in_group(refs, ks), proj if n else g_token)
        blk = jnp.stack([me ^ ks[0], me ^ ks[1]])
        if n == 0:
            proj, h = _in_proj_pair(f"in_proj_{n}", blk, xs, W_in, proj, c_in, tp, g_mix)
        else:
            proj = _in_proj_pair(f"in_proj_{n}", blk, h, W_in, proj, c_in, tp)
    n = len(pairs)
    PA, PB, WO, conv = _wait_copies(
        "gather_wait_mixer", placed[1:5], sems[2 * n], sems[2 * n + 1],
        lambda refs: _gather_entries(refs, mixer_views), proj)
    acm, pa, pb, x1 = _mixer_fwd(xs, proj, b_gate, g_v, ws, bs3, conv, PA, PB, WO, min(512, T))
    W1, W2 = _wait_copies(
        "gather_wait_ffn", placed[5:], sems[2 * n + 2], sems[2 * n + 3],
        lambda refs: _gather_entries(refs, ffn_views), x1)
    hf, s, dpre, dx2, dx1, st_ffn = _ffn_fwd_bwd(x1, tgt, g_ff, g_fin, W1, W2, min(512, T))

    d_ff2 = _weight_grad("dw_ff2", s, dx2, D, tk)
    d_ff1 = _weight_grad("dw_ff1", hf, dpre, D, tk)
    ff_views = [lane_view(c_ff1), row_view(r_ff2)]
    ff_sems, ff_arrays, ff_token = _start_copies(
        "scatter_start_ffn", [d_ff1, d_ff2, slots((D, c_ff1), BF16), slots((r_ff2, D), BF16)],
        lambda refs: [_scatter_entries(ff_views)(refs)])

    dproj, dstk, st_mix, dbg, dws, dbs = _mixer_bwd(
        dx1, proj, pa, pb, b_gate, g_v, ws, bs3, conv, PA, PB, WO, tt, deps=(ff_token,))
    d_p3 = _weight_grad_stack("dw_proj", acm, dstk, tk)

    def proj_entries(refs):
        d3, dws_ref, land3, land_ws = refs
        entries = []
        for w in range(3):
            entries += _to_peers(lambda pid, me, w=w: d3.at[w, pl.ds(pid * r_p, r_p), :],
                                 lambda me, w=w: land3.at[me, w])
        return entries + _to_peers(lambda pid, me: dws_ref, lambda me: land_ws.at[me])

    p_sems, p_arrays, p_token = _start_copies(
        "scatter_start_proj", [d_p3, dws, slots((3, r_p, D), BF16), slots(dws.shape, F32)],
        lambda refs: [proj_entries(refs)])

    d_in = _weight_grad("dw_in", h, dproj, D, tk, deps=(p_token,))
    in_views = [lane_view(c_in)]
    in_sems, in_arrays, in_token = _start_copies(
        "scatter_start_in", [d_in, slots((D, c_in), BF16)], lambda refs: [_scatter_entries(in_views)(refs)])
    grad_x, st_in = _in_proj_bwd(dproj, xs, dx1, g_mix, W_in, min(512, T), deps=(in_token,))

    small, d_conv = _pack_small(st_ffn, st_mix, st_in, dbg, dbs, conv_rows)
    r_conv, r_small = _small_exchange(d_conv, small)
    d_ff1, d_ff2, s_ff1, s_ff2 = _wait_copies(
        "scatter_wait_ffn", ff_arrays, ff_sems[0], ff_sems[1], _scatter_entries(ff_views), r_small)
    d_p3, dws, s_p3, s_ws = _wait_copies(
        "scatter_wait_proj", p_arrays, p_sems[0], p_sems[1], proj_entries, r_small)
    d_in, s_in = _wait_copies("scatter_wait_in", in_arrays, in_sems[0], in_sems[1], _scatter_entries(in_views), r_small)

    own_cols = lambda i, me, tr: (i, me)
    own_rows = lambda rows: (lambda i, me, tr: (me * (rows // tr) + i, 0))
    names = ["w_in", "w_ff1", "w_ff2", "w_proj_a", "w_proj_b", "w_out"]
    quads = _reduce_adamw(me.reshape(1), [
        (d_in, own_cols, s_in, w_in[0], m_w_in[0], v_w_in[0], None),
        (d_ff1, own_cols, s_ff1, w_ff1[0], m_w_ff1[0], v_w_ff1[0], None),
        (d_ff2, own_rows(r_ff2), s_ff2, w_ff2[0], m_w_ff2[0], v_w_ff2[0], None),
        (d_p3, own_rows(r_p), s_p3, w_proj_a[0], m_w_proj_a[0], v_w_proj_a[0], 0),
        (d_p3, own_rows(r_p), s_p3, w_proj_b[0], m_w_proj_b[0], v_w_proj_b[0], 1),
        (d_p3, own_rows(r_p), s_p3, w_out[0], m_w_out[0], v_w_out[0], 2),
    ], 8)
    big = dict(zip(names, quads))

    two = lambda a: a.reshape(2, D)
    one = lambda a: a.reshape(1, D)
    small_params = [
        (norm_mix_g, m_norm_mix_g, v_norm_mix_g),
        (norm_v_g, m_norm_v_g, v_norm_v_g),
        (norm_ff_g, m_norm_ff_g, v_norm_ff_g),
        (one(norm_final_g), one(m_norm_final_g), one(v_norm_final_g)),
        (two(b_gate), two(m_b_gate), two(v_b_gate)),
        (one(b_s), one(m_b_s), one(v_b_s)),
        (w_s[0], m_w_s[0], v_w_s[0]),
        (pad8(conv_w[0]), pad8(m_conv_w[0]), pad8(v_conv_w[0])),
    ]
    res = _small_update(r_small, dws, s_ws, r_conv, small_params)
    loss = res[0].reshape(())
    names = ["norm_mix_g", "norm_v_g", "norm_ff_g", "norm_final_g", "b_gate", "b_s", "w_s", "conv_w"]
    shapes = {"norm_mix_g": norm_mix_g.shape, "norm_v_g": norm_v_g.shape, "norm_ff_g": norm_ff_g.shape,
              "norm_final_g": norm_final_g.shape, "b_gate": b_gate.shape, "b_s": b_s.shape, "w_s": w_s.shape}
    out = {}
    for n, name in enumerate(names):
        quad = res[1 + 4 * n:5 + 4 * n]
        if name == "conv_w":
            out[name] = [q[:conv_rows][None] for q in quad]
        else:
            out[name] = [q.reshape(shapes[name]) for q in quad]
    for name, quad in big.items():
        out[name] = [q[None] for q in quad]

    order = ["norm_mix_g", "w_in", "b_gate", "norm_v_g", "w_s", "b_s", "conv_w", "w_proj_a", "w_proj_b", "w_out",
             "norm_ff_g", "w_ff1", "w_ff2", "norm_final_g"]
    grads = [out[n][0] for n in order]
    deltas = [out[n][1] for n in order]
    new_m = [out[n][2] for n in order]
    new_v = [out[n][3] for n in order]
    return (loss, grad_x.reshape(x.shape), *grads, *deltas, *new_m, *new_v)
```

```python
import math

import jax
import jax.numpy as jnp
from jax import lax
from jax.experimental import pallas as pl
from jax.experimental.pallas import tpu as pltpu

F32 = jnp.float32
BF16 = jnp.bfloat16

N_DEV = 8
D = 1024
D_FF = 4096
IN_COLS = 7 * D
SG = 128
N_GROUPS = 8
CHUNK = 64
EPS = 1e-6
HALO = 16
LANE = 128
VMEM_LIMIT = 62 * 1024 * 1024

ADAM_LR = 0.001
ADAM_B1 = 0.9
ADAM_B2 = 0.999
ADAM_EPS = 1e-08
ADAM_WD = 0.01
ADAM_STEP = 10

SQRT_HALF = math.sqrt(0.5)
PDF_EXP2_SCALE = -0.5 * math.log2(math.e)
PDF_EXP2_SHIFT = math.log2(1.0 / math.sqrt(2.0 * math.pi))

_REL = [(dx, dy, dc) for dx in (0, 1) for dy in (0, 1) for dc in (0, 1)]

_VMEM = pl.BlockSpec(memory_space=pltpu.VMEM)
_ANY = pl.BlockSpec(memory_space=pl.ANY)


def _pcall(body, **kw):
    return pl.pallas_call(body, **kw)


def _params(sem=None):
    if sem is None:
        return pltpu.CompilerParams(vmem_limit_bytes=VMEM_LIMIT)
    return pltpu.CompilerParams(dimension_semantics=sem, vmem_limit_bytes=VMEM_LIMIT)


def _const_spec(shape):
    nd = len(shape)
    return pl.BlockSpec(shape, lambda *_: (0,) * nd, pipeline_mode=pl.Buffered(1))


def _after(body, n_in, deps):
    def wrapped(*refs):
        return body(*refs[:n_in], *refs[n_in + len(deps):])
    return wrapped


def _mm(a, b):
    return jnp.dot(a, b, preferred_element_type=F32)


def _nt(a, b):
    return lax.dot_general(a, b, (((1,), (1,)), ((), ())), preferred_element_type=F32)


def _tn(a, b):
    return lax.dot_general(a, b, (((0,), (0,)), ((), ())), preferred_element_type=F32)


def _rms(x):
    r = lax.rsqrt(jnp.mean(x * x, axis=-1, keepdims=True) + EPS)
    return x * r, r


def _rms_bwd(dyg, xh, r):
    return r * (dyg - xh * jnp.mean(dyg * xh, axis=-1, keepdims=True))


def _gelu(x):
    cdf = 0.5 * (1.0 + lax.erf(x * SQRT_HALF))
    return x * cdf, cdf


def _gelu_grad(x, cdf):
    return cdf + x * jnp.exp2(x * x * PDF_EXP2_SCALE + PDF_EXP2_SHIFT)


def _masked_ws(ws):
    i = lax.broadcasted_iota(jnp.int32, (SG, SG), 0)
    j = lax.broadcasted_iota(jnp.int32, (SG, SG), 1)
    keep = jnp.logical_or(j < CHUNK, i >= CHUNK)
    return jnp.where(keep[None], ws, jnp.zeros_like(ws))


def _shift_down(halo, q, k):
    ext = jnp.concatenate([halo, q], axis=0)
    return pltpu.roll(ext, k, 0)[halo.shape[0]:]


def _shift_up(q, nxt, k):
    ext = jnp.concatenate([q, nxt], axis=0)
    return pltpu.roll(ext, ext.shape[0] - k, 0)[:q.shape[0]]


def _col(k):
    return slice(k * D, (k + 1) * D)


def _position():
    x, y, c = lax.axis_index("x"), lax.axis_index("y"), lax.axis_index("c")
    return (x, y, c), 4 * x + 2 * y + c


def _exchange(items, send_sems, recv_sems, local_sems):
    (x, y, c), me = _position()
    started = []
    for w, (src_of, dst_of) in enumerate(items):
        own = pltpu.make_async_copy(src_of(me), dst_of(me), local_sems.at[w])
        own.start()
        started.append(own)
        for k in range(1, N_DEV):
            dx, dy, dc = _REL[k]
            peer = (1 - x if dx else x, 1 - y if dy else y, 1 - c if dc else c)
            pid = 4 * peer[0] + 2 * peer[1] + peer[2]
            cp = pltpu.make_async_remote_copy(
                src_ref=src_of(pid), dst_ref=dst_of(me),
                send_sem=send_sems.at[w * N_DEV + k], recv_sem=recv_sems.at[w * N_DEV + k],
                device_id=peer, device_id_type=pl.DeviceIdType.MESH)
            cp.start()
            started.append(cp)
    for cp in started:
        cp.wait()


def _lane_block(ref, p, width):
    return ref.at[:, pl.ds(pl.multiple_of(p * width, LANE), width)]


def _row_block(ref, p, rows):
    return ref.at[pl.ds(p * rows, rows), :]


def _small_exchange(d_conv, small):
    def body(dconv, sm, rconv, rsm, send_sems, recv_sems, local_sems):
        items = [
            (lambda p: dconv.at[p], lambda p: rconv.at[p]),
            (lambda p: sm, lambda p: rsm.at[p]),
        ]
        _exchange(items, send_sems, recv_sems, local_sems)

    n_items = 2
    return _pcall(
        body, name="small_exchange",
        out_shape=(jax.ShapeDtypeStruct((N_DEV,) + d_conv.shape[1:], F32),
                   jax.ShapeDtypeStruct((N_DEV,) + small.shape, F32)),
        in_specs=[_ANY] * n_items, out_specs=[_ANY] * n_items,
        scratch_shapes=[pltpu.SemaphoreType.DMA((n_items * N_DEV,)), pltpu.SemaphoreType.DMA((n_items * N_DEV,)),
                        pltpu.SemaphoreType.DMA((n_items,))],
        compiler_params=_params(),
    )(d_conv, small)


_HBM = pl.BlockSpec(memory_space=pltpu.HBM)
_SEM = pl.BlockSpec(memory_space=pltpu.SEMAPHORE)
_EFFECT = pltpu.SideEffectType.DATAFLOW_SIDE_EFFECTING


REL_ORDER = (1, 2, 4, 3, 5, 6, 7)


def _to_peers(src_of, dst_of, order=REL_ORDER):
    return [(src_of, dst_of, k) for k in order]


def _remote_copies(entries, send_sems, recv_sems):
    (x, y, c), me = _position()
    copies = []
    for n, (src_of, dst_of, k) in enumerate(entries):
        dx, dy, dc = _REL[k]
        peer = (1 - x if dx else x, 1 - y if dy else y, 1 - c if dc else c)
        pid = 4 * peer[0] + 2 * peer[1] + peer[2]
        copies.append(pltpu.make_async_remote_copy(
            src_ref=src_of(pid, me), dst_ref=dst_of(me), send_sem=send_sems.at[n], recv_sem=recv_sems.at[n],
            device_id=peer, device_id_type=pl.DeviceIdType.MESH))
    return copies


def _start_copies(name, arrays, make_groups):
    n = len(arrays)
    sizes = [len(g) for g in make_groups([None] * n)]

    def body(*refs):
        sems, token = refs[n:n + 2 * len(sizes)], refs[-1]
        for g, entries in enumerate(make_groups(refs[:n])):
            for cp in _remote_copies(entries, sems[2 * g], sems[2 * g + 1]):
                cp.start()
        token[...] = jnp.zeros_like(token)

    out_shape = []
    for size in sizes:
        out_shape += [pltpu.SemaphoreType.DMA((size,))] * 2
    out_shape += [pltpu.HBM(a.shape, a.dtype) for a in arrays] + [jax.ShapeDtypeStruct((8, LANE), F32)]
    res = _pcall(
        body, name=name, out_shape=out_shape,
        in_specs=[_HBM] * n, out_specs=[_SEM] * (2 * len(sizes)) + [_HBM] * n + [_VMEM],
        input_output_aliases={i: 2 * len(sizes) + i for i in range(n)},
        compiler_params=pltpu.CompilerParams(has_side_effects=_EFFECT),
    )(*[pltpu.with_memory_space_constraint(a, pltpu.HBM) for a in arrays])
    return res[:2 * len(sizes)], res[2 * len(sizes):-1], res[-1]


def _wait_copies(name, arrays, send_sems, recv_sems, make_entries, after):
    n = len(arrays)

    def body(*refs):
        for cp in _remote_copies(make_entries(refs[:n]), refs[n], refs[n + 1]):
            cp.wait_send()
            cp.wait_recv()

    return _pcall(
        body, name=name, out_shape=[pltpu.HBM(a.shape, a.dtype) for a in arrays],
        in_specs=[_HBM] * n + [_SEM, _SEM, _ANY], out_specs=[_HBM] * n,
        input_output_aliases={i: i for i in range(n)},
        compiler_params=pltpu.CompilerParams(has_side_effects=_EFFECT),
    )(*arrays, send_sems, recv_sems, after)


def _place_weights(w_in, w_ff1, w_ff2, w_pa, w_pb, w_o, conv8):
    n_items = 7
    c_in, c_ff1 = w_in.shape[1], w_ff1.shape[1]
    r_ff2, r_p = w_ff2.shape[0], w_pa.shape[0]

    def body(win_ref, w1_ref, w2_ref, pa_ref, pb_ref, wo_ref, cw_ref,
             win_o, pa_o, pb_o, wo_o, cw_o, w1_o, w2_o,
             s_win, s_w1, s_w2, s_pa, s_pb, s_wo, sems):
        _, me = _position()
        for src, stage in ((win_ref, s_win), (w1_ref, s_w1), (w2_ref, s_w2),
                           (pa_ref, s_pa), (pb_ref, s_pb), (wo_ref, s_wo)):
            stage[...] = src[...].astype(BF16)
        pairs = [
            (s_win, _lane_block(win_o, me, c_in)), (s_pa, _row_block(pa_o, me, r_p)),
            (s_pb, _row_block(pb_o, me, r_p)), (s_wo, _row_block(wo_o, me, r_p)),
            (cw_ref, _lane_block(cw_o, me, conv8.shape[1])),
            (s_w1, _lane_block(w1_o, me, c_ff1)), (s_w2, _row_block(w2_o, me, r_ff2)),
        ]
        copies = [pltpu.make_async_copy(s, d, sems.at[n]) for n, (s, d) in enumerate(pairs)]
        for cp in copies:
            cp.start()
        for cp in copies:
            cp.wait()

    out_shape = (
        jax.ShapeDtypeStruct((D, N_DEV * c_in), BF16),
        jax.ShapeDtypeStruct((N_DEV * r_p, D), BF16),
        jax.ShapeDtypeStruct((N_DEV * r_p, D), BF16),
        jax.ShapeDtypeStruct((N_DEV * r_p, D), BF16),
        jax.ShapeDtypeStruct((conv8.shape[0], N_DEV * conv8.shape[1]), F32),
        jax.ShapeDtypeStruct((D, N_DEV * c_ff1), BF16),
        jax.ShapeDtypeStruct((N_DEV * r_ff2, D), BF16),
    )
    return _pcall(
        body, name="place_weights", out_shape=out_shape,
        in_specs=[_VMEM] * n_items, out_specs=[_ANY] * n_items,
        scratch_shapes=[pltpu.VMEM(w.shape, BF16) for w in (w_in, w_ff1, w_ff2, w_pa, w_pb, w_o)]
        + [pltpu.SemaphoreType.DMA((n_items,))],
        compiler_params=_params(),
    )(w_in, w_ff1, w_ff2, w_pa, w_pb, w_o, conv8)


def _gather_entries(refs, views, order=REL_ORDER):
    entries = []
    for r, v in zip(refs, views):
        entries += _to_peers(lambda pid, me, r=r, v=v: v(r, me), lambda me, r=r, v=v: v(r, me), order)
    return entries


def _scatter_entries(views):
    def make(refs):
        srcs, lands = refs[:len(views)], refs[len(views):]
        entries = []
        for r, v, l in zip(srcs, views, lands):
            entries += _to_peers(lambda pid, me, r=r, v=v: v(r, pid), lambda me, l=l: l.at[me])
        return entries
    return make


def _in_proj_blocks(name, blk, act, w_in, proj, width, tt, g_mix=None):
    T = act.shape[0]
    nt = T // tt
    nb = blk.shape[0]
    normed = g_mix is not None

    def body(blk_ref, act_ref, *refs):
        if normed:
            g_ref, refs = refs[0], refs[1:]
        w_refs, refs = refs[:nb], refs[nb + 1:]
        if normed:
            proj_out, h_out, w_s, o_s, sems = refs
        else:
            proj_out, w_s, o_s, sems = refs
        i = pl.program_id(0)

        @pl.when(i == 0)
        def _():
            for part, w_ref in enumerate(w_refs):
                w_s[:, part * width:(part + 1) * width] = w_ref[...]

        def copies(slot, step):
            rows = pl.ds(step * tt, tt)
            return [pltpu.make_async_copy(
                o_s.at[slot, :, part * width:(part + 1) * width],
                proj_out.at[rows, pl.ds(pl.multiple_of(blk_ref[part] * width, LANE), width)],
                sems.at[slot, part]) for part in range(nb)]

        slot = i % 2

        @pl.when(i >= 2)
        def _():
            for cp in copies(slot, i - 2):
                cp.wait()

        if normed:
            xh, _ = _rms(act_ref[...])
            h = (xh * g_ref[...]).astype(BF16)
            h_out[...] = h
        else:
            h = act_ref[...]
        o_s[slot] = _mm(h, w_s[...]).astype(BF16)
        for cp in copies(slot, i):
            cp.start()

        @pl.when(i == nt - 1)
        def _():
            for cp in copies(slot, i) + (copies(1 - slot, i - 1) if nt > 1 else []):
                cp.wait()

    tile = pl.BlockSpec((tt, D), lambda i, b: (i, 0))
    w_spec = lambda part: pl.BlockSpec((D, width), lambda i, b: (0, b[part]), pipeline_mode=pl.Buffered(1))
    extra_in = [_const_spec(g_mix.shape)] if normed else []
    extra_args = [g_mix] if normed else []
    out_shape = [jax.ShapeDtypeStruct(proj.shape, proj.dtype)] + ([jax.ShapeDtypeStruct((T, D), BF16)] if normed else [])
    res = _pcall(
        body, name=name, out_shape=out_shape,
        grid_spec=pltpu.PrefetchScalarGridSpec(
            num_scalar_prefetch=1, grid=(nt,),
            in_specs=[tile] + extra_in + [w_spec(part) for part in range(nb)] + [_ANY],
            out_specs=[_ANY] + ([tile] if normed else []),
            scratch_shapes=[pltpu.VMEM((D, nb * width), BF16), pltpu.VMEM((2, tt, nb * width), BF16),
                            pltpu.SemaphoreType.DMA((2, nb))]),
        input_output_aliases={2 + nb + len(extra_in): 0},
        compiler_params=_params(("arbitrary",)),
    )(blk, act, *extra_args, *([w_in] * nb), proj)
    return res if normed else res[0]


def _mixer_fwd(x, proj, b_gate, g_v, w_s, b_s3, conv, w_pa, w_pb, w_o, tt):
    T = x.shape[0]
    nt = T // tt
    nb = tt // SG

    def body(x_ref, proj_ref, bg_ref, gv_ref, ws_ref, bs_ref, cw_ref, pa_w, pb_w, wo_w,
             acm_ref, pa_ref, pb_ref, x1_ref, q_carry, mix_s):
        @pl.when(pl.program_id(0) == 0)
        def _():
            q_carry[...] = jnp.zeros_like(q_carry)

        def proj(k):
            return proj_ref[:, _col(k)].astype(F32)

        vg, _ = _gelu(proj(1))
        vh, _ = _rms(vg)
        vp = (vh * gv_ref[...]).astype(BF16)
        wm = _masked_ws(ws_ref[...]).astype(BF16)
        for n in range(nb):
            rows = slice(n * SG, (n + 1) * SG)
            for g in range(N_GROUPS):
                cols = slice(g * SG, (g + 1) * SG)
                mix_s[rows, cols] = _mm(wm[g], vp[rows, cols]) + bs_ref[g]
        ug, _ = _gelu(proj(0))
        a = (ug * mix_s[...]).astype(BF16)
        acm_ref[:, _col(0)] = a
        pa = _mm(a, pa_w[...])
        pa_ref[...] = pa.astype(BF16)
        m = jax.nn.sigmoid(proj(5) + bg_ref[:, :D]) * pa

        bgate = proj(2)
        q = proj(3) * proj(4)
        halo = q_carry[...]
        cv = cw_ref[0:1, :] * _shift_down(halo, q, 2) + cw_ref[1:2, :] * _shift_down(halo, q, 1) + cw_ref[2:3, :] * q
        q_carry[...] = q[tt - q_carry.shape[0]:, :]
        c = (bgate * cv).astype(BF16)
        acm_ref[:, _col(1)] = c
        pb = _mm(c, pb_w[...])
        pb_ref[...] = pb.astype(BF16)
        m = (m + jax.nn.sigmoid(proj(6) + bg_ref[:, D:]) * pb).astype(BF16)
        acm_ref[:, _col(2)] = m
        x1_ref[...] = x_ref[...] + _mm(m, wo_w[...])

    tile = lambda w: pl.BlockSpec((tt, w), lambda i: (i, 0))
    out_shape = ([jax.ShapeDtypeStruct((T, 3 * D), BF16)] + [jax.ShapeDtypeStruct((T, D), BF16)] * 2
                 + [jax.ShapeDtypeStruct((T, D), F32)])
    return _pcall(
        body, name="mixer_fwd", grid=(nt,), out_shape=out_shape,
        in_specs=[tile(D), tile(IN_COLS), _const_spec(b_gate.shape),
                  _const_spec(g_v.shape), _const_spec(w_s.shape), _const_spec(b_s3.shape), _const_spec(conv.shape),
                  _const_spec(w_pa.shape), _const_spec(w_pb.shape), _const_spec(w_o.shape)],
        out_specs=[tile(3 * D)] + [tile(D)] * 3,
        scratch_shapes=[pltpu.VMEM((8, D), F32), pltpu.VMEM((tt, D), F32)],
        compiler_params=_params(("arbitrary",)),
    )(x, proj, b_gate, g_v, w_s, b_s3, conv, w_pa, w_pb, w_o)


ST_GFIN, ST_GFF, ST_LOSS = 0, 1, 2


def _ffn_fwd_bwd(x1, tgt, g_ff, g_fin, w1, w2, tt):
    T = x1.shape[0]
    nt = T // tt
    nk = D_FF // D

    def body(x1_ref, tgt_ref, gff_ref, gfin_ref, w1_ref, w2_ref,
             hf_ref, s_ref, dpre_ref, dx2_ref, dx1_ref, st_ref, z_s):
        @pl.when(pl.program_id(0) == 0)
        def _():
            st_ref[...] = jnp.zeros_like(st_ref)

        x1 = x1_ref[...]
        xh1, r1 = _rms(x1)
        hf = (xh1 * gff_ref[...]).astype(BF16)
        hf_ref[...] = hf
        acc = jnp.zeros((tt, D), F32)
        for k in range(nk):
            z = jnp.maximum(_mm(hf, w1_ref[:, _col(k)]), 0.0)
            z_s[:, _col(k)] = z.astype(BF16)
            s = (z * z).astype(BF16)
            s_ref[:, _col(k)] = s
            acc = acc + _mm(s, w2_ref[_col(k), :])
        x2 = x1 + acc
        xh2, r2 = _rms(x2)
        diff = xh2 * gfin_ref[...] - tgt_ref[...]
        st_ref[ST_LOSS:ST_LOSS + 1, :] += jnp.sum(diff * diff, axis=0, keepdims=True)
        dy = diff * (1.0 / D)
        st_ref[ST_GFIN:ST_GFIN + 1, :] += jnp.sum(dy * xh2, axis=0, keepdims=True)
        dx2 = _rms_bwd(dy * gfin_ref[...], xh2, r2)
        dx2b = dx2.astype(BF16)
        dx2_ref[...] = dx2b
        dhf = jnp.zeros((tt, D), F32)
        for k in range(nk):
            dpre = (_nt(dx2b, w2_ref[_col(k), :]) * (2.0 * z_s[:, _col(k)].astype(F32))).astype(BF16)
            dpre_ref[:, _col(k)] = dpre
            dhf = dhf + _nt(dpre, w1_ref[:, _col(k)])
        st_ref[ST_GFF:ST_GFF + 1, :] += jnp.sum(dhf * xh1, axis=0, keepdims=True)
        dx1_ref[...] = dx2 + _rms_bwd(dhf * gff_ref[...], xh1, r1)

    tile = lambda w: pl.BlockSpec((tt, w), lambda i: (i, 0))
    out_shape = (jax.ShapeDtypeStruct((T, D), BF16), jax.ShapeDtypeStruct((T, D_FF), BF16),
                 jax.ShapeDtypeStruct((T, D_FF), BF16), jax.ShapeDtypeStruct((T, D), BF16),
                 jax.ShapeDtypeStruct((T, D), F32), jax.ShapeDtypeStruct((8, D), F32))
    return _pcall(
        body, name="ffn_fwd_bwd", grid=(nt,), out_shape=out_shape,
        in_specs=[tile(D), tile(D), _const_spec(g_ff.shape), _const_spec(g_fin.shape),
                  _const_spec(w1.shape), _const_spec(w2.shape)],
        out_specs=[tile(D), tile(D_FF), tile(D_FF), tile(D), tile(D), pl.BlockSpec((8, D), lambda i: (0, 0))],
        scratch_shapes=[pltpu.VMEM((tt, D_FF), BF16)],
        compiler_params=_params(("arbitrary",)),
    )(x1, tgt, g_ff, g_fin, w1, w2)


ST_GV, ST_CONV = 0, 1


def _mixer_bwd(dx1, proj, pa, pb, b_gate, g_v, w_s, b_s3, conv, w_pa, w_pb, w_o, tt, deps=()):
    T = dx1.shape[0]
    nt = T // tt
    nb = tt // SG
    hb = tt // HALO

    def body(dx1_ref, proj_ref, cgh_ref, xsh_ref, pa_ref, pb_ref,
             bg_ref, gv_ref, ws_ref, bs_ref, cw_ref, pa_w, pb_w, wo_w,
             dproj_ref, dstk_ref, st_ref, dbg_ref, dws_ref, dbs_ref, d_carry, mix_s, dvp_s):
        i = pl.program_id(0)

        @pl.when(i == 0)
        def _():
            st_ref[...] = jnp.zeros_like(st_ref)
            dbg_ref[...] = jnp.zeros_like(dbg_ref)
            dws_ref[...] = jnp.zeros_like(dws_ref)
            dbs_ref[...] = jnp.zeros_like(dbs_ref)
            d_carry[...] = jnp.zeros_like(d_carry)

        def pj(k):
            return proj_ref[:, _col(k)].astype(F32)

        def put(k, val):
            dproj_ref[:, _col(k)] = val.astype(BF16)

        dx1b = dx1_ref[...].astype(BF16)
        dstk_ref[:, _col(2)] = dx1b
        dm = _nt(dx1b, wo_w[...])
        s_a = jax.nn.sigmoid(pj(5) + bg_ref[:, :D])
        s_b = jax.nn.sigmoid(pj(6) + bg_ref[:, D:])
        dpa = dm * s_a
        dpb = dm * s_b
        dpa_b = dpa.astype(BF16)
        dpb_b = dpb.astype(BF16)
        dstk_ref[:, _col(0)] = dpa_b
        dstk_ref[:, _col(1)] = dpb_b
        dga = dpa * pa_ref[...].astype(F32) * (1.0 - s_a)
        dgb = dpb * pb_ref[...].astype(F32) * (1.0 - s_b)
        dbg_ref[0:1, :D] += jnp.sum(dga, axis=0, keepdims=True)
        dbg_ref[0:1, D:] += jnp.sum(dgb, axis=0, keepdims=True)
        put(5, dga)
        put(6, dgb)
        da = _nt(dpa_b, pa_w[...])
        dc = _nt(dpb_b, pb_w[...])

        v = pj(1)
        vg, v_cdf = _gelu(v)
        vh, rv = _rms(vg)
        vp = (vh * gv_ref[...]).astype(BF16)
        wm = _masked_ws(ws_ref[...]).astype(BF16)
        for n in range(nb):
            rows = slice(n * SG, (n + 1) * SG)
            for g in range(N_GROUPS):
                cols = slice(g * SG, (g + 1) * SG)
                mix_s[rows, cols] = _mm(wm[g], vp[rows, cols]) + bs_ref[g]
        u = pj(0)
        ug, u_cdf = _gelu(u)
        put(0, da * mix_s[...] * _gelu_grad(u, u_cdf))
        dmix = da * ug
        dmix_b = dmix.astype(BF16)
        for n in range(nb):
            rows = slice(n * SG, (n + 1) * SG)
            for g in range(N_GROUPS):
                cols = slice(g * SG, (g + 1) * SG)
                blk = dmix_b[rows, cols]
                dws_ref[g] += _nt(blk, vp[rows, cols])
                dbs_ref[g] += dmix[rows, cols]
                dvp_s[rows, cols] = _tn(wm[g], blk)
        dvp = dvp_s[...]
        st_ref[ST_GV:ST_GV + 1, :] += jnp.sum(dvp * vh, axis=0, keepdims=True)
        put(1, _rms_bwd(dvp * gv_ref[...], vh, rv) * _gelu_grad(v, v_cdf))

        bgate, cg, xs = pj(2), pj(3), pj(4)
        q = cg * xs
        has_prev = (i < nt - 1).astype(F32)
        halo = cgh_ref[...].astype(F32) * xsh_ref[...].astype(F32) * has_prev
        q2 = _shift_down(halo, q, 2)
        q1 = _shift_down(halo, q, 1)
        w0, w1, w2 = cw_ref[0:1, :], cw_ref[1:2, :], cw_ref[2:3, :]
        put(2, dc * (w0 * q2 + w1 * q1 + w2 * q))
        dcv = dc * bgate
        st_ref[ST_CONV:ST_CONV + 1, :] += jnp.sum(dcv * q2, axis=0, keepdims=True)
        st_ref[ST_CONV + 1:ST_CONV + 2, :] += jnp.sum(dcv * q1, axis=0, keepdims=True)
        st_ref[ST_CONV + 2:ST_CONV + 3, :] += jnp.sum(dcv * q, axis=0, keepdims=True)
        nxt = d_carry[...]
        dq = w2 * dcv + w1 * _shift_up(dcv, nxt, 1) + w0 * _shift_up(dcv, nxt, 2)
        d_carry[...] = dcv[:d_carry.shape[0], :]
        put(3, dq * xs)
        put(4, dq * cg)

    rev = lambda i: nt - 1 - i
    tile = lambda w: pl.BlockSpec((tt, w), lambda i: (rev(i), 0))
    halo_spec = lambda k: pl.BlockSpec((HALO, D), lambda i: (jnp.maximum(rev(i) * hb - 1, 0), k))
    res = lambda shape: pl.BlockSpec(shape, lambda i: (0,) * len(shape))
    out_shape = (jax.ShapeDtypeStruct((T, IN_COLS), BF16), jax.ShapeDtypeStruct((T, 3 * D), BF16),
                 jax.ShapeDtypeStruct((8, D), F32), jax.ShapeDtypeStruct((8, 2 * D), F32),
                 jax.ShapeDtypeStruct((N_GROUPS, SG, SG), F32), jax.ShapeDtypeStruct((N_GROUPS, SG, SG), F32))
    return _pcall(
        _after(body, 14, deps), name="mixer_bwd", grid=(nt,), out_shape=out_shape,
        in_specs=[tile(D), tile(IN_COLS), halo_spec(3), halo_spec(4), tile(D), tile(D),
                  _const_spec(b_gate.shape), _const_spec(g_v.shape), _const_spec(w_s.shape),
                  _const_spec(b_s3.shape), _const_spec(conv.shape),
                  _const_spec(w_pa.shape), _const_spec(w_pb.shape), _const_spec(w_o.shape)] + [_ANY] * len(deps),
        out_specs=[tile(IN_COLS), tile(3 * D), res((8, D)), res((8, 2 * D)),
                   res((N_GROUPS, SG, SG)), res((N_GROUPS, SG, SG))],
        scratch_shapes=[pltpu.VMEM((8, D), F32), pltpu.VMEM((tt, D), F32), pltpu.VMEM((tt, D), F32)],
        compiler_params=_params(("arbitrary",)),
    )(dx1, proj, proj, proj, pa, pb, b_gate, g_v, w_s, b_s3, conv, w_pa, w_pb, w_o, *deps)


def _in_proj_bwd(dproj, x, dx1, g_mix, w_in, tt, deps=()):
    T = x.shape[0]

    def body(dproj_ref, x_ref, dx1_ref, gmix_ref, win_ref, gx_ref, st_ref):
        @pl.when(pl.program_id(0) == 0)
        def _():
            st_ref[...] = jnp.zeros_like(st_ref)

        dh = _nt(dproj_ref[...], win_ref[...])
        xh, r = _rms(x_ref[...])
        st_ref[0:1, :] += jnp.sum(dh * xh, axis=0, keepdims=True)
        gx_ref[...] = dx1_ref[...] + _rms_bwd(dh * gmix_ref[...], xh, r)

    tile = lambda w: pl.BlockSpec((tt, w), lambda i: (i, 0))
    return _pcall(
        _after(body, 5, deps), name="in_proj_bwd", grid=(T // tt,),
        out_shape=(jax.ShapeDtypeStruct((T, D), F32), jax.ShapeDtypeStruct((8, D), F32)),
        in_specs=[tile(IN_COLS), tile(D), tile(D), _const_spec(g_mix.shape), _const_spec(w_in.shape)]
        + [_ANY] * len(deps),
        out_specs=[tile(D), pl.BlockSpec((8, D), lambda i: (0, 0))],
        compiler_params=_params(("arbitrary",)),
    )(dproj, x, dx1, g_mix, w_in, *deps)


def _weight_grad(name, act, dout, bc, tk, deps=()):
    T, n_in = act.shape
    n_out = dout.shape[1]
    nk = T // tk
    bi = min(n_in, D)

    def body(a_ref, d_ref, o_ref, acc):
        k = pl.program_id(2)

        @pl.when(k == 0)
        def _():
            acc[...] = jnp.zeros_like(acc)

        acc[...] += _tn(a_ref[...], d_ref[...])

        @pl.when(k == nk - 1)
        def _():
            o_ref[...] = acc[...].astype(o_ref.dtype)

    return _pcall(
        _after(body, 2, deps), name=name, grid=(n_in // bi, n_out // bc, nk),
        out_shape=jax.ShapeDtypeStruct((n_in, n_out), BF16),
        in_specs=[pl.BlockSpec((tk, bi), lambda i, j, k: (k, i)), pl.BlockSpec((tk, bc), lambda i, j, k: (k, j))]
        + [_ANY] * len(deps),
        out_specs=pl.BlockSpec((bi, bc), lambda i, j, k: (i, j)),
        scratch_shapes=[pltpu.VMEM((bi, bc), F32)],
        compiler_params=_params(("arbitrary", "arbitrary", "arbitrary")),
    )(act, dout, *deps)


def _weight_grad_stack(name, acts, douts, tk):
    T, n = acts.shape[0], acts.shape[1] // D
    nk = T // tk

    def body(a_ref, d_ref, o_ref, acc):
        k = pl.program_id(1)

        @pl.when(k == 0)
        def _():
            acc[...] = jnp.zeros_like(acc)

        acc[...] += _tn(a_ref[...], d_ref[...])

        @pl.when(k == nk - 1)
        def _():
            o_ref[...] = acc[...].astype(o_ref.dtype)

    tile = pl.BlockSpec((tk, D), lambda w, k: (k, w))
    return _pcall(
        body, name=name, grid=(n, nk), out_shape=jax.ShapeDtypeStruct((n, D, D), BF16),
        in_specs=[tile, tile], out_specs=pl.BlockSpec((None, D, D), lambda w, k: (w, 0, 0)),
        scratch_shapes=[pltpu.VMEM((D, D), F32)],
        compiler_params=_params(("arbitrary", "arbitrary")),
    )(acts, douts)


def _adamw(w, g, m, v):
    m = ADAM_B1 * m + (1.0 - ADAM_B1) * g
    v = ADAM_B2 * v + (1.0 - ADAM_B2) * (g * g)
    m_hat = m / (1.0 - ADAM_B1 ** ADAM_STEP)
    v_hat = v / (1.0 - ADAM_B2 ** ADAM_STEP)
    delta = -ADAM_LR * (m_hat / (jnp.sqrt(v_hat) + ADAM_EPS) + ADAM_WD * w)
    return delta, m, v


def _slot_sum(ref, own=None, me=None):
    g = None
    for s in range(N_DEV):
        term = ref[s] if own is None else jnp.where(me == s, own, ref[s])
        g = term.astype(F32) if g is None else g + term.astype(F32)
    return g


def _reduce_adamw(me, items, steps):
    n = len(items)

    def body(me_ref, *refs):
        ins, outs = refs[:5 * n], refs[5 * n:]
        for j in range(n):
            own_ref, slot_ref, w_ref, m_ref, v_ref = ins[5 * j:5 * j + 5]
            g_out, d_out, m_out, v_out = outs[4 * j:4 * j + 4]
            g = _slot_sum(slot_ref, own_ref[...], me_ref[0])
            g_out[...] = g
            d_out[...], m_out[...], v_out[...] = _adamw(w_ref[...], g, m_ref[...], v_ref[...])

    in_specs, out_specs, out_shape, args = [], [], [], []
    for partial, own_block, slots, w, m, v, lead in items:
        rows, cols = w.shape
        tr = rows // steps
        tile = pl.BlockSpec((tr, cols), lambda i, me_ref: (i, 0))
        if lead is None:
            in_specs += [
                pl.BlockSpec((tr, cols), lambda i, me_ref, own_block=own_block, tr=tr: own_block(i, me_ref[0], tr)),
                pl.BlockSpec((N_DEV, tr, cols), lambda i, me_ref: (0, i, 0))]
        else:
            in_specs += [
                pl.BlockSpec((None, tr, cols),
                             lambda i, me_ref, own_block=own_block, tr=tr, lead=lead: (lead, *own_block(i, me_ref[0], tr))),
                pl.BlockSpec((N_DEV, None, tr, cols), lambda i, me_ref, lead=lead: (0, lead, i, 0))]
        in_specs += [tile, tile, tile]
        out_specs += [tile] * 4
        out_shape += [jax.ShapeDtypeStruct((rows, cols), F32)] * 4
        args += [partial, slots, w, m, v]
    res = _pcall(
        body, name="reduce_adamw", out_shape=out_shape,
        grid_spec=pltpu.PrefetchScalarGridSpec(
            num_scalar_prefetch=1, grid=(steps,), in_specs=in_specs, out_specs=out_specs),
        compiler_params=_params(("arbitrary",)),
    )(me, *args)
    return [res[4 * j:4 * j + 4] for j in range(n)]


SM_GMIX, SM_GV, SM_GFF, SM_GFIN, SM_LOSS, SM_BGATE, SM_BS, SM_ROWS = 0, 1, 2, 3, 4, 5, 7, 8


def _pack_small(st_ffn, st_mix, st_in, dbg, dbs, conv_rows):
    def body(ffn_ref, mix_ref, in_ref, dbg_ref, dbs_ref, sm_ref, conv_ref):
        sm_ref[SM_GMIX:SM_GMIX + 1, :] = in_ref[0:1, :]
        sm_ref[SM_GV:SM_GV + 1, :] = mix_ref[ST_GV:ST_GV + 1, :]
        sm_ref[SM_GFF:SM_GFF + 1, :] = ffn_ref[ST_GFF:ST_GFF + 1, :]
        sm_ref[SM_GFIN:SM_GFIN + 1, :] = ffn_ref[ST_GFIN:ST_GFIN + 1, :]
        sm_ref[SM_LOSS:SM_LOSS + 1, :] = ffn_ref[ST_LOSS:ST_LOSS + 1, :]
        sm_ref[SM_BGATE:SM_BGATE + 1, :] = dbg_ref[0:1, :D]
        sm_ref[SM_BGATE + 1:SM_BGATE + 2, :] = dbg_ref[0:1, D:]
        for g in range(N_GROUPS):
            sm_ref[SM_BS:SM_BS + 1, g * SG:(g + 1) * SG] = jnp.sum(dbs_ref[g].T, axis=0, keepdims=True)
        conv_ref[...] = jnp.zeros_like(conv_ref)
        for p in range(N_DEV):
            conv_ref[p, 0:conv_rows, :] = mix_ref[ST_CONV:ST_CONV + conv_rows, p * LANE:(p + 1) * LANE]

    return _pcall(
        body, name="pack_small",
        out_shape=(jax.ShapeDtypeStruct((SM_ROWS, D), F32), jax.ShapeDtypeStruct((N_DEV, 8, LANE), F32)),
        in_specs=[_VMEM] * 5, out_specs=[_VMEM] * 2, compiler_params=_params(),
    )(st_ffn, st_mix, st_in, dbg, dbs)


def _small_update(sm_slots, ws_own, ws_slots, conv_slots, params):
    flat = [a for t in params for a in t]

    def body(sm_ref, wso_ref, ws_ref, conv_ref, *refs):
        ins, outs = refs[:len(flat)], refs[len(flat):]
        loss_ref, outs = outs[0], outs[1:]
        _, me = _position()
        sm = _slot_sum(sm_ref)
        loss_ref[...] = (0.5 / D) * jnp.sum(sm[SM_LOSS:SM_LOSS + 1, :], axis=1, keepdims=True)
        grads = [sm[SM_GMIX:SM_GMIX + 1, :], sm[SM_GV:SM_GV + 1, :], sm[SM_GFF:SM_GFF + 1, :],
                 sm[SM_GFIN:SM_GFIN + 1, :], sm[SM_BGATE:SM_BGATE + 2, :], sm[SM_BS:SM_BS + 1, :],
                 _masked_ws(_slot_sum(ws_ref, wso_ref[...], me)), _slot_sum(conv_ref)]
        for n, g in enumerate(grads):
            w_ref, m_ref, v_ref = ins[3 * n:3 * n + 3]
            g_out, d_out, m_out, v_out = outs[4 * n:4 * n + 4]
            g_out[...] = g
            d_out[...], m_out[...], v_out[...] = _adamw(w_ref[...], g, m_ref[...], v_ref[...])

    out_shape = [jax.ShapeDtypeStruct((1, 1), F32)]
    for w, _, _ in params:
        out_shape += [jax.ShapeDtypeStruct(w.shape, F32)] * 4
    return _pcall(
        body, name="small_update", out_shape=out_shape,
        in_specs=[_VMEM] * (4 + len(flat)), out_specs=[_VMEM] * len(out_shape), compiler_params=_params(),
    )(sm_slots, ws_own, ws_slots, conv_slots, *flat)


def kernel(x, norm_mix_g, w_in, b_gate, norm_v_g, w_s, b_s, conv_w, w_proj_a, w_proj_b, w_out, norm_ff_g, w_ff1, w_ff2, norm_final_g, loss_target, m_norm_mix_g, m_w_in, m_b_gate, m_norm_v_g, m_w_s, m_b_s, m_conv_w, m_w_proj_a, m_w_proj_b, m_w_out, m_norm_ff_g, m_w_ff1, m_w_ff2, m_norm_final_g, v_norm_mix_g, v_w_in, v_b_gate, v_norm_v_g, v_w_s, v_b_s, v_conv_w, v_w_proj_a, v_w_proj_b, v_w_out, v_norm_ff_g, v_w_ff1, v_w_ff2, v_norm_final_g):
    T = x.shape[1]
    tt = min(256, T)
    tk = min(4096, T)
    conv_rows = conv_w.shape[1]

    pad8 = lambda a: jnp.pad(a, ((0, 8 - a.shape[0]), (0, 0)))
    xs = x.reshape(T, D)
    tgt = loss_target.reshape(T, D)
    g_mix, g_v, g_ff, g_fin = norm_mix_g, norm_v_g, norm_ff_g, norm_final_g.reshape(1, D)
    ws = w_s[0]
    bs3 = b_s.reshape(N_GROUPS, SG, 1)

    c_in, c_ff1 = w_in.shape[2], w_ff1.shape[2]
    r_ff2, r_p = w_ff2.shape[1], w_proj_a.shape[1]
    lane_view = lambda width: (lambda ref, p: _lane_block(ref, p, width))
    row_view = lambda rows: (lambda ref, p: _row_block(ref, p, rows))
    slots = lambda shape, dtype: lax.empty((N_DEV,) + shape, dtype)

    placed = _place_weights(w_in[0], w_ff1[0], w_ff2[0], w_proj_a[0], w_proj_b[0], w_out[0], pad8(conv_w[0]))
    mixer_views = [row_view(r_p), row_view(r_p), row_view(r_p), lane_view(LANE)]
    ffn_views = [lane_view(c_ff1), row_view(r_ff2)]
    pairs = ((0, 1), (2, 4, 3, 5), (6, 7))
    tiles = (2048, 1024, 2048)
    in_group = lambda refs, ks: _gather_entries(refs[:1], [lane_view(c_in)], tuple(k for k in ks if k))

    def gather_groups(refs):
        return ([in_group(refs, ks) for ks in pairs]
                + [_gather_entries(refs[1:5], mixer_views), _gather_entries(refs[5:], ffn_views)])

    sems, placed, g_token = _start_copies("gather_start", placed, gather_groups)
    _, me = _position()
    W_in = placed[0]
    proj = lax.empty((T, IN_COLS), BF16)
    for n, ks in enumerate(pairs):
        W_in, = _wait_copies(f"gather_wait_in_{n}", [W_in], sems[2 * n], sems[2 * n + 1],
                             lambda refs, ks=ks: in_group(refs, ks), proj if n else g_token)
        blk = jnp.stack([me ^ k for k in ks])
        tp = min(tiles[n], T)
        if n == 0:
            proj, h = _in_proj_blocks(f"in_proj_{n}", blk, xs, W_in, proj, c_in, tp, g_mix)
        else:
            proj = _in_proj_blocks(f"in_proj_{n}", blk, h, W_in, proj, c_in, tp)
    n = len(pairs)
    PA, PB, WO, conv = _wait_copies(
        "gather_wait_mixer", placed[1:5], sems[2 * n], sems[2 * n + 1],
        lambda refs: _gather_entries(refs, mixer_views), proj)
    acm, pa, pb, x1 = _mixer_fwd(xs, proj, b_gate, g_v, ws, bs3, conv, PA, PB, WO, min(512, T))
    W1, W2 = _wait_copies(
        "gather_wait_ffn", placed[5:], sems[2 * n + 2], sems[2 * n + 3],
        lambda refs: _gather_entries(refs, ffn_views), x1)
    hf, s, dpre, dx2, dx1, st_ffn = _ffn_fwd_bwd(x1, tgt, g_ff, g_fin, W1, W2, min(512, T))

    d_ff2 = _weight_grad("dw_ff2", s, dx2, D, tk)
    d_ff1 = _weight_grad("dw_ff1", hf, dpre, D, tk)
    ff_views = [lane_view(c_ff1), row_view(r_ff2)]
    ff_sems, ff_arrays, ff_token = _start_copies(
        "scatter_start_ffn", [d_ff1, d_ff2, slots((D, c_ff1), BF16), slots((r_ff2, D), BF16)],
        lambda refs: [_scatter_entries(ff_views)(refs)])

    dproj, dstk, st_mix, dbg, dws, dbs = _mixer_bwd(
        dx1, proj, pa, pb, b_gate, g_v, ws, bs3, conv, PA, PB, WO, tt, deps=(ff_token,))
    d_p3 = _weight_grad_stack("dw_proj", acm, dstk, tk)

    def proj_entries(refs):
        d3, dws_ref, land3, land_ws = refs
        entries = []
        for w in range(3):
            entries += _to_peers(lambda pid, me, w=w: d3.at[w, pl.ds(pid * r_p, r_p), :],
                                 lambda me, w=w: land3.at[me, w])
        return entries + _to_peers(lambda pid, me: dws_ref, lambda me: land_ws.at[me])

    p_sems, p_arrays, p_token = _start_copies(
        "scatter_start_proj", [d_p3, dws, slots((3, r_p, D), BF16), slots(dws.shape, F32)],
        lambda refs: [proj_entries(refs)])

    d_in = _weight_grad("dw_in", h, dproj, D, tk, deps=(p_token,))
    in_views = [lane_view(c_in)]
    in_sems, in_arrays, in_token = _start_copies(
        "scatter_start_in", [d_in, slots((D, c_in), BF16)], lambda refs: [_scatter_entries(in_views)(refs)])
    grad_x, st_in = _in_proj_bwd(dproj, xs, dx1, g_mix, W_in, min(512, T), deps=(in_token,))

    small, d_conv = _pack_small(st_ffn, st_mix, st_in, dbg, dbs, conv_rows)
    r_conv, r_small = _small_exchange(d_conv, small)
    d_ff1, d_ff2, s_ff1, s_ff2 = _wait_copies(
        "scatter_wait_ffn", ff_arrays, ff_sems[0], ff_sems[1], _scatter_entries(ff_views), r_small)
    d_p3, dws, s_p3, s_ws = _wait_copies(
        "scatter_wait_proj", p_arrays, p_sems[0], p_sems[1], proj_entries, r_small)
    d_in, s_in = _wait_copies("scatter_wait_in", in_arrays, in_sems[0], in_sems[1], _scatter_entries(in_views), r_small)

    own_cols = lambda i, me, tr: (i, me)
    own_rows = lambda rows: (lambda i, me, tr: (me * (rows // tr) + i, 0))
    names = ["w_in", "w_ff1", "w_ff2", "w_proj_a", "w_proj_b", "w_out"]
    quads = _reduce_adamw(me.reshape(1), [
        (d_in, own_cols, s_in, w_in[0], m_w_in[0], v_w_in[0], None),
        (d_ff1, own_cols, s_ff1, w_ff1[0], m_w_ff1[0], v_w_ff1[0], None),
        (d_ff2, own_rows(r_ff2), s_ff2, w_ff2[0], m_w_ff2[0], v_w_ff2[0], None),
        (d_p3, own_rows(r_p), s_p3, w_proj_a[0], m_w_proj_a[0], v_w_proj_a[0], 0),
        (d_p3, own_rows(r_p), s_p3, w_proj_b[0], m_w_proj_b[0], v_w_proj_b[0], 1),
        (d_p3, own_rows(r_p), s_p3, w_out[0], m_w_out[0], v_w_out[0], 2),
    ], 8)
    big = dict(zip(names, quads))

    two = lambda a: a.reshape(2, D)
    one = lambda a: a.reshape(1, D)
    small_params = [
        (norm_mix_g, m_norm_mix_g, v_norm_mix_g),
        (norm_v_g, m_norm_v_g, v_norm_v_g),
        (norm_ff_g, m_norm_ff_g, v_norm_ff_g),
        (one(norm_final_g), one(m_norm_final_g), one(v_norm_final_g)),
        (two(b_gate), two(m_b_gate), two(v_b_gate)),
        (one(b_s), one(m_b_s), one(v_b_s)),
        (w_s[0], m_w_s[0], v_w_s[0]),
        (pad8(conv_w[0]), pad8(m_conv_w[0]), pad8(v_conv_w[0])),
    ]
    res = _small_update(r_small, dws, s_ws, r_conv, small_params)
    loss = res[0].reshape(())
    names = ["norm_mix_g", "norm_v_g", "norm_ff_g", "norm_final_g", "b_gate", "b_s", "w_s", "conv_w"]
    shapes = {"norm_mix_g": norm_mix_g.shape, "norm_v_g": norm_v_g.shape, "norm_ff_g": norm_ff_g.shape,
              "norm_final_g": norm_final_g.shape, "b_gate": b_gate.shape, "b_s": b_s.shape, "w_s": w_s.shape}
    out = {}
    for n, name in enumerate(names):
        quad = res[1 + 4 * n:5 + 4 * n]
        if name == "conv_w":
            out[name] = [q[:conv_rows][None] for q in quad]
        else:
            out[name] = [q.reshape(shapes[name]) for q in quad]
    for name, quad in big.items():
        out[name] = [q[None] for q in quad]

    order = ["norm_mix_g", "w_in", "b_gate", "norm_v_g", "w_s", "b_s", "conv_w", "w_proj_a", "w_proj_b", "w_out",
             "norm_ff_g", "w_ff1", "w_ff2", "norm_final_g"]
    grads = [out[n][0] for n in order]
    deltas = [out[n][1] for n in order]
    new_m = [out[n][2] for n in order]
    new_v = [out[n][3] for n in order]
    return (loss, grad_x.reshape(x.shape), *grads, *deltas, *new_m, *new_v)
```

```python
import math

import jax
import jax.numpy as jnp
from jax import lax
from jax.experimental import pallas as pl
from jax.experimental.pallas import tpu as pltpu

F32 = jnp.float32
BF16 = jnp.bfloat16

N_DEV = 8
D = 1024
D_FF = 4096
IN_COLS = 7 * D
SG = 128
N_GROUPS = 8
CHUNK = 64
EPS = 1e-6
HALO = 16
LANE = 128
VMEM_LIMIT = 62 * 1024 * 1024

ADAM_LR = 0.001
ADAM_B1 = 0.9
ADAM_B2 = 0.999
ADAM_EPS = 1e-08
ADAM_WD = 0.01
ADAM_STEP = 10

SQRT_HALF = math.sqrt(0.5)
PDF_EXP2_SCALE = -0.5 * math.log2(math.e)
PDF_EXP2_SHIFT = math.log2(1.0 / math.sqrt(2.0 * math.pi))

_REL = [(dx, dy, dc) for dx in (0, 1) for dy in (0, 1) for dc in (0, 1)]

_VMEM = pl.BlockSpec(memory_space=pltpu.VMEM)
_ANY = pl.BlockSpec(memory_space=pl.ANY)


def _pcall(body, **kw):
    return pl.pallas_call(body, **kw)


def _params(sem=None):
    if sem is None:
        return pltpu.CompilerParams(vmem_limit_bytes=VMEM_LIMIT)
    return pltpu.CompilerParams(dimension_semantics=sem, vmem_limit_bytes=VMEM_LIMIT)


def _const_spec(shape):
    nd = len(shape)
    return pl.BlockSpec(shape, lambda *_: (0,) * nd, pipeline_mode=pl.Buffered(1))


def _after(body, n_in, deps):
    def wrapped(*refs):
        return body(*refs[:n_in], *refs[n_in + len(deps):])
    return wrapped


def _mm(a, b):
    return jnp.dot(a, b, preferred_element_type=F32)


def _nt(a, b):
    return lax.dot_general(a, b, (((1,), (1,)), ((), ())), preferred_element_type=F32)


def _tn(a, b):
    return lax.dot_general(a, b, (((0,), (0,)), ((), ())), preferred_element_type=F32)


def _rms(x):
    r = lax.rsqrt(jnp.mean(x * x, axis=-1, keepdims=True) + EPS)
    return x * r, r


def _rms_bwd(dyg, xh, r):
    return r * (dyg - xh * jnp.mean(dyg * xh, axis=-1, keepdims=True))


def _gelu(x):
    cdf = 0.5 * (1.0 + lax.erf(x * SQRT_HALF))
    return x * cdf, cdf


def _gelu_grad(x, cdf):
    return cdf + x * jnp.exp2(x * x * PDF_EXP2_SCALE + PDF_EXP2_SHIFT)


def _masked_ws(ws):
    i = lax.broadcasted_iota(jnp.int32, (SG, SG), 0)
    j = lax.broadcasted_iota(jnp.int32, (SG, SG), 1)
    keep = jnp.logical_or(j < CHUNK, i >= CHUNK)
    return jnp.where(keep[None], ws, jnp.zeros_like(ws))


def _shift_down(halo, q, k):
    ext = jnp.concatenate([halo, q], axis=0)
    return pltpu.roll(ext, k, 0)[halo.shape[0]:]


def _shift_up(q, nxt, k):
    ext = jnp.concatenate([q, nxt], axis=0)
    return pltpu.roll(ext, ext.shape[0] - k, 0)[:q.shape[0]]


def _col(k):
    return slice(k * D, (k + 1) * D)


def _position():
    x, y, c = lax.axis_index("x"), lax.axis_index("y"), lax.axis_index("c")
    return (x, y, c), 4 * x + 2 * y + c


def _exchange(items, send_sems, recv_sems, local_sems):
    (x, y, c), me = _position()
    started = []
    for w, (src_of, dst_of) in enumerate(items):
        own = pltpu.make_async_copy(src_of(me), dst_of(me), local_sems.at[w])
        own.start()
        started.append(own)
        for k in range(1, N_DEV):
            dx, dy, dc = _REL[k]
            peer = (1 - x if dx else x, 1 - y if dy else y, 1 - c if dc else c)
            pid = 4 * peer[0] + 2 * peer[1] + peer[2]
            cp = pltpu.make_async_remote_copy(
                src_ref=src_of(pid), dst_ref=dst_of(me),
                send_sem=send_sems.at[w * N_DEV + k], recv_sem=recv_sems.at[w * N_DEV + k],
                device_id=peer, device_id_type=pl.DeviceIdType.MESH)
            cp.start()
            started.append(cp)
    for cp in started:
        cp.wait()


def _lane_block(ref, p, width):
    return ref.at[:, pl.ds(pl.multiple_of(p * width, LANE), width)]


def _row_block(ref, p, rows):
    return ref.at[pl.ds(p * rows, rows), :]


def _small_exchange(d_conv, small):
    def body(dconv, sm, rconv, rsm, send_sems, recv_sems, local_sems):
        items = [
            (lambda p: dconv.at[p], lambda p: rconv.at[p]),
            (lambda p: sm, lambda p: rsm.at[p]),
        ]
        _exchange(items, send_sems, recv_sems, local_sems)

    n_items = 2
    return _pcall(
        body, name="small_exchange",
        out_shape=(jax.ShapeDtypeStruct((N_DEV,) + d_conv.shape[1:], F32),
                   jax.ShapeDtypeStruct((N_DEV,) + small.shape, F32)),
        in_specs=[_ANY] * n_items, out_specs=[_ANY] * n_items,
        scratch_shapes=[pltpu.SemaphoreType.DMA((n_items * N_DEV,)), pltpu.SemaphoreType.DMA((n_items * N_DEV,)),
                        pltpu.SemaphoreType.DMA((n_items,))],
        compiler_params=_params(),
    )(d_conv, small)


_HBM = pl.BlockSpec(memory_space=pltpu.HBM)
_SEM = pl.BlockSpec(memory_space=pltpu.SEMAPHORE)
_EFFECT = pltpu.SideEffectType.DATAFLOW_SIDE_EFFECTING


REL_ORDER = (1, 2, 4, 3, 5, 6, 7)


def _to_peers(src_of, dst_of, order=REL_ORDER):
    return [(src_of, dst_of, k) for k in order]


def _remote_copies(entries, send_sems, recv_sems):
    (x, y, c), me = _position()
    copies = []
    for n, (src_of, dst_of, k) in enumerate(entries):
        dx, dy, dc = _REL[k]
        peer = (1 - x if dx else x, 1 - y if dy else y, 1 - c if dc else c)
        pid = 4 * peer[0] + 2 * peer[1] + peer[2]
        copies.append(pltpu.make_async_remote_copy(
            src_ref=src_of(pid, me), dst_ref=dst_of(me), send_sem=send_sems.at[n], recv_sem=recv_sems.at[n],
            device_id=peer, device_id_type=pl.DeviceIdType.MESH))
    return copies


def _start_copies(name, arrays, make_groups):
    n = len(arrays)
    sizes = [len(g) for g in make_groups([None] * n)]

    def body(*refs):
        sems, token = refs[n:n + 2 * len(sizes)], refs[-1]
        for g, entries in enumerate(make_groups(refs[:n])):
            for cp in _remote_copies(entries, sems[2 * g], sems[2 * g + 1]):
                cp.start()
        token[...] = jnp.zeros_like(token)

    out_shape = []
    for size in sizes:
        out_shape += [pltpu.SemaphoreType.DMA((size,))] * 2
    out_shape += [pltpu.HBM(a.shape, a.dtype) for a in arrays] + [jax.ShapeDtypeStruct((8, LANE), F32)]
    res = _pcall(
        body, name=name, out_shape=out_shape,
        in_specs=[_HBM] * n, out_specs=[_SEM] * (2 * len(sizes)) + [_HBM] * n + [_VMEM],
        input_output_aliases={i: 2 * len(sizes) + i for i in range(n)},
        compiler_params=pltpu.CompilerParams(has_side_effects=_EFFECT),
    )(*[pltpu.with_memory_space_constraint(a, pltpu.HBM) for a in arrays])
    return res[:2 * len(sizes)], res[2 * len(sizes):-1], res[-1]


def _wait_copies(name, arrays, send_sems, recv_sems, make_entries, after):
    n = len(arrays)

    def body(*refs):
        for cp in _remote_copies(make_entries(refs[:n]), refs[n], refs[n + 1]):
            cp.wait_send()
            cp.wait_recv()

    return _pcall(
        body, name=name, out_shape=[pltpu.HBM(a.shape, a.dtype) for a in arrays],
        in_specs=[_HBM] * n + [_SEM, _SEM, _ANY], out_specs=[_HBM] * n,
        input_output_aliases={i: i for i in range(n)},
        compiler_params=pltpu.CompilerParams(has_side_effects=_EFFECT),
    )(*arrays, send_sems, recv_sems, after)


def _place_weights(w_in, w_ff1, w_ff2, w_pa, w_pb, w_o, conv8):
    n_items = 7
    c_in, c_ff1 = w_in.shape[1], w_ff1.shape[1]
    r_ff2, r_p = w_ff2.shape[0], w_pa.shape[0]

    def body(win_ref, w1_ref, w2_ref, pa_ref, pb_ref, wo_ref, cw_ref,
             win_o, pa_o, pb_o, wo_o, cw_o, w1_o, w2_o,
             s_win, s_w1, s_w2, s_pa, s_pb, s_wo, sems):
        _, me = _position()
        for src, stage in ((win_ref, s_win), (w1_ref, s_w1), (w2_ref, s_w2),
                           (pa_ref, s_pa), (pb_ref, s_pb), (wo_ref, s_wo)):
            stage[...] = src[...].astype(BF16)
        pairs = [
            (s_win, _lane_block(win_o, me, c_in)), (s_pa, _row_block(pa_o, me, r_p)),
            (s_pb, _row_block(pb_o, me, r_p)), (s_wo, _row_block(wo_o, me, r_p)),
            (cw_ref, _lane_block(cw_o, me, conv8.shape[1])),
            (s_w1, _lane_block(w1_o, me, c_ff1)), (s_w2, _row_block(w2_o, me, r_ff2)),
        ]
        copies = [pltpu.make_async_copy(s, d, sems.at[n]) for n, (s, d) in enumerate(pairs)]
        for cp in copies:
            cp.start()
        for cp in copies:
            cp.wait()

    out_shape = (
        jax.ShapeDtypeStruct((D, N_DEV * c_in), BF16),
        jax.ShapeDtypeStruct((N_DEV * r_p, D), BF16),
        jax.ShapeDtypeStruct((N_DEV * r_p, D), BF16),
        jax.ShapeDtypeStruct((N_DEV * r_p, D), BF16),
        jax.ShapeDtypeStruct((conv8.shape[0], N_DEV * conv8.shape[1]), F32),
        jax.ShapeDtypeStruct((D, N_DEV * c_ff1), BF16),
        jax.ShapeDtypeStruct((N_DEV * r_ff2, D), BF16),
    )
    return _pcall(
        body, name="place_weights", out_shape=out_shape,
        in_specs=[_VMEM] * n_items, out_specs=[_ANY] * n_items,
        scratch_shapes=[pltpu.VMEM(w.shape, BF16) for w in (w_in, w_ff1, w_ff2, w_pa, w_pb, w_o)]
        + [pltpu.SemaphoreType.DMA((n_items,))],
        compiler_params=_params(),
    )(w_in, w_ff1, w_ff2, w_pa, w_pb, w_o, conv8)


def _gather_entries(refs, views, order=REL_ORDER):
    entries = []
    for r, v in zip(refs, views):
        entries += _to_peers(lambda pid, me, r=r, v=v: v(r, me), lambda me, r=r, v=v: v(r, me), order)
    return entries


def _scatter_entries(views):
    def make(refs):
        srcs, lands = refs[:len(views)], refs[len(views):]
        entries = []
        for r, v, l in zip(srcs, views, lands):
            entries += _to_peers(lambda pid, me, r=r, v=v: v(r, pid), lambda me, l=l: l.at[me])
        return entries
    return make


def _in_proj_blocks(name, ks, me, act, w_in, proj, width, tt, g_mix=None):
    T = act.shape[0]
    nt = T // tt
    nb = len(ks)
    normed = g_mix is not None

    def body(me_ref, act_ref, *refs):
        if normed:
            g_ref, refs = refs[0], refs[1:]
        w_refs, refs = refs[:nb], refs[nb + 1:]
        if normed:
            proj_out, h_out, w_s, o_s, sems = refs
        else:
            proj_out, w_s, o_s, sems = refs
        i = pl.program_id(0)

        @pl.when(i == 0)
        def _():
            for part, w_ref in enumerate(w_refs):
                w_s[:, part * width:(part + 1) * width] = w_ref[...]

        def copies(slot, step):
            rows = pl.ds(step * tt, tt)
            return [pltpu.make_async_copy(
                o_s.at[slot, :, part * width:(part + 1) * width],
                proj_out.at[rows, pl.ds(pl.multiple_of((me_ref[0] ^ ks[part]) * width, LANE), width)],
                sems.at[slot, part]) for part in range(nb)]

        slot = i % 2

        @pl.when(i >= 2)
        def _():
            for cp in copies(slot, i - 2):
                cp.wait()

        if normed:
            xh, _ = _rms(act_ref[...])
            h = (xh * g_ref[...]).astype(BF16)
            h_out[...] = h
        else:
            h = act_ref[...]
        o_s[slot] = _mm(h, w_s[...]).astype(BF16)
        for cp in copies(slot, i):
            cp.start()

        @pl.when(i == nt - 1)
        def _():
            for cp in copies(slot, i) + (copies(1 - slot, i - 1) if nt > 1 else []):
                cp.wait()

    tile = pl.BlockSpec((tt, D), lambda i, b: (i, 0))
    w_spec = lambda part: pl.BlockSpec((D, width), lambda i, m: (0, m[0] ^ ks[part]), pipeline_mode=pl.Buffered(1))
    extra_in = [_const_spec(g_mix.shape)] if normed else []
    extra_args = [g_mix] if normed else []
    out_shape = [jax.ShapeDtypeStruct(proj.shape, proj.dtype)] + ([jax.ShapeDtypeStruct((T, D), BF16)] if normed else [])
    res = _pcall(
        body, name=name, out_shape=out_shape,
        grid_spec=pltpu.PrefetchScalarGridSpec(
            num_scalar_prefetch=1, grid=(nt,),
            in_specs=[tile] + extra_in + [w_spec(part) for part in range(nb)] + [_ANY],
            out_specs=[_ANY] + ([tile] if normed else []),
            scratch_shapes=[pltpu.VMEM((D, nb * width), BF16), pltpu.VMEM((2, tt, nb * width), BF16),
                            pltpu.SemaphoreType.DMA((2, nb))]),
        input_output_aliases={2 + nb + len(extra_in): 0},
        compiler_params=_params(("arbitrary",)),
    )(me, act, *extra_args, *([w_in] * nb), proj)
    return res if normed else res[0]


def _mixer_fwd(x, proj, b_gate, g_v, w_s, b_s3, conv, w_pa, w_pb, w_o, tt):
    T = x.shape[0]
    nt = T // tt
    nb = tt // SG

    def body(x_ref, proj_ref, bg_ref, gv_ref, ws_ref, bs_ref, cw_ref, pa_w, pb_w, wo_w,
             acm_ref, pa_ref, pb_ref, x1_ref, q_carry, mix_s):
        @pl.when(pl.program_id(0) == 0)
        def _():
            q_carry[...] = jnp.zeros_like(q_carry)

        def proj(k):
            return proj_ref[:, _col(k)].astype(F32)

        vg, _ = _gelu(proj(1))
        vh, _ = _rms(vg)
        vp = (vh * gv_ref[...]).astype(BF16)
        wm = _masked_ws(ws_ref[...]).astype(BF16)
        for n in range(nb):
            rows = slice(n * SG, (n + 1) * SG)
            for g in range(N_GROUPS):
                cols = slice(g * SG, (g + 1) * SG)
                mix_s[rows, cols] = _mm(wm[g], vp[rows, cols]) + bs_ref[g]
        ug, _ = _gelu(proj(0))
        a = (ug * mix_s[...]).astype(BF16)
        acm_ref[:, _col(0)] = a
        pa = _mm(a, pa_w[...])
        pa_ref[...] = pa.astype(BF16)
        m = jax.nn.sigmoid(proj(5) + bg_ref[:, :D]) * pa

        bgate = proj(2)
        q = proj(3) * proj(4)
        halo = q_carry[...]
        cv = cw_ref[0:1, :] * _shift_down(halo, q, 2) + cw_ref[1:2, :] * _shift_down(halo, q, 1) + cw_ref[2:3, :] * q
        q_carry[...] = q[tt - q_carry.shape[0]:, :]
        c = (bgate * cv).astype(BF16)
        acm_ref[:, _col(1)] = c
        pb = _mm(c, pb_w[...])
        pb_ref[...] = pb.astype(BF16)
        m = (m + jax.nn.sigmoid(proj(6) + bg_ref[:, D:]) * pb).astype(BF16)
        acm_ref[:, _col(2)] = m
        x1_ref[...] = x_ref[...] + _mm(m, wo_w[...])

    tile = lambda w: pl.BlockSpec((tt, w), lambda i: (i, 0))
    out_shape = ([jax.ShapeDtypeStruct((T, 3 * D), BF16)] + [jax.ShapeDtypeStruct((T, D), BF16)] * 2
                 + [jax.ShapeDtypeStruct((T, D), F32)])
    return _pcall(
        body, name="mixer_fwd", grid=(nt,), out_shape=out_shape,
        in_specs=[tile(D), tile(IN_COLS), _const_spec(b_gate.shape),
                  _const_spec(g_v.shape), _const_spec(w_s.shape), _const_spec(b_s3.shape), _const_spec(conv.shape),
                  _const_spec(w_pa.shape), _const_spec(w_pb.shape), _const_spec(w_o.shape)],
        out_specs=[tile(3 * D)] + [tile(D)] * 3,
        scratch_shapes=[pltpu.VMEM((8, D), F32), pltpu.VMEM((tt, D), F32)],
        compiler_params=_params(("arbitrary",)),
    )(x, proj, b_gate, g_v, w_s, b_s3, conv, w_pa, w_pb, w_o)


ST_GFIN, ST_GFF, ST_LOSS = 0, 1, 2


def _ffn_fwd_bwd(x1, tgt, g_ff, g_fin, w1, w2, tt):
    T = x1.shape[0]
    nt = T // tt
    nk = D_FF // D

    def body(x1_ref, tgt_ref, gff_ref, gfin_ref, w1_ref, w2_ref,
             hf_ref, s_ref, dpre_ref, dx2_ref, dx1_ref, st_ref, z_s):
        @pl.when(pl.program_id(0) == 0)
        def _():
            st_ref[...] = jnp.zeros_like(st_ref)

        x1 = x1_ref[...]
        xh1, r1 = _rms(x1)
        hf = (xh1 * gff_ref[...]).astype(BF16)
        hf_ref[...] = hf
        acc = jnp.zeros((tt, D), F32)
        for k in range(nk):
            z = jnp.maximum(_mm(hf, w1_ref[:, _col(k)]), 0.0)
            z_s[:, _col(k)] = z.astype(BF16)
            s = (z * z).astype(BF16)
            s_ref[:, _col(k)] = s
            acc = acc + _mm(s, w2_ref[_col(k), :])
        x2 = x1 + acc
        xh2, r2 = _rms(x2)
        diff = xh2 * gfin_ref[...] - tgt_ref[...]
        st_ref[ST_LOSS:ST_LOSS + 1, :] += jnp.sum(diff * diff, axis=0, keepdims=True)
        dy = diff * (1.0 / D)
        st_ref[ST_GFIN:ST_GFIN + 1, :] += jnp.sum(dy * xh2, axis=0, keepdims=True)
        dx2 = _rms_bwd(dy * gfin_ref[...], xh2, r2)
        dx2b = dx2.astype(BF16)
        dx2_ref[...] = dx2b
        dhf = jnp.zeros((tt, D), F32)
        for k in range(nk):
            dpre = (_nt(dx2b, w2_ref[_col(k), :]) * (2.0 * z_s[:, _col(k)].astype(F32))).astype(BF16)
            dpre_ref[:, _col(k)] = dpre
            dhf = dhf + _nt(dpre, w1_ref[:, _col(k)])
        st_ref[ST_GFF:ST_GFF + 1, :] += jnp.sum(dhf * xh1, axis=0, keepdims=True)
        dx1_ref[...] = dx2 + _rms_bwd(dhf * gff_ref[...], xh1, r1)

    tile = lambda w: pl.BlockSpec((tt, w), lambda i: (i, 0))
    out_shape = (jax.ShapeDtypeStruct((T, D), BF16), jax.ShapeDtypeStruct((T, D_FF), BF16),
                 jax.ShapeDtypeStruct((T, D_FF), BF16), jax.ShapeDtypeStruct((T, D), BF16),
                 jax.ShapeDtypeStruct((T, D), F32), jax.ShapeDtypeStruct((8, D), F32))
    return _pcall(
        body, name="ffn_fwd_bwd", grid=(nt,), out_shape=out_shape,
        in_specs=[tile(D), tile(D), _const_spec(g_ff.shape), _const_spec(g_fin.shape),
                  _const_spec(w1.shape), _const_spec(w2.shape)],
        out_specs=[tile(D), tile(D_FF), tile(D_FF), tile(D), tile(D), pl.BlockSpec((8, D), lambda i: (0, 0))],
        scratch_shapes=[pltpu.VMEM((tt, D_FF), BF16)],
        compiler_params=_params(("arbitrary",)),
    )(x1, tgt, g_ff, g_fin, w1, w2)


ST_GV, ST_CONV = 0, 1


def _mixer_bwd(dx1, proj, pa, pb, b_gate, g_v, w_s, b_s3, conv, w_pa, w_pb, w_o, tt, deps=()):
    T = dx1.shape[0]
    nt = T // tt
    nb = tt // SG
    hb = tt // HALO

    def body(dx1_ref, proj_ref, cgh_ref, xsh_ref, pa_ref, pb_ref,
             bg_ref, gv_ref, ws_ref, bs_ref, cw_ref, pa_w, pb_w, wo_w,
             dproj_ref, dstk_ref, st_ref, dbg_ref, dws_ref, dbs_ref, d_carry, mix_s, dvp_s):
        i = pl.program_id(0)

        @pl.when(i == 0)
        def _():
            st_ref[...] = jnp.zeros_like(st_ref)
            dbg_ref[...] = jnp.zeros_like(dbg_ref)
            dws_ref[...] = jnp.zeros_like(dws_ref)
            dbs_ref[...] = jnp.zeros_like(dbs_ref)
            d_carry[...] = jnp.zeros_like(d_carry)

        def pj(k):
            return proj_ref[:, _col(k)].astype(F32)

        def put(k, val):
            dproj_ref[:, _col(k)] = val.astype(BF16)

        dx1b = dx1_ref[...].astype(BF16)
        dstk_ref[:, _col(2)] = dx1b
        dm = _nt(dx1b, wo_w[...])
        s_a = jax.nn.sigmoid(pj(5) + bg_ref[:, :D])
        s_b = jax.nn.sigmoid(pj(6) + bg_ref[:, D:])
        dpa = dm * s_a
        dpb = dm * s_b
        dpa_b = dpa.astype(BF16)
        dpb_b = dpb.astype(BF16)
        dstk_ref[:, _col(0)] = dpa_b
        dstk_ref[:, _col(1)] = dpb_b
        dga = dpa * pa_ref[...].astype(F32) * (1.0 - s_a)
        dgb = dpb * pb_ref[...].astype(F32) * (1.0 - s_b)
        dbg_ref[0:1, :D] += jnp.sum(dga, axis=0, keepdims=True)
        dbg_ref[0:1, D:] += jnp.sum(dgb, axis=0, keepdims=True)
        put(5, dga)
        put(6, dgb)
        da = _nt(dpa_b, pa_w[...])
        dc = _nt(dpb_b, pb_w[...])

        v = pj(1)
        vg, v_cdf = _gelu(v)
        vh, rv = _rms(vg)
        vp = (vh * gv_ref[...]).astype(BF16)
        wm = _masked_ws(ws_ref[...]).astype(BF16)
        for n in range(nb):
            rows = slice(n * SG, (n + 1) * SG)
            for g in range(N_GROUPS):
                cols = slice(g * SG, (g + 1) * SG)
                mix_s[rows, cols] = _mm(wm[g], vp[rows, cols]) + bs_ref[g]
        u = pj(0)
        ug, u_cdf = _gelu(u)
        put(0, da * mix_s[...] * _gelu_grad(u, u_cdf))
        dmix = da * ug
        dmix_b = dmix.astype(BF16)
        for n in range(nb):
            rows = slice(n * SG, (n + 1) * SG)
            for g in range(N_GROUPS):
                cols = slice(g * SG, (g + 1) * SG)
                blk = dmix_b[rows, cols]
                dws_ref[g] += _nt(blk, vp[rows, cols])
                dbs_ref[g] += dmix[rows, cols]
                dvp_s[rows, cols] = _tn(wm[g], blk)
        dvp = dvp_s[...]
        st_ref[ST_GV:ST_GV + 1, :] += jnp.sum(dvp * vh, axis=0, keepdims=True)
        put(1, _rms_bwd(dvp * gv_ref[...], vh, rv) * _gelu_grad(v, v_cdf))

        bgate, cg, xs = pj(2), pj(3), pj(4)
        q = cg * xs
        has_prev = (i < nt - 1).astype(F32)
        halo = cgh_ref[...].astype(F32) * xsh_ref[...].astype(F32) * has_prev
        q2 = _shift_down(halo, q, 2)
        q1 = _shift_down(halo, q, 1)
        w0, w1, w2 = cw_ref[0:1, :], cw_ref[1:2, :], cw_ref[2:3, :]
        put(2, dc * (w0 * q2 + w1 * q1 + w2 * q))
        dcv = dc * bgate
        st_ref[ST_CONV:ST_CONV + 1, :] += jnp.sum(dcv * q2, axis=0, keepdims=True)
        st_ref[ST_CONV + 1:ST_CONV + 2, :] += jnp.sum(dcv * q1, axis=0, keepdims=True)
        st_ref[ST_CONV + 2:ST_CONV + 3, :] += jnp.sum(dcv * q, axis=0, keepdims=True)
        nxt = d_carry[...]
        dq = w2 * dcv + w1 * _shift_up(dcv, nxt, 1) + w0 * _shift_up(dcv, nxt, 2)
        d_carry[...] = dcv[:d_carry.shape[0], :]
        put(3, dq * xs)
        put(4, dq * cg)

    rev = lambda i: nt - 1 - i
    tile = lambda w: pl.BlockSpec((tt, w), lambda i: (rev(i), 0))
    halo_spec = lambda k: pl.BlockSpec((HALO, D), lambda i: (jnp.maximum(rev(i) * hb - 1, 0), k))
    res = lambda shape: pl.BlockSpec(shape, lambda i: (0,) * len(shape))
    out_shape = (jax.ShapeDtypeStruct((T, IN_COLS), BF16), jax.ShapeDtypeStruct((T, 3 * D), BF16),
                 jax.ShapeDtypeStruct((8, D), F32), jax.ShapeDtypeStruct((8, 2 * D), F32),
                 jax.ShapeDtypeStruct((N_GROUPS, SG, SG), F32), jax.ShapeDtypeStruct((N_GROUPS, SG, SG), F32))
    return _pcall(
        _after(body, 14, deps), name="mixer_bwd", grid=(nt,), out_shape=out_shape,
        in_specs=[tile(D), tile(IN_COLS), halo_spec(3), halo_spec(4), tile(D), tile(D),
                  _const_spec(b_gate.shape), _const_spec(g_v.shape), _const_spec(w_s.shape),
                  _const_spec(b_s3.shape), _const_spec(conv.shape),
                  _const_spec(w_pa.shape), _const_spec(w_pb.shape), _const_spec(w_o.shape)] + [_ANY] * len(deps),
        out_specs=[tile(IN_COLS), tile(3 * D), res((8, D)), res((8, 2 * D)),
                   res((N_GROUPS, SG, SG)), res((N_GROUPS, SG, SG))],
        scratch_shapes=[pltpu.VMEM((8, D), F32), pltpu.VMEM((tt, D), F32), pltpu.VMEM((tt, D), F32)],
        compiler_params=_params(("arbitrary",)),
    )(dx1, proj, proj, proj, pa, pb, b_gate, g_v, w_s, b_s3, conv, w_pa, w_pb, w_o, *deps)


def _in_proj_bwd(dproj, x, dx1, g_mix, w_in, tt, deps=()):
    T = x.shape[0]

    def body(dproj_ref, x_ref, dx1_ref, gmix_ref, win_ref, gx_ref, st_ref):
        @pl.when(pl.program_id(0) == 0)
        def _():
            st_ref[...] = jnp.zeros_like(st_ref)

        dh = _nt(dproj_ref[...], win_ref[...])
        xh, r = _rms(x_ref[...])
        st_ref[0:1, :] += jnp.sum(dh * xh, axis=0, keepdims=True)
        gx_ref[...] = dx1_ref[...] + _rms_bwd(dh * gmix_ref[...], xh, r)

    tile = lambda w: pl.BlockSpec((tt, w), lambda i: (i, 0))
    return _pcall(
        _after(body, 5, deps), name="in_proj_bwd", grid=(T // tt,),
        out_shape=(jax.ShapeDtypeStruct((T, D), F32), jax.ShapeDtypeStruct((8, D), F32)),
        in_specs=[tile(IN_COLS), tile(D), tile(D), _const_spec(g_mix.shape), _const_spec(w_in.shape)]
        + [_ANY] * len(deps),
        out_specs=[tile(D), pl.BlockSpec((8, D), lambda i: (0, 0))],
        compiler_params=_params(("arbitrary",)),
    )(dproj, x, dx1, g_mix, w_in, *deps)


def _weight_grad(name, act, dout, bc, tk, deps=()):
    T, n_in = act.shape
    n_out = dout.shape[1]
    nk = T // tk
    bi = min(n_in, D)

    def body(a_ref, d_ref, o_ref, acc):
        k = pl.program_id(2)

        @pl.when(k == 0)
        def _():
            acc[...] = jnp.zeros_like(acc)

        acc[...] += _tn(a_ref[...], d_ref[...])

        @pl.when(k == nk - 1)
        def _():
            o_ref[...] = acc[...].astype(o_ref.dtype)

    return _pcall(
        _after(body, 2, deps), name=name, grid=(n_in // bi, n_out // bc, nk),
        out_shape=jax.ShapeDtypeStruct((n_in, n_out), BF16),
        in_specs=[pl.BlockSpec((tk, bi), lambda i, j, k: (k, i)), pl.BlockSpec((tk, bc), lambda i, j, k: (k, j))]
        + [_ANY] * len(deps),
        out_specs=pl.BlockSpec((bi, bc), lambda i, j, k: (i, j)),
        scratch_shapes=[pltpu.VMEM((bi, bc), F32)],
        compiler_params=_params(("arbitrary", "arbitrary", "arbitrary")),
    )(act, dout, *deps)


def _weight_grad_stack(name, acts, douts, tk):
    T, n = acts.shape[0], acts.shape[1] // D
    nk = T // tk

    def body(a_ref, d_ref, o_ref, acc):
        k = pl.program_id(1)

        @pl.when(k == 0)
        def _():
            acc[...] = jnp.zeros_like(acc)

        acc[...] += _tn(a_ref[...], d_ref[...])

        @pl.when(k == nk - 1)
        def _():
            o_ref[...] = acc[...].astype(o_ref.dtype)

    tile = pl.BlockSpec((tk, D), lambda w, k: (k, w))
    return _pcall(
        body, name=name, grid=(n, nk), out_shape=jax.ShapeDtypeStruct((n, D, D), BF16),
        in_specs=[tile, tile], out_specs=pl.BlockSpec((None, D, D), lambda w, k: (w, 0, 0)),
        scratch_shapes=[pltpu.VMEM((D, D), F32)],
        compiler_params=_params(("arbitrary", "arbitrary")),
    )(acts, douts)


def _adamw(w, g, m, v):
    m = ADAM_B1 * m + (1.0 - ADAM_B1) * g
    v = ADAM_B2 * v + (1.0 - ADAM_B2) * (g * g)
    m_hat = m / (1.0 - ADAM_B1 ** ADAM_STEP)
    v_hat = v / (1.0 - ADAM_B2 ** ADAM_STEP)
    delta = -ADAM_LR * (m_hat / (jnp.sqrt(v_hat) + ADAM_EPS) + ADAM_WD * w)
    return delta, m, v


def _slot_sum(ref, own=None, me=None):
    g = None
    for s in range(N_DEV):
        term = ref[s] if own is None else jnp.where(me == s, own, ref[s])
        g = term.astype(F32) if g is None else g + term.astype(F32)
    return g


def _reduce_adamw(me, items, steps):
    n = len(items)

    def body(me_ref, *refs):
        ins, outs = refs[:5 * n], refs[5 * n:]
        for j in range(n):
            own_ref, slot_ref, w_ref, m_ref, v_ref = ins[5 * j:5 * j + 5]
            g_out, d_out, m_out, v_out = outs[4 * j:4 * j + 4]
            g = _slot_sum(slot_ref, own_ref[...], me_ref[0])
            g_out[...] = g
            d_out[...], m_out[...], v_out[...] = _adamw(w_ref[...], g, m_ref[...], v_ref[...])

    in_specs, out_specs, out_shape, args = [], [], [], []
    for partial, own_block, slots, w, m, v, lead in items:
        rows, cols = w.shape
        tr = rows // steps
        tile = pl.BlockSpec((tr, cols), lambda i, me_ref: (i, 0))
        if lead is None:
            in_specs += [
                pl.BlockSpec((tr, cols), lambda i, me_ref, own_block=own_block, tr=tr: own_block(i, me_ref[0], tr)),
                pl.BlockSpec((N_DEV, tr, cols), lambda i, me_ref: (0, i, 0))]
        else:
            in_specs += [
                pl.BlockSpec((None, tr, cols),
                             lambda i, me_ref, own_block=own_block, tr=tr, lead=lead: (lead, *own_block(i, me_ref[0], tr))),
                pl.BlockSpec((N_DEV, None, tr, cols), lambda i, me_ref, lead=lead: (0, lead, i, 0))]
        in_specs += [tile, tile, tile]
        out_specs += [tile] * 4
        out_shape += [jax.ShapeDtypeStruct((rows, cols), F32)] * 4
        args += [partial, slots, w, m, v]
    res = _pcall(
        body, name="reduce_adamw", out_shape=out_shape,
        grid_spec=pltpu.PrefetchScalarGridSpec(
            num_scalar_prefetch=1, grid=(steps,), in_specs=in_specs, out_specs=out_specs),
        compiler_params=_params(("arbitrary",)),
    )(me, *args)
    return [res[4 * j:4 * j + 4] for j in range(n)]


SM_GMIX, SM_GV, SM_GFF, SM_GFIN, SM_LOSS, SM_BGATE, SM_BS, SM_ROWS = 0, 1, 2, 3, 4, 5, 7, 8


def _pack_small(st_ffn, st_mix, st_in, dbg, dbs, conv_rows):
    def body(ffn_ref, mix_ref, in_ref, dbg_ref, dbs_ref, sm_ref, conv_ref):
        sm_ref[SM_GMIX:SM_GMIX + 1, :] = in_ref[0:1, :]
        sm_ref[SM_GV:SM_GV + 1, :] = mix_ref[ST_GV:ST_GV + 1, :]
        sm_ref[SM_GFF:SM_GFF + 1, :] = ffn_ref[ST_GFF:ST_GFF + 1, :]
        sm_ref[SM_GFIN:SM_GFIN + 1, :] = ffn_ref[ST_GFIN:ST_GFIN + 1, :]
        sm_ref[SM_LOSS:SM_LOSS + 1, :] = ffn_ref[ST_LOSS:ST_LOSS + 1, :]
        sm_ref[SM_BGATE:SM_BGATE + 1, :] = dbg_ref[0:1, :D]
        sm_ref[SM_BGATE + 1:SM_BGATE + 2, :] = dbg_ref[0:1, D:]
        for g in range(N_GROUPS):
            sm_ref[SM_BS:SM_BS + 1, g * SG:(g + 1) * SG] = jnp.sum(dbs_ref[g].T, axis=0, keepdims=True)
        conv_ref[...] = jnp.zeros_like(conv_ref)
        for p in range(N_DEV):
            conv_ref[p, 0:conv_rows, :] = mix_ref[ST_CONV:ST_CONV + conv_rows, p * LANE:(p + 1) * LANE]

    return _pcall(
        body, name="pack_small",
        out_shape=(jax.ShapeDtypeStruct((SM_ROWS, D), F32), jax.ShapeDtypeStruct((N_DEV, 8, LANE), F32)),
        in_specs=[_VMEM] * 5, out_specs=[_VMEM] * 2, compiler_params=_params(),
    )(st_ffn, st_mix, st_in, dbg, dbs)


def _small_update(sm_slots, ws_own, ws_slots, conv_slots, params, conv_rows):
    flat = [a for t in params for a in t]

    def body(sm_ref, wso_ref, ws_ref, conv_ref, *refs):
        ins, outs = refs[:len(flat)], refs[len(flat):]
        loss_ref, outs = outs[0], outs[1:]
        _, me = _position()
        sm = _slot_sum(sm_ref)
        loss_ref[...] = (0.5 / D) * jnp.sum(sm[SM_LOSS:SM_LOSS + 1, :], axis=1, keepdims=True)

        def update(n, g, at=lambda r: r):
            w_ref, m_ref, v_ref = [at(r) for r in ins[3 * n:3 * n + 3]]
            g_out, d_out, m_out, v_out = [at(r) for r in outs[4 * n:4 * n + 4]]
            g_out[...] = g
            d_out[...], m_out[...], v_out[...] = _adamw(w_ref[...], g, m_ref[...], v_ref[...])

        for n, row in enumerate((SM_GMIX, SM_GV, SM_GFF, SM_GFIN)):
            update(n, sm[row:row + 1, :])
        for half in range(2):
            update(4, sm[SM_BGATE + half:SM_BGATE + half + 1, :], lambda r, half=half: r.at[:, pl.ds(half * D, D)])
        for g in range(N_GROUPS):
            update(5, sm[SM_BS:SM_BS + 1, g * SG:(g + 1) * SG], lambda r, g=g: r.at[0, pl.ds(g, 1), :])
        update(6, _masked_ws(_slot_sum(ws_ref, wso_ref[...], me)), lambda r: r.at[0])
        update(7, _slot_sum(conv_ref)[:conv_rows], lambda r: r.at[0])

    out_shape = [jax.ShapeDtypeStruct((1, 1), F32)]
    for w, _, _ in params:
        out_shape += [jax.ShapeDtypeStruct(w.shape, F32)] * 4
    return _pcall(
        body, name="small_update", out_shape=out_shape,
        in_specs=[_VMEM] * (4 + len(flat)), out_specs=[_VMEM] * len(out_shape), compiler_params=_params(),
    )(sm_slots, ws_own, ws_slots, conv_slots, *flat)


def kernel(x, norm_mix_g, w_in, b_gate, norm_v_g, w_s, b_s, conv_w, w_proj_a, w_proj_b, w_out, norm_ff_g, w_ff1, w_ff2, norm_final_g, loss_target, m_norm_mix_g, m_w_in, m_b_gate, m_norm_v_g, m_w_s, m_b_s, m_conv_w, m_w_proj_a, m_w_proj_b, m_w_out, m_norm_ff_g, m_w_ff1, m_w_ff2, m_norm_final_g, v_norm_mix_g, v_w_in, v_b_gate, v_norm_v_g, v_w_s, v_b_s, v_conv_w, v_w_proj_a, v_w_proj_b, v_w_out, v_norm_ff_g, v_w_ff1, v_w_ff2, v_norm_final_g):
    T = x.shape[1]
    tt = min(256, T)
    tk = min(4096, T)
    conv_rows = conv_w.shape[1]

    pad8 = lambda a: jnp.pad(a, ((0, 8 - a.shape[0]), (0, 0)))
    xs = x.reshape(T, D)
    tgt = loss_target.reshape(T, D)
    g_mix, g_v, g_ff, g_fin = norm_mix_g, norm_v_g, norm_ff_g, norm_final_g.reshape(1, D)
    ws = w_s[0]
    bs3 = b_s.reshape(N_GROUPS, SG, 1)

    c_in, c_ff1 = w_in.shape[2], w_ff1.shape[2]
    r_ff2, r_p = w_ff2.shape[1], w_proj_a.shape[1]
    lane_view = lambda width: (lambda ref, p: _lane_block(ref, p, width))
    row_view = lambda rows: (lambda ref, p: _row_block(ref, p, rows))
    slots = lambda shape, dtype: lax.empty((N_DEV,) + shape, dtype)

    placed = _place_weights(w_in[0], w_ff1[0], w_ff2[0], w_proj_a[0], w_proj_b[0], w_out[0], pad8(conv_w[0]))
    mixer_views = [row_view(r_p), row_view(r_p), row_view(r_p), lane_view(LANE)]
    ffn_views = [lane_view(c_ff1), row_view(r_ff2)]
    pairs = ((0, 1), (2, 4), (3, 5), (6, 7))
    in_group = lambda refs, ks: _gather_entries(refs[:1], [lane_view(c_in)], tuple(k for k in ks if k))

    def gather_groups(refs):
        return ([in_group(refs, ks) for ks in pairs]
                + [_gather_entries(refs[1:5], mixer_views), _gather_entries(refs[5:], ffn_views)])

    sems, placed, g_token = _start_copies("gather_start", placed, gather_groups)
    me = _position()[1].reshape(1)
    W_in = placed[0]
    tp = min(2048, T)
    proj = lax.empty((T, IN_COLS), BF16)
    for n, ks in enumerate(pairs):
        W_in, = _wait_copies(f"gather_wait_in_{n}", [W_in], sems[2 * n], sems[2 * n + 1],
                             lambda refs, ks=ks: in_group(refs, ks), proj if n else g_token)
        if n == 0:
            proj, h = _in_proj_blocks(f"in_proj_{n}", ks, me, xs, W_in, proj, c_in, tp, g_mix)
        else:
            proj = _in_proj_blocks(f"in_proj_{n}", ks, me, h, W_in, proj, c_in, tp)
    n = len(pairs)
    PA, PB, WO, conv = _wait_copies(
        "gather_wait_mixer", placed[1:5], sems[2 * n], sems[2 * n + 1],
        lambda refs: _gather_entries(refs, mixer_views), proj)
    acm, pa, pb, x1 = _mixer_fwd(xs, proj, b_gate, g_v, ws, bs3, conv, PA, PB, WO, min(512, T))
    W1, W2 = _wait_copies(
        "gather_wait_ffn", placed[5:], sems[2 * n + 2], sems[2 * n + 3],
        lambda refs: _gather_entries(refs, ffn_views), x1)
    hf, s, dpre, dx2, dx1, st_ffn = _ffn_fwd_bwd(x1, tgt, g_ff, g_fin, W1, W2, min(512, T))

    d_ff2 = _weight_grad("dw_ff2", s, dx2, D, tk)
    d_ff1 = _weight_grad("dw_ff1", hf, dpre, D, tk)
    ff_views = [lane_view(c_ff1), row_view(r_ff2)]
    ff_sems, ff_arrays, ff_token = _start_copies(
        "scatter_start_ffn", [d_ff1, d_ff2, slots((D, c_ff1), BF16), slots((r_ff2, D), BF16)],
        lambda refs: [_scatter_entries(ff_views)(refs)])

    dproj, dstk, st_mix, dbg, dws, dbs = _mixer_bwd(
        dx1, proj, pa, pb, b_gate, g_v, ws, bs3, conv, PA, PB, WO, tt, deps=(ff_token,))
    d_p3 = _weight_grad_stack("dw_proj", acm, dstk, tk)

    def proj_entries(refs):
        d3, dws_ref, land3, land_ws = refs
        entries = []
        for w in range(3):
            entries += _to_peers(lambda pid, me, w=w: d3.at[w, pl.ds(pid * r_p, r_p), :],
                                 lambda me, w=w: land3.at[me, w])
        return entries + _to_peers(lambda pid, me: dws_ref, lambda me: land_ws.at[me])

    p_sems, p_arrays, p_token = _start_copies(
        "scatter_start_proj", [d_p3, dws, slots((3, r_p, D), BF16), slots(dws.shape, F32)],
        lambda refs: [proj_entries(refs)])

    d_in = _weight_grad("dw_in", h, dproj, D, tk, deps=(p_token,))
    in_views = [lane_view(c_in)]
    in_sems, in_arrays, in_token = _start_copies(
        "scatter_start_in", [d_in, slots((D, c_in), BF16)], lambda refs: [_scatter_entries(in_views)(refs)])
    grad_x, st_in = _in_proj_bwd(dproj, xs, dx1, g_mix, W_in, min(512, T), deps=(in_token,))

    small, d_conv = _pack_small(st_ffn, st_mix, st_in, dbg, dbs, conv_rows)
    r_conv, r_small = _small_exchange(d_conv, small)
    d_ff1, d_ff2, s_ff1, s_ff2 = _wait_copies(
        "scatter_wait_ffn", ff_arrays, ff_sems[0], ff_sems[1], _scatter_entries(ff_views), r_small)
    d_p3, dws, s_p3, s_ws = _wait_copies(
        "scatter_wait_proj", p_arrays, p_sems[0], p_sems[1], proj_entries, r_small)
    d_in, s_in = _wait_copies("scatter_wait_in", in_arrays, in_sems[0], in_sems[1], _scatter_entries(in_views), r_small)

    own_cols = lambda i, me, tr: (i, me)
    own_rows = lambda rows: (lambda i, me, tr: (me * (rows // tr) + i, 0))
    names = ["w_in", "w_ff1", "w_ff2", "w_proj_a", "w_proj_b", "w_out"]
    quads = _reduce_adamw(me, [
        (d_in, own_cols, s_in, w_in[0], m_w_in[0], v_w_in[0], None),
        (d_ff1, own_cols, s_ff1, w_ff1[0], m_w_ff1[0], v_w_ff1[0], None),
        (d_ff2, own_rows(r_ff2), s_ff2, w_ff2[0], m_w_ff2[0], v_w_ff2[0], None),
        (d_p3, own_rows(r_p), s_p3, w_proj_a[0], m_w_proj_a[0], v_w_proj_a[0], 0),
        (d_p3, own_rows(r_p), s_p3, w_proj_b[0], m_w_proj_b[0], v_w_proj_b[0], 1),
        (d_p3, own_rows(r_p), s_p3, w_out[0], m_w_out[0], v_w_out[0], 2),
    ], 8)
    big = dict(zip(names, quads))

    one = lambda a: a.reshape(1, D)
    small_params = [
        (norm_mix_g, m_norm_mix_g, v_norm_mix_g),
        (norm_v_g, m_norm_v_g, v_norm_v_g),
        (norm_ff_g, m_norm_ff_g, v_norm_ff_g),
        (one(norm_final_g), one(m_norm_final_g), one(v_norm_final_g)),
        (b_gate, m_b_gate, v_b_gate),
        (b_s, m_b_s, v_b_s),
        (w_s, m_w_s, v_w_s),
        (conv_w, m_conv_w, v_conv_w),
    ]
    res = _small_update(r_small, dws, s_ws, r_conv, small_params, conv_rows)
    loss = res[0].reshape(())
    names = ["norm_mix_g", "norm_v_g", "norm_ff_g", "norm_final_g", "b_gate", "b_s", "w_s", "conv_w"]
    out = {name: list(res[1 + 4 * n:5 + 4 * n]) for n, name in enumerate(names)}
    out["norm_final_g"] = [q.reshape(norm_final_g.shape) for q in out["norm_final_g"]]
    for name, quad in big.items():
        out[name] = [q[None] for q in quad]

    order = ["norm_mix_g", "w_in", "b_gate", "norm_v_g", "w_s", "b_s", "conv_w", "w_proj_a", "w_proj_b", "w_out",
             "norm_ff_g", "w_ff1", "w_ff2", "norm_final_g"]
    grads = [out[n][0] for n in order]
    deltas = [out[n][1] for n in order]
    new_m = [out[n][2] for n in order]
    new_v = [out[n][3] for n in order]
    return (loss, grad_x.reshape(x.shape), *grads, *deltas, *new_m, *new_v)
```

```python
import math

import jax
import jax.numpy as jnp
from jax import lax
from jax.experimental import pallas as pl
from jax.experimental.pallas import tpu as pltpu

F32 = jnp.float32
BF16 = jnp.bfloat16

N_DEV = 8
D = 1024
D_FF = 4096
IN_COLS = 7 * D
SG = 128
N_GROUPS = 8
CHUNK = 64
EPS = 1e-6
HALO = 16
LANE = 128
VMEM_LIMIT = 62 * 1024 * 1024

ADAM_LR = 0.001
ADAM_B1 = 0.9
ADAM_B2 = 0.999
ADAM_EPS = 1e-08
ADAM_WD = 0.01
ADAM_STEP = 10

SQRT_HALF = math.sqrt(0.5)
PDF_EXP2_SCALE = -0.5 * math.log2(math.e)
PDF_EXP2_SHIFT = math.log2(1.0 / math.sqrt(2.0 * math.pi))

_REL = [(dx, dy, dc) for dx in (0, 1) for dy in (0, 1) for dc in (0, 1)]

_VMEM = pl.BlockSpec(memory_space=pltpu.VMEM)
_ANY = pl.BlockSpec(memory_space=pl.ANY)


def _pcall(body, **kw):
    return pl.pallas_call(body, **kw)


def _params(sem=None):
    if sem is None:
        return pltpu.CompilerParams(vmem_limit_bytes=VMEM_LIMIT)
    return pltpu.CompilerParams(dimension_semantics=sem, vmem_limit_bytes=VMEM_LIMIT)


def _const_spec(shape):
    nd = len(shape)
    return pl.BlockSpec(shape, lambda *_: (0,) * nd, pipeline_mode=pl.Buffered(1))


def _after(body, n_in, deps):
    def wrapped(*refs):
        return body(*refs[:n_in], *refs[n_in + len(deps):])
    return wrapped


def _mm(a, b):
    return jnp.dot(a, b, preferred_element_type=F32)


def _nt(a, b):
    return lax.dot_general(a, b, (((1,), (1,)), ((), ())), preferred_element_type=F32)


def _tn(a, b):
    return lax.dot_general(a, b, (((0,), (0,)), ((), ())), preferred_element_type=F32)


def _rms(x):
    r = lax.rsqrt(jnp.mean(x * x, axis=-1, keepdims=True) + EPS)
    return x * r, r


def _rms_bwd(dyg, xh, r):
    return r * (dyg - xh * jnp.mean(dyg * xh, axis=-1, keepdims=True))


def _gelu(x):
    cdf = 0.5 * (1.0 + lax.erf(x * SQRT_HALF))
    return x * cdf, cdf


def _gelu_grad(x, cdf):
    return cdf + x * jnp.exp2(x * x * PDF_EXP2_SCALE + PDF_EXP2_SHIFT)


def _masked_ws(ws):
    i = lax.broadcasted_iota(jnp.int32, (SG, SG), 0)
    j = lax.broadcasted_iota(jnp.int32, (SG, SG), 1)
    keep = jnp.logical_or(j < CHUNK, i >= CHUNK)
    return jnp.where(keep[None], ws, jnp.zeros_like(ws))


def _shift_down(halo, q, k):
    ext = jnp.concatenate([halo, q], axis=0)
    return pltpu.roll(ext, k, 0)[halo.shape[0]:]


def _shift_up(q, nxt, k):
    ext = jnp.concatenate([q, nxt], axis=0)
    return pltpu.roll(ext, ext.shape[0] - k, 0)[:q.shape[0]]


def _col(k):
    return slice(k * D, (k + 1) * D)


def _position():
    x, y, c = lax.axis_index("x"), lax.axis_index("y"), lax.axis_index("c")
    return (x, y, c), 4 * x + 2 * y + c


def _exchange(items, send_sems, recv_sems, local_sems):
    (x, y, c), me = _position()
    started = []
    for w, (src_of, dst_of) in enumerate(items):
        own = pltpu.make_async_copy(src_of(me), dst_of(me), local_sems.at[w])
        own.start()
        started.append(own)
        for k in range(1, N_DEV):
            dx, dy, dc = _REL[k]
            peer = (1 - x if dx else x, 1 - y if dy else y, 1 - c if dc else c)
            pid = 4 * peer[0] + 2 * peer[1] + peer[2]
            cp = pltpu.make_async_remote_copy(
                src_ref=src_of(pid), dst_ref=dst_of(me),
                send_sem=send_sems.at[w * N_DEV + k], recv_sem=recv_sems.at[w * N_DEV + k],
                device_id=peer, device_id_type=pl.DeviceIdType.MESH)
            cp.start()
            started.append(cp)
    for cp in started:
        cp.wait()


def _lane_block(ref, p, width):
    return ref.at[:, pl.ds(pl.multiple_of(p * width, LANE), width)]


def _row_block(ref, p, rows):
    return ref.at[pl.ds(p * rows, rows), :]


def _small_exchange(d_conv, small):
    def body(dconv, sm, rconv, rsm, send_sems, recv_sems, local_sems):
        items = [
            (lambda p: dconv.at[p], lambda p: rconv.at[p]),
            (lambda p: sm, lambda p: rsm.at[p]),
        ]
        _exchange(items, send_sems, recv_sems, local_sems)

    n_items = 2
    return _pcall(
        body, name="small_exchange",
        out_shape=(jax.ShapeDtypeStruct((N_DEV,) + d_conv.shape[1:], F32),
                   jax.ShapeDtypeStruct((N_DEV,) + small.shape, F32)),
        in_specs=[_ANY] * n_items, out_specs=[_ANY] * n_items,
        scratch_shapes=[pltpu.SemaphoreType.DMA((n_items * N_DEV,)), pltpu.SemaphoreType.DMA((n_items * N_DEV,)),
                        pltpu.SemaphoreType.DMA((n_items,))],
        compiler_params=_params(),
    )(d_conv, small)


_HBM = pl.BlockSpec(memory_space=pltpu.HBM)
_SEM = pl.BlockSpec(memory_space=pltpu.SEMAPHORE)
_EFFECT = pltpu.SideEffectType.DATAFLOW_SIDE_EFFECTING


REL_ORDER = (1, 2, 4, 3, 5, 6, 7)


def _to_peers(src_of, dst_of, order=REL_ORDER):
    return [(src_of, dst_of, k) for k in order]


def _remote_copies(entries, send_sems, recv_sems):
    (x, y, c), me = _position()
    copies = []
    for n, (src_of, dst_of, k) in enumerate(entries):
        dx, dy, dc = _REL[k]
        peer = (1 - x if dx else x, 1 - y if dy else y, 1 - c if dc else c)
        pid = 4 * peer[0] + 2 * peer[1] + peer[2]
        copies.append(pltpu.make_async_remote_copy(
            src_ref=src_of(pid, me), dst_ref=dst_of(me), send_sem=send_sems.at[n], recv_sem=recv_sems.at[n],
            device_id=peer, device_id_type=pl.DeviceIdType.MESH))
    return copies


def _start_copies(name, arrays, make_groups):
    n = len(arrays)
    sizes = [len(g) for g in make_groups([None] * n)]

    def body(*refs):
        sems, token = refs[n:n + 2 * len(sizes)], refs[-1]
        for g, entries in enumerate(make_groups(refs[:n])):
            for cp in _remote_copies(entries, sems[2 * g], sems[2 * g + 1]):
                cp.start()
        token[...] = jnp.zeros_like(token)

    out_shape = []
    for size in sizes:
        out_shape += [pltpu.SemaphoreType.DMA((size,))] * 2
    out_shape += [pltpu.HBM(a.shape, a.dtype) for a in arrays] + [jax.ShapeDtypeStruct((8, LANE), F32)]
    res = _pcall(
        body, name=name, out_shape=out_shape,
        in_specs=[_HBM] * n, out_specs=[_SEM] * (2 * len(sizes)) + [_HBM] * n + [_VMEM],
        input_output_aliases={i: 2 * len(sizes) + i for i in range(n)},
        compiler_params=pltpu.CompilerParams(has_side_effects=_EFFECT),
    )(*[pltpu.with_memory_space_constraint(a, pltpu.HBM) for a in arrays])
    return res[:2 * len(sizes)], res[2 * len(sizes):-1], res[-1]


def _wait_copies(name, arrays, send_sems, recv_sems, make_entries, after):
    n = len(arrays)

    def body(*refs):
        for cp in _remote_copies(make_entries(refs[:n]), refs[n], refs[n + 1]):
            cp.wait_send()
            cp.wait_recv()

    return _pcall(
        body, name=name, out_shape=[pltpu.HBM(a.shape, a.dtype) for a in arrays],
        in_specs=[_HBM] * n + [_SEM, _SEM, _ANY], out_specs=[_HBM] * n,
        input_output_aliases={i: i for i in range(n)},
        compiler_params=pltpu.CompilerParams(has_side_effects=_EFFECT),
    )(*arrays, send_sems, recv_sems, after)


def _place_weights(w_in, w_ff1, w_ff2, w_pa, w_pb, w_o, conv_flat):
    n_items = 7
    c_in, c_ff1 = w_in.shape[1], w_ff1.shape[1]
    r_ff2, r_p = w_ff2.shape[0], w_pa.shape[0]

    def body(win_ref, w1_ref, w2_ref, pa_ref, pb_ref, wo_ref, cw_ref,
             win_o, pa_o, pb_o, wo_o, cw_o, w1_o, w2_o,
             s_win, s_w1, s_w2, s_pa, s_pb, s_wo, s_cw, sems):
        _, me = _position()
        s_cw[...] = jnp.zeros_like(s_cw)
        for j in range(conv_flat.shape[1] // LANE):
            s_cw[j:j + 1, :] = cw_ref[:, j * LANE:(j + 1) * LANE]
        for src, stage in ((win_ref, s_win), (w1_ref, s_w1), (w2_ref, s_w2),
                           (pa_ref, s_pa), (pb_ref, s_pb), (wo_ref, s_wo)):
            stage[...] = src[...].astype(BF16)
        pairs = [
            (s_win, _lane_block(win_o, me, c_in)), (s_pa, _row_block(pa_o, me, r_p)),
            (s_pb, _row_block(pb_o, me, r_p)), (s_wo, _row_block(wo_o, me, r_p)),
            (s_cw, _lane_block(cw_o, me, LANE)),
            (s_w1, _lane_block(w1_o, me, c_ff1)), (s_w2, _row_block(w2_o, me, r_ff2)),
        ]
        copies = [pltpu.make_async_copy(s, d, sems.at[n]) for n, (s, d) in enumerate(pairs)]
        for cp in copies:
            cp.start()
        for cp in copies:
            cp.wait()

    out_shape = (
        jax.ShapeDtypeStruct((D, N_DEV * c_in), BF16),
        jax.ShapeDtypeStruct((N_DEV * r_p, D), BF16),
        jax.ShapeDtypeStruct((N_DEV * r_p, D), BF16),
        jax.ShapeDtypeStruct((N_DEV * r_p, D), BF16),
        jax.ShapeDtypeStruct((8, N_DEV * LANE), F32),
        jax.ShapeDtypeStruct((D, N_DEV * c_ff1), BF16),
        jax.ShapeDtypeStruct((N_DEV * r_ff2, D), BF16),
    )
    return _pcall(
        body, name="place_weights", out_shape=out_shape,
        in_specs=[_VMEM] * n_items, out_specs=[_ANY] * n_items,
        scratch_shapes=[pltpu.VMEM(w.shape, BF16) for w in (w_in, w_ff1, w_ff2, w_pa, w_pb, w_o)]
        + [pltpu.VMEM((8, LANE), F32), pltpu.SemaphoreType.DMA((n_items,))],
        compiler_params=_params(),
    )(w_in, w_ff1, w_ff2, w_pa, w_pb, w_o, conv_flat)


def _gather_entries(refs, views, order=REL_ORDER):
    entries = []
    for r, v in zip(refs, views):
        entries += _to_peers(lambda pid, me, r=r, v=v: v(r, me), lambda me, r=r, v=v: v(r, me), order)
    return entries


def _scatter_entries(views):
    def make(refs):
        srcs, lands = refs[:len(views)], refs[len(views):]
        entries = []
        for r, v, l in zip(srcs, views, lands):
            entries += _to_peers(lambda pid, me, r=r, v=v: v(r, pid), lambda me, l=l: l.at[me])
        return entries
    return make


def _in_proj_blocks(name, ks, me, act, w_in, proj, width, tt, g_mix=None):
    T = act.shape[0]
    nt = T // tt
    nb = len(ks)
    normed = g_mix is not None

    def body(me_ref, act_ref, *refs):
        if normed:
            g_ref, refs = refs[0], refs[1:]
        w_refs, refs = refs[:nb], refs[nb + 1:]
        if normed:
            proj_out, h_out, w_s, o_s, sems = refs
        else:
            proj_out, w_s, o_s, sems = refs
        i = pl.program_id(0)

        @pl.when(i == 0)
        def _():
            for part, w_ref in enumerate(w_refs):
                w_s[:, part * width:(part + 1) * width] = w_ref[...]

        def copies(slot, step):
            rows = pl.ds(step * tt, tt)
            return [pltpu.make_async_copy(
                o_s.at[slot, :, part * width:(part + 1) * width],
                proj_out.at[rows, pl.ds(pl.multiple_of((me_ref[0] ^ ks[part]) * width, LANE), width)],
                sems.at[slot, part]) for part in range(nb)]

        slot = i % 2

        @pl.when(i >= 2)
        def _():
            for cp in copies(slot, i - 2):
                cp.wait()

        if normed:
            xh, _ = _rms(act_ref[...])
            h = (xh * g_ref[...]).astype(BF16)
            h_out[...] = h
        else:
            h = act_ref[...]
        o_s[slot] = _mm(h, w_s[...]).astype(BF16)
        for cp in copies(slot, i):
            cp.start()

        @pl.when(i == nt - 1)
        def _():
            for cp in copies(slot, i) + (copies(1 - slot, i - 1) if nt > 1 else []):
                cp.wait()

    tile = pl.BlockSpec((tt, D), lambda i, b: (i, 0))
    w_spec = lambda part: pl.BlockSpec((D, width), lambda i, m: (0, m[0] ^ ks[part]), pipeline_mode=pl.Buffered(1))
    extra_in = [_const_spec(g_mix.shape)] if normed else []
    extra_args = [g_mix] if normed else []
    out_shape = [jax.ShapeDtypeStruct(proj.shape, proj.dtype)] + ([jax.ShapeDtypeStruct((T, D), BF16)] if normed else [])
    res = _pcall(
        body, name=name, out_shape=out_shape,
        grid_spec=pltpu.PrefetchScalarGridSpec(
            num_scalar_prefetch=1, grid=(nt,),
            in_specs=[tile] + extra_in + [w_spec(part) for part in range(nb)] + [_ANY],
            out_specs=[_ANY] + ([tile] if normed else []),
            scratch_shapes=[pltpu.VMEM((D, nb * width), BF16), pltpu.VMEM((2, tt, nb * width), BF16),
                            pltpu.SemaphoreType.DMA((2, nb))]),
        input_output_aliases={2 + nb + len(extra_in): 0},
        compiler_params=_params(("arbitrary",)),
    )(me, act, *extra_args, *([w_in] * nb), proj)
    return res if normed else res[0]


def _mixer_fwd(x, proj, b_gate, g_v, w_s, b_s3, conv, w_pa, w_pb, w_o, tt):
    T = x.shape[0]
    nt = T // tt
    nb = tt // SG

    def body(x_ref, proj_ref, bg_ref, gv_ref, ws_ref, bs_ref, cw_ref, pa_w, pb_w, wo_w,
             acm_ref, pa_ref, pb_ref, x1_ref, q_carry, mix_s):
        @pl.when(pl.program_id(0) == 0)
        def _():
            q_carry[...] = jnp.zeros_like(q_carry)

        def proj(k):
            return proj_ref[:, _col(k)].astype(F32)

        vg, _ = _gelu(proj(1))
        vh, _ = _rms(vg)
        vp = (vh * gv_ref[...]).astype(BF16)
        wm = _masked_ws(ws_ref[...]).astype(BF16)
        for n in range(nb):
            rows = slice(n * SG, (n + 1) * SG)
            for g in range(N_GROUPS):
                cols = slice(g * SG, (g + 1) * SG)
                mix_s[rows, cols] = _mm(wm[g], vp[rows, cols]) + bs_ref[g]
        ug, _ = _gelu(proj(0))
        a = (ug * mix_s[...]).astype(BF16)
        acm_ref[:, _col(0)] = a
        pa = _mm(a, pa_w[...])
        pa_ref[...] = pa.astype(BF16)
        m = jax.nn.sigmoid(proj(5) + bg_ref[:, :D]) * pa

        bgate = proj(2)
        q = proj(3) * proj(4)
        halo = q_carry[...]
        cv = cw_ref[0:1, :] * _shift_down(halo, q, 2) + cw_ref[1:2, :] * _shift_down(halo, q, 1) + cw_ref[2:3, :] * q
        q_carry[...] = q[tt - q_carry.shape[0]:, :]
        c = (bgate * cv).astype(BF16)
        acm_ref[:, _col(1)] = c
        pb = _mm(c, pb_w[...])
        pb_ref[...] = pb.astype(BF16)
        m = (m + jax.nn.sigmoid(proj(6) + bg_ref[:, D:]) * pb).astype(BF16)
        acm_ref[:, _col(2)] = m
        x1_ref[...] = x_ref[...] + _mm(m, wo_w[...])

    tile = lambda w: pl.BlockSpec((tt, w), lambda i: (i, 0))
    out_shape = ([jax.ShapeDtypeStruct((T, 3 * D), BF16)] + [jax.ShapeDtypeStruct((T, D), BF16)] * 2
                 + [jax.ShapeDtypeStruct((T, D), F32)])
    return _pcall(
        body, name="mixer_fwd", grid=(nt,), out_shape=out_shape,
        in_specs=[tile(D), tile(IN_COLS), _const_spec(b_gate.shape),
                  _const_spec(g_v.shape), _const_spec(w_s.shape), _const_spec(b_s3.shape), _const_spec(conv.shape),
                  _const_spec(w_pa.shape), _const_spec(w_pb.shape), _const_spec(w_o.shape)],
        out_specs=[tile(3 * D)] + [tile(D)] * 3,
        scratch_shapes=[pltpu.VMEM((8, D), F32), pltpu.VMEM((tt, D), F32)],
        compiler_params=_params(("arbitrary",)),
    )(x, proj, b_gate, g_v, w_s, b_s3, conv, w_pa, w_pb, w_o)


ST_GFIN, ST_GFF, ST_LOSS = 0, 1, 2


def _ffn_fwd_bwd(x1, tgt, g_ff, g_fin, w1, w2, tt):
    T = x1.shape[0]
    nt = T // tt
    nk = D_FF // D

    def body(x1_ref, tgt_ref, gff_ref, gfin_ref, w1_ref, w2_ref,
             hf_ref, s_ref, dpre_ref, dx2_ref, dx1_ref, st_ref, z_s):
        @pl.when(pl.program_id(0) == 0)
        def _():
            st_ref[...] = jnp.zeros_like(st_ref)

        x1 = x1_ref[...]
        xh1, r1 = _rms(x1)
        hf = (xh1 * gff_ref[...]).astype(BF16)
        hf_ref[...] = hf
        acc = jnp.zeros((tt, D), F32)
        for k in range(nk):
            z = jnp.maximum(_mm(hf, w1_ref[:, _col(k)]), 0.0)
            z_s[:, _col(k)] = z.astype(BF16)
            s = (z * z).astype(BF16)
            s_ref[:, _col(k)] = s
            acc = acc + _mm(s, w2_ref[_col(k), :])
        x2 = x1 + acc
        xh2, r2 = _rms(x2)
        diff = xh2 * gfin_ref[...] - tgt_ref[...]
        st_ref[ST_LOSS:ST_LOSS + 1, :] += jnp.sum(diff * diff, axis=0, keepdims=True)
        dy = diff * (1.0 / D)
        st_ref[ST_GFIN:ST_GFIN + 1, :] += jnp.sum(dy * xh2, axis=0, keepdims=True)
        dx2 = _rms_bwd(dy * gfin_ref[...], xh2, r2)
        dx2b = dx2.astype(BF16)
        dx2_ref[...] = dx2b
        dhf = jnp.zeros((tt, D), F32)
        for k in range(nk):
            dpre = (_nt(dx2b, w2_ref[_col(k), :]) * (2.0 * z_s[:, _col(k)].astype(F32))).astype(BF16)
            dpre_ref[:, _col(k)] = dpre
            dhf = dhf + _nt(dpre, w1_ref[:, _col(k)])
        st_ref[ST_GFF:ST_GFF + 1, :] += jnp.sum(dhf * xh1, axis=0, keepdims=True)
        dx1_ref[...] = dx2 + _rms_bwd(dhf * gff_ref[...], xh1, r1)

    tile = lambda w: pl.BlockSpec((tt, w), lambda i: (i, 0))
    out_shape = (jax.ShapeDtypeStruct((T, D), BF16), jax.ShapeDtypeStruct((T, D_FF), BF16),
                 jax.ShapeDtypeStruct((T, D_FF), BF16), jax.ShapeDtypeStruct((T, D), BF16),
                 jax.ShapeDtypeStruct((T, D), F32), jax.ShapeDtypeStruct((8, D), F32))
    return _pcall(
        body, name="ffn_fwd_bwd", grid=(nt,), out_shape=out_shape,
        in_specs=[tile(D), tile(D), _const_spec(g_ff.shape), _const_spec(g_fin.shape),
                  _const_spec(w1.shape), _const_spec(w2.shape)],
        out_specs=[tile(D), tile(D_FF), tile(D_FF), tile(D), tile(D), pl.BlockSpec((8, D), lambda i: (0, 0))],
        scratch_shapes=[pltpu.VMEM((tt, D_FF), BF16)],
        compiler_params=_params(("arbitrary",)),
    )(x1, tgt, g_ff, g_fin, w1, w2)


ST_GV, ST_CONV = 0, 1


def _mixer_bwd(dx1, proj, pa, pb, b_gate, g_v, w_s, b_s3, conv, w_pa, w_pb, w_o, tt, deps=()):
    T = dx1.shape[0]
    nt = T // tt
    nb = tt // SG
    hb = tt // HALO

    def body(dx1_ref, proj_ref, cgh_ref, xsh_ref, pa_ref, pb_ref,
             bg_ref, gv_ref, ws_ref, bs_ref, cw_ref, pa_w, pb_w, wo_w,
             dproj_ref, dstk_ref, st_ref, dbg_ref, dws_ref, dbs_ref, d_carry, mix_s, dvp_s):
        i = pl.program_id(0)

        @pl.when(i == 0)
        def _():
            st_ref[...] = jnp.zeros_like(st_ref)
            dbg_ref[...] = jnp.zeros_like(dbg_ref)
            dws_ref[...] = jnp.zeros_like(dws_ref)
            dbs_ref[...] = jnp.zeros_like(dbs_ref)
            d_carry[...] = jnp.zeros_like(d_carry)

        def pj(k):
            return proj_ref[:, _col(k)].astype(F32)

        def put(k, val):
            dproj_ref[:, _col(k)] = val.astype(BF16)

        dx1b = dx1_ref[...].astype(BF16)
        dstk_ref[:, _col(2)] = dx1b
        dm = _nt(dx1b, wo_w[...])
        s_a = jax.nn.sigmoid(pj(5) + bg_ref[:, :D])
        s_b = jax.nn.sigmoid(pj(6) + bg_ref[:, D:])
        dpa = dm * s_a
        dpb = dm * s_b
        dpa_b = dpa.astype(BF16)
        dpb_b = dpb.astype(BF16)
        dstk_ref[:, _col(0)] = dpa_b
        dstk_ref[:, _col(1)] = dpb_b
        dga = dpa * pa_ref[...].astype(F32) * (1.0 - s_a)
        dgb = dpb * pb_ref[...].astype(F32) * (1.0 - s_b)
        dbg_ref[0:1, :D] += jnp.sum(dga, axis=0, keepdims=True)
        dbg_ref[0:1, D:] += jnp.sum(dgb, axis=0, keepdims=True)
        put(5, dga)
        put(6, dgb)
        da = _nt(dpa_b, pa_w[...])
        dc = _nt(dpb_b, pb_w[...])

        v = pj(1)
        vg, v_cdf = _gelu(v)
        vh, rv = _rms(vg)
        vp = (vh * gv_ref[...]).astype(BF16)
        wm = _masked_ws(ws_ref[...]).astype(BF16)
        for n in range(nb):
            rows = slice(n * SG, (n + 1) * SG)
            for g in range(N_GROUPS):
                cols = slice(g * SG, (g + 1) * SG)
                mix_s[rows, cols] = _mm(wm[g], vp[rows, cols]) + bs_ref[g]
        u = pj(0)
        ug, u_cdf = _gelu(u)
        put(0, da * mix_s[...] * _gelu_grad(u, u_cdf))
        dmix = da * ug
        dmix_b = dmix.astype(BF16)
        for n in range(nb):
            rows = slice(n * SG, (n + 1) * SG)
            for g in range(N_GROUPS):
                cols = slice(g * SG, (g + 1) * SG)
                blk = dmix_b[rows, cols]
                dws_ref[g] += _nt(blk, vp[rows, cols])
                dbs_ref[g] += dmix[rows, cols]
                dvp_s[rows, cols] = _tn(wm[g], blk)
        dvp = dvp_s[...]
        st_ref[ST_GV:ST_GV + 1, :] += jnp.sum(dvp * vh, axis=0, keepdims=True)
        put(1, _rms_bwd(dvp * gv_ref[...], vh, rv) * _gelu_grad(v, v_cdf))

        bgate, cg, xs = pj(2), pj(3), pj(4)
        q = cg * xs
        has_prev = (i < nt - 1).astype(F32)
        halo = cgh_ref[...].astype(F32) * xsh_ref[...].astype(F32) * has_prev
        q2 = _shift_down(halo, q, 2)
        q1 = _shift_down(halo, q, 1)
        w0, w1, w2 = cw_ref[0:1, :], cw_ref[1:2, :], cw_ref[2:3, :]
        put(2, dc * (w0 * q2 + w1 * q1 + w2 * q))
        dcv = dc * bgate
        st_ref[ST_CONV:ST_CONV + 1, :] += jnp.sum(dcv * q2, axis=0, keepdims=True)
        st_ref[ST_CONV + 1:ST_CONV + 2, :] += jnp.sum(dcv * q1, axis=0, keepdims=True)
        st_ref[ST_CONV + 2:ST_CONV + 3, :] += jnp.sum(dcv * q, axis=0, keepdims=True)
        nxt = d_carry[...]
        dq = w2 * dcv + w1 * _shift_up(dcv, nxt, 1) + w0 * _shift_up(dcv, nxt, 2)
        d_carry[...] = dcv[:d_carry.shape[0], :]
        put(3, dq * xs)
        put(4, dq * cg)

    rev = lambda i: nt - 1 - i
    tile = lambda w: pl.BlockSpec((tt, w), lambda i: (rev(i), 0))
    halo_spec = lambda k: pl.BlockSpec((HALO, D), lambda i: (jnp.maximum(rev(i) * hb - 1, 0), k))
    res = lambda shape: pl.BlockSpec(shape, lambda i: (0,) * len(shape))
    out_shape = (jax.ShapeDtypeStruct((T, IN_COLS), BF16), jax.ShapeDtypeStruct((T, 3 * D), BF16),
                 jax.ShapeDtypeStruct((8, D), F32), jax.ShapeDtypeStruct((8, 2 * D), F32),
                 jax.ShapeDtypeStruct((N_GROUPS, SG, SG), F32), jax.ShapeDtypeStruct((N_GROUPS, SG, SG), F32))
    return _pcall(
        _after(body, 14, deps), name="mixer_bwd", grid=(nt,), out_shape=out_shape,
        in_specs=[tile(D), tile(IN_COLS), halo_spec(3), halo_spec(4), tile(D), tile(D),
                  _const_spec(b_gate.shape), _const_spec(g_v.shape), _const_spec(w_s.shape),
                  _const_spec(b_s3.shape), _const_spec(conv.shape),
                  _const_spec(w_pa.shape), _const_spec(w_pb.shape), _const_spec(w_o.shape)] + [_ANY] * len(deps),
        out_specs=[tile(IN_COLS), tile(3 * D), res((8, D)), res((8, 2 * D)),
                   res((N_GROUPS, SG, SG)), res((N_GROUPS, SG, SG))],
        scratch_shapes=[pltpu.VMEM((8, D), F32), pltpu.VMEM((tt, D), F32), pltpu.VMEM((tt, D), F32)],
        compiler_params=_params(("arbitrary",)),
    )(dx1, proj, proj, proj, pa, pb, b_gate, g_v, w_s, b_s3, conv, w_pa, w_pb, w_o, *deps)


def _in_proj_bwd(dproj, x, dx1, g_mix, w_in, tt, deps=()):
    T = x.shape[0]

    def body(dproj_ref, x_ref, dx1_ref, gmix_ref, win_ref, gx_ref, st_ref):
        @pl.when(pl.program_id(0) == 0)
        def _():
            st_ref[...] = jnp.zeros_like(st_ref)

        dh = _nt(dproj_ref[...], win_ref[...])
        xh, r = _rms(x_ref[...])
        st_ref[0:1, :] += jnp.sum(dh * xh, axis=0, keepdims=True)
        gx_ref[...] = dx1_ref[...] + _rms_bwd(dh * gmix_ref[...], xh, r)

    tile = lambda w: pl.BlockSpec((tt, w), lambda i: (i, 0))
    return _pcall(
        _after(body, 5, deps), name="in_proj_bwd", grid=(T // tt,),
        out_shape=(jax.ShapeDtypeStruct((T, D), F32), jax.ShapeDtypeStruct((8, D), F32)),
        in_specs=[tile(IN_COLS), tile(D), tile(D), _const_spec(g_mix.shape), _const_spec(w_in.shape)]
        + [_ANY] * len(deps),
        out_specs=[tile(D), pl.BlockSpec((8, D), lambda i: (0, 0))],
        compiler_params=_params(("arbitrary",)),
    )(dproj, x, dx1, g_mix, w_in, *deps)


def _weight_grad(name, act, dout, bc, tk, deps=()):
    T, n_in = act.shape
    n_out = dout.shape[1]
    nk = T // tk
    bi = min(n_in, D)

    def body(a_ref, d_ref, o_ref, acc):
        k = pl.program_id(2)

        @pl.when(k == 0)
        def _():
            acc[...] = jnp.zeros_like(acc)

        acc[...] += _tn(a_ref[...], d_ref[...])

        @pl.when(k == nk - 1)
        def _():
            o_ref[...] = acc[...].astype(o_ref.dtype)

    return _pcall(
        _after(body, 2, deps), name=name, grid=(n_in // bi, n_out // bc, nk),
        out_shape=jax.ShapeDtypeStruct((n_in, n_out), BF16),
        in_specs=[pl.BlockSpec((tk, bi), lambda i, j, k: (k, i)), pl.BlockSpec((tk, bc), lambda i, j, k: (k, j))]
        + [_ANY] * len(deps),
        out_specs=pl.BlockSpec((bi, bc), lambda i, j, k: (i, j)),
        scratch_shapes=[pltpu.VMEM((bi, bc), F32)],
        compiler_params=_params(("arbitrary", "arbitrary", "arbitrary")),
    )(act, dout, *deps)


def _weight_grad_stack(name, acts, douts, tk):
    T, n = acts.shape[0], acts.shape[1] // D
    nk = T // tk

    def body(a_ref, d_ref, o_ref, acc):
        k = pl.program_id(1)

        @pl.when(k == 0)
        def _():
            acc[...] = jnp.zeros_like(acc)

        acc[...] += _tn(a_ref[...], d_ref[...])

        @pl.when(k == nk - 1)
        def _():
            o_ref[...] = acc[...].astype(o_ref.dtype)

    tile = pl.BlockSpec((tk, D), lambda w, k: (k, w))
    return _pcall(
        body, name=name, grid=(n, nk), out_shape=jax.ShapeDtypeStruct((n, D, D), BF16),
        in_specs=[tile, tile], out_specs=pl.BlockSpec((None, D, D), lambda w, k: (w, 0, 0)),
        scratch_shapes=[pltpu.VMEM((D, D), F32)],
        compiler_params=_params(("arbitrary", "arbitrary")),
    )(acts, douts)


def _adamw(w, g, m, v):
    m = ADAM_B1 * m + (1.0 - ADAM_B1) * g
    v = ADAM_B2 * v + (1.0 - ADAM_B2) * (g * g)
    m_hat = m / (1.0 - ADAM_B1 ** ADAM_STEP)
    v_hat = v / (1.0 - ADAM_B2 ** ADAM_STEP)
    delta = -ADAM_LR * (m_hat / (jnp.sqrt(v_hat) + ADAM_EPS) + ADAM_WD * w)
    return delta, m, v


def _slot_sum(ref, own=None, me=None):
    g = None
    for s in range(N_DEV):
        term = ref[s] if own is None else jnp.where(me == s, own, ref[s])
        g = term.astype(F32) if g is None else g + term.astype(F32)
    return g


def _reduce_adamw(me, items, steps):
    n = len(items)

    def body(me_ref, *refs):
        ins, outs = refs[:5 * n], refs[5 * n:]
        for j in range(n):
            own_ref, slot_ref, w_ref, m_ref, v_ref = ins[5 * j:5 * j + 5]
            g_out, d_out, m_out, v_out = outs[4 * j:4 * j + 4]
            g = _slot_sum(slot_ref, own_ref[...], me_ref[0])
            g_out[...] = g
            d_out[...], m_out[...], v_out[...] = _adamw(w_ref[...], g, m_ref[...], v_ref[...])

    in_specs, out_specs, out_shape, args = [], [], [], []
    for partial, own_block, slots, w, m, v, lead in items:
        rows, cols = w.shape
        tr = rows // steps
        tile = pl.BlockSpec((tr, cols), lambda i, me_ref: (i, 0))
        if lead is None:
            in_specs += [
                pl.BlockSpec((tr, cols), lambda i, me_ref, own_block=own_block, tr=tr: own_block(i, me_ref[0], tr)),
                pl.BlockSpec((N_DEV, tr, cols), lambda i, me_ref: (0, i, 0))]
        else:
            in_specs += [
                pl.BlockSpec((None, tr, cols),
                             lambda i, me_ref, own_block=own_block, tr=tr, lead=lead: (lead, *own_block(i, me_ref[0], tr))),
                pl.BlockSpec((N_DEV, None, tr, cols), lambda i, me_ref, lead=lead: (0, lead, i, 0))]
        in_specs += [tile, tile, tile]
        out_specs += [tile] * 4
        out_shape += [jax.ShapeDtypeStruct((rows, cols), F32)] * 4
        args += [partial, slots, w, m, v]
    res = _pcall(
        body, name="reduce_adamw", out_shape=out_shape,
        grid_spec=pltpu.PrefetchScalarGridSpec(
            num_scalar_prefetch=1, grid=(steps,), in_specs=in_specs, out_specs=out_specs),
        compiler_params=_params(("arbitrary",)),
    )(me, *args)
    return [res[4 * j:4 * j + 4] for j in range(n)]


SM_GMIX, SM_GV, SM_GFF, SM_GFIN, SM_LOSS, SM_BGATE, SM_BS, SM_ROWS = 0, 1, 2, 3, 4, 5, 7, 8


def _pack_small(st_ffn, st_mix, st_in, dbg, dbs, conv_rows):
    def body(ffn_ref, mix_ref, in_ref, dbg_ref, dbs_ref, sm_ref, conv_ref):
        sm_ref[SM_GMIX:SM_GMIX + 1, :] = in_ref[0:1, :]
        sm_ref[SM_GV:SM_GV + 1, :] = mix_ref[ST_GV:ST_GV + 1, :]
        sm_ref[SM_GFF:SM_GFF + 1, :] = ffn_ref[ST_GFF:ST_GFF + 1, :]
        sm_ref[SM_GFIN:SM_GFIN + 1, :] = ffn_ref[ST_GFIN:ST_GFIN + 1, :]
        sm_ref[SM_LOSS:SM_LOSS + 1, :] = ffn_ref[ST_LOSS:ST_LOSS + 1, :]
        sm_ref[SM_BGATE:SM_BGATE + 1, :] = dbg_ref[0:1, :D]
        sm_ref[SM_BGATE + 1:SM_BGATE + 2, :] = dbg_ref[0:1, D:]
        for g in range(N_GROUPS):
            sm_ref[SM_BS:SM_BS + 1, g * SG:(g + 1) * SG] = jnp.sum(dbs_ref[g].T, axis=0, keepdims=True)
        conv_ref[...] = jnp.zeros_like(conv_ref)
        for p in range(N_DEV):
            conv_ref[p, 0:conv_rows, :] = mix_ref[ST_CONV:ST_CONV + conv_rows, p * LANE:(p + 1) * LANE]

    return _pcall(
        body, name="pack_small",
        out_shape=(jax.ShapeDtypeStruct((SM_ROWS, D), F32), jax.ShapeDtypeStruct((N_DEV, 8, LANE), F32)),
        in_specs=[_VMEM] * 5, out_specs=[_VMEM] * 2, compiler_params=_params(),
    )(st_ffn, st_mix, st_in, dbg, dbs)


def _small_update(sm_slots, ws_own, ws_slots, conv_slots, params, conv_rows):
    flat = [a for t in params for a in t]

    def body(sm_ref, wso_ref, ws_ref, conv_ref, *refs):
        ins, outs = refs[:len(flat)], refs[len(flat):]
        loss_ref, outs = outs[0], outs[1:]
        _, me = _position()
        sm = _slot_sum(sm_ref)
        loss_ref[...] = (0.5 / D) * jnp.sum(sm[SM_LOSS:SM_LOSS + 1, :], axis=1, keepdims=True)

        def update(n, g, at=lambda r: r):
            w_ref, m_ref, v_ref = [at(r) for r in ins[3 * n:3 * n + 3]]
            g_out, d_out, m_out, v_out = [at(r) for r in outs[4 * n:4 * n + 4]]
            g_out[...] = g
            d_out[...], m_out[...], v_out[...] = _adamw(w_ref[...], g, m_ref[...], v_ref[...])

        for n, row in enumerate((SM_GMIX, SM_GV, SM_GFF, SM_GFIN)):
            update(n, sm[row:row + 1, :])
        for half in range(2):
            update(4, sm[SM_BGATE + half:SM_BGATE + half + 1, :], lambda r, half=half: r.at[:, pl.ds(half * D, D)])
        for g in range(N_GROUPS):
            update(5, sm[SM_BS:SM_BS + 1, g * SG:(g + 1) * SG], lambda r, g=g: r.at[0, pl.ds(g, 1), :])
        update(6, _masked_ws(_slot_sum(ws_ref, wso_ref[...], me)), lambda r: r.at[0])
        d_conv = _slot_sum(conv_ref)
        for j in range(conv_rows):
            update(7, d_conv[j:j + 1, :], lambda r, j=j: r.at[:, pl.ds(j * LANE, LANE)])

    out_shape = [jax.ShapeDtypeStruct((1, 1), F32)]
    for w, _, _ in params:
        out_shape += [jax.ShapeDtypeStruct(w.shape, F32)] * 4
    return _pcall(
        body, name="small_update", out_shape=out_shape,
        in_specs=[_VMEM] * (4 + len(flat)), out_specs=[_VMEM] * len(out_shape), compiler_params=_params(),
    )(sm_slots, ws_own, ws_slots, conv_slots, *flat)


def kernel(x, norm_mix_g, w_in, b_gate, norm_v_g, w_s, b_s, conv_w, w_proj_a, w_proj_b, w_out, norm_ff_g, w_ff1, w_ff2, norm_final_g, loss_target, m_norm_mix_g, m_w_in, m_b_gate, m_norm_v_g, m_w_s, m_b_s, m_conv_w, m_w_proj_a, m_w_proj_b, m_w_out, m_norm_ff_g, m_w_ff1, m_w_ff2, m_norm_final_g, v_norm_mix_g, v_w_in, v_b_gate, v_norm_v_g, v_w_s, v_b_s, v_conv_w, v_w_proj_a, v_w_proj_b, v_w_out, v_norm_ff_g, v_w_ff1, v_w_ff2, v_norm_final_g):
    T = x.shape[1]
    tt = min(256, T)
    tk = min(4096, T)
    conv_rows = conv_w.shape[1]

    flat = lambda a: a.reshape(1, conv_rows * LANE)
    xs = x.reshape(T, D)
    tgt = loss_target.reshape(T, D)
    g_mix, g_v, g_ff, g_fin = norm_mix_g, norm_v_g, norm_ff_g, norm_final_g.reshape(1, D)
    ws = w_s[0]
    bs3 = b_s.reshape(N_GROUPS, SG, 1)

    c_in, c_ff1 = w_in.shape[2], w_ff1.shape[2]
    r_ff2, r_p = w_ff2.shape[1], w_proj_a.shape[1]
    lane_view = lambda width: (lambda ref, p: _lane_block(ref, p, width))
    row_view = lambda rows: (lambda ref, p: _row_block(ref, p, rows))
    slots = lambda shape, dtype: lax.empty((N_DEV,) + shape, dtype)

    placed = _place_weights(w_in[0], w_ff1[0], w_ff2[0], w_proj_a[0], w_proj_b[0], w_out[0], flat(conv_w))
    mixer_views = [row_view(r_p), row_view(r_p), row_view(r_p), lane_view(LANE)]
    ffn_views = [lane_view(c_ff1), row_view(r_ff2)]
    pairs = ((0, 1), (2, 4), (3, 5), (6, 7))
    in_group = lambda refs, ks: _gather_entries(refs[:1], [lane_view(c_in)], tuple(k for k in ks if k))

    def gather_groups(refs):
        return ([in_group(refs, ks) for ks in pairs]
                + [_gather_entries(refs[1:5], mixer_views), _gather_entries(refs[5:], ffn_views)])

    sems, placed, g_token = _start_copies("gather_start", placed, gather_groups)
    me = _position()[1].reshape(1)
    W_in = placed[0]
    tp = min(2048, T)
    proj = lax.empty((T, IN_COLS), BF16)
    for n, ks in enumerate(pairs):
        W_in, = _wait_copies(f"gather_wait_in_{n}", [W_in], sems[2 * n], sems[2 * n + 1],
                             lambda refs, ks=ks: in_group(refs, ks), proj if n else g_token)
        if n == 0:
            proj, h = _in_proj_blocks(f"in_proj_{n}", ks, me, xs, W_in, proj, c_in, tp, g_mix)
        else:
            proj = _in_proj_blocks(f"in_proj_{n}", ks, me, h, W_in, proj, c_in, tp)
    n = len(pairs)
    PA, PB, WO, conv = _wait_copies(
        "gather_wait_mixer", placed[1:5], sems[2 * n], sems[2 * n + 1],
        lambda refs: _gather_entries(refs, mixer_views), proj)
    acm, pa, pb, x1 = _mixer_fwd(xs, proj, b_gate, g_v, ws, bs3, conv, PA, PB, WO, min(512, T))
    W1, W2 = _wait_copies(
        "gather_wait_ffn", placed[5:], sems[2 * n + 2], sems[2 * n + 3],
        lambda refs: _gather_entries(refs, ffn_views), x1)
    hf, s, dpre, dx2, dx1, st_ffn = _ffn_fwd_bwd(x1, tgt, g_ff, g_fin, W1, W2, min(512, T))

    d_ff2 = _weight_grad("dw_ff2", s, dx2, D, tk)
    d_ff1 = _weight_grad("dw_ff1", hf, dpre, D, tk)
    ff_views = [lane_view(c_ff1), row_view(r_ff2)]
    ff_sems, ff_arrays, ff_token = _start_copies(
        "scatter_start_ffn", [d_ff1, d_ff2, slots((D, c_ff1), BF16), slots((r_ff2, D), BF16)],
        lambda refs: [_scatter_entries(ff_views)(refs)])

    dproj, dstk, st_mix, dbg, dws, dbs = _mixer_bwd(
        dx1, proj, pa, pb, b_gate, g_v, ws, bs3, conv, PA, PB, WO, tt, deps=(ff_token,))
    d_p3 = _weight_grad_stack("dw_proj", acm, dstk, tk)

    def proj_entries(refs):
        d3, dws_ref, land3, land_ws = refs
        entries = []
        for w in range(3):
            entries += _to_peers(lambda pid, me, w=w: d3.at[w, pl.ds(pid * r_p, r_p), :],
                                 lambda me, w=w: land3.at[me, w])
        return entries + _to_peers(lambda pid, me: dws_ref, lambda me: land_ws.at[me])

    p_sems, p_arrays, p_token = _start_copies(
        "scatter_start_proj", [d_p3, dws, slots((3, r_p, D), BF16), slots(dws.shape, F32)],
        lambda refs: [proj_entries(refs)])

    d_in = _weight_grad("dw_in", h, dproj, D, tk, deps=(p_token,))
    in_views = [lane_view(c_in)]
    in_sems, in_arrays, in_token = _start_copies(
        "scatter_start_in", [d_in, slots((D, c_in), BF16)], lambda refs: [_scatter_entries(in_views)(refs)])
    grad_x, st_in = _in_proj_bwd(dproj, xs, dx1, g_mix, W_in, min(512, T), deps=(in_token,))

    small, d_conv = _pack_small(st_ffn, st_mix, st_in, dbg, dbs, conv_rows)
    r_conv, r_small = _small_exchange(d_conv, small)
    d_ff1, d_ff2, s_ff1, s_ff2 = _wait_copies(
        "scatter_wait_ffn", ff_arrays, ff_sems[0], ff_sems[1], _scatter_entries(ff_views), r_small)
    d_p3, dws, s_p3, s_ws = _wait_copies(
        "scatter_wait_proj", p_arrays, p_sems[0], p_sems[1], proj_entries, r_small)
    d_in, s_in = _wait_copies("scatter_wait_in", in_arrays, in_sems[0], in_sems[1], _scatter_entries(in_views), r_small)

    own_cols = lambda i, me, tr: (i, me)
    own_rows = lambda rows: (lambda i, me, tr: (me * (rows // tr) + i, 0))
    names = ["w_in", "w_ff1", "w_ff2", "w_proj_a", "w_proj_b", "w_out"]
    quads = _reduce_adamw(me, [
        (d_in, own_cols, s_in, w_in[0], m_w_in[0], v_w_in[0], None),
        (d_ff1, own_cols, s_ff1, w_ff1[0], m_w_ff1[0], v_w_ff1[0], None),
        (d_ff2, own_rows(r_ff2), s_ff2, w_ff2[0], m_w_ff2[0], v_w_ff2[0], None),
        (d_p3, own_rows(r_p), s_p3, w_proj_a[0], m_w_proj_a[0], v_w_proj_a[0], 0),
        (d_p3, own_rows(r_p), s_p3, w_proj_b[0], m_w_proj_b[0], v_w_proj_b[0], 1),
        (d_p3, own_rows(r_p), s_p3, w_out[0], m_w_out[0], v_w_out[0], 2),
    ], 8)
    big = dict(zip(names, quads))

    one = lambda a: a.reshape(1, D)
    small_params = [
        (norm_mix_g, m_norm_mix_g, v_norm_mix_g),
        (norm_v_g, m_norm_v_g, v_norm_v_g),
        (norm_ff_g, m_norm_ff_g, v_norm_ff_g),
        (one(norm_final_g), one(m_norm_final_g), one(v_norm_final_g)),
        (b_gate, m_b_gate, v_b_gate),
        (b_s, m_b_s, v_b_s),
        (w_s, m_w_s, v_w_s),
        (flat(conv_w), flat(m_conv_w), flat(v_conv_w)),
    ]
    res = _small_update(r_small, dws, s_ws, r_conv, small_params, conv_rows)
    loss = res[0].reshape(())
    names = ["norm_mix_g", "norm_v_g", "norm_ff_g", "norm_final_g", "b_gate", "b_s", "w_s", "conv_w"]
    out = {name: list(res[1 + 4 * n:5 + 4 * n]) for n, name in enumerate(names)}
    out["norm_final_g"] = [q.reshape(norm_final_g.shape) for q in out["norm_final_g"]]
    out["conv_w"] = [q.reshape(conv_w.shape) for q in out["conv_w"]]
    for name, quad in big.items():
        out[name] = [q[None] for q in quad]

    order = ["norm_mix_g", "w_in", "b_gate", "norm_v_g", "w_s", "b_s", "conv_w", "w_proj_a", "w_proj_b", "w_out",
             "norm_ff_g", "w_ff1", "w_ff2", "norm_final_g"]
    grads = [out[n][0] for n in order]
    deltas = [out[n][1] for n in order]
    new_m = [out[n][2] for n in order]
    new_v = [out[n][3] for n in order]
    return (loss, grad_x.reshape(x.shape), *grads, *deltas, *new_m, *new_v)
```

```python
import math

import jax
import jax.numpy as jnp
from jax import lax
from jax.experimental import pallas as pl
from jax.experimental.pallas import tpu as pltpu

F32 = jnp.float32
BF16 = jnp.bfloat16

N_DEV = 8
D = 1024
D_FF = 4096
IN_COLS = 7 * D
SG = 128
N_GROUPS = 8
CHUNK = 64
EPS = 1e-6
HALO = 16
LANE = 128
VMEM_LIMIT = 62 * 1024 * 1024

ADAM_LR = 0.001
ADAM_B1 = 0.9
ADAM_B2 = 0.999
ADAM_EPS = 1e-08
ADAM_WD = 0.01
ADAM_STEP = 10

SQRT_HALF = math.sqrt(0.5)
PDF_EXP2_SCALE = -0.5 * math.log2(math.e)
PDF_EXP2_SHIFT = math.log2(1.0 / math.sqrt(2.0 * math.pi))

_REL = [(dx, dy, dc) for dx in (0, 1) for dy in (0, 1) for dc in (0, 1)]

_VMEM = pl.BlockSpec(memory_space=pltpu.VMEM)
_ANY = pl.BlockSpec(memory_space=pl.ANY)


def _pcall(body, **kw):
    return pl.pallas_call(body, **kw)


def _params(sem=None):
    if sem is None:
        return pltpu.CompilerParams(vmem_limit_bytes=VMEM_LIMIT)
    return pltpu.CompilerParams(dimension_semantics=sem, vmem_limit_bytes=VMEM_LIMIT)


def _const_spec(shape):
    nd = len(shape)
    return pl.BlockSpec(shape, lambda *_: (0,) * nd, pipeline_mode=pl.Buffered(1))


def _after(body, n_in, deps):
    def wrapped(*refs):
        return body(*refs[:n_in], *refs[n_in + len(deps):])
    return wrapped


def _mm(a, b):
    return jnp.dot(a, b, preferred_element_type=F32)


def _nt(a, b):
    return lax.dot_general(a, b, (((1,), (1,)), ((), ())), preferred_element_type=F32)


def _tn(a, b):
    return lax.dot_general(a, b, (((0,), (0,)), ((), ())), preferred_element_type=F32)


def _rms(x):
    r = lax.rsqrt(jnp.mean(x * x, axis=-1, keepdims=True) + EPS)
    return x * r, r


def _rms_bwd(dyg, xh, r):
    return r * (dyg - xh * jnp.mean(dyg * xh, axis=-1, keepdims=True))


def _gelu(x):
    cdf = 0.5 * (1.0 + lax.erf(x * SQRT_HALF))
    return x * cdf, cdf


def _gelu_grad(x, cdf):
    return cdf + x * jnp.exp2(x * x * PDF_EXP2_SCALE + PDF_EXP2_SHIFT)


def _masked_ws(ws):
    i = lax.broadcasted_iota(jnp.int32, (SG, SG), 0)
    j = lax.broadcasted_iota(jnp.int32, (SG, SG), 1)
    keep = jnp.logical_or(j < CHUNK, i >= CHUNK)
    return jnp.where(keep[None], ws, jnp.zeros_like(ws))


def _shift_down(halo, q, k):
    ext = jnp.concatenate([halo, q], axis=0)
    return pltpu.roll(ext, k, 0)[halo.shape[0]:]


def _shift_up(q, nxt, k):
    ext = jnp.concatenate([q, nxt], axis=0)
    return pltpu.roll(ext, ext.shape[0] - k, 0)[:q.shape[0]]


def _col(k):
    return slice(k * D, (k + 1) * D)


def _position():
    x, y, c = lax.axis_index("x"), lax.axis_index("y"), lax.axis_index("c")
    return (x, y, c), 4 * x + 2 * y + c


def _lane_block(ref, p, width):
    return ref.at[:, pl.ds(pl.multiple_of(p * width, LANE), width)]


def _row_block(ref, p, rows):
    return ref.at[pl.ds(p * rows, rows), :]


_HBM = pl.BlockSpec(memory_space=pltpu.HBM)
_SEM = pl.BlockSpec(memory_space=pltpu.SEMAPHORE)
_EFFECT = pltpu.SideEffectType.DATAFLOW_SIDE_EFFECTING


REL_ORDER = (1, 2, 4, 3, 5, 6, 7)


def _to_peers(src_of, dst_of, order=REL_ORDER):
    return [(src_of, dst_of, k) for k in order]


def _remote_copies(entries, send_sems, recv_sems):
    (x, y, c), me = _position()
    copies = []
    for n, (src_of, dst_of, k) in enumerate(entries):
        dx, dy, dc = _REL[k]
        peer = (1 - x if dx else x, 1 - y if dy else y, 1 - c if dc else c)
        pid = 4 * peer[0] + 2 * peer[1] + peer[2]
        copies.append(pltpu.make_async_remote_copy(
            src_ref=src_of(pid, me), dst_ref=dst_of(me), send_sem=send_sems.at[n], recv_sem=recv_sems.at[n],
            device_id=peer, device_id_type=pl.DeviceIdType.MESH))
    return copies


def _start_copies(name, arrays, make_groups):
    n = len(arrays)
    sizes = [len(g) for g in make_groups([None] * n)]

    def body(*refs):
        sems, token = refs[n:n + 2 * len(sizes)], refs[-1]
        for g, entries in enumerate(make_groups(refs[:n])):
            for cp in _remote_copies(entries, sems[2 * g], sems[2 * g + 1]):
                cp.start()
        token[...] = jnp.zeros_like(token)

    out_shape = []
    for size in sizes:
        out_shape += [pltpu.SemaphoreType.DMA((size,))] * 2
    out_shape += [pltpu.HBM(a.shape, a.dtype) for a in arrays] + [jax.ShapeDtypeStruct((8, LANE), F32)]
    res = _pcall(
        body, name=name, out_shape=out_shape,
        in_specs=[_HBM] * n, out_specs=[_SEM] * (2 * len(sizes)) + [_HBM] * n + [_VMEM],
        input_output_aliases={i: 2 * len(sizes) + i for i in range(n)},
        compiler_params=pltpu.CompilerParams(has_side_effects=_EFFECT),
    )(*[pltpu.with_memory_space_constraint(a, pltpu.HBM) for a in arrays])
    return res[:2 * len(sizes)], res[2 * len(sizes):-1], res[-1]


def _wait_copies(name, arrays, send_sems, recv_sems, make_entries, after):
    n = len(arrays)

    def body(*refs):
        for cp in _remote_copies(make_entries(refs[:n]), refs[n], refs[n + 1]):
            cp.wait_send()
            cp.wait_recv()

    return _pcall(
        body, name=name, out_shape=[pltpu.HBM(a.shape, a.dtype) for a in arrays],
        in_specs=[_HBM] * n + [_SEM, _SEM, _ANY], out_specs=[_HBM] * n,
        input_output_aliases={i: i for i in range(n)},
        compiler_params=pltpu.CompilerParams(has_side_effects=_EFFECT),
    )(*arrays, send_sems, recv_sems, after)


def _place_weights(w_in, w_ff1, w_ff2, w_pa, w_pb, w_o, conv_flat):
    n_items = 7
    c_in, c_ff1 = w_in.shape[1], w_ff1.shape[1]
    r_ff2, r_p = w_ff2.shape[0], w_pa.shape[0]

    def body(win_ref, w1_ref, w2_ref, pa_ref, pb_ref, wo_ref, cw_ref,
             win_o, pa_o, pb_o, wo_o, cw_o, w1_o, w2_o,
             s_win, s_w1, s_w2, s_pa, s_pb, s_wo, s_cw, sems):
        _, me = _position()
        s_cw[...] = jnp.zeros_like(s_cw)
        for j in range(conv_flat.shape[1] // LANE):
            s_cw[j:j + 1, :] = cw_ref[:, j * LANE:(j + 1) * LANE]
        for src, stage in ((win_ref, s_win), (w1_ref, s_w1), (w2_ref, s_w2),
                           (pa_ref, s_pa), (pb_ref, s_pb), (wo_ref, s_wo)):
            stage[...] = src[...].astype(BF16)
        pairs = [
            (s_win, _lane_block(win_o, me, c_in)), (s_pa, _row_block(pa_o, me, r_p)),
            (s_pb, _row_block(pb_o, me, r_p)), (s_wo, _row_block(wo_o, me, r_p)),
            (s_cw, _lane_block(cw_o, me, LANE)),
            (s_w1, _lane_block(w1_o, me, c_ff1)), (s_w2, _row_block(w2_o, me, r_ff2)),
        ]
        copies = [pltpu.make_async_copy(s, d, sems.at[n]) for n, (s, d) in enumerate(pairs)]
        for cp in copies:
            cp.start()
        for cp in copies:
            cp.wait()

    out_shape = (
        jax.ShapeDtypeStruct((D, N_DEV * c_in), BF16),
        jax.ShapeDtypeStruct((N_DEV * r_p, D), BF16),
        jax.ShapeDtypeStruct((N_DEV * r_p, D), BF16),
        jax.ShapeDtypeStruct((N_DEV * r_p, D), BF16),
        jax.ShapeDtypeStruct((8, N_DEV * LANE), F32),
        jax.ShapeDtypeStruct((D, N_DEV * c_ff1), BF16),
        jax.ShapeDtypeStruct((N_DEV * r_ff2, D), BF16),
    )
    return _pcall(
        body, name="place_weights", out_shape=out_shape,
        in_specs=[_VMEM] * n_items, out_specs=[_ANY] * n_items,
        scratch_shapes=[pltpu.VMEM(w.shape, BF16) for w in (w_in, w_ff1, w_ff2, w_pa, w_pb, w_o)]
        + [pltpu.VMEM((8, LANE), F32), pltpu.SemaphoreType.DMA((n_items,))],
        compiler_params=_params(),
    )(w_in, w_ff1, w_ff2, w_pa, w_pb, w_o, conv_flat)


def _gather_entries(refs, views, order=REL_ORDER):
    entries = []
    for r, v in zip(refs, views):
        entries += _to_peers(lambda pid, me, r=r, v=v: v(r, me), lambda me, r=r, v=v: v(r, me), order)
    return entries


def _scatter_entries(views):
    def make(refs):
        srcs, lands = refs[:len(views)], refs[len(views):]
        entries = []
        for r, v, l in zip(srcs, views, lands):
            entries += _to_peers(lambda pid, me, r=r, v=v: v(r, pid), lambda me, l=l: l.at[me])
        return entries
    return make


def _in_proj_blocks(name, ks, me, act, w_in, proj, width, tt, g_mix=None):
    T = act.shape[0]
    nt = T // tt
    nb = len(ks)
    normed = g_mix is not None

    def body(me_ref, act_ref, *refs):
        if normed:
            g_ref, refs = refs[0], refs[1:]
        w_refs, refs = refs[:nb], refs[nb + 1:]
        if normed:
            proj_out, h_out, w_s, o_s, sems = refs
        else:
            proj_out, w_s, o_s, sems = refs
        i = pl.program_id(0)

        @pl.when(i == 0)
        def _():
            for part, w_ref in enumerate(w_refs):
                w_s[:, part * width:(part + 1) * width] = w_ref[...]

        def copies(slot, step):
            rows = pl.ds(step * tt, tt)
            return [pltpu.make_async_copy(
                o_s.at[slot, :, part * width:(part + 1) * width],
                proj_out.at[rows, pl.ds(pl.multiple_of((me_ref[0] ^ ks[part]) * width, LANE), width)],
                sems.at[slot, part]) for part in range(nb)]

        slot = i % 2

        @pl.when(i >= 2)
        def _():
            for cp in copies(slot, i - 2):
                cp.wait()

        if normed:
            xh, _ = _rms(act_ref[...])
            h = (xh * g_ref[...]).astype(BF16)
            h_out[...] = h
        else:
            h = act_ref[...]
        o_s[slot] = _mm(h, w_s[...]).astype(BF16)
        for cp in copies(slot, i):
            cp.start()

        @pl.when(i == nt - 1)
        def _():
            for cp in copies(slot, i) + (copies(1 - slot, i - 1) if nt > 1 else []):
                cp.wait()

    tile = pl.BlockSpec((tt, D), lambda i, b: (i, 0))
    w_spec = lambda part: pl.BlockSpec((D, width), lambda i, m: (0, m[0] ^ ks[part]), pipeline_mode=pl.Buffered(1))
    extra_in = [_const_spec(g_mix.shape)] if normed else []
    extra_args = [g_mix] if normed else []
    out_shape = [jax.ShapeDtypeStruct(proj.shape, proj.dtype)] + ([jax.ShapeDtypeStruct((T, D), BF16)] if normed else [])
    res = _pcall(
        body, name=name, out_shape=out_shape,
        grid_spec=pltpu.PrefetchScalarGridSpec(
            num_scalar_prefetch=1, grid=(nt,),
            in_specs=[tile] + extra_in + [w_spec(part) for part in range(nb)] + [_ANY],
            out_specs=[_ANY] + ([tile] if normed else []),
            scratch_shapes=[pltpu.VMEM((D, nb * width), BF16), pltpu.VMEM((2, tt, nb * width), BF16),
                            pltpu.SemaphoreType.DMA((2, nb))]),
        input_output_aliases={2 + nb + len(extra_in): 0},
        compiler_params=_params(("arbitrary",)),
    )(me, act, *extra_args, *([w_in] * nb), proj)
    return res if normed else res[0]


def _mixer_fwd(x, proj, b_gate, g_v, w_s, b_s3, conv, w_pa, w_pb, w_o, tt):
    T = x.shape[0]
    nt = T // tt
    nb = tt // SG

    def body(x_ref, proj_ref, bg_ref, gv_ref, ws_ref, bs_ref, cw_ref, pa_w, pb_w, wo_w,
             acm_ref, pa_ref, pb_ref, x1_ref, q_carry, mix_s):
        @pl.when(pl.program_id(0) == 0)
        def _():
            q_carry[...] = jnp.zeros_like(q_carry)

        def proj(k):
            return proj_ref[:, _col(k)].astype(F32)

        vg, _ = _gelu(proj(1))
        vh, _ = _rms(vg)
        vp = (vh * gv_ref[...]).astype(BF16)
        wm = _masked_ws(ws_ref[...]).astype(BF16)
        for n in range(nb):
            rows = slice(n * SG, (n + 1) * SG)
            for g in range(N_GROUPS):
                cols = slice(g * SG, (g + 1) * SG)
                mix_s[rows, cols] = _mm(wm[g], vp[rows, cols]) + bs_ref[g]
        ug, _ = _gelu(proj(0))
        a = (ug * mix_s[...]).astype(BF16)
        acm_ref[:, _col(0)] = a
        pa = _mm(a, pa_w[...])
        pa_ref[...] = pa.astype(BF16)
        m = jax.nn.sigmoid(proj(5) + bg_ref[:, :D]) * pa

        bgate = proj(2)
        q = proj(3) * proj(4)
        halo = q_carry[...]
        cv = cw_ref[0:1, :] * _shift_down(halo, q, 2) + cw_ref[1:2, :] * _shift_down(halo, q, 1) + cw_ref[2:3, :] * q
        q_carry[...] = q[tt - q_carry.shape[0]:, :]
        c = (bgate * cv).astype(BF16)
        acm_ref[:, _col(1)] = c
        pb = _mm(c, pb_w[...])
        pb_ref[...] = pb.astype(BF16)
        m = (m + jax.nn.sigmoid(proj(6) + bg_ref[:, D:]) * pb).astype(BF16)
        acm_ref[:, _col(2)] = m
        x1_ref[...] = x_ref[...] + _mm(m, wo_w[...])

    tile = lambda w: pl.BlockSpec((tt, w), lambda i: (i, 0))
    out_shape = ([jax.ShapeDtypeStruct((T, 3 * D), BF16)] + [jax.ShapeDtypeStruct((T, D), BF16)] * 2
                 + [jax.ShapeDtypeStruct((T, D), F32)])
    return _pcall(
        body, name="mixer_fwd", grid=(nt,), out_shape=out_shape,
        in_specs=[tile(D), tile(IN_COLS), _const_spec(b_gate.shape),
                  _const_spec(g_v.shape), _const_spec(w_s.shape), _const_spec(b_s3.shape), _const_spec(conv.shape),
                  _const_spec(w_pa.shape), _const_spec(w_pb.shape), _const_spec(w_o.shape)],
        out_specs=[tile(3 * D)] + [tile(D)] * 3,
        scratch_shapes=[pltpu.VMEM((8, D), F32), pltpu.VMEM((tt, D), F32)],
        compiler_params=_params(("arbitrary",)),
    )(x, proj, b_gate, g_v, w_s, b_s3, conv, w_pa, w_pb, w_o)


ST_GFIN, ST_GFF, ST_LOSS = 0, 1, 2


def _ffn_fwd_bwd(x1, tgt, g_ff, g_fin, w1, w2, tt):
    T = x1.shape[0]
    nt = T // tt
    nk = D_FF // D

    def body(x1_ref, tgt_ref, gff_ref, gfin_ref, w1_ref, w2_ref,
             hf_ref, s_ref, dpre_ref, dx2_ref, dx1_ref, st_ref, z_s):
        @pl.when(pl.program_id(0) == 0)
        def _():
            st_ref[...] = jnp.zeros_like(st_ref)

        x1 = x1_ref[...]
        xh1, r1 = _rms(x1)
        hf = (xh1 * gff_ref[...]).astype(BF16)
        hf_ref[...] = hf
        acc = jnp.zeros((tt, D), F32)
        for k in range(nk):
            z = jnp.maximum(_mm(hf, w1_ref[:, _col(k)]), 0.0)
            z_s[:, _col(k)] = z.astype(BF16)
            s = (z * z).astype(BF16)
            s_ref[:, _col(k)] = s
            acc = acc + _mm(s, w2_ref[_col(k), :])
        x2 = x1 + acc
        xh2, r2 = _rms(x2)
        diff = xh2 * gfin_ref[...] - tgt_ref[...]
        st_ref[ST_LOSS:ST_LOSS + 1, :] += jnp.sum(diff * diff, axis=0, keepdims=True)
        dy = diff * (1.0 / D)
        st_ref[ST_GFIN:ST_GFIN + 1, :] += jnp.sum(dy * xh2, axis=0, keepdims=True)
        dx2 = _rms_bwd(dy * gfin_ref[...], xh2, r2)
        dx2b = dx2.astype(BF16)
        dx2_ref[...] = dx2b
        dhf = jnp.zeros((tt, D), F32)
        for k in range(nk):
            dpre = (_nt(dx2b, w2_ref[_col(k), :]) * (2.0 * z_s[:, _col(k)].astype(F32))).astype(BF16)
            dpre_ref[:, _col(k)] = dpre
            dhf = dhf + _nt(dpre, w1_ref[:, _col(k)])
        st_ref[ST_GFF:ST_GFF + 1, :] += jnp.sum(dhf * xh1, axis=0, keepdims=True)
        dx1_ref[...] = dx2 + _rms_bwd(dhf * gff_ref[...], xh1, r1)

    tile = lambda w: pl.BlockSpec((tt, w), lambda i: (i, 0))
    out_shape = (jax.ShapeDtypeStruct((T, D), BF16), jax.ShapeDtypeStruct((T, D_FF), BF16),
                 jax.ShapeDtypeStruct((T, D_FF), BF16), jax.ShapeDtypeStruct((T, D), BF16),
                 jax.ShapeDtypeStruct((T, D), F32), jax.ShapeDtypeStruct((8, D), F32))
    return _pcall(
        body, name="ffn_fwd_bwd", grid=(nt,), out_shape=out_shape,
        in_specs=[tile(D), tile(D), _const_spec(g_ff.shape), _const_spec(g_fin.shape),
                  _const_spec(w1.shape), _const_spec(w2.shape)],
        out_specs=[tile(D), tile(D_FF), tile(D_FF), tile(D), tile(D), pl.BlockSpec((8, D), lambda i: (0, 0))],
        scratch_shapes=[pltpu.VMEM((tt, D_FF), BF16)],
        compiler_params=_params(("arbitrary",)),
    )(x1, tgt, g_ff, g_fin, w1, w2)


ST_GV, ST_CONV = 0, 1


def _mixer_bwd(dx1, proj, pa, pb, b_gate, g_v, w_s, b_s3, conv, w_pa, w_pb, w_o, tt, deps=()):
    T = dx1.shape[0]
    nt = T // tt
    nb = tt // SG
    hb = tt // HALO

    def body(dx1_ref, proj_ref, cgh_ref, xsh_ref, pa_ref, pb_ref,
             bg_ref, gv_ref, ws_ref, bs_ref, cw_ref, pa_w, pb_w, wo_w,
             dproj_ref, dstk_ref, st_ref, dbg_ref, dws_ref, dbs_ref, d_carry, mix_s, dvp_s):
        i = pl.program_id(0)

        @pl.when(i == 0)
        def _():
            st_ref[...] = jnp.zeros_like(st_ref)
            dbg_ref[...] = jnp.zeros_like(dbg_ref)
            dws_ref[...] = jnp.zeros_like(dws_ref)
            dbs_ref[...] = jnp.zeros_like(dbs_ref)
            d_carry[...] = jnp.zeros_like(d_carry)

        def pj(k):
            return proj_ref[:, _col(k)].astype(F32)

        def put(k, val):
            dproj_ref[:, _col(k)] = val.astype(BF16)

        dx1b = dx1_ref[...].astype(BF16)
        dstk_ref[:, _col(2)] = dx1b
        dm = _nt(dx1b, wo_w[...])
        s_a = jax.nn.sigmoid(pj(5) + bg_ref[:, :D])
        s_b = jax.nn.sigmoid(pj(6) + bg_ref[:, D:])
        dpa = dm * s_a
        dpb = dm * s_b
        dpa_b = dpa.astype(BF16)
        dpb_b = dpb.astype(BF16)
        dstk_ref[:, _col(0)] = dpa_b
        dstk_ref[:, _col(1)] = dpb_b
        dga = dpa * pa_ref[...].astype(F32) * (1.0 - s_a)
        dgb = dpb * pb_ref[...].astype(F32) * (1.0 - s_b)
        dbg_ref[0:1, :D] += jnp.sum(dga, axis=0, keepdims=True)
        dbg_ref[0:1, D:] += jnp.sum(dgb, axis=0, keepdims=True)
        put(5, dga)
        put(6, dgb)
        da = _nt(dpa_b, pa_w[...])
        dc = _nt(dpb_b, pb_w[...])

        v = pj(1)
        vg, v_cdf = _gelu(v)
        vh, rv = _rms(vg)
        vp = (vh * gv_ref[...]).astype(BF16)
        wm = _masked_ws(ws_ref[...]).astype(BF16)
        for n in range(nb):
            rows = slice(n * SG, (n + 1) * SG)
            for g in range(N_GROUPS):
                cols = slice(g * SG, (g + 1) * SG)
                mix_s[rows, cols] = _mm(wm[g], vp[rows, cols]) + bs_ref[g]
        u = pj(0)
        ug, u_cdf = _gelu(u)
        put(0, da * mix_s[...] * _gelu_grad(u, u_cdf))
        dmix = da * ug
        dmix_b = dmix.astype(BF16)
        for n in range(nb):
            rows = slice(n * SG, (n + 1) * SG)
            for g in range(N_GROUPS):
                cols = slice(g * SG, (g + 1) * SG)
                blk = dmix_b[rows, cols]
                dws_ref[g] += _nt(blk, vp[rows, cols])
                dbs_ref[g] += dmix[rows, cols]
                dvp_s[rows, cols] = _tn(wm[g], blk)
        dvp = dvp_s[...]
        st_ref[ST_GV:ST_GV + 1, :] += jnp.sum(dvp * vh, axis=0, keepdims=True)
        put(1, _rms_bwd(dvp * gv_ref[...], vh, rv) * _gelu_grad(v, v_cdf))

        bgate, cg, xs = pj(2), pj(3), pj(4)
        q = cg * xs
        has_prev = (i < nt - 1).astype(F32)
        halo = cgh_ref[...].astype(F32) * xsh_ref[...].astype(F32) * has_prev
        q2 = _shift_down(halo, q, 2)
        q1 = _shift_down(halo, q, 1)
        w0, w1, w2 = cw_ref[0:1, :], cw_ref[1:2, :], cw_ref[2:3, :]
        put(2, dc * (w0 * q2 + w1 * q1 + w2 * q))
        dcv = dc * bgate
        st_ref[ST_CONV:ST_CONV + 1, :] += jnp.sum(dcv * q2, axis=0, keepdims=True)
        st_ref[ST_CONV + 1:ST_CONV + 2, :] += jnp.sum(dcv * q1, axis=0, keepdims=True)
        st_ref[ST_CONV + 2:ST_CONV + 3, :] += jnp.sum(dcv * q, axis=0, keepdims=True)
        nxt = d_carry[...]
        dq = w2 * dcv + w1 * _shift_up(dcv, nxt, 1) + w0 * _shift_up(dcv, nxt, 2)
        d_carry[...] = dcv[:d_carry.shape[0], :]
        put(3, dq * xs)
        put(4, dq * cg)

    rev = lambda i: nt - 1 - i
    tile = lambda w: pl.BlockSpec((tt, w), lambda i: (rev(i), 0))
    halo_spec = lambda k: pl.BlockSpec((HALO, D), lambda i: (jnp.maximum(rev(i) * hb - 1, 0), k))
    res = lambda shape: pl.BlockSpec(shape, lambda i: (0,) * len(shape))
    out_shape = (jax.ShapeDtypeStruct((T, IN_COLS), BF16), jax.ShapeDtypeStruct((T, 3 * D), BF16),
                 jax.ShapeDtypeStruct((8, D), F32), jax.ShapeDtypeStruct((8, 2 * D), F32),
                 jax.ShapeDtypeStruct((N_GROUPS, SG, SG), F32), jax.ShapeDtypeStruct((N_GROUPS, SG, SG), F32))
    return _pcall(
        _after(body, 14, deps), name="mixer_bwd", grid=(nt,), out_shape=out_shape,
        in_specs=[tile(D), tile(IN_COLS), halo_spec(3), halo_spec(4), tile(D), tile(D),
                  _const_spec(b_gate.shape), _const_spec(g_v.shape), _const_spec(w_s.shape),
                  _const_spec(b_s3.shape), _const_spec(conv.shape),
                  _const_spec(w_pa.shape), _const_spec(w_pb.shape), _const_spec(w_o.shape)] + [_ANY] * len(deps),
        out_specs=[tile(IN_COLS), tile(3 * D), res((8, D)), res((8, 2 * D)),
                   res((N_GROUPS, SG, SG)), res((N_GROUPS, SG, SG))],
        scratch_shapes=[pltpu.VMEM((8, D), F32), pltpu.VMEM((tt, D), F32), pltpu.VMEM((tt, D), F32)],
        compiler_params=_params(("arbitrary",)),
    )(dx1, proj, proj, proj, pa, pb, b_gate, g_v, w_s, b_s3, conv, w_pa, w_pb, w_o, *deps)


def _in_proj_bwd(dproj, x, dx1, g_mix, w_in, tt, deps=()):
    T = x.shape[0]

    def body(dproj_ref, x_ref, dx1_ref, gmix_ref, win_ref, gx_ref, st_ref):
        @pl.when(pl.program_id(0) == 0)
        def _():
            st_ref[...] = jnp.zeros_like(st_ref)

        dh = _nt(dproj_ref[...], win_ref[...])
        xh, r = _rms(x_ref[...])
        st_ref[0:1, :] += jnp.sum(dh * xh, axis=0, keepdims=True)
        gx_ref[...] = dx1_ref[...] + _rms_bwd(dh * gmix_ref[...], xh, r)

    tile = lambda w: pl.BlockSpec((tt, w), lambda i: (i, 0))
    return _pcall(
        _after(body, 5, deps), name="in_proj_bwd", grid=(T // tt,),
        out_shape=(jax.ShapeDtypeStruct((T, D), F32), jax.ShapeDtypeStruct((8, D), F32)),
        in_specs=[tile(IN_COLS), tile(D), tile(D), _const_spec(g_mix.shape), _const_spec(w_in.shape)]
        + [_ANY] * len(deps),
        out_specs=[tile(D), pl.BlockSpec((8, D), lambda i: (0, 0))],
        compiler_params=_params(("arbitrary",)),
    )(dproj, x, dx1, g_mix, w_in, *deps)


def _weight_grad(name, act, dout, bc, tk, deps=()):
    T, n_in = act.shape
    n_out = dout.shape[1]
    nk = T // tk
    bi = min(n_in, D)

    def body(a_ref, d_ref, o_ref, acc):
        k = pl.program_id(2)

        @pl.when(k == 0)
        def _():
            acc[...] = jnp.zeros_like(acc)

        acc[...] += _tn(a_ref[...], d_ref[...])

        @pl.when(k == nk - 1)
        def _():
            o_ref[...] = acc[...].astype(o_ref.dtype)

    return _pcall(
        _after(body, 2, deps), name=name, grid=(n_in // bi, n_out // bc, nk),
        out_shape=jax.ShapeDtypeStruct((n_in, n_out), BF16),
        in_specs=[pl.BlockSpec((tk, bi), lambda i, j, k: (k, i)), pl.BlockSpec((tk, bc), lambda i, j, k: (k, j))]
        + [_ANY] * len(deps),
        out_specs=pl.BlockSpec((bi, bc), lambda i, j, k: (i, j)),
        scratch_shapes=[pltpu.VMEM((bi, bc), F32)],
        compiler_params=_params(("arbitrary", "arbitrary", "arbitrary")),
    )(act, dout, *deps)


def _weight_grad_stack(name, acts, douts, tk):
    T, n = acts.shape[0], acts.shape[1] // D
    nk = T // tk

    def body(a_ref, d_ref, o_ref, acc):
        k = pl.program_id(1)

        @pl.when(k == 0)
        def _():
            acc[...] = jnp.zeros_like(acc)

        acc[...] += _tn(a_ref[...], d_ref[...])

        @pl.when(k == nk - 1)
        def _():
            o_ref[...] = acc[...].astype(o_ref.dtype)

    tile = pl.BlockSpec((tk, D), lambda w, k: (k, w))
    return _pcall(
        body, name=name, grid=(n, nk), out_shape=jax.ShapeDtypeStruct((n, D, D), BF16),
        in_specs=[tile, tile], out_specs=pl.BlockSpec((None, D, D), lambda w, k: (w, 0, 0)),
        scratch_shapes=[pltpu.VMEM((D, D), F32)],
        compiler_params=_params(("arbitrary", "arbitrary")),
    )(acts, douts)


def _adamw(w, g, m, v):
    m = ADAM_B1 * m + (1.0 - ADAM_B1) * g
    v = ADAM_B2 * v + (1.0 - ADAM_B2) * (g * g)
    m_hat = m / (1.0 - ADAM_B1 ** ADAM_STEP)
    v_hat = v / (1.0 - ADAM_B2 ** ADAM_STEP)
    delta = -ADAM_LR * (m_hat / (jnp.sqrt(v_hat) + ADAM_EPS) + ADAM_WD * w)
    return delta, m, v


def _slot_sum(ref, own=None, me=None):
    g = None
    for s in range(N_DEV):
        term = ref[s] if own is None else jnp.where(me == s, own, ref[s])
        g = term.astype(F32) if g is None else g + term.astype(F32)
    return g


def _reduce_adamw(me, items, steps):
    n = len(items)

    def body(me_ref, *refs):
        ins, outs = refs[:5 * n], refs[5 * n:]
        for j in range(n):
            own_ref, slot_ref, w_ref, m_ref, v_ref = ins[5 * j:5 * j + 5]
            g_out, d_out, m_out, v_out = outs[4 * j:4 * j + 4]
            g = _slot_sum(slot_ref, own_ref[...], me_ref[0])
            g_out[...] = g
            d_out[...], m_out[...], v_out[...] = _adamw(w_ref[...], g, m_ref[...], v_ref[...])

    in_specs, out_specs, out_shape, args = [], [], [], []
    for partial, own_block, slots, w, m, v, lead in items:
        rows, cols = w.shape
        tr = rows // steps
        tile = pl.BlockSpec((tr, cols), lambda i, me_ref: (i, 0))
        if lead is None:
            in_specs += [
                pl.BlockSpec((tr, cols), lambda i, me_ref, own_block=own_block, tr=tr: own_block(i, me_ref[0], tr)),
                pl.BlockSpec((N_DEV, tr, cols), lambda i, me_ref: (0, i, 0))]
        else:
            in_specs += [
                pl.BlockSpec((None, tr, cols),
                             lambda i, me_ref, own_block=own_block, tr=tr, lead=lead: (lead, *own_block(i, me_ref[0], tr))),
                pl.BlockSpec((N_DEV, None, tr, cols), lambda i, me_ref, lead=lead: (0, lead, i, 0))]
        in_specs += [tile, tile, tile]
        out_specs += [tile] * 4
        out_shape += [jax.ShapeDtypeStruct((rows, cols), F32)] * 4
        args += [partial, slots, w, m, v]
    res = _pcall(
        body, name="reduce_adamw", out_shape=out_shape,
        grid_spec=pltpu.PrefetchScalarGridSpec(
            num_scalar_prefetch=1, grid=(steps,), in_specs=in_specs, out_specs=out_specs),
        compiler_params=_params(("arbitrary",)),
    )(me, *args)
    return [res[4 * j:4 * j + 4] for j in range(n)]


SM_GMIX, SM_GV, SM_GFF, SM_GFIN, SM_LOSS, SM_BGATE, SM_BS, SM_ROWS = 0, 1, 2, 3, 4, 5, 7, 8


def _pack_small(st_ffn, st_mix, st_in, dbg, dbs, conv_rows):
    def body(ffn_ref, mix_ref, in_ref, dbg_ref, dbs_ref, sm_ref, conv_ref):
        sm_ref[SM_GMIX:SM_GMIX + 1, :] = in_ref[0:1, :]
        sm_ref[SM_GV:SM_GV + 1, :] = mix_ref[ST_GV:ST_GV + 1, :]
        sm_ref[SM_GFF:SM_GFF + 1, :] = ffn_ref[ST_GFF:ST_GFF + 1, :]
        sm_ref[SM_GFIN:SM_GFIN + 1, :] = ffn_ref[ST_GFIN:ST_GFIN + 1, :]
        sm_ref[SM_LOSS:SM_LOSS + 1, :] = ffn_ref[ST_LOSS:ST_LOSS + 1, :]
        sm_ref[SM_BGATE:SM_BGATE + 1, :] = dbg_ref[0:1, :D]
        sm_ref[SM_BGATE + 1:SM_BGATE + 2, :] = dbg_ref[0:1, D:]
        for g in range(N_GROUPS):
            sm_ref[SM_BS:SM_BS + 1, g * SG:(g + 1) * SG] = jnp.sum(dbs_ref[g].T, axis=0, keepdims=True)
        conv_ref[...] = jnp.zeros_like(conv_ref)
        for p in range(N_DEV):
            conv_ref[p, 0:conv_rows, :] = mix_ref[ST_CONV:ST_CONV + conv_rows, p * LANE:(p + 1) * LANE]

    return _pcall(
        body, name="pack_small",
        out_shape=(jax.ShapeDtypeStruct((SM_ROWS, D), F32), jax.ShapeDtypeStruct((N_DEV, 8, LANE), F32)),
        in_specs=[_VMEM] * 5, out_specs=[_VMEM] * 2, compiler_params=_params(),
    )(st_ffn, st_mix, st_in, dbg, dbs)


def _small_update(sm_own, sm_slots, ws_own, ws_slots, conv_own, conv_slots, params, conv_rows):
    flat = [a for t in params for a in t]

    def body(smo_ref, sm_ref, wso_ref, ws_ref, convo_ref, conv_ref, *refs):
        ins, outs = refs[:len(flat)], refs[len(flat):]
        loss_ref, outs = outs[0], outs[1:]
        _, me = _position()
        sm = _slot_sum(sm_ref, smo_ref[...], me)
        loss_ref[...] = (0.5 / D) * jnp.sum(sm[SM_LOSS:SM_LOSS + 1, :], axis=1, keepdims=True)

        def update(n, g, at=lambda r: r):
            w_ref, m_ref, v_ref = [at(r) for r in ins[3 * n:3 * n + 3]]
            g_out, d_out, m_out, v_out = [at(r) for r in outs[4 * n:4 * n + 4]]
            g_out[...] = g
            d_out[...], m_out[...], v_out[...] = _adamw(w_ref[...], g, m_ref[...], v_ref[...])

        for n, row in enumerate((SM_GMIX, SM_GV, SM_GFF, SM_GFIN)):
            update(n, sm[row:row + 1, :])
        for half in range(2):
            update(4, sm[SM_BGATE + half:SM_BGATE + half + 1, :], lambda r, half=half: r.at[:, pl.ds(half * D, D)])
        for g in range(N_GROUPS):
            update(5, sm[SM_BS:SM_BS + 1, g * SG:(g + 1) * SG], lambda r, g=g: r.at[0, pl.ds(g, 1), :])
        update(6, _masked_ws(_slot_sum(ws_ref, wso_ref[...], me)), lambda r: r.at[0])
        d_conv = _slot_sum(conv_ref, convo_ref[me], me)
        for j in range(conv_rows):
            update(7, d_conv[j:j + 1, :], lambda r, j=j: r.at[:, pl.ds(j * LANE, LANE)])

    out_shape = [jax.ShapeDtypeStruct((1, 1), F32)]
    for w, _, _ in params:
        out_shape += [jax.ShapeDtypeStruct(w.shape, F32)] * 4
    return _pcall(
        body, name="small_update", out_shape=out_shape,
        in_specs=[_VMEM] * (6 + len(flat)), out_specs=[_VMEM] * len(out_shape), compiler_params=_params(),
    )(sm_own, sm_slots, ws_own, ws_slots, conv_own, conv_slots, *flat)


def kernel(x, norm_mix_g, w_in, b_gate, norm_v_g, w_s, b_s, conv_w, w_proj_a, w_proj_b, w_out, norm_ff_g, w_ff1, w_ff2, norm_final_g, loss_target, m_norm_mix_g, m_w_in, m_b_gate, m_norm_v_g, m_w_s, m_b_s, m_conv_w, m_w_proj_a, m_w_proj_b, m_w_out, m_norm_ff_g, m_w_ff1, m_w_ff2, m_norm_final_g, v_norm_mix_g, v_w_in, v_b_gate, v_norm_v_g, v_w_s, v_b_s, v_conv_w, v_w_proj_a, v_w_proj_b, v_w_out, v_norm_ff_g, v_w_ff1, v_w_ff2, v_norm_final_g):
    T = x.shape[1]
    tt = min(256, T)
    tk = min(4096, T)
    conv_rows = conv_w.shape[1]

    flat = lambda a: a.reshape(1, conv_rows * LANE)
    xs = x.reshape(T, D)
    tgt = loss_target.reshape(T, D)
    g_mix, g_v, g_ff, g_fin = norm_mix_g, norm_v_g, norm_ff_g, norm_final_g.reshape(1, D)
    ws = w_s[0]
    bs3 = b_s.reshape(N_GROUPS, SG, 1)

    c_in, c_ff1 = w_in.shape[2], w_ff1.shape[2]
    r_ff2, r_p = w_ff2.shape[1], w_proj_a.shape[1]
    lane_view = lambda width: (lambda ref, p: _lane_block(ref, p, width))
    row_view = lambda rows: (lambda ref, p: _row_block(ref, p, rows))
    slots = lambda shape, dtype: lax.empty((N_DEV,) + shape, dtype)

    placed = _place_weights(w_in[0], w_ff1[0], w_ff2[0], w_proj_a[0], w_proj_b[0], w_out[0], flat(conv_w))
    mixer_views = [row_view(r_p), row_view(r_p), row_view(r_p), lane_view(LANE)]
    ffn_views = [lane_view(c_ff1), row_view(r_ff2)]
    pairs = ((0, 1), (2, 4), (3, 5), (6, 7))
    in_group = lambda refs, ks: _gather_entries(refs[:1], [lane_view(c_in)], tuple(k for k in ks if k))

    def gather_groups(refs):
        return ([in_group(refs, ks) for ks in pairs]
                + [_gather_entries(refs[1:5], mixer_views), _gather_entries(refs[5:], ffn_views)])

    sems, placed, g_token = _start_copies("gather_start", placed, gather_groups)
    me = _position()[1].reshape(1)
    W_in = placed[0]
    tp = min(2048, T)
    proj = lax.empty((T, IN_COLS), BF16)
    for n, ks in enumerate(pairs):
        W_in, = _wait_copies(f"gather_wait_in_{n}", [W_in], sems[2 * n], sems[2 * n + 1],
                             lambda refs, ks=ks: in_group(refs, ks), proj if n else g_token)
        if n == 0:
            proj, h = _in_proj_blocks(f"in_proj_{n}", ks, me, xs, W_in, proj, c_in, tp, g_mix)
        else:
            proj = _in_proj_blocks(f"in_proj_{n}", ks, me, h, W_in, proj, c_in, tp)
    n = len(pairs)
    PA, PB, WO, conv = _wait_copies(
        "gather_wait_mixer", placed[1:5], sems[2 * n], sems[2 * n + 1],
        lambda refs: _gather_entries(refs, mixer_views), proj)
    acm, pa, pb, x1 = _mixer_fwd(xs, proj, b_gate, g_v, ws, bs3, conv, PA, PB, WO, min(512, T))
    W1, W2 = _wait_copies(
        "gather_wait_ffn", placed[5:], sems[2 * n + 2], sems[2 * n + 3],
        lambda refs: _gather_entries(refs, ffn_views), x1)
    hf, s, dpre, dx2, dx1, st_ffn = _ffn_fwd_bwd(x1, tgt, g_ff, g_fin, W1, W2, min(512, T))

    d_ff2 = _weight_grad("dw_ff2", s, dx2, D, tk)
    d_ff1 = _weight_grad("dw_ff1", hf, dpre, D, tk)
    ff_views = [lane_view(c_ff1), row_view(r_ff2)]
    ff_sems, ff_arrays, ff_token = _start_copies(
        "scatter_start_ffn", [d_ff1, d_ff2, slots((D, c_ff1), BF16), slots((r_ff2, D), BF16)],
        lambda refs: [_scatter_entries(ff_views)(refs)])

    dproj, dstk, st_mix, dbg, dws, dbs = _mixer_bwd(
        dx1, proj, pa, pb, b_gate, g_v, ws, bs3, conv, PA, PB, WO, tt, deps=(ff_token,))
    d_p3 = _weight_grad_stack("dw_proj", acm, dstk, tk)

    def proj_entries(refs):
        d3, dws_ref, land3, land_ws = refs
        entries = []
        for w in range(3):
            entries += _to_peers(lambda pid, me, w=w: d3.at[w, pl.ds(pid * r_p, r_p), :],
                                 lambda me, w=w: land3.at[me, w])
        return entries + _to_peers(lambda pid, me: dws_ref, lambda me: land_ws.at[me])

    p_sems, p_arrays, p_token = _start_copies(
        "scatter_start_proj", [d_p3, dws, slots((3, r_p, D), BF16), slots(dws.shape, F32)],
        lambda refs: [proj_entries(refs)])

    d_in = _weight_grad("dw_in", h, dproj, D, tk, deps=(p_token,))
    in_views = [lane_view(c_in)]
    in_sems, in_arrays, in_token = _start_copies(
        "scatter_start_in", [d_in, slots((D, c_in), BF16)], lambda refs: [_scatter_entries(in_views)(refs)])
    grad_x, st_in = _in_proj_bwd(dproj, xs, dx1, g_mix, W_in, min(512, T), deps=(in_token,))

    small, d_conv = _pack_small(st_ffn, st_mix, st_in, dbg, dbs, conv_rows)

    def small_entries(refs):
        sm, dc, land_sm, land_dc = refs
        return (_to_peers(lambda pid, me: sm, lambda me: land_sm.at[me])
                + _to_peers(lambda pid, me: dc.at[pid], lambda me: land_dc.at[me]))

    sm_sems, sm_arrays, sm_token = _start_copies(
        "small_start", [small, d_conv, slots(small.shape, F32), slots(d_conv.shape[1:], F32)],
        lambda refs: [small_entries(refs)])
    d_ff1, d_ff2, s_ff1, s_ff2 = _wait_copies(
        "scatter_wait_ffn", ff_arrays, ff_sems[0], ff_sems[1], _scatter_entries(ff_views), sm_token)
    d_p3, dws, s_p3, s_ws = _wait_copies(
        "scatter_wait_proj", p_arrays, p_sems[0], p_sems[1], proj_entries, sm_token)
    d_in, s_in = _wait_copies("scatter_wait_in", in_arrays, in_sems[0], in_sems[1], _scatter_entries(in_views), sm_token)

    own_cols = lambda i, me, tr: (i, me)
    own_rows = lambda rows: (lambda i, me, tr: (me * (rows // tr) + i, 0))
    names = ["w_in", "w_ff1", "w_ff2", "w_proj_a", "w_proj_b", "w_out"]
    quads = _reduce_adamw(me, [
        (d_in, own_cols, s_in, w_in[0], m_w_in[0], v_w_in[0], None),
        (d_ff1, own_cols, s_ff1, w_ff1[0], m_w_ff1[0], v_w_ff1[0], None),
        (d_ff2, own_rows(r_ff2), s_ff2, w_ff2[0], m_w_ff2[0], v_w_ff2[0], None),
        (d_p3, own_rows(r_p), s_p3, w_proj_a[0], m_w_proj_a[0], v_w_proj_a[0], 0),
        (d_p3, own_rows(r_p), s_p3, w_proj_b[0], m_w_proj_b[0], v_w_proj_b[0], 1),
        (d_p3, own_rows(r_p), s_p3, w_out[0], m_w_out[0], v_w_out[0], 2),
    ], 8)
    big = dict(zip(names, quads))

    one = lambda a: a.reshape(1, D)
    small_params = [
        (norm_mix_g, m_norm_mix_g, v_norm_mix_g),
        (norm_v_g, m_norm_v_g, v_norm_v_g),
        (norm_ff_g, m_norm_ff_g, v_norm_ff_g),
        (one(norm_final_g), one(m_norm_final_g), one(v_norm_final_g)),
        (b_gate, m_b_gate, v_b_gate),
        (b_s, m_b_s, v_b_s),
        (w_s, m_w_s, v_w_s),
        (flat(conv_w), flat(m_conv_w), flat(v_conv_w)),
    ]
    small, d_conv, r_small, r_conv = _wait_copies(
        "small_wait", sm_arrays, sm_sems[0], sm_sems[1], small_entries, quads[0][0])
    res = _small_update(small, r_small, dws, s_ws, d_conv, r_conv, small_params, conv_rows)
    loss = res[0].reshape(())
    names = ["norm_mix_g", "norm_v_g", "norm_ff_g", "norm_final_g", "b_gate", "b_s", "w_s", "conv_w"]
    out = {name: list(res[1 + 4 * n:5 + 4 * n]) for n, name in enumerate(names)}
    out["norm_final_g"] = [q.reshape(norm_final_g.shape) for q in out["norm_final_g"]]
    out["conv_w"] = [q.reshape(conv_w.shape) for q in out["conv_w"]]
    for name, quad in big.items():
        out[name] = [q[None] for q in quad]

    order = ["norm_mix_g", "w_in", "b_gate", "norm_v_g", "w_s", "b_s", "conv_w", "w_proj_a", "w_proj_b", "w_out",
             "norm_ff_g", "w_ff1", "w_ff2", "norm_final_g"]
    grads = [out[n][0] for n in order]
    deltas = [out[n][1] for n in order]
    new_m = [out[n][2] for n in order]
    new_v = [out[n][3] for n in order]
    return (loss, grad_x.reshape(x.shape), *grads, *deltas, *new_m, *new_v)
```

```python
import math

import jax
import jax.numpy as jnp
from jax import lax
from jax.experimental import pallas as pl
from jax.experimental.pallas import tpu as pltpu

F32 = jnp.float32
BF16 = jnp.bfloat16

N_DEV = 8
D = 1024
D_FF = 4096
IN_COLS = 7 * D
SG = 128
N_GROUPS = 8
CHUNK = 64
EPS = 1e-6
HALO = 16
LANE = 128
VMEM_LIMIT = 62 * 1024 * 1024

ADAM_LR = 0.001
ADAM_B1 = 0.9
ADAM_B2 = 0.999
ADAM_EPS = 1e-08
ADAM_WD = 0.01
ADAM_STEP = 10

SQRT_HALF = math.sqrt(0.5)
PDF_EXP2_SCALE = -0.5 * math.log2(math.e)
PDF_EXP2_SHIFT = math.log2(1.0 / math.sqrt(2.0 * math.pi))

_REL = [(dx, dy, dc) for dx in (0, 1) for dy in (0, 1) for dc in (0, 1)]

_VMEM = pl.BlockSpec(memory_space=pltpu.VMEM)
_ANY = pl.BlockSpec(memory_space=pl.ANY)


def _pcall(body, **kw):
    return pl.pallas_call(body, **kw)


def _params(sem=None):
    if sem is None:
        return pltpu.CompilerParams(vmem_limit_bytes=VMEM_LIMIT)
    return pltpu.CompilerParams(dimension_semantics=sem, vmem_limit_bytes=VMEM_LIMIT)


def _const_spec(shape):
    nd = len(shape)
    return pl.BlockSpec(shape, lambda *_: (0,) * nd, pipeline_mode=pl.Buffered(1))


def _after(body, n_in, deps):
    def wrapped(*refs):
        return body(*refs[:n_in], *refs[n_in + len(deps):])
    return wrapped


def _mm(a, b):
    return jnp.dot(a, b, preferred_element_type=F32)


def _nt(a, b):
    return lax.dot_general(a, b, (((1,), (1,)), ((), ())), preferred_element_type=F32)


def _tn(a, b):
    return lax.dot_general(a, b, (((0,), (0,)), ((), ())), preferred_element_type=F32)


def _rms(x):
    r = lax.rsqrt(jnp.mean(x * x, axis=-1, keepdims=True) + EPS)
    return x * r, r


def _rms_bwd(dyg, xh, r):
    return r * (dyg - xh * jnp.mean(dyg * xh, axis=-1, keepdims=True))


def _gelu(x):
    cdf = 0.5 * (1.0 + lax.erf(x * SQRT_HALF))
    return x * cdf, cdf


def _gelu_grad(x, cdf):
    return cdf + x * jnp.exp2(x * x * PDF_EXP2_SCALE + PDF_EXP2_SHIFT)


def _masked_ws(ws):
    i = lax.broadcasted_iota(jnp.int32, (SG, SG), 0)
    j = lax.broadcasted_iota(jnp.int32, (SG, SG), 1)
    keep = jnp.logical_or(j < CHUNK, i >= CHUNK)
    return jnp.where(keep[None], ws, jnp.zeros_like(ws))


def _shift_down(halo, q, k):
    ext = jnp.concatenate([halo, q], axis=0)
    return pltpu.roll(ext, k, 0)[halo.shape[0]:]


def _shift_up(q, nxt, k):
    ext = jnp.concatenate([q, nxt], axis=0)
    return pltpu.roll(ext, ext.shape[0] - k, 0)[:q.shape[0]]


def _col(k):
    return slice(k * D, (k + 1) * D)


def _position():
    x, y, c = lax.axis_index("x"), lax.axis_index("y"), lax.axis_index("c")
    return (x, y, c), 4 * x + 2 * y + c


def _lane_block(ref, p, width):
    return ref.at[:, pl.ds(pl.multiple_of(p * width, LANE), width)]


def _row_block(ref, p, rows):
    return ref.at[pl.ds(p * rows, rows), :]


_HBM = pl.BlockSpec(memory_space=pltpu.HBM)
_SEM = pl.BlockSpec(memory_space=pltpu.SEMAPHORE)
_EFFECT = pltpu.SideEffectType.DATAFLOW_SIDE_EFFECTING


REL_ORDER = (1, 2, 4, 3, 5, 6, 7)


def _to_peers(src_of, dst_of, order=REL_ORDER):
    return [(src_of, dst_of, k) for k in order]


def _remote_copies(entries, send_sems, recv_sems):
    (x, y, c), me = _position()
    copies = []
    for n, (src_of, dst_of, k) in enumerate(entries):
        dx, dy, dc = _REL[k]
        peer = (1 - x if dx else x, 1 - y if dy else y, 1 - c if dc else c)
        pid = 4 * peer[0] + 2 * peer[1] + peer[2]
        copies.append(pltpu.make_async_remote_copy(
            src_ref=src_of(pid, me), dst_ref=dst_of(me), send_sem=send_sems.at[n], recv_sem=recv_sems.at[n],
            device_id=peer, device_id_type=pl.DeviceIdType.MESH))
    return copies


def _start_copies(name, arrays, make_groups):
    n = len(arrays)
    sizes = [len(g) for g in make_groups([None] * n)]

    def body(*refs):
        sems, token = refs[n:n + 2 * len(sizes)], refs[-1]
        for g, entries in enumerate(make_groups(refs[:n])):
            for cp in _remote_copies(entries, sems[2 * g], sems[2 * g + 1]):
                cp.start()
        token[...] = jnp.zeros_like(token)

    out_shape = []
    for size in sizes:
        out_shape += [pltpu.SemaphoreType.DMA((size,))] * 2
    out_shape += [pltpu.HBM(a.shape, a.dtype) for a in arrays] + [jax.ShapeDtypeStruct((8, LANE), F32)]
    res = _pcall(
        body, name=name, out_shape=out_shape,
        in_specs=[_HBM] * n, out_specs=[_SEM] * (2 * len(sizes)) + [_HBM] * n + [_VMEM],
        input_output_aliases={i: 2 * len(sizes) + i for i in range(n)},
        compiler_params=pltpu.CompilerParams(has_side_effects=_EFFECT),
    )(*[pltpu.with_memory_space_constraint(a, pltpu.HBM) for a in arrays])
    return res[:2 * len(sizes)], res[2 * len(sizes):-1], res[-1]


def _wait_copies(name, arrays, send_sems, recv_sems, make_entries, after):
    n = len(arrays)

    def body(*refs):
        for cp in _remote_copies(make_entries(refs[:n]), refs[n], refs[n + 1]):
            cp.wait_send()
            cp.wait_recv()

    return _pcall(
        body, name=name, out_shape=[pltpu.HBM(a.shape, a.dtype) for a in arrays],
        in_specs=[_HBM] * n + [_SEM, _SEM, _ANY], out_specs=[_HBM] * n,
        input_output_aliases={i: i for i in range(n)},
        compiler_params=pltpu.CompilerParams(has_side_effects=_EFFECT),
    )(*arrays, send_sems, recv_sems, after)


def _place_weights(w_in, w_ff1, w_ff2, w_pa, w_pb, w_o, conv_flat):
    n_items = 7
    c_in, c_ff1 = w_in.shape[1], w_ff1.shape[1]
    r_ff2, r_p = w_ff2.shape[0], w_pa.shape[0]

    def body(win_ref, w1_ref, w2_ref, pa_ref, pb_ref, wo_ref, cw_ref,
             win_o, pa_o, pb_o, wo_o, cw_o, w1_o, w2_o,
             s_win, s_w1, s_w2, s_pa, s_pb, s_wo, s_cw, sems):
        _, me = _position()
        s_cw[...] = jnp.zeros_like(s_cw)
        for j in range(conv_flat.shape[1] // LANE):
            s_cw[j:j + 1, :] = cw_ref[:, j * LANE:(j + 1) * LANE]
        for src, stage in ((win_ref, s_win), (w1_ref, s_w1), (w2_ref, s_w2),
                           (pa_ref, s_pa), (pb_ref, s_pb), (wo_ref, s_wo)):
            stage[...] = src[...].astype(BF16)
        pairs = [
            (s_win, _lane_block(win_o, me, c_in)), (s_pa, _row_block(pa_o, me, r_p)),
            (s_pb, _row_block(pb_o, me, r_p)), (s_wo, _row_block(wo_o, me, r_p)),
            (s_cw, _lane_block(cw_o, me, LANE)),
            (s_w1, _lane_block(w1_o, me, c_ff1)), (s_w2, _row_block(w2_o, me, r_ff2)),
        ]
        copies = [pltpu.make_async_copy(s, d, sems.at[n]) for n, (s, d) in enumerate(pairs)]
        for cp in copies:
            cp.start()
        for cp in copies:
            cp.wait()

    out_shape = (
        jax.ShapeDtypeStruct((D, N_DEV * c_in), BF16),
        jax.ShapeDtypeStruct((N_DEV * r_p, D), BF16),
        jax.ShapeDtypeStruct((N_DEV * r_p, D), BF16),
        jax.ShapeDtypeStruct((N_DEV * r_p, D), BF16),
        jax.ShapeDtypeStruct((8, N_DEV * LANE), F32),
        jax.ShapeDtypeStruct((D, N_DEV * c_ff1), BF16),
        jax.ShapeDtypeStruct((N_DEV * r_ff2, D), BF16),
    )
    return _pcall(
        body, name="place_weights", out_shape=out_shape,
        in_specs=[_VMEM] * n_items, out_specs=[_ANY] * n_items,
        scratch_shapes=[pltpu.VMEM(w.shape, BF16) for w in (w_in, w_ff1, w_ff2, w_pa, w_pb, w_o)]
        + [pltpu.VMEM((8, LANE), F32), pltpu.SemaphoreType.DMA((n_items,))],
        compiler_params=_params(),
    )(w_in, w_ff1, w_ff2, w_pa, w_pb, w_o, conv_flat)


def _gather_entries(refs, views, order=REL_ORDER):
    entries = []
    for r, v in zip(refs, views):
        entries += _to_peers(lambda pid, me, r=r, v=v: v(r, me), lambda me, r=r, v=v: v(r, me), order)
    return entries


def _scatter_entries(views):
    def make(refs):
        srcs, lands = refs[:len(views)], refs[len(views):]
        entries = []
        for r, v, l in zip(srcs, views, lands):
            entries += _to_peers(lambda pid, me, r=r, v=v: v(r, pid), lambda me, l=l: l.at[me])
        return entries
    return make


def _in_proj_blocks(name, ks, me, act, w_in, proj, width, tt, g_mix=None):
    T = act.shape[0]
    nt = T // tt
    nb = len(ks)
    normed = g_mix is not None

    def body(me_ref, act_ref, *refs):
        if normed:
            g_ref, refs = refs[0], refs[1:]
        w_refs, refs = refs[:nb], refs[nb + 1:]
        if normed:
            proj_out, h_out, w_s, o_s, sems = refs
        else:
            proj_out, w_s, o_s, sems = refs
        i = pl.program_id(0)

        @pl.when(i == 0)
        def _():
            for part, w_ref in enumerate(w_refs):
                w_s[:, part * width:(part + 1) * width] = w_ref[...]

        def copies(slot, step):
            rows = pl.ds(step * tt, tt)
            return [pltpu.make_async_copy(
                o_s.at[slot, :, part * width:(part + 1) * width],
                proj_out.at[rows, pl.ds(pl.multiple_of((me_ref[0] ^ ks[part]) * width, LANE), width)],
                sems.at[slot, part]) for part in range(nb)]

        slot = i % 2

        @pl.when(i >= 2)
        def _():
            for cp in copies(slot, i - 2):
                cp.wait()

        if normed:
            xh, _ = _rms(act_ref[...])
            h = (xh * g_ref[...]).astype(BF16)
            h_out[...] = h
        else:
            h = act_ref[...]
        o_s[slot] = _mm(h, w_s[...]).astype(BF16)
        for cp in copies(slot, i):
            cp.start()

        @pl.when(i == nt - 1)
        def _():
            for cp in copies(slot, i) + (copies(1 - slot, i - 1) if nt > 1 else []):
                cp.wait()

    tile = pl.BlockSpec((tt, D), lambda i, b: (i, 0))
    w_spec = lambda part: pl.BlockSpec((D, width), lambda i, m: (0, m[0] ^ ks[part]), pipeline_mode=pl.Buffered(1))
    extra_in = [_const_spec(g_mix.shape)] if normed else []
    extra_args = [g_mix] if normed else []
    out_shape = [jax.ShapeDtypeStruct(proj.shape, proj.dtype)] + ([jax.ShapeDtypeStruct((T, D), BF16)] if normed else [])
    res = _pcall(
        body, name=name, out_shape=out_shape,
        grid_spec=pltpu.PrefetchScalarGridSpec(
            num_scalar_prefetch=1, grid=(nt,),
            in_specs=[tile] + extra_in + [w_spec(part) for part in range(nb)] + [_ANY],
            out_specs=[_ANY] + ([tile] if normed else []),
            scratch_shapes=[pltpu.VMEM((D, nb * width), BF16), pltpu.VMEM((2, tt, nb * width), BF16),
                            pltpu.SemaphoreType.DMA((2, nb))]),
        input_output_aliases={2 + nb + len(extra_in): 0},
        compiler_params=_params(("arbitrary",)),
    )(me, act, *extra_args, *([w_in] * nb), proj)
    return res if normed else res[0]


def _mixer_fwd(x, proj, b_gate, g_v, w_s, b_s3, conv, w_pa, w_pb, w_o, tt):
    T = x.shape[0]
    nt = T // tt
    nb = tt // SG

    def body(x_ref, proj_ref, bg_ref, gv_ref, ws_ref, bs_ref, cw_ref, pa_w, pb_w, wo_w,
             acm_ref, pa_ref, pb_ref, x1_ref, q_carry, mix_s):
        @pl.when(pl.program_id(0) == 0)
        def _():
            q_carry[...] = jnp.zeros_like(q_carry)

        def proj(k):
            return proj_ref[:, _col(k)].astype(F32)

        vg, _ = _gelu(proj(1))
        vh, _ = _rms(vg)
        vp = (vh * gv_ref[...]).astype(BF16)
        wm = _masked_ws(ws_ref[...]).astype(BF16)
        for n in range(nb):
            rows = slice(n * SG, (n + 1) * SG)
            for g in range(N_GROUPS):
                cols = slice(g * SG, (g + 1) * SG)
                mix_s[rows, cols] = _mm(wm[g], vp[rows, cols]) + bs_ref[g]
        ug, _ = _gelu(proj(0))
        a = (ug * mix_s[...]).astype(BF16)
        acm_ref[:, _col(0)] = a
        pa = _mm(a, pa_w[...])
        pa_ref[...] = pa.astype(BF16)
        m = jax.nn.sigmoid(proj(5) + bg_ref[:, :D]) * pa

        bgate = proj(2)
        q = proj(3) * proj(4)
        halo = q_carry[...]
        cv = cw_ref[0:1, :] * _shift_down(halo, q, 2) + cw_ref[1:2, :] * _shift_down(halo, q, 1) + cw_ref[2:3, :] * q
        q_carry[...] = q[tt - q_carry.shape[0]:, :]
        c = (bgate * cv).astype(BF16)
        acm_ref[:, _col(1)] = c
        pb = _mm(c, pb_w[...])
        pb_ref[...] = pb.astype(BF16)
        m = (m + jax.nn.sigmoid(proj(6) + bg_ref[:, D:]) * pb).astype(BF16)
        acm_ref[:, _col(2)] = m
        x1_ref[...] = x_ref[...] + _mm(m, wo_w[...])

    tile = lambda w: pl.BlockSpec((tt, w), lambda i: (i, 0))
    out_shape = ([jax.ShapeDtypeStruct((T, 3 * D), BF16)] + [jax.ShapeDtypeStruct((T, D), BF16)] * 2
                 + [jax.ShapeDtypeStruct((T, D), F32)])
    return _pcall(
        body, name="mixer_fwd", grid=(nt,), out_shape=out_shape,
        in_specs=[tile(D), tile(IN_COLS), _const_spec(b_gate.shape),
                  _const_spec(g_v.shape), _const_spec(w_s.shape), _const_spec(b_s3.shape), _const_spec(conv.shape),
                  _const_spec(w_pa.shape), _const_spec(w_pb.shape), _const_spec(w_o.shape)],
        out_specs=[tile(3 * D)] + [tile(D)] * 3,
        scratch_shapes=[pltpu.VMEM((8, D), F32), pltpu.VMEM((tt, D), F32)],
        compiler_params=_params(("arbitrary",)),
    )(x, proj, b_gate, g_v, w_s, b_s3, conv, w_pa, w_pb, w_o)


ST_GFIN, ST_GFF, ST_LOSS = 0, 1, 2


def _ffn_fwd_bwd(x1, tgt, g_ff, g_fin, w1, w2, tt):
    T = x1.shape[0]
    nt = T // tt
    nk = D_FF // D

    def body(x1_ref, tgt_ref, gff_ref, gfin_ref, w1_ref, w2_ref,
             hf_ref, s_ref, dpre_ref, dx2_ref, dx1_ref, st_ref, z_s):
        @pl.when(pl.program_id(0) == 0)
        def _():
            st_ref[...] = jnp.zeros_like(st_ref)

        x1 = x1_ref[...]
        xh1, r1 = _rms(x1)
        hf = (xh1 * gff_ref[...]).astype(BF16)
        hf_ref[...] = hf
        acc = jnp.zeros((tt, D), F32)
        for k in range(nk):
            z = jnp.maximum(_mm(hf, w1_ref[:, _col(k)]), 0.0)
            z_s[:, _col(k)] = z.astype(BF16)
            s = (z * z).astype(BF16)
            s_ref[:, _col(k)] = s
            acc = acc + _mm(s, w2_ref[_col(k), :])
        x2 = x1 + acc
        xh2, r2 = _rms(x2)
        diff = xh2 * gfin_ref[...] - tgt_ref[...]
        st_ref[ST_LOSS:ST_LOSS + 1, :] += jnp.sum(diff * diff, axis=0, keepdims=True)
        dy = diff * (1.0 / D)
        st_ref[ST_GFIN:ST_GFIN + 1, :] += jnp.sum(dy * xh2, axis=0, keepdims=True)
        dx2 = _rms_bwd(dy * gfin_ref[...], xh2, r2)
        dx2b = dx2.astype(BF16)
        dx2_ref[...] = dx2b
        dhf = jnp.zeros((tt, D), F32)
        for k in range(nk):
            dpre = (_nt(dx2b, w2_ref[_col(k), :]) * (2.0 * z_s[:, _col(k)].astype(F32))).astype(BF16)
            dpre_ref[:, _col(k)] = dpre
            dhf = dhf + _nt(dpre, w1_ref[:, _col(k)])
        st_ref[ST_GFF:ST_GFF + 1, :] += jnp.sum(dhf * xh1, axis=0, keepdims=True)
        dx1_ref[...] = dx2 + _rms_bwd(dhf * gff_ref[...], xh1, r1)

    tile = lambda w: pl.BlockSpec((tt, w), lambda i: (i, 0))
    out_shape = (jax.ShapeDtypeStruct((T, D), BF16), jax.ShapeDtypeStruct((T, D_FF), BF16),
                 jax.ShapeDtypeStruct((T, D_FF), BF16), jax.ShapeDtypeStruct((T, D), BF16),
                 jax.ShapeDtypeStruct((T, D), F32), jax.ShapeDtypeStruct((8, D), F32))
    return _pcall(
        body, name="ffn_fwd_bwd", grid=(nt,), out_shape=out_shape,
        in_specs=[tile(D), tile(D), _const_spec(g_ff.shape), _const_spec(g_fin.shape),
                  _const_spec(w1.shape), _const_spec(w2.shape)],
        out_specs=[tile(D), tile(D_FF), tile(D_FF), tile(D), tile(D), pl.BlockSpec((8, D), lambda i: (0, 0))],
        scratch_shapes=[pltpu.VMEM((tt, D_FF), BF16)],
        compiler_params=_params(("arbitrary",)),
    )(x1, tgt, g_ff, g_fin, w1, w2)


ST_GV, ST_CONV = 0, 1


def _mixer_bwd(dx1, proj, pa, pb, b_gate, g_v, w_s, b_s3, conv, w_pa, w_pb, w_o, tt, deps=()):
    T = dx1.shape[0]
    nt = T // tt
    nb = tt // SG
    hb = tt // HALO

    def body(dx1_ref, proj_ref, cgh_ref, xsh_ref, pa_ref, pb_ref,
             bg_ref, gv_ref, ws_ref, bs_ref, cw_ref, pa_w, pb_w, wo_w,
             dproj_ref, dstk_ref, st_ref, dbg_ref, dws_ref, dbs_ref, d_carry, mix_s, dvp_s):
        i = pl.program_id(0)

        @pl.when(i == 0)
        def _():
            st_ref[...] = jnp.zeros_like(st_ref)
            dbg_ref[...] = jnp.zeros_like(dbg_ref)
            dws_ref[...] = jnp.zeros_like(dws_ref)
            dbs_ref[...] = jnp.zeros_like(dbs_ref)
            d_carry[...] = jnp.zeros_like(d_carry)

        def pj(k):
            return proj_ref[:, _col(k)].astype(F32)

        def put(k, val):
            dproj_ref[:, _col(k)] = val.astype(BF16)

        dx1b = dx1_ref[...].astype(BF16)
        dstk_ref[:, _col(2)] = dx1b
        dm = _nt(dx1b, wo_w[...])
        s_a = jax.nn.sigmoid(pj(5) + bg_ref[:, :D])
        s_b = jax.nn.sigmoid(pj(6) + bg_ref[:, D:])
        dpa = dm * s_a
        dpb = dm * s_b
        dpa_b = dpa.astype(BF16)
        dpb_b = dpb.astype(BF16)
        dstk_ref[:, _col(0)] = dpa_b
        dstk_ref[:, _col(1)] = dpb_b
        dga = dpa * pa_ref[...].astype(F32) * (1.0 - s_a)
        dgb = dpb * pb_ref[...].astype(F32) * (1.0 - s_b)
        dbg_ref[0:1, :D] += jnp.sum(dga, axis=0, keepdims=True)
        dbg_ref[0:1, D:] += jnp.sum(dgb, axis=0, keepdims=True)
        put(5, dga)
        put(6, dgb)
        da = _nt(dpa_b, pa_w[...])
        dc = _nt(dpb_b, pb_w[...])

        v = pj(1)
        vg, v_cdf = _gelu(v)
        vh, rv = _rms(vg)
        vp = (vh * gv_ref[...]).astype(BF16)
        wm = _masked_ws(ws_ref[...]).astype(BF16)
        for n in range(nb):
            rows = slice(n * SG, (n + 1) * SG)
            for g in range(N_GROUPS):
                cols = slice(g * SG, (g + 1) * SG)
                mix_s[rows, cols] = _mm(wm[g], vp[rows, cols]) + bs_ref[g]
        u = pj(0)
        ug, u_cdf = _gelu(u)
        put(0, da * mix_s[...] * _gelu_grad(u, u_cdf))
        dmix = da * ug
        dmix_b = dmix.astype(BF16)
        for n in range(nb):
            rows = slice(n * SG, (n + 1) * SG)
            for g in range(N_GROUPS):
                cols = slice(g * SG, (g + 1) * SG)
                blk = dmix_b[rows, cols]
                dws_ref[g] += _nt(blk, vp[rows, cols])
                dbs_ref[g] += dmix[rows, cols]
                dvp_s[rows, cols] = _tn(wm[g], blk)
        dvp = dvp_s[...]
        st_ref[ST_GV:ST_GV + 1, :] += jnp.sum(dvp * vh, axis=0, keepdims=True)
        put(1, _rms_bwd(dvp * gv_ref[...], vh, rv) * _gelu_grad(v, v_cdf))

        bgate, cg, xs = pj(2), pj(3), pj(4)
        q = cg * xs
        has_prev = (i < nt - 1).astype(F32)
        halo = cgh_ref[...].astype(F32) * xsh_ref[...].astype(F32) * has_prev
        q2 = _shift_down(halo, q, 2)
        q1 = _shift_down(halo, q, 1)
        w0, w1, w2 = cw_ref[0:1, :], cw_ref[1:2, :], cw_ref[2:3, :]
        put(2, dc * (w0 * q2 + w1 * q1 + w2 * q))
        dcv = dc * bgate
        st_ref[ST_CONV:ST_CONV + 1, :] += jnp.sum(dcv * q2, axis=0, keepdims=True)
        st_ref[ST_CONV + 1:ST_CONV + 2, :] += jnp.sum(dcv * q1, axis=0, keepdims=True)
        st_ref[ST_CONV + 2:ST_CONV + 3, :] += jnp.sum(dcv * q, axis=0, keepdims=True)
        nxt = d_carry[...]
        dq = w2 * dcv + w1 * _shift_up(dcv, nxt, 1) + w0 * _shift_up(dcv, nxt, 2)
        d_carry[...] = dcv[:d_carry.shape[0], :]
        put(3, dq * xs)
        put(4, dq * cg)

    rev = lambda i: nt - 1 - i
    tile = lambda w: pl.BlockSpec((tt, w), lambda i: (rev(i), 0))
    halo_spec = lambda k: pl.BlockSpec((HALO, D), lambda i: (jnp.maximum(rev(i) * hb - 1, 0), k))
    res = lambda shape: pl.BlockSpec(shape, lambda i: (0,) * len(shape))
    out_shape = (jax.ShapeDtypeStruct((T, IN_COLS), BF16), jax.ShapeDtypeStruct((T, 3 * D), BF16),
                 jax.ShapeDtypeStruct((8, D), F32), jax.ShapeDtypeStruct((8, 2 * D), F32),
                 jax.ShapeDtypeStruct((N_GROUPS, SG, SG), F32), jax.ShapeDtypeStruct((N_GROUPS, SG, SG), F32))
    return _pcall(
        _after(body, 14, deps), name="mixer_bwd", grid=(nt,), out_shape=out_shape,
        in_specs=[tile(D), tile(IN_COLS), halo_spec(3), halo_spec(4), tile(D), tile(D),
                  _const_spec(b_gate.shape), _const_spec(g_v.shape), _const_spec(w_s.shape),
                  _const_spec(b_s3.shape), _const_spec(conv.shape),
                  _const_spec(w_pa.shape), _const_spec(w_pb.shape), _const_spec(w_o.shape)] + [_ANY] * len(deps),
        out_specs=[tile(IN_COLS), tile(3 * D), res((8, D)), res((8, 2 * D)),
                   res((N_GROUPS, SG, SG)), res((N_GROUPS, SG, SG))],
        scratch_shapes=[pltpu.VMEM((8, D), F32), pltpu.VMEM((tt, D), F32), pltpu.VMEM((tt, D), F32)],
        compiler_params=_params(("arbitrary",)),
    )(dx1, proj, proj, proj, pa, pb, b_gate, g_v, w_s, b_s3, conv, w_pa, w_pb, w_o, *deps)


def _in_proj_bwd(dproj, x, dx1, g_mix, w_in, tt, deps=()):
    T = x.shape[0]

    def body(dproj_ref, x_ref, dx1_ref, gmix_ref, win_ref, gx_ref, st_ref):
        @pl.when(pl.program_id(0) == 0)
        def _():
            st_ref[...] = jnp.zeros_like(st_ref)

        dh = _nt(dproj_ref[...], win_ref[...])
        xh, r = _rms(x_ref[...])
        st_ref[0:1, :] += jnp.sum(dh * xh, axis=0, keepdims=True)
        gx_ref[...] = dx1_ref[...] + _rms_bwd(dh * gmix_ref[...], xh, r)

    tile = lambda w: pl.BlockSpec((tt, w), lambda i: (i, 0))
    return _pcall(
        _after(body, 5, deps), name="in_proj_bwd", grid=(T // tt,),
        out_shape=(jax.ShapeDtypeStruct((T, D), F32), jax.ShapeDtypeStruct((8, D), F32)),
        in_specs=[tile(IN_COLS), tile(D), tile(D), _const_spec(g_mix.shape), _const_spec(w_in.shape)]
        + [_ANY] * len(deps),
        out_specs=[tile(D), pl.BlockSpec((8, D), lambda i: (0, 0))],
        compiler_params=_params(("arbitrary",)),
    )(dproj, x, dx1, g_mix, w_in, *deps)


def _weight_grad(name, act, dout, bc, tk, deps=()):
    T, n_in = act.shape
    n_out = dout.shape[1]
    nk = T // tk
    bi = min(n_in, D)

    def body(a_ref, d_ref, o_ref, acc):
        k = pl.program_id(2)

        @pl.when(k == 0)
        def _():
            acc[...] = jnp.zeros_like(acc)

        acc[...] += _tn(a_ref[...], d_ref[...])

        @pl.when(k == nk - 1)
        def _():
            o_ref[...] = acc[...].astype(o_ref.dtype)

    return _pcall(
        _after(body, 2, deps), name=name, grid=(n_in // bi, n_out // bc, nk),
        out_shape=jax.ShapeDtypeStruct((n_in, n_out), BF16),
        in_specs=[pl.BlockSpec((tk, bi), lambda i, j, k: (k, i)), pl.BlockSpec((tk, bc), lambda i, j, k: (k, j))]
        + [_ANY] * len(deps),
        out_specs=pl.BlockSpec((bi, bc), lambda i, j, k: (i, j)),
        scratch_shapes=[pltpu.VMEM((bi, bc), F32)],
        compiler_params=_params(("arbitrary", "arbitrary", "arbitrary")),
    )(act, dout, *deps)


def _weight_grad_stack(name, acts, douts, tk):
    T, n = acts.shape[0], acts.shape[1] // D
    nk = T // tk

    def body(a_ref, d_ref, o_ref, acc):
        k = pl.program_id(1)

        @pl.when(k == 0)
        def _():
            acc[...] = jnp.zeros_like(acc)

        acc[...] += _tn(a_ref[...], d_ref[...])

        @pl.when(k == nk - 1)
        def _():
            o_ref[...] = acc[...].astype(o_ref.dtype)

    tile = pl.BlockSpec((tk, D), lambda w, k: (k, w))
    return _pcall(
        body, name=name, grid=(n, nk), out_shape=jax.ShapeDtypeStruct((n, D, D), BF16),
        in_specs=[tile, tile], out_specs=pl.BlockSpec((None, D, D), lambda w, k: (w, 0, 0)),
        scratch_shapes=[pltpu.VMEM((D, D), F32)],
        compiler_params=_params(("arbitrary", "arbitrary")),
    )(acts, douts)


def _adamw(w, g, m, v):
    m = ADAM_B1 * m + (1.0 - ADAM_B1) * g
    v = ADAM_B2 * v + (1.0 - ADAM_B2) * (g * g)
    m_hat = m / (1.0 - ADAM_B1 ** ADAM_STEP)
    v_hat = v / (1.0 - ADAM_B2 ** ADAM_STEP)
    delta = -ADAM_LR * (m_hat / (jnp.sqrt(v_hat) + ADAM_EPS) + ADAM_WD * w)
    return delta, m, v


def _slot_sum(ref, own=None, me=None):
    g = None
    for s in range(N_DEV):
        term = ref[s] if own is None else jnp.where(me == s, own, ref[s])
        g = term.astype(F32) if g is None else g + term.astype(F32)
    return g


def _reduce_adamw(me, items, steps):
    n = len(items)

    def body(me_ref, *refs):
        ins, outs = refs[:5 * n], refs[5 * n:]
        for j in range(n):
            own_ref, slot_ref, w_ref, m_ref, v_ref = ins[5 * j:5 * j + 5]
            g_out, d_out, m_out, v_out = outs[4 * j:4 * j + 4]
            g = _slot_sum(slot_ref, own_ref[...], me_ref[0])
            g_out[...] = g
            d_out[...], m_out[...], v_out[...] = _adamw(w_ref[...], g, m_ref[...], v_ref[...])

    in_specs, out_specs, out_shape, args = [], [], [], []
    for partial, own_block, slots, w, m, v, lead in items:
        rows, cols = w.shape
        tr = rows // steps
        tile = pl.BlockSpec((tr, cols), lambda i, me_ref: (i, 0))
        if lead is None:
            in_specs += [
                pl.BlockSpec((tr, cols), lambda i, me_ref, own_block=own_block, tr=tr: own_block(i, me_ref[0], tr)),
                pl.BlockSpec((N_DEV, tr, cols), lambda i, me_ref: (0, i, 0))]
        else:
            in_specs += [
                pl.BlockSpec((None, tr, cols),
                             lambda i, me_ref, own_block=own_block, tr=tr, lead=lead: (lead, *own_block(i, me_ref[0], tr))),
                pl.BlockSpec((N_DEV, None, tr, cols), lambda i, me_ref, lead=lead: (0, lead, i, 0))]
        in_specs += [tile, tile, tile]
        out_specs += [tile] * 4
        out_shape += [jax.ShapeDtypeStruct((rows, cols), F32)] * 4
        args += [partial, slots, w, m, v]
    res = _pcall(
        body, name="reduce_adamw", out_shape=out_shape,
        grid_spec=pltpu.PrefetchScalarGridSpec(
            num_scalar_prefetch=1, grid=(steps,), in_specs=in_specs, out_specs=out_specs),
        compiler_params=_params(("arbitrary",)),
    )(me, *args)
    return [res[4 * j:4 * j + 4] for j in range(n)]


SM_GMIX, SM_GV, SM_GFF, SM_GFIN, SM_LOSS, SM_BGATE, SM_BS, SM_ROWS = 0, 1, 2, 3, 4, 5, 7, 8


def _pack_small(st_ffn, st_mix, st_in, dbg, dbs, conv_rows):
    def body(ffn_ref, mix_ref, in_ref, dbg_ref, dbs_ref, sm_ref, conv_ref):
        sm_ref[SM_GMIX:SM_GMIX + 1, :] = in_ref[0:1, :]
        sm_ref[SM_GV:SM_GV + 1, :] = mix_ref[ST_GV:ST_GV + 1, :]
        sm_ref[SM_GFF:SM_GFF + 1, :] = ffn_ref[ST_GFF:ST_GFF + 1, :]
        sm_ref[SM_GFIN:SM_GFIN + 1, :] = ffn_ref[ST_GFIN:ST_GFIN + 1, :]
        sm_ref[SM_LOSS:SM_LOSS + 1, :] = ffn_ref[ST_LOSS:ST_LOSS + 1, :]
        sm_ref[SM_BGATE:SM_BGATE + 1, :] = dbg_ref[0:1, :D]
        sm_ref[SM_BGATE + 1:SM_BGATE + 2, :] = dbg_ref[0:1, D:]
        for g in range(N_GROUPS):
            sm_ref[SM_BS:SM_BS + 1, g * SG:(g + 1) * SG] = jnp.sum(dbs_ref[g].T, axis=0, keepdims=True)
        conv_ref[...] = jnp.zeros_like(conv_ref)
        for p in range(N_DEV):
            conv_ref[p, 0:conv_rows, :] = mix_ref[ST_CONV:ST_CONV + conv_rows, p * LANE:(p + 1) * LANE]

    return _pcall(
        body, name="pack_small",
        out_shape=(jax.ShapeDtypeStruct((SM_ROWS, D), F32), jax.ShapeDtypeStruct((N_DEV, 8, LANE), F32)),
        in_specs=[_VMEM] * 5, out_specs=[_VMEM] * 2, compiler_params=_params(),
    )(st_ffn, st_mix, st_in, dbg, dbs)


def _small_update(sm_own, sm_slots, ws_own, ws_slots, conv_own, conv_slots, params, conv_rows):
    flat = [a for t in params for a in t]

    def body(smo_ref, sm_ref, wso_ref, ws_ref, convo_ref, conv_ref, *refs):
        ins, outs = refs[:len(flat)], refs[len(flat):]
        loss_ref, outs = outs[0], outs[1:]
        _, me = _position()
        sm = _slot_sum(sm_ref, smo_ref[...], me)
        loss_ref[...] = (0.5 / D) * jnp.sum(sm[SM_LOSS:SM_LOSS + 1, :], axis=1, keepdims=True)

        def update(n, g, at=lambda r: r):
            w_ref, m_ref, v_ref = [at(r) for r in ins[3 * n:3 * n + 3]]
            g_out, d_out, m_out, v_out = [at(r) for r in outs[4 * n:4 * n + 4]]
            g_out[...] = g
            d_out[...], m_out[...], v_out[...] = _adamw(w_ref[...], g, m_ref[...], v_ref[...])

        for n, row in enumerate((SM_GMIX, SM_GV, SM_GFF, SM_GFIN)):
            update(n, sm[row:row + 1, :])
        for half in range(2):
            update(4, sm[SM_BGATE + half:SM_BGATE + half + 1, :], lambda r, half=half: r.at[:, pl.ds(half * D, D)])
        for g in range(N_GROUPS):
            update(5, sm[SM_BS:SM_BS + 1, g * SG:(g + 1) * SG], lambda r, g=g: r.at[0, pl.ds(g, 1), :])
        update(6, _masked_ws(_slot_sum(ws_ref, wso_ref[...], me)), lambda r: r.at[0])
        d_conv = _slot_sum(conv_ref, convo_ref[me], me)
        for j in range(conv_rows):
            update(7, d_conv[j:j + 1, :], lambda r, j=j: r.at[:, pl.ds(j * LANE, LANE)])

    out_shape = [jax.ShapeDtypeStruct((1, 1), F32)]
    for w, _, _ in params:
        out_shape += [jax.ShapeDtypeStruct(w.shape, F32)] * 4
    return _pcall(
        body, name="small_update", out_shape=out_shape,
        in_specs=[_VMEM] * (6 + len(flat)), out_specs=[_VMEM] * len(out_shape), compiler_params=_params(),
    )(sm_own, sm_slots, ws_own, ws_slots, conv_own, conv_slots, *flat)


def kernel(x, norm_mix_g, w_in, b_gate, norm_v_g, w_s, b_s, conv_w, w_proj_a, w_proj_b, w_out, norm_ff_g, w_ff1, w_ff2, norm_final_g, loss_target, m_norm_mix_g, m_w_in, m_b_gate, m_norm_v_g, m_w_s, m_b_s, m_conv_w, m_w_proj_a, m_w_proj_b, m_w_out, m_norm_ff_g, m_w_ff1, m_w_ff2, m_norm_final_g, v_norm_mix_g, v_w_in, v_b_gate, v_norm_v_g, v_w_s, v_b_s, v_conv_w, v_w_proj_a, v_w_proj_b, v_w_out, v_norm_ff_g, v_w_ff1, v_w_ff2, v_norm_final_g):
    T = x.shape[1]
    tt = min(256, T)
    tk = min(4096, T)
    conv_rows = conv_w.shape[1]

    flat = lambda a: a.reshape(1, conv_rows * LANE)
    xs = x.reshape(T, D)
    tgt = loss_target.reshape(T, D)
    g_mix, g_v, g_ff, g_fin = norm_mix_g, norm_v_g, norm_ff_g, norm_final_g.reshape(1, D)
    ws = w_s[0]
    bs3 = b_s.reshape(N_GROUPS, SG, 1)

    c_in, c_ff1 = w_in.shape[2], w_ff1.shape[2]
    r_ff2, r_p = w_ff2.shape[1], w_proj_a.shape[1]
    lane_view = lambda width: (lambda ref, p: _lane_block(ref, p, width))
    row_view = lambda rows: (lambda ref, p: _row_block(ref, p, rows))
    slots = lambda shape, dtype: lax.empty((N_DEV,) + shape, dtype)

    placed = _place_weights(w_in[0], w_ff1[0], w_ff2[0], w_proj_a[0], w_proj_b[0], w_out[0], flat(conv_w))
    mixer_views = [row_view(r_p), row_view(r_p), row_view(r_p), lane_view(LANE)]
    ffn_views = [lane_view(c_ff1), row_view(r_ff2)]
    pairs = ((0, 1), (2, 4), (3, 5), (6, 7))
    in_group = lambda refs, ks: _gather_entries(refs[:1], [lane_view(c_in)], tuple(k for k in ks if k))

    def gather_groups(refs):
        return ([in_group(refs, ks) for ks in pairs]
                + [_gather_entries(refs[1:5], mixer_views), _gather_entries(refs[5:], ffn_views)])

    sems, placed, g_token = _start_copies("gather_start", placed, gather_groups)
    me = _position()[1].reshape(1)
    W_in = placed[0]
    tp = min(2048, T)
    proj = lax.empty((T, IN_COLS), BF16)
    for n, ks in enumerate(pairs):
        W_in, = _wait_copies(f"gather_wait_in_{n}", [W_in], sems[2 * n], sems[2 * n + 1],
                             lambda refs, ks=ks: in_group(refs, ks), proj if n else g_token)
        if n == 0:
            proj, h = _in_proj_blocks(f"in_proj_{n}", ks, me, xs, W_in, proj, c_in, tp, g_mix)
        else:
            proj = _in_proj_blocks(f"in_proj_{n}", ks, me, h, W_in, proj, c_in, tp)
    n = len(pairs)
    PA, PB, WO, conv = _wait_copies(
        "gather_wait_mixer", placed[1:5], sems[2 * n], sems[2 * n + 1],
        lambda refs: _gather_entries(refs, mixer_views), proj)
    acm, pa, pb, x1 = _mixer_fwd(xs, proj, b_gate, g_v, ws, bs3, conv, PA, PB, WO, min(512, T))
    W1, W2 = _wait_copies(
        "gather_wait_ffn", placed[5:], sems[2 * n + 2], sems[2 * n + 3],
        lambda refs: _gather_entries(refs, ffn_views), x1)
    hf, s, dpre, dx2, dx1, st_ffn = _ffn_fwd_bwd(x1, tgt, g_ff, g_fin, W1, W2, min(512, T))

    d_ff2 = _weight_grad("dw_ff2", s, dx2, D, tk)
    d_ff1 = _weight_grad("dw_ff1", hf, dpre, D, tk)
    ff_views = [lane_view(c_ff1), row_view(r_ff2)]

    dproj, dstk, st_mix, dbg, dws, dbs = _mixer_bwd(
        dx1, proj, pa, pb, b_gate, g_v, ws, bs3, conv, PA, PB, WO, tt)
    d_p3 = _weight_grad_stack("dw_proj", acm, dstk, tk)

    def proj_entries(refs):
        d3, dws_ref, land3, land_ws = refs
        entries = []
        for w in range(3):
            entries += _to_peers(lambda pid, me, w=w: d3.at[w, pl.ds(pid * r_p, r_p), :],
                                 lambda me, w=w: land3.at[me, w])
        return entries + _to_peers(lambda pid, me: dws_ref, lambda me: land_ws.at[me])

    pf_sems, pf_arrays, p_token = _start_copies(
        "scatter_start_proj_ffn",
        [d_p3, dws, slots((3, r_p, D), BF16), slots(dws.shape, F32),
         d_ff1, d_ff2, slots((D, c_ff1), BF16), slots((r_ff2, D), BF16)],
        lambda refs: [proj_entries(refs[:4]), _scatter_entries(ff_views)(refs[4:])])
    p_sems, p_arrays, ff_sems, ff_arrays = pf_sems[:2], pf_arrays[:4], pf_sems[2:], pf_arrays[4:]

    d_in = _weight_grad("dw_in", h, dproj, D, tk, deps=(p_token,))
    in_views = [lane_view(c_in)]
    in_sems, in_arrays, in_token = _start_copies(
        "scatter_start_in", [d_in, slots((D, c_in), BF16)], lambda refs: [_scatter_entries(in_views)(refs)])
    grad_x, st_in = _in_proj_bwd(dproj, xs, dx1, g_mix, W_in, min(512, T), deps=(in_token,))

    small, d_conv = _pack_small(st_ffn, st_mix, st_in, dbg, dbs, conv_rows)

    def small_entries(refs):
        sm, dc, land_sm, land_dc = refs
        return (_to_peers(lambda pid, me: sm, lambda me: land_sm.at[me])
                + _to_peers(lambda pid, me: dc.at[pid], lambda me: land_dc.at[me]))

    sm_sems, sm_arrays, sm_token = _start_copies(
        "small_start", [small, d_conv, slots(small.shape, F32), slots(d_conv.shape[1:], F32)],
        lambda refs: [small_entries(refs)])
    d_ff1, d_ff2, s_ff1, s_ff2 = _wait_copies(
        "scatter_wait_ffn", ff_arrays, ff_sems[0], ff_sems[1], _scatter_entries(ff_views), sm_token)
    d_p3, dws, s_p3, s_ws = _wait_copies(
        "scatter_wait_proj", p_arrays, p_sems[0], p_sems[1], proj_entries, sm_token)
    d_in, s_in = _wait_copies("scatter_wait_in", in_arrays, in_sems[0], in_sems[1], _scatter_entries(in_views), sm_token)

    own_cols = lambda i, me, tr: (i, me)
    own_rows = lambda rows: (lambda i, me, tr: (me * (rows // tr) + i, 0))
    names = ["w_in", "w_ff1", "w_ff2", "w_proj_a", "w_proj_b", "w_out"]
    quads = _reduce_adamw(me, [
        (d_in, own_cols, s_in, w_in[0], m_w_in[0], v_w_in[0], None),
        (d_ff1, own_cols, s_ff1, w_ff1[0], m_w_ff1[0], v_w_ff1[0], None),
        (d_ff2, own_rows(r_ff2), s_ff2, w_ff2[0], m_w_ff2[0], v_w_ff2[0], None),
        (d_p3, own_rows(r_p), s_p3, w_proj_a[0], m_w_proj_a[0], v_w_proj_a[0], 0),
        (d_p3, own_rows(r_p), s_p3, w_proj_b[0], m_w_proj_b[0], v_w_proj_b[0], 1),
        (d_p3, own_rows(r_p), s_p3, w_out[0], m_w_out[0], v_w_out[0], 2),
    ], 8)
    big = dict(zip(names, quads))

    one = lambda a: a.reshape(1, D)
    small_params = [
        (norm_mix_g, m_norm_mix_g, v_norm_mix_g),
        (norm_v_g, m_norm_v_g, v_norm_v_g),
        (norm_ff_g, m_norm_ff_g, v_norm_ff_g),
        (one(norm_final_g), one(m_norm_final_g), one(v_norm_final_g)),
        (b_gate, m_b_gate, v_b_gate),
        (b_s, m_b_s, v_b_s),
        (w_s, m_w_s, v_w_s),
        (flat(conv_w), flat(m_conv_w), flat(v_conv_w)),
    ]
    small, d_conv, r_small, r_conv = _wait_copies(
        "small_wait", sm_arrays, sm_sems[0], sm_sems[1], small_entries, quads[0][0])
    res = _small_update(small, r_small, dws, s_ws, d_conv, r_conv, small_params, conv_rows)
    loss = res[0].reshape(())
    names = ["norm_mix_g", "norm_v_g", "norm_ff_g", "norm_final_g", "b_gate", "b_s", "w_s", "conv_w"]
    out = {name: list(res[1 + 4 * n:5 + 4 * n]) for n, name in enumerate(names)}
    out["norm_final_g"] = [q.reshape(norm_final_g.shape) for q in out["norm_final_g"]]
    out["conv_w"] = [q.reshape(conv_w.shape) for q in out["conv_w"]]
    for name, quad in big.items():
        out[name] = [q[None] for q in quad]

    order = ["norm_mix_g", "w_in", "b_gate", "norm_v_g", "w_s", "b_s", "conv_w", "w_proj_a", "w_proj_b", "w_out",
             "norm_ff_g", "w_ff1", "w_ff2", "norm_final_g"]
    grads = [out[n][0] for n in order]
    deltas = [out[n][1] for n in order]
    new_m = [out[n][2] for n in order]
    new_v = [out[n][3] for n in order]
    return (loss, grad_x.reshape(x.shape), *grads, *deltas, *new_m, *new_v)
```

```python
import math

import jax
import jax.numpy as jnp
from jax import lax
from jax.experimental import pallas as pl
from jax.experimental.pallas import tpu as pltpu

F32 = jnp.float32
BF16 = jnp.bfloat16

N_DEV = 8
D = 1024
D_FF = 4096
IN_COLS = 7 * D
SG = 128
N_GROUPS = 8
CHUNK = 64
EPS = 1e-6
HALO = 16
LANE = 128
VMEM_LIMIT = 62 * 1024 * 1024

ADAM_LR = 0.001
ADAM_B1 = 0.9
ADAM_B2 = 0.999
ADAM_EPS = 1e-08
ADAM_WD = 0.01
ADAM_STEP = 10

SQRT_HALF = math.sqrt(0.5)
PDF_EXP2_SCALE = -0.5 * math.log2(math.e)
PDF_EXP2_SHIFT = math.log2(1.0 / math.sqrt(2.0 * math.pi))

_REL = [(dx, dy, dc) for dx in (0, 1) for dy in (0, 1) for dc in (0, 1)]

_VMEM = pl.BlockSpec(memory_space=pltpu.VMEM)
_ANY = pl.BlockSpec(memory_space=pl.ANY)


def _pcall(body, **kw):
    return pl.pallas_call(body, **kw)


def _params(sem=None):
    if sem is None:
        return pltpu.CompilerParams(vmem_limit_bytes=VMEM_LIMIT)
    return pltpu.CompilerParams(dimension_semantics=sem, vmem_limit_bytes=VMEM_LIMIT)


def _const_spec(shape):
    nd = len(shape)
    return pl.BlockSpec(shape, lambda *_: (0,) * nd, pipeline_mode=pl.Buffered(1))


def _after(body, n_in, deps):
    def wrapped(*refs):
        return body(*refs[:n_in], *refs[n_in + len(deps):])
    return wrapped


def _mm(a, b):
    return jnp.dot(a, b, preferred_element_type=F32)


def _nt(a, b):
    return lax.dot_general(a, b, (((1,), (1,)), ((), ())), preferred_element_type=F32)


def _tn(a, b):
    return lax.dot_general(a, b, (((0,), (0,)), ((), ())), preferred_element_type=F32)


def _rms(x):
    r = lax.rsqrt(jnp.mean(x * x, axis=-1, keepdims=True) + EPS)
    return x * r, r


def _rms_bwd(dyg, xh, r):
    return r * (dyg - xh * jnp.mean(dyg * xh, axis=-1, keepdims=True))


def _gelu(x):
    cdf = 0.5 * (1.0 + lax.erf(x * SQRT_HALF))
    return x * cdf, cdf


def _gelu_grad(x, cdf):
    return cdf + x * jnp.exp2(x * x * PDF_EXP2_SCALE + PDF_EXP2_SHIFT)


def _masked_ws(ws):
    i = lax.broadcasted_iota(jnp.int32, (SG, SG), 0)
    j = lax.broadcasted_iota(jnp.int32, (SG, SG), 1)
    keep = jnp.logical_or(j < CHUNK, i >= CHUNK)
    return jnp.where(keep[None], ws, jnp.zeros_like(ws))


def _shift_down(halo, q, k):
    ext = jnp.concatenate([halo, q], axis=0)
    return pltpu.roll(ext, k, 0)[halo.shape[0]:]


def _shift_up(q, nxt, k):
    ext = jnp.concatenate([q, nxt], axis=0)
    return pltpu.roll(ext, ext.shape[0] - k, 0)[:q.shape[0]]


def _col(k):
    return slice(k * D, (k + 1) * D)


def _position():
    x, y, c = lax.axis_index("x"), lax.axis_index("y"), lax.axis_index("c")
    return (x, y, c), 4 * x + 2 * y + c


def _lane_block(ref, p, width):
    return ref.at[:, pl.ds(pl.multiple_of(p * width, LANE), width)]


def _row_block(ref, p, rows):
    return ref.at[pl.ds(p * rows, rows), :]


_HBM = pl.BlockSpec(memory_space=pltpu.HBM)
_SEM = pl.BlockSpec(memory_space=pltpu.SEMAPHORE)
_EFFECT = pltpu.SideEffectType.DATAFLOW_SIDE_EFFECTING


REL_ORDER = (1, 2, 4, 3, 5, 6, 7)


def _to_peers(src_of, dst_of, order=REL_ORDER, by_relation=False):
    if by_relation:
        return [(src_of, lambda me, k=k: dst_of(k), k) for k in order]
    return [(src_of, dst_of, k) for k in order]


def _remote_copies(entries, send_sems, recv_sems):
    (x, y, c), me = _position()
    copies = []
    for n, (src_of, dst_of, k) in enumerate(entries):
        dx, dy, dc = _REL[k]
        peer = (1 - x if dx else x, 1 - y if dy else y, 1 - c if dc else c)
        pid = 4 * peer[0] + 2 * peer[1] + peer[2]
        copies.append(pltpu.make_async_remote_copy(
            src_ref=src_of(pid, me), dst_ref=dst_of(me), send_sem=send_sems.at[n], recv_sem=recv_sems.at[n],
            device_id=peer, device_id_type=pl.DeviceIdType.MESH))
    return copies


def _start_copies(name, arrays, make_groups):
    n = len(arrays)
    sizes = [len(g) for g in make_groups([None] * n)]

    def body(*refs):
        sems, token = refs[n:n + 2 * len(sizes)], refs[-1]
        for g, entries in enumerate(make_groups(refs[:n])):
            for cp in _remote_copies(entries, sems[2 * g], sems[2 * g + 1]):
                cp.start()
        token[...] = jnp.zeros_like(token)

    out_shape = []
    for size in sizes:
        out_shape += [pltpu.SemaphoreType.DMA((size,))] * 2
    out_shape += [pltpu.HBM(a.shape, a.dtype) for a in arrays] + [jax.ShapeDtypeStruct((8, LANE), F32)]
    res = _pcall(
        body, name=name, out_shape=out_shape,
        in_specs=[_HBM] * n, out_specs=[_SEM] * (2 * len(sizes)) + [_HBM] * n + [_VMEM],
        input_output_aliases={i: 2 * len(sizes) + i for i in range(n)},
        compiler_params=pltpu.CompilerParams(has_side_effects=_EFFECT),
    )(*[pltpu.with_memory_space_constraint(a, pltpu.HBM) for a in arrays])
    return res[:2 * len(sizes)], res[2 * len(sizes):-1], res[-1]


def _wait_copies(name, arrays, send_sems, recv_sems, make_entries, after):
    n = len(arrays)

    def body(*refs):
        for cp in _remote_copies(make_entries(refs[:n]), refs[n], refs[n + 1]):
            cp.wait_send()
            cp.wait_recv()

    return _pcall(
        body, name=name, out_shape=[pltpu.HBM(a.shape, a.dtype) for a in arrays],
        in_specs=[_HBM] * n + [_SEM, _SEM, _ANY], out_specs=[_HBM] * n,
        input_output_aliases={i: i for i in range(n)},
        compiler_params=pltpu.CompilerParams(has_side_effects=_EFFECT),
    )(*arrays, send_sems, recv_sems, after)


def _place_weights(w_in, w_ff1, w_ff2, w_pa, w_pb, w_o, conv_flat):
    n_items = 7
    c_in, c_ff1 = w_in.shape[1], w_ff1.shape[1]
    r_ff2, r_p = w_ff2.shape[0], w_pa.shape[0]

    def body(win_ref, w1_ref, w2_ref, pa_ref, pb_ref, wo_ref, cw_ref,
             win_o, pa_o, pb_o, wo_o, cw_o, w1_o, w2_o,
             s_win, s_w1, s_w2, s_pa, s_pb, s_wo, s_cw, sems):
        _, me = _position()
        s_cw[...] = jnp.zeros_like(s_cw)
        for j in range(conv_flat.shape[1] // LANE):
            s_cw[j:j + 1, :] = cw_ref[:, j * LANE:(j + 1) * LANE]
        for src, stage in ((win_ref, s_win), (w1_ref, s_w1), (w2_ref, s_w2),
                           (pa_ref, s_pa), (pb_ref, s_pb), (wo_ref, s_wo)):
            stage[...] = src[...].astype(BF16)
        pairs = [
            (s_win, _lane_block(win_o, me, c_in)), (s_pa, _row_block(pa_o, me, r_p)),
            (s_pb, _row_block(pb_o, me, r_p)), (s_wo, _row_block(wo_o, me, r_p)),
            (s_cw, _lane_block(cw_o, me, LANE)),
            (s_w1, _lane_block(w1_o, me, c_ff1)), (s_w2, _row_block(w2_o, me, r_ff2)),
        ]
        copies = [pltpu.make_async_copy(s, d, sems.at[n]) for n, (s, d) in enumerate(pairs)]
        for cp in copies:
            cp.start()
        for cp in copies:
            cp.wait()

    out_shape = (
        jax.ShapeDtypeStruct((D, N_DEV * c_in), BF16),
        jax.ShapeDtypeStruct((N_DEV * r_p, D), BF16),
        jax.ShapeDtypeStruct((N_DEV * r_p, D), BF16),
        jax.ShapeDtypeStruct((N_DEV * r_p, D), BF16),
        jax.ShapeDtypeStruct((8, N_DEV * LANE), F32),
        jax.ShapeDtypeStruct((D, N_DEV * c_ff1), BF16),
        jax.ShapeDtypeStruct((N_DEV * r_ff2, D), BF16),
    )
    return _pcall(
        body, name="place_weights", out_shape=out_shape,
        in_specs=[_VMEM] * n_items, out_specs=[_ANY] * n_items,
        scratch_shapes=[pltpu.VMEM(w.shape, BF16) for w in (w_in, w_ff1, w_ff2, w_pa, w_pb, w_o)]
        + [pltpu.VMEM((8, LANE), F32), pltpu.SemaphoreType.DMA((n_items,))],
        compiler_params=_params(),
    )(w_in, w_ff1, w_ff2, w_pa, w_pb, w_o, conv_flat)


def _gather_entries(refs, views, order=REL_ORDER):
    entries = []
    for r, v in zip(refs, views):
        entries += _to_peers(lambda pid, me, r=r, v=v: v(r, me), lambda me, r=r, v=v: v(r, me), order)
    return entries


def _scatter_entries(views):
    def make(refs):
        srcs, lands = refs[:len(views)], refs[len(views):]
        entries = []
        for r, v, l in zip(srcs, views, lands):
            entries += _to_peers(lambda pid, me, r=r, v=v: v(r, pid), lambda k, l=l: l.at[k - 1], by_relation=True)
        return entries
    return make


def _in_proj_blocks(name, ks, me, act, w_in, proj, width, tt, g_mix=None):
    T = act.shape[0]
    nt = T // tt
    nb = len(ks)
    normed = g_mix is not None

    def body(me_ref, act_ref, *refs):
        if normed:
            g_ref, refs = refs[0], refs[1:]
        w_refs, refs = refs[:nb], refs[nb + 1:]
        if normed:
            proj_out, h_out, w_s, o_s, sems = refs
        else:
            proj_out, w_s, o_s, sems = refs
        i = pl.program_id(0)

        @pl.when(i == 0)
        def _():
            for part, w_ref in enumerate(w_refs):
                w_s[:, part * width:(part + 1) * width] = w_ref[...]

        def copies(slot, step):
            rows = pl.ds(step * tt, tt)
            return [pltpu.make_async_copy(
                o_s.at[slot, :, part * width:(part + 1) * width],
                proj_out.at[rows, pl.ds(pl.multiple_of((me_ref[0] ^ ks[part]) * width, LANE), width)],
                sems.at[slot, part]) for part in range(nb)]

        slot = i % 2

        @pl.when(i >= 2)
        def _():
            for cp in copies(slot, i - 2):
                cp.wait()

        if normed:
            xh, _ = _rms(act_ref[...])
            h = (xh * g_ref[...]).astype(BF16)
            h_out[...] = h
        else:
            h = act_ref[...]
        o_s[slot] = _mm(h, w_s[...]).astype(BF16)
        for cp in copies(slot, i):
            cp.start()

        @pl.when(i == nt - 1)
        def _():
            for cp in copies(slot, i) + (copies(1 - slot, i - 1) if nt > 1 else []):
                cp.wait()

    tile = pl.BlockSpec((tt, D), lambda i, b: (i, 0))
    w_spec = lambda part: pl.BlockSpec((D, width), lambda i, m: (0, m[0] ^ ks[part]), pipeline_mode=pl.Buffered(1))
    extra_in = [_const_spec(g_mix.shape)] if normed else []
    extra_args = [g_mix] if normed else []
    out_shape = [jax.ShapeDtypeStruct(proj.shape, proj.dtype)] + ([jax.ShapeDtypeStruct((T, D), BF16)] if normed else [])
    res = _pcall(
        body, name=name, out_shape=out_shape,
        grid_spec=pltpu.PrefetchScalarGridSpec(
            num_scalar_prefetch=1, grid=(nt,),
            in_specs=[tile] + extra_in + [w_spec(part) for part in range(nb)] + [_ANY],
            out_specs=[_ANY] + ([tile] if normed else []),
            scratch_shapes=[pltpu.VMEM((D, nb * width), BF16), pltpu.VMEM((2, tt, nb * width), BF16),
                            pltpu.SemaphoreType.DMA((2, nb))]),
        input_output_aliases={2 + nb + len(extra_in): 0},
        compiler_params=_params(("arbitrary",)),
    )(me, act, *extra_args, *([w_in] * nb), proj)
    return res if normed else res[0]


def _mixer_fwd(x, proj, b_gate, g_v, w_s, b_s3, conv, w_pa, w_pb, w_o, tt):
    T = x.shape[0]
    nt = T // tt
    nb = tt // SG

    def body(x_ref, proj_ref, bg_ref, gv_ref, ws_ref, bs_ref, cw_ref, pa_w, pb_w, wo_w,
             acm_ref, pa_ref, pb_ref, x1_ref, q_carry, mix_s):
        @pl.when(pl.program_id(0) == 0)
        def _():
            q_carry[...] = jnp.zeros_like(q_carry)

        def proj(k):
            return proj_ref[:, _col(k)].astype(F32)

        vg, _ = _gelu(proj(1))
        vh, _ = _rms(vg)
        vp = (vh * gv_ref[...]).astype(BF16)
        wm = _masked_ws(ws_ref[...]).astype(BF16)
        for n in range(nb):
            rows = slice(n * SG, (n + 1) * SG)
            for g in range(N_GROUPS):
                cols = slice(g * SG, (g + 1) * SG)
                mix_s[rows, cols] = _mm(wm[g], vp[rows, cols]) + bs_ref[g]
        ug, _ = _gelu(proj(0))
        a = (ug * mix_s[...]).astype(BF16)
        acm_ref[:, _col(0)] = a
        pa = _mm(a, pa_w[...])
        pa_ref[...] = pa.astype(BF16)
        m = jax.nn.sigmoid(proj(5) + bg_ref[:, :D]) * pa

        bgate = proj(2)
        q = proj(3) * proj(4)
        halo = q_carry[...]
        cv = cw_ref[0:1, :] * _shift_down(halo, q, 2) + cw_ref[1:2, :] * _shift_down(halo, q, 1) + cw_ref[2:3, :] * q
        q_carry[...] = q[tt - q_carry.shape[0]:, :]
        c = (bgate * cv).astype(BF16)
        acm_ref[:, _col(1)] = c
        pb = _mm(c, pb_w[...])
        pb_ref[...] = pb.astype(BF16)
        m = (m + jax.nn.sigmoid(proj(6) + bg_ref[:, D:]) * pb).astype(BF16)
        acm_ref[:, _col(2)] = m
        x1_ref[...] = x_ref[...] + _mm(m, wo_w[...])

    tile = lambda w: pl.BlockSpec((tt, w), lambda i: (i, 0))
    out_shape = ([jax.ShapeDtypeStruct((T, 3 * D), BF16)] + [jax.ShapeDtypeStruct((T, D), BF16)] * 2
                 + [jax.ShapeDtypeStruct((T, D), F32)])
    return _pcall(
        body, name="mixer_fwd", grid=(nt,), out_shape=out_shape,
        in_specs=[tile(D), tile(IN_COLS), _const_spec(b_gate.shape),
                  _const_spec(g_v.shape), _const_spec(w_s.shape), _const_spec(b_s3.shape), _const_spec(conv.shape),
                  _const_spec(w_pa.shape), _const_spec(w_pb.shape), _const_spec(w_o.shape)],
        out_specs=[tile(3 * D)] + [tile(D)] * 3,
        scratch_shapes=[pltpu.VMEM((8, D), F32), pltpu.VMEM((tt, D), F32)],
        compiler_params=_params(("arbitrary",)),
    )(x, proj, b_gate, g_v, w_s, b_s3, conv, w_pa, w_pb, w_o)


ST_GFIN, ST_GFF, ST_LOSS = 0, 1, 2


def _ffn_fwd_bwd(x1, tgt, g_ff, g_fin, w1, w2, tt):
    T = x1.shape[0]
    nt = T // tt
    nk = D_FF // D

    def body(x1_ref, tgt_ref, gff_ref, gfin_ref, w1_ref, w2_ref,
             hf_ref, s_ref, dpre_ref, dx2_ref, dx1_ref, st_ref, z_s):
        @pl.when(pl.program_id(0) == 0)
        def _():
            st_ref[...] = jnp.zeros_like(st_ref)

        x1 = x1_ref[...]
        xh1, r1 = _rms(x1)
        hf = (xh1 * gff_ref[...]).astype(BF16)
        hf_ref[...] = hf
        acc = jnp.zeros((tt, D), F32)
        for k in range(nk):
            z = jnp.maximum(_mm(hf, w1_ref[:, _col(k)]), 0.0)
            z_s[:, _col(k)] = z.astype(BF16)
            s = (z * z).astype(BF16)
            s_ref[:, _col(k)] = s
            acc = acc + _mm(s, w2_ref[_col(k), :])
        x2 = x1 + acc
        xh2, r2 = _rms(x2)
        diff = xh2 * gfin_ref[...] - tgt_ref[...]
        st_ref[ST_LOSS:ST_LOSS + 1, :] += jnp.sum(diff * diff, axis=0, keepdims=True)
        dy = diff * (1.0 / D)
        st_ref[ST_GFIN:ST_GFIN + 1, :] += jnp.sum(dy * xh2, axis=0, keepdims=True)
        dx2 = _rms_bwd(dy * gfin_ref[...], xh2, r2)
        dx2b = dx2.astype(BF16)
        dx2_ref[...] = dx2b
        dhf = jnp.zeros((tt, D), F32)
        for k in range(nk):
            dpre = (_nt(dx2b, w2_ref[_col(k), :]) * (2.0 * z_s[:, _col(k)].astype(F32))).astype(BF16)
            dpre_ref[:, _col(k)] = dpre
            dhf = dhf + _nt(dpre, w1_ref[:, _col(k)])
        st_ref[ST_GFF:ST_GFF + 1, :] += jnp.sum(dhf * xh1, axis=0, keepdims=True)
        dx1_ref[...] = dx2 + _rms_bwd(dhf * gff_ref[...], xh1, r1)

    tile = lambda w: pl.BlockSpec((tt, w), lambda i: (i, 0))
    out_shape = (jax.ShapeDtypeStruct((T, D), BF16), jax.ShapeDtypeStruct((T, D_FF), BF16),
                 jax.ShapeDtypeStruct((T, D_FF), BF16), jax.ShapeDtypeStruct((T, D), BF16),
                 jax.ShapeDtypeStruct((T, D), F32), jax.ShapeDtypeStruct((8, D), F32))
    return _pcall(
        body, name="ffn_fwd_bwd", grid=(nt,), out_shape=out_shape,
        in_specs=[tile(D), tile(D), _const_spec(g_ff.shape), _const_spec(g_fin.shape),
                  _const_spec(w1.shape), _const_spec(w2.shape)],
        out_specs=[tile(D), tile(D_FF), tile(D_FF), tile(D), tile(D), pl.BlockSpec((8, D), lambda i: (0, 0))],
        scratch_shapes=[pltpu.VMEM((tt, D_FF), BF16)],
        compiler_params=_params(("arbitrary",)),
    )(x1, tgt, g_ff, g_fin, w1, w2)


ST_GV, ST_CONV = 0, 1


def _mixer_bwd(dx1, proj, pa, pb, b_gate, g_v, w_s, b_s3, conv, w_pa, w_pb, w_o, tt, deps=()):
    T = dx1.shape[0]
    nt = T // tt
    nb = tt // SG
    hb = tt // HALO

    def body(dx1_ref, proj_ref, cgh_ref, xsh_ref, pa_ref, pb_ref,
             bg_ref, gv_ref, ws_ref, bs_ref, cw_ref, pa_w, pb_w, wo_w,
             dproj_ref, dstk_ref, st_ref, dbg_ref, dws_ref, dbs_ref, d_carry, mix_s, dvp_s):
        i = pl.program_id(0)

        @pl.when(i == 0)
        def _():
            st_ref[...] = jnp.zeros_like(st_ref)
            dbg_ref[...] = jnp.zeros_like(dbg_ref)
            dws_ref[...] = jnp.zeros_like(dws_ref)
            dbs_ref[...] = jnp.zeros_like(dbs_ref)
            d_carry[...] = jnp.zeros_like(d_carry)

        def pj(k):
            return proj_ref[:, _col(k)].astype(F32)

        def put(k, val):
            dproj_ref[:, _col(k)] = val.astype(BF16)

        dx1b = dx1_ref[...].astype(BF16)
        dstk_ref[:, _col(2)] = dx1b
        dm = _nt(dx1b, wo_w[...])
        s_a = jax.nn.sigmoid(pj(5) + bg_ref[:, :D])
        s_b = jax.nn.sigmoid(pj(6) + bg_ref[:, D:])
        dpa = dm * s_a
        dpb = dm * s_b
        dpa_b = dpa.astype(BF16)
        dpb_b = dpb.astype(BF16)
        dstk_ref[:, _col(0)] = dpa_b
        dstk_ref[:, _col(1)] = dpb_b
        dga = dpa * pa_ref[...].astype(F32) * (1.0 - s_a)
        dgb = dpb * pb_ref[...].astype(F32) * (1.0 - s_b)
        dbg_ref[0:1, :D] += jnp.sum(dga, axis=0, keepdims=True)
        dbg_ref[0:1, D:] += jnp.sum(dgb, axis=0, keepdims=True)
        put(5, dga)
        put(6, dgb)
        da = _nt(dpa_b, pa_w[...])
        dc = _nt(dpb_b, pb_w[...])

        v = pj(1)
        vg, v_cdf = _gelu(v)
        vh, rv = _rms(vg)
        vp = (vh * gv_ref[...]).astype(BF16)
        wm = _masked_ws(ws_ref[...]).astype(BF16)
        for n in range(nb):
            rows = slice(n * SG, (n + 1) * SG)
            for g in range(N_GROUPS):
                cols = slice(g * SG, (g + 1) * SG)
                mix_s[rows, cols] = _mm(wm[g], vp[rows, cols]) + bs_ref[g]
        u = pj(0)
        ug, u_cdf = _gelu(u)
        put(0, da * mix_s[...] * _gelu_grad(u, u_cdf))
        dmix = da * ug
        dmix_b = dmix.astype(BF16)
        for n in range(nb):
            rows = slice(n * SG, (n + 1) * SG)
            for g in range(N_GROUPS):
                cols = slice(g * SG, (g + 1) * SG)
                blk = dmix_b[rows, cols]
                dws_ref[g] += _nt(blk, vp[rows, cols])
                dbs_ref[g] += dmix[rows, cols]
                dvp_s[rows, cols] = _tn(wm[g], blk)
        dvp = dvp_s[...]
        st_ref[ST_GV:ST_GV + 1, :] += jnp.sum(dvp * vh, axis=0, keepdims=True)
        put(1, _rms_bwd(dvp * gv_ref[...], vh, rv) * _gelu_grad(v, v_cdf))

        bgate, cg, xs = pj(2), pj(3), pj(4)
        q = cg * xs
        has_prev = (i < nt - 1).astype(F32)
        halo = cgh_ref[...].astype(F32) * xsh_ref[...].astype(F32) * has_prev
        q2 = _shift_down(halo, q, 2)
        q1 = _shift_down(halo, q, 1)
        w0, w1, w2 = cw_ref[0:1, :], cw_ref[1:2, :], cw_ref[2:3, :]
        put(2, dc * (w0 * q2 + w1 * q1 + w2 * q))
        dcv = dc * bgate
        st_ref[ST_CONV:ST_CONV + 1, :] += jnp.sum(dcv * q2, axis=0, keepdims=True)
        st_ref[ST_CONV + 1:ST_CONV + 2, :] += jnp.sum(dcv * q1, axis=0, keepdims=True)
        st_ref[ST_CONV + 2:ST_CONV + 3, :] += jnp.sum(dcv * q, axis=0, keepdims=True)
        nxt = d_carry[...]
        dq = w2 * dcv + w1 * _shift_up(dcv, nxt, 1) + w0 * _shift_up(dcv, nxt, 2)
        d_carry[...] = dcv[:d_carry.shape[0], :]
        put(3, dq * xs)
        put(4, dq * cg)

    rev = lambda i: nt - 1 - i
    tile = lambda w: pl.BlockSpec((tt, w), lambda i: (rev(i), 0))
    halo_spec = lambda k: pl.BlockSpec((HALO, D), lambda i: (jnp.maximum(rev(i) * hb - 1, 0), k))
    res = lambda shape: pl.BlockSpec(shape, lambda i: (0,) * len(shape))
    out_shape = (jax.ShapeDtypeStruct((T, IN_COLS), BF16), jax.ShapeDtypeStruct((T, 3 * D), BF16),
                 jax.ShapeDtypeStruct((8, D), F32), jax.ShapeDtypeStruct((8, 2 * D), F32),
                 jax.ShapeDtypeStruct((N_GROUPS, SG, SG), F32), jax.ShapeDtypeStruct((N_GROUPS, SG, SG), F32))
    return _pcall(
        _after(body, 14, deps), name="mixer_bwd", grid=(nt,), out_shape=out_shape,
        in_specs=[tile(D), tile(IN_COLS), halo_spec(3), halo_spec(4), tile(D), tile(D),
                  _const_spec(b_gate.shape), _const_spec(g_v.shape), _const_spec(w_s.shape),
                  _const_spec(b_s3.shape), _const_spec(conv.shape),
                  _const_spec(w_pa.shape), _const_spec(w_pb.shape), _const_spec(w_o.shape)] + [_ANY] * len(deps),
        out_specs=[tile(IN_COLS), tile(3 * D), res((8, D)), res((8, 2 * D)),
                   res((N_GROUPS, SG, SG)), res((N_GROUPS, SG, SG))],
        scratch_shapes=[pltpu.VMEM((8, D), F32), pltpu.VMEM((tt, D), F32), pltpu.VMEM((tt, D), F32)],
        compiler_params=_params(("arbitrary",)),
    )(dx1, proj, proj, proj, pa, pb, b_gate, g_v, w_s, b_s3, conv, w_pa, w_pb, w_o, *deps)


def _in_proj_bwd(dproj, x, dx1, g_mix, w_in, tt, deps=()):
    T = x.shape[0]

    def body(dproj_ref, x_ref, dx1_ref, gmix_ref, win_ref, gx_ref, st_ref):
        @pl.when(pl.program_id(0) == 0)
        def _():
            st_ref[...] = jnp.zeros_like(st_ref)

        dh = _nt(dproj_ref[...], win_ref[...])
        xh, r = _rms(x_ref[...])
        st_ref[0:1, :] += jnp.sum(dh * xh, axis=0, keepdims=True)
        gx_ref[...] = dx1_ref[...] + _rms_bwd(dh * gmix_ref[...], xh, r)

    tile = lambda w: pl.BlockSpec((tt, w), lambda i: (i, 0))
    return _pcall(
        _after(body, 5, deps), name="in_proj_bwd", grid=(T // tt,),
        out_shape=(jax.ShapeDtypeStruct((T, D), F32), jax.ShapeDtypeStruct((8, D), F32)),
        in_specs=[tile(IN_COLS), tile(D), tile(D), _const_spec(g_mix.shape), _const_spec(w_in.shape)]
        + [_ANY] * len(deps),
        out_specs=[tile(D), pl.BlockSpec((8, D), lambda i: (0, 0))],
        compiler_params=_params(("arbitrary",)),
    )(dproj, x, dx1, g_mix, w_in, *deps)


def _weight_grad(name, act, dout, bc, tk, deps=()):
    T, n_in = act.shape
    n_out = dout.shape[1]
    nk = T // tk
    bi = min(n_in, D)

    def body(a_ref, d_ref, o_ref, acc):
        k = pl.program_id(2)

        @pl.when(k == 0)
        def _():
            acc[...] = jnp.zeros_like(acc)

        acc[...] += _tn(a_ref[...], d_ref[...])

        @pl.when(k == nk - 1)
        def _():
            o_ref[...] = acc[...].astype(o_ref.dtype)

    return _pcall(
        _after(body, 2, deps), name=name, grid=(n_in // bi, n_out // bc, nk),
        out_shape=jax.ShapeDtypeStruct((n_in, n_out), BF16),
        in_specs=[pl.BlockSpec((tk, bi), lambda i, j, k: (k, i)), pl.BlockSpec((tk, bc), lambda i, j, k: (k, j))]
        + [_ANY] * len(deps),
        out_specs=pl.BlockSpec((bi, bc), lambda i, j, k: (i, j)),
        scratch_shapes=[pltpu.VMEM((bi, bc), F32)],
        compiler_params=_params(("arbitrary", "arbitrary", "arbitrary")),
    )(act, dout, *deps)


def _weight_grad_stack(name, acts, douts, tk):
    T, n = acts.shape[0], acts.shape[1] // D
    nk = T // tk

    def body(a_ref, d_ref, o_ref, acc):
        k = pl.program_id(1)

        @pl.when(k == 0)
        def _():
            acc[...] = jnp.zeros_like(acc)

        acc[...] += _tn(a_ref[...], d_ref[...])

        @pl.when(k == nk - 1)
        def _():
            o_ref[...] = acc[...].astype(o_ref.dtype)

    tile = pl.BlockSpec((tk, D), lambda w, k: (k, w))
    return _pcall(
        body, name=name, grid=(n, nk), out_shape=jax.ShapeDtypeStruct((n, D, D), BF16),
        in_specs=[tile, tile], out_specs=pl.BlockSpec((None, D, D), lambda w, k: (w, 0, 0)),
        scratch_shapes=[pltpu.VMEM((D, D), F32)],
        compiler_params=_params(("arbitrary", "arbitrary")),
    )(acts, douts)


def _adamw(w, g, m, v):
    m = ADAM_B1 * m + (1.0 - ADAM_B1) * g
    v = ADAM_B2 * v + (1.0 - ADAM_B2) * (g * g)
    m_hat = m / (1.0 - ADAM_B1 ** ADAM_STEP)
    v_hat = v / (1.0 - ADAM_B2 ** ADAM_STEP)
    delta = -ADAM_LR * (m_hat / (jnp.sqrt(v_hat) + ADAM_EPS) + ADAM_WD * w)
    return delta, m, v


def _slot_sum(ref, own=None, me=None):
    g = None
    for s in range(N_DEV):
        term = ref[s] if own is None else jnp.where(me == s, own, ref[s])
        g = term.astype(F32) if g is None else g + term.astype(F32)
    return g


def _reduce_adamw(me, items, steps):
    n = len(items)

    def body(me_ref, *refs):
        ins, outs = refs[:5 * n], refs[5 * n:]
        for j in range(n):
            own_ref, slot_ref, w_ref, m_ref, v_ref = ins[5 * j:5 * j + 5]
            g_out, d_out, m_out, v_out = outs[4 * j:4 * j + 4]
            g = own_ref[...].astype(F32)
            for k in range(N_DEV - 1):
                g = g + slot_ref[k].astype(F32)
            g_out[...] = g
            d_out[...], m_out[...], v_out[...] = _adamw(w_ref[...], g, m_ref[...], v_ref[...])

    in_specs, out_specs, out_shape, args = [], [], [], []
    for partial, own_block, slots, w, m, v, lead in items:
        rows, cols = w.shape
        tr = rows // steps
        tile = pl.BlockSpec((tr, cols), lambda i, me_ref: (i, 0))
        if lead is None:
            in_specs += [
                pl.BlockSpec((tr, cols), lambda i, me_ref, own_block=own_block, tr=tr: own_block(i, me_ref[0], tr)),
                pl.BlockSpec((N_DEV - 1, tr, cols), lambda i, me_ref: (0, i, 0))]
        else:
            in_specs += [
                pl.BlockSpec((None, tr, cols),
                             lambda i, me_ref, own_block=own_block, tr=tr, lead=lead: (lead, *own_block(i, me_ref[0], tr))),
                pl.BlockSpec((N_DEV - 1, None, tr, cols), lambda i, me_ref, lead=lead: (0, lead, i, 0))]
        in_specs += [tile, tile, tile]
        out_specs += [tile] * 4
        out_shape += [jax.ShapeDtypeStruct((rows, cols), F32)] * 4
        args += [partial, slots, w, m, v]
    res = _pcall(
        body, name="reduce_adamw", out_shape=out_shape,
        grid_spec=pltpu.PrefetchScalarGridSpec(
            num_scalar_prefetch=1, grid=(steps,), in_specs=in_specs, out_specs=out_specs),
        compiler_params=_params(("arbitrary",)),
    )(me, *args)
    return [res[4 * j:4 * j + 4] for j in range(n)]


SM_GMIX, SM_GV, SM_GFF, SM_GFIN, SM_LOSS, SM_BGATE, SM_BS, SM_ROWS = 0, 1, 2, 3, 4, 5, 7, 8


def _pack_small(st_ffn, st_mix, st_in, dbg, dbs, conv_rows):
    def body(ffn_ref, mix_ref, in_ref, dbg_ref, dbs_ref, sm_ref, conv_ref):
        sm_ref[SM_GMIX:SM_GMIX + 1, :] = in_ref[0:1, :]
        sm_ref[SM_GV:SM_GV + 1, :] = mix_ref[ST_GV:ST_GV + 1, :]
        sm_ref[SM_GFF:SM_GFF + 1, :] = ffn_ref[ST_GFF:ST_GFF + 1, :]
        sm_ref[SM_GFIN:SM_GFIN + 1, :] = ffn_ref[ST_GFIN:ST_GFIN + 1, :]
        sm_ref[SM_LOSS:SM_LOSS + 1, :] = ffn_ref[ST_LOSS:ST_LOSS + 1, :]
        sm_ref[SM_BGATE:SM_BGATE + 1, :] = dbg_ref[0:1, :D]
        sm_ref[SM_BGATE + 1:SM_BGATE + 2, :] = dbg_ref[0:1, D:]
        for g in range(N_GROUPS):
            sm_ref[SM_BS:SM_BS + 1, g * SG:(g + 1) * SG] = jnp.sum(dbs_ref[g].T, axis=0, keepdims=True)
        conv_ref[...] = jnp.zeros_like(conv_ref)
        for p in range(N_DEV):
            conv_ref[p, 0:conv_rows, :] = mix_ref[ST_CONV:ST_CONV + conv_rows, p * LANE:(p + 1) * LANE]

    return _pcall(
        body, name="pack_small",
        out_shape=(jax.ShapeDtypeStruct((SM_ROWS, D), F32), jax.ShapeDtypeStruct((N_DEV, 8, LANE), F32)),
        in_specs=[_VMEM] * 5, out_specs=[_VMEM] * 2, compiler_params=_params(),
    )(st_ffn, st_mix, st_in, dbg, dbs)


def _small_update(sm_own, sm_slots, ws_own, ws_slots, conv_own, conv_slots, params, conv_rows):
    flat = [a for t in params for a in t]

    def body(smo_ref, sm_ref, wso_ref, ws_ref, convo_ref, conv_ref, *refs):
        ins, outs = refs[:len(flat)], refs[len(flat):]
        loss_ref, outs = outs[0], outs[1:]
        _, me = _position()
        sm = _slot_sum(sm_ref, smo_ref[...], me)
        loss_ref[...] = (0.5 / D) * jnp.sum(sm[SM_LOSS:SM_LOSS + 1, :], axis=1, keepdims=True)

        def update(n, g, at=lambda r: r):
            w_ref, m_ref, v_ref = [at(r) for r in ins[3 * n:3 * n + 3]]
            g_out, d_out, m_out, v_out = [at(r) for r in outs[4 * n:4 * n + 4]]
            g_out[...] = g
            d_out[...], m_out[...], v_out[...] = _adamw(w_ref[...], g, m_ref[...], v_ref[...])

        for n, row in enumerate((SM_GMIX, SM_GV, SM_GFF, SM_GFIN)):
            update(n, sm[row:row + 1, :])
        for half in range(2):
            update(4, sm[SM_BGATE + half:SM_BGATE + half + 1, :], lambda r, half=half: r.at[:, pl.ds(half * D, D)])
        for g in range(N_GROUPS):
            update(5, sm[SM_BS:SM_BS + 1, g * SG:(g + 1) * SG], lambda r, g=g: r.at[0, pl.ds(g, 1), :])
        update(6, _masked_ws(_slot_sum(ws_ref, wso_ref[...], me)), lambda r: r.at[0])
        d_conv = _slot_sum(conv_ref, convo_ref[me], me)
        for j in range(conv_rows):
            update(7, d_conv[j:j + 1, :], lambda r, j=j: r.at[:, pl.ds(j * LANE, LANE)])

    out_shape = [jax.ShapeDtypeStruct((1, 1), F32)]
    for w, _, _ in params:
        out_shape += [jax.ShapeDtypeStruct(w.shape, F32)] * 4
    return _pcall(
        body, name="small_update", out_shape=out_shape,
        in_specs=[_VMEM] * (6 + len(flat)), out_specs=[_VMEM] * len(out_shape), compiler_params=_params(),
    )(sm_own, sm_slots, ws_own, ws_slots, conv_own, conv_slots, *flat)


def kernel(x, norm_mix_g, w_in, b_gate, norm_v_g, w_s, b_s, conv_w, w_proj_a, w_proj_b, w_out, norm_ff_g, w_ff1, w_ff2, norm_final_g, loss_target, m_norm_mix_g, m_w_in, m_b_gate, m_norm_v_g, m_w_s, m_b_s, m_conv_w, m_w_proj_a, m_w_proj_b, m_w_out, m_norm_ff_g, m_w_ff1, m_w_ff2, m_norm_final_g, v_norm_mix_g, v_w_in, v_b_gate, v_norm_v_g, v_w_s, v_b_s, v_conv_w, v_w_proj_a, v_w_proj_b, v_w_out, v_norm_ff_g, v_w_ff1, v_w_ff2, v_norm_final_g):
    T = x.shape[1]
    tt = min(256, T)
    tk = min(4096, T)
    conv_rows = conv_w.shape[1]

    flat = lambda a: a.reshape(1, conv_rows * LANE)
    xs = x.reshape(T, D)
    tgt = loss_target.reshape(T, D)
    g_mix, g_v, g_ff, g_fin = norm_mix_g, norm_v_g, norm_ff_g, norm_final_g.reshape(1, D)
    ws = w_s[0]
    bs3 = b_s.reshape(N_GROUPS, SG, 1)

    c_in, c_ff1 = w_in.shape[2], w_ff1.shape[2]
    r_ff2, r_p = w_ff2.shape[1], w_proj_a.shape[1]
    lane_view = lambda width: (lambda ref, p: _lane_block(ref, p, width))
    row_view = lambda rows: (lambda ref, p: _row_block(ref, p, rows))
    slots = lambda shape, dtype: lax.empty((N_DEV,) + shape, dtype)
    peer_slots = lambda shape, dtype: lax.empty((N_DEV - 1,) + shape, dtype)

    placed = _place_weights(w_in[0], w_ff1[0], w_ff2[0], w_proj_a[0], w_proj_b[0], w_out[0], flat(conv_w))
    mixer_views = [row_view(r_p), row_view(r_p), row_view(r_p), lane_view(LANE)]
    ffn_views = [lane_view(c_ff1), row_view(r_ff2)]
    pairs = ((0, 1), (2, 4), (3, 5), (6, 7))
    in_group = lambda refs, ks: _gather_entries(refs[:1], [lane_view(c_in)], tuple(k for k in ks if k))

    def gather_groups(refs):
        return ([in_group(refs, ks) for ks in pairs]
                + [_gather_entries(refs[1:5], mixer_views), _gather_entries(refs[5:], ffn_views)])

    sems, placed, g_token = _start_copies("gather_start", placed, gather_groups)
    me = _position()[1].reshape(1)
    W_in = placed[0]
    tp = min(2048, T)
    proj = lax.empty((T, IN_COLS), BF16)
    for n, ks in enumerate(pairs):
        W_in, = _wait_copies(f"gather_wait_in_{n}", [W_in], sems[2 * n], sems[2 * n + 1],
                             lambda refs, ks=ks: in_group(refs, ks), proj if n else g_token)
        if n == 0:
            proj, h = _in_proj_blocks(f"in_proj_{n}", ks, me, xs, W_in, proj, c_in, tp, g_mix)
        else:
            proj = _in_proj_blocks(f"in_proj_{n}", ks, me, h, W_in, proj, c_in, tp)
    n = len(pairs)
    PA, PB, WO, conv = _wait_copies(
        "gather_wait_mixer", placed[1:5], sems[2 * n], sems[2 * n + 1],
        lambda refs: _gather_entries(refs, mixer_views), proj)
    acm, pa, pb, x1 = _mixer_fwd(xs, proj, b_gate, g_v, ws, bs3, conv, PA, PB, WO, min(512, T))
    W1, W2 = _wait_copies(
        "gather_wait_ffn", placed[5:], sems[2 * n + 2], sems[2 * n + 3],
        lambda refs: _gather_entries(refs, ffn_views), x1)
    hf, s, dpre, dx2, dx1, st_ffn = _ffn_fwd_bwd(x1, tgt, g_ff, g_fin, W1, W2, min(512, T))

    d_ff2 = _weight_grad("dw_ff2", s, dx2, D, tk)
    d_ff1 = _weight_grad("dw_ff1", hf, dpre, D, tk)
    ff_views = [lane_view(c_ff1), row_view(r_ff2)]

    dproj, dstk, st_mix, dbg, dws, dbs = _mixer_bwd(
        dx1, proj, pa, pb, b_gate, g_v, ws, bs3, conv, PA, PB, WO, tt)
    d_p3 = _weight_grad_stack("dw_proj", acm, dstk, tk)

    def proj_entries(refs):
        d3, dws_ref, land3, land_ws = refs
        entries = []
        for w in range(3):
            entries += _to_peers(lambda pid, me, w=w: d3.at[w, pl.ds(pid * r_p, r_p), :],
                                 lambda k, w=w: land3.at[k - 1, w], by_relation=True)
        return entries + _to_peers(lambda pid, me: dws_ref, lambda me: land_ws.at[me])

    pf_sems, pf_arrays, p_token = _start_copies(
        "scatter_start_proj_ffn",
        [d_p3, dws, peer_slots((3, r_p, D), BF16), slots(dws.shape, F32),
         d_ff1, d_ff2, peer_slots((D, c_ff1), BF16), peer_slots((r_ff2, D), BF16)],
        lambda refs: [proj_entries(refs[:4]), _scatter_entries(ff_views)(refs[4:])])
    p_sems, p_arrays, ff_sems, ff_arrays = pf_sems[:2], pf_arrays[:4], pf_sems[2:], pf_arrays[4:]

    d_in = _weight_grad("dw_in", h, dproj, D, tk, deps=(p_token,))
    in_views = [lane_view(c_in)]
    in_sems, in_arrays, in_token = _start_copies(
        "scatter_start_in", [d_in, peer_slots((D, c_in), BF16)], lambda refs: [_scatter_entries(in_views)(refs)])
    grad_x, st_in = _in_proj_bwd(dproj, xs, dx1, g_mix, W_in, min(512, T), deps=(in_token,))

    small, d_conv = _pack_small(st_ffn, st_mix, st_in, dbg, dbs, conv_rows)

    def small_entries(refs):
        sm, dc, land_sm, land_dc = refs
        return (_to_peers(lambda pid, me: sm, lambda me: land_sm.at[me])
                + _to_peers(lambda pid, me: dc.at[pid], lambda me: land_dc.at[me]))

    sm_sems, sm_arrays, sm_token = _start_copies(
        "small_start", [small, d_conv, slots(small.shape, F32), slots(d_conv.shape[1:], F32)],
        lambda refs: [small_entries(refs)])
    d_ff1, d_ff2, s_ff1, s_ff2 = _wait_copies(
        "scatter_wait_ffn", ff_arrays, ff_sems[0], ff_sems[1], _scatter_entries(ff_views), sm_token)
    d_p3, dws, s_p3, s_ws = _wait_copies(
        "scatter_wait_proj", p_arrays, p_sems[0], p_sems[1], proj_entries, sm_token)
    d_in, s_in = _wait_copies("scatter_wait_in", in_arrays, in_sems[0], in_sems[1], _scatter_entries(in_views), sm_token)

    own_cols = lambda i, me, tr: (i, me)
    own_rows = lambda rows: (lambda i, me, tr: (me * (rows // tr) + i, 0))
    names = ["w_in", "w_ff1", "w_ff2", "w_proj_a", "w_proj_b", "w_out"]
    quads = _reduce_adamw(me, [
        (d_in, own_cols, s_in, w_in[0], m_w_in[0], v_w_in[0], None),
        (d_ff1, own_cols, s_ff1, w_ff1[0], m_w_ff1[0], v_w_ff1[0], None),
        (d_ff2, own_rows(r_ff2), s_ff2, w_ff2[0], m_w_ff2[0], v_w_ff2[0], None),
        (d_p3, own_rows(r_p), s_p3, w_proj_a[0], m_w_proj_a[0], v_w_proj_a[0], 0),
        (d_p3, own_rows(r_p), s_p3, w_proj_b[0], m_w_proj_b[0], v_w_proj_b[0], 1),
        (d_p3, own_rows(r_p), s_p3, w_out[0], m_w_out[0], v_w_out[0], 2),
    ], 8)
    big = dict(zip(names, quads))

    one = lambda a: a.reshape(1, D)
    small_params = [
        (norm_mix_g, m_norm_mix_g, v_norm_mix_g),
        (norm_v_g, m_norm_v_g, v_norm_v_g),
        (norm_ff_g, m_norm_ff_g, v_norm_ff_g),
        (one(norm_final_g), one(m_norm_final_g), one(v_norm_final_g)),
        (b_gate, m_b_gate, v_b_gate),
        (b_s, m_b_s, v_b_s),
        (w_s, m_w_s, v_w_s),
        (flat(conv_w), flat(m_conv_w), flat(v_conv_w)),
    ]
    small, d_conv, r_small, r_conv = _wait_copies(
        "small_wait", sm_arrays, sm_sems[0], sm_sems[1], small_entries, quads[0][0])
    res = _small_update(small, r_small, dws, s_ws, d_conv, r_conv, small_params, conv_rows)
    loss = res[0].reshape(())
    names = ["norm_mix_g", "norm_v_g", "norm_ff_g", "norm_final_g", "b_gate", "b_s", "w_s", "conv_w"]
    out = {name: list(res[1 + 4 * n:5 + 4 * n]) for n, name in enumerate(names)}
    out["norm_final_g"] = [q.reshape(norm_final_g.shape) for q in out["norm_final_g"]]
    out["conv_w"] = [q.reshape(conv_w.shape) for q in out["conv_w"]]
    for name, quad in big.items():
        out[name] = [q[None] for q in quad]

    order = ["norm_mix_g", "w_in", "b_gate", "norm_v_g", "w_s", "b_s", "conv_w", "w_proj_a", "w_proj_b", "w_out",
             "norm_ff_g", "w_ff1", "w_ff2", "norm_final_g"]
    grads = [out[n][0] for n in order]
    deltas = [out[n][1] for n in order]
    new_m = [out[n][2] for n in order]
    new_v = [out[n][3] for n in order]
    return (loss, grad_x.reshape(x.shape), *grads, *deltas, *new_m, *new_v)
```

```python
import math

import jax
import jax.numpy as jnp
from jax import lax
from jax.experimental import pallas as pl
from jax.experimental.pallas import tpu as pltpu

F32 = jnp.float32
BF16 = jnp.bfloat16

N_DEV = 8
D = 1024
D_FF = 4096
IN_COLS = 7 * D
SG = 128
N_GROUPS = 8
CHUNK = 64
EPS = 1e-6
HALO = 16
LANE = 128
VMEM_LIMIT = 62 * 1024 * 1024

ADAM_LR = 0.001
ADAM_B1 = 0.9
ADAM_B2 = 0.999
ADAM_EPS = 1e-08
ADAM_WD = 0.01
ADAM_STEP = 10

SQRT_HALF = math.sqrt(0.5)
PDF_EXP2_SCALE = -0.5 * math.log2(math.e)
PDF_EXP2_SHIFT = math.log2(1.0 / math.sqrt(2.0 * math.pi))

_REL = [(dx, dy, dc) for dx in (0, 1) for dy in (0, 1) for dc in (0, 1)]

_VMEM = pl.BlockSpec(memory_space=pltpu.VMEM)
_ANY = pl.BlockSpec(memory_space=pl.ANY)


def _pcall(body, **kw):
    return pl.pallas_call(body, **kw)


def _params(sem=None):
    if sem is None:
        return pltpu.CompilerParams(vmem_limit_bytes=VMEM_LIMIT)
    return pltpu.CompilerParams(dimension_semantics=sem, vmem_limit_bytes=VMEM_LIMIT)


def _const_spec(shape):
    nd = len(shape)
    return pl.BlockSpec(shape, lambda *_: (0,) * nd, pipeline_mode=pl.Buffered(1))


def _after(body, n_in, deps):
    def wrapped(*refs):
        return body(*refs[:n_in], *refs[n_in + len(deps):])
    return wrapped


def _mm(a, b):
    return jnp.dot(a, b, preferred_element_type=F32)


def _nt(a, b):
    return lax.dot_general(a, b, (((1,), (1,)), ((), ())), preferred_element_type=F32)


def _tn(a, b):
    return lax.dot_general(a, b, (((0,), (0,)), ((), ())), preferred_element_type=F32)


def _rms(x):
    r = lax.rsqrt(jnp.mean(x * x, axis=-1, keepdims=True) + EPS)
    return x * r, r


def _rms_bwd(dyg, xh, r):
    return r * (dyg - xh * jnp.mean(dyg * xh, axis=-1, keepdims=True))


def _gelu(x):
    cdf = 0.5 * (1.0 + lax.erf(x * SQRT_HALF))
    return x * cdf, cdf


def _gelu_grad(x, cdf):
    return cdf + x * jnp.exp2(x * x * PDF_EXP2_SCALE + PDF_EXP2_SHIFT)


def _masked_ws(ws):
    i = lax.broadcasted_iota(jnp.int32, (SG, SG), 0)
    j = lax.broadcasted_iota(jnp.int32, (SG, SG), 1)
    keep = jnp.logical_or(j < CHUNK, i >= CHUNK)
    return jnp.where(keep[None], ws, jnp.zeros_like(ws))


def _shift_down(halo, q, k):
    ext = jnp.concatenate([halo, q], axis=0)
    return pltpu.roll(ext, k, 0)[halo.shape[0]:]


def _shift_up(q, nxt, k):
    ext = jnp.concatenate([q, nxt], axis=0)
    return pltpu.roll(ext, ext.shape[0] - k, 0)[:q.shape[0]]


def _col(k):
    return slice(k * D, (k + 1) * D)


def _position():
    x, y, c = lax.axis_index("x"), lax.axis_index("y"), lax.axis_index("c")
    return (x, y, c), 4 * x + 2 * y + c


def _lane_block(ref, p, width):
    return ref.at[:, pl.ds(pl.multiple_of(p * width, LANE), width)]


def _row_block(ref, p, rows):
    return ref.at[pl.ds(p * rows, rows), :]


_HBM = pl.BlockSpec(memory_space=pltpu.HBM)
_SEM = pl.BlockSpec(memory_space=pltpu.SEMAPHORE)
_EFFECT = pltpu.SideEffectType.DATAFLOW_SIDE_EFFECTING


REL_ORDER = (1, 2, 4, 3, 5, 6, 7)


def _to_peers(src_of, dst_of, order=REL_ORDER, by_relation=False):
    if by_relation:
        return [(src_of, lambda me, k=k: dst_of(k), k) for k in order]
    return [(src_of, dst_of, k) for k in order]


def _remote_copies(entries, send_sems, recv_sems):
    (x, y, c), me = _position()
    copies = []
    for n, (src_of, dst_of, k) in enumerate(entries):
        dx, dy, dc = _REL[k]
        peer = (1 - x if dx else x, 1 - y if dy else y, 1 - c if dc else c)
        pid = 4 * peer[0] + 2 * peer[1] + peer[2]
        copies.append(pltpu.make_async_remote_copy(
            src_ref=src_of(pid, me), dst_ref=dst_of(me), send_sem=send_sems.at[n], recv_sem=recv_sems.at[n],
            device_id=peer, device_id_type=pl.DeviceIdType.MESH))
    return copies


def _start_copies(name, arrays, make_groups):
    n = len(arrays)
    sizes = [len(g) for g in make_groups([None] * n)]

    def body(*refs):
        sems, token = refs[n:n + 2 * len(sizes)], refs[-1]
        for g, entries in enumerate(make_groups(refs[:n])):
            for cp in _remote_copies(entries, sems[2 * g], sems[2 * g + 1]):
                cp.start()
        token[...] = jnp.zeros_like(token)

    out_shape = []
    for size in sizes:
        out_shape += [pltpu.SemaphoreType.DMA((size,))] * 2
    out_shape += [pltpu.HBM(a.shape, a.dtype) for a in arrays] + [jax.ShapeDtypeStruct((8, LANE), F32)]
    res = _pcall(
        body, name=name, out_shape=out_shape,
        in_specs=[_HBM] * n, out_specs=[_SEM] * (2 * len(sizes)) + [_HBM] * n + [_VMEM],
        input_output_aliases={i: 2 * len(sizes) + i for i in range(n)},
        compiler_params=pltpu.CompilerParams(has_side_effects=_EFFECT),
    )(*[pltpu.with_memory_space_constraint(a, pltpu.HBM) for a in arrays])
    return res[:2 * len(sizes)], res[2 * len(sizes):-1], res[-1]


def _wait_copies(name, arrays, send_sems, recv_sems, make_entries, after):
    n = len(arrays)

    def body(*refs):
        for cp in _remote_copies(make_entries(refs[:n]), refs[n], refs[n + 1]):
            cp.wait_send()
            cp.wait_recv()

    return _pcall(
        body, name=name, out_shape=[pltpu.HBM(a.shape, a.dtype) for a in arrays],
        in_specs=[_HBM] * n + [_SEM, _SEM, _ANY], out_specs=[_HBM] * n,
        input_output_aliases={i: i for i in range(n)},
        compiler_params=pltpu.CompilerParams(has_side_effects=_EFFECT),
    )(*arrays, send_sems, recv_sems, after)


def _place_weights(w_in, w_ff1, w_ff2, w_pa, w_pb, w_o, conv_flat):
    n_items = 7
    c_in, c_ff1 = w_in.shape[1], w_ff1.shape[1]
    r_ff2, r_p = w_ff2.shape[0], w_pa.shape[0]

    def body(win_ref, w1_ref, w2_ref, pa_ref, pb_ref, wo_ref, cw_ref,
             win_o, pa_o, pb_o, wo_o, cw_o, w1_o, w2_o,
             s_win, s_w1, s_w2, s_pa, s_pb, s_wo, s_cw, sems):
        _, me = _position()
        s_cw[...] = jnp.zeros_like(s_cw)
        for j in range(conv_flat.shape[1] // LANE):
            s_cw[j:j + 1, :] = cw_ref[:, j * LANE:(j + 1) * LANE]
        for src, stage in ((win_ref, s_win), (w1_ref, s_w1), (w2_ref, s_w2),
                           (pa_ref, s_pa), (pb_ref, s_pb), (wo_ref, s_wo)):
            stage[...] = src[...].astype(BF16)
        pairs = [
            (s_win, _lane_block(win_o, me, c_in)), (s_pa, _row_block(pa_o, me, r_p)),
            (s_pb, _row_block(pb_o, me, r_p)), (s_wo, _row_block(wo_o, me, r_p)),
            (s_cw, _lane_block(cw_o, me, LANE)),
            (s_w1, _lane_block(w1_o, me, c_ff1)), (s_w2, _row_block(w2_o, me, r_ff2)),
        ]
        copies = [pltpu.make_async_copy(s, d, sems.at[n]) for n, (s, d) in enumerate(pairs)]
        for cp in copies:
            cp.start()
        for cp in copies:
            cp.wait()

    out_shape = (
        jax.ShapeDtypeStruct((D, N_DEV * c_in), BF16),
        jax.ShapeDtypeStruct((N_DEV * r_p, D), BF16),
        jax.ShapeDtypeStruct((N_DEV * r_p, D), BF16),
        jax.ShapeDtypeStruct((N_DEV * r_p, D), BF16),
        jax.ShapeDtypeStruct((8, N_DEV * LANE), F32),
        jax.ShapeDtypeStruct((D, N_DEV * c_ff1), BF16),
        jax.ShapeDtypeStruct((N_DEV * r_ff2, D), BF16),
    )
    return _pcall(
        body, name="place_weights", out_shape=out_shape,
        in_specs=[_VMEM] * n_items, out_specs=[_ANY] * n_items,
        scratch_shapes=[pltpu.VMEM(w.shape, BF16) for w in (w_in, w_ff1, w_ff2, w_pa, w_pb, w_o)]
        + [pltpu.VMEM((8, LANE), F32), pltpu.SemaphoreType.DMA((n_items,))],
        compiler_params=_params(),
    )(w_in, w_ff1, w_ff2, w_pa, w_pb, w_o, conv_flat)


def _gather_entries(refs, views, order=REL_ORDER):
    entries = []
    for r, v in zip(refs, views):
        entries += _to_peers(lambda pid, me, r=r, v=v: v(r, me), lambda me, r=r, v=v: v(r, me), order)
    return entries


def _scatter_entries(views):
    def make(refs):
        srcs, lands = refs[:len(views)], refs[len(views):]
        entries = []
        for r, v, l in zip(srcs, views, lands):
            entries += _to_peers(lambda pid, me, r=r, v=v: v(r, pid), lambda k, l=l: l.at[k - 1], by_relation=True)
        return entries
    return make


def _in_proj_blocks(name, ks, me, act, w_in, proj, width, tt, g_mix=None):
    T = act.shape[0]
    nt = T // tt
    nb = len(ks)
    normed = g_mix is not None

    def body(me_ref, act_ref, *refs):
        if normed:
            g_ref, refs = refs[0], refs[1:]
        w_refs, refs = refs[:nb], refs[nb + 1:]
        if normed:
            proj_out, h_out, w_s, o_s, sems = refs
        else:
            proj_out, w_s, o_s, sems = refs
        i = pl.program_id(0)

        @pl.when(i == 0)
        def _():
            for part, w_ref in enumerate(w_refs):
                w_s[:, part * width:(part + 1) * width] = w_ref[...]

        def copies(slot, step):
            rows = pl.ds(step * tt, tt)
            return [pltpu.make_async_copy(
                o_s.at[slot, :, part * width:(part + 1) * width],
                proj_out.at[rows, pl.ds(pl.multiple_of((me_ref[0] ^ ks[part]) * width, LANE), width)],
                sems.at[slot, part]) for part in range(nb)]

        slot = i % 2

        @pl.when(i >= 2)
        def _():
            for cp in copies(slot, i - 2):
                cp.wait()

        if normed:
            xh, _ = _rms(act_ref[...])
            h = (xh * g_ref[...]).astype(BF16)
            h_out[...] = h
        else:
            h = act_ref[...]
        o_s[slot] = _mm(h, w_s[...]).astype(BF16)
        for cp in copies(slot, i):
            cp.start()

        @pl.when(i == nt - 1)
        def _():
            for cp in copies(slot, i) + (copies(1 - slot, i - 1) if nt > 1 else []):
                cp.wait()

    tile = pl.BlockSpec((tt, D), lambda i, b: (i, 0))
    w_spec = lambda part: pl.BlockSpec((D, width), lambda i, m: (0, m[0] ^ ks[part]), pipeline_mode=pl.Buffered(1))
    extra_in = [_const_spec(g_mix.shape)] if normed else []
    extra_args = [g_mix] if normed else []
    out_shape = [jax.ShapeDtypeStruct(proj.shape, proj.dtype)] + ([jax.ShapeDtypeStruct((T, D), BF16)] if normed else [])
    res = _pcall(
        body, name=name, out_shape=out_shape,
        grid_spec=pltpu.PrefetchScalarGridSpec(
            num_scalar_prefetch=1, grid=(nt,),
            in_specs=[tile] + extra_in + [w_spec(part) for part in range(nb)] + [_ANY],
            out_specs=[_ANY] + ([tile] if normed else []),
            scratch_shapes=[pltpu.VMEM((D, nb * width), BF16), pltpu.VMEM((2, tt, nb * width), BF16),
                            pltpu.SemaphoreType.DMA((2, nb))]),
        input_output_aliases={2 + nb + len(extra_in): 0},
        compiler_params=_params(("arbitrary",)),
    )(me, act, *extra_args, *([w_in] * nb), proj)
    return res if normed else res[0]


def _mixer_fwd(x, proj, b_gate, g_v, w_s, b_s3, conv, w_pa, w_pb, w_o, tt):
    T = x.shape[0]
    nt = T // tt
    nb = tt // SG

    def body(x_ref, proj_ref, bg_ref, gv_ref, ws_ref, bs_ref, cw_ref, pa_w, pb_w, wo_w,
             acm_ref, pa_ref, pb_ref, x1_ref, q_carry, mix_s):
        @pl.when(pl.program_id(0) == 0)
        def _():
            q_carry[...] = jnp.zeros_like(q_carry)

        def proj(k):
            return proj_ref[:, _col(k)].astype(F32)

        vg, _ = _gelu(proj(1))
        vh, _ = _rms(vg)
        vp = (vh * gv_ref[...]).astype(BF16)
        wm = _masked_ws(ws_ref[...]).astype(BF16)
        for n in range(nb):
            rows = slice(n * SG, (n + 1) * SG)
            for g in range(N_GROUPS):
                cols = slice(g * SG, (g + 1) * SG)
                mix_s[rows, cols] = _mm(wm[g], vp[rows, cols]) + bs_ref[g]
        ug, _ = _gelu(proj(0))
        a = (ug * mix_s[...]).astype(BF16)
        acm_ref[:, _col(0)] = a
        pa = _mm(a, pa_w[...])
        pa_ref[...] = pa.astype(BF16)
        m = jax.nn.sigmoid(proj(5) + bg_ref[:, :D]) * pa

        bgate = proj(2)
        q = proj(3) * proj(4)
        halo = q_carry[...]
        cv = cw_ref[0:1, :] * _shift_down(halo, q, 2) + cw_ref[1:2, :] * _shift_down(halo, q, 1) + cw_ref[2:3, :] * q
        q_carry[...] = q[tt - q_carry.shape[0]:, :]
        c = (bgate * cv).astype(BF16)
        acm_ref[:, _col(1)] = c
        pb = _mm(c, pb_w[...])
        pb_ref[...] = pb.astype(BF16)
        m = (m + jax.nn.sigmoid(proj(6) + bg_ref[:, D:]) * pb).astype(BF16)
        acm_ref[:, _col(2)] = m
        x1_ref[...] = x_ref[...] + _mm(m, wo_w[...])

    tile = lambda w: pl.BlockSpec((tt, w), lambda i: (i, 0))
    out_shape = ([jax.ShapeDtypeStruct((T, 3 * D), BF16)] + [jax.ShapeDtypeStruct((T, D), BF16)] * 2
                 + [jax.ShapeDtypeStruct((T, D), F32)])
    return _pcall(
        body, name="mixer_fwd", grid=(nt,), out_shape=out_shape,
        in_specs=[tile(D), tile(IN_COLS), _const_spec(b_gate.shape),
                  _const_spec(g_v.shape), _const_spec(w_s.shape), _const_spec(b_s3.shape), _const_spec(conv.shape),
                  _const_spec(w_pa.shape), _const_spec(w_pb.shape), _const_spec(w_o.shape)],
        out_specs=[tile(3 * D)] + [tile(D)] * 3,
        scratch_shapes=[pltpu.VMEM((8, D), F32), pltpu.VMEM((tt, D), F32)],
        compiler_params=_params(("arbitrary",)),
    )(x, proj, b_gate, g_v, w_s, b_s3, conv, w_pa, w_pb, w_o)


ST_GFIN, ST_GFF, ST_LOSS = 0, 1, 2


def _ffn_fwd_bwd(x1, tgt, g_ff, g_fin, w1, w2, tt):
    T = x1.shape[0]
    nt = T // tt
    nk = D_FF // D

    def body(x1_ref, tgt_ref, gff_ref, gfin_ref, w1_ref, w2_ref,
             hf_ref, s_ref, dpre_ref, dx2_ref, dx1_ref, st_ref, z_s):
        @pl.when(pl.program_id(0) == 0)
        def _():
            st_ref[...] = jnp.zeros_like(st_ref)

        x1 = x1_ref[...]
        xh1, r1 = _rms(x1)
        hf = (xh1 * gff_ref[...]).astype(BF16)
        hf_ref[...] = hf
        acc = jnp.zeros((tt, D), F32)
        for k in range(nk):
            z = jnp.maximum(_mm(hf, w1_ref[:, _col(k)]), 0.0)
            z_s[:, _col(k)] = z.astype(BF16)
            s = (z * z).astype(BF16)
            s_ref[:, _col(k)] = s
            acc = acc + _mm(s, w2_ref[_col(k), :])
        x2 = x1 + acc
        xh2, r2 = _rms(x2)
        diff = xh2 * gfin_ref[...] - tgt_ref[...]
        st_ref[ST_LOSS:ST_LOSS + 1, :] += jnp.sum(diff * diff, axis=0, keepdims=True)
        dy = diff * (1.0 / D)
        st_ref[ST_GFIN:ST_GFIN + 1, :] += jnp.sum(dy * xh2, axis=0, keepdims=True)
        dx2 = _rms_bwd(dy * gfin_ref[...], xh2, r2)
        dx2b = dx2.astype(BF16)
        dx2_ref[...] = dx2b
        dhf = jnp.zeros((tt, D), F32)
        for k in range(nk):
            dpre = (_nt(dx2b, w2_ref[_col(k), :]) * (2.0 * z_s[:, _col(k)].astype(F32))).astype(BF16)
            dpre_ref[:, _col(k)] = dpre
            dhf = dhf + _nt(dpre, w1_ref[:, _col(k)])
        st_ref[ST_GFF:ST_GFF + 1, :] += jnp.sum(dhf * xh1, axis=0, keepdims=True)
        dx1_ref[...] = dx2 + _rms_bwd(dhf * gff_ref[...], xh1, r1)

    tile = lambda w: pl.BlockSpec((tt, w), lambda i: (i, 0))
    out_shape = (jax.ShapeDtypeStruct((T, D), BF16), jax.ShapeDtypeStruct((T, D_FF), BF16),
                 jax.ShapeDtypeStruct((T, D_FF), BF16), jax.ShapeDtypeStruct((T, D), BF16),
                 jax.ShapeDtypeStruct((T, D), F32), jax.ShapeDtypeStruct((8, D), F32))
    return _pcall(
        body, name="ffn_fwd_bwd", grid=(nt,), out_shape=out_shape,
        in_specs=[tile(D), tile(D), _const_spec(g_ff.shape), _const_spec(g_fin.shape),
                  _const_spec(w1.shape), _const_spec(w2.shape)],
        out_specs=[tile(D), tile(D_FF), tile(D_FF), tile(D), tile(D), pl.BlockSpec((8, D), lambda i: (0, 0))],
        scratch_shapes=[pltpu.VMEM((tt, D_FF), BF16)],
        compiler_params=_params(("arbitrary",)),
    )(x1, tgt, g_ff, g_fin, w1, w2)


ST_GV, ST_CONV = 0, 1


def _mixer_bwd(dx1, proj, pa, pb, b_gate, g_v, w_s, b_s3, conv, w_pa, w_pb, w_o, tt, deps=()):
    T = dx1.shape[0]
    nt = T // tt
    nb = tt // SG
    hb = tt // HALO

    def body(dx1_ref, proj_ref, cgh_ref, xsh_ref, pa_ref, pb_ref,
             bg_ref, gv_ref, ws_ref, bs_ref, cw_ref, pa_w, pb_w, wo_w,
             dproj_ref, dstk_ref, st_ref, dbg_ref, dws_ref, dbs_ref, d_carry, mix_s, dvp_s):
        i = pl.program_id(0)

        @pl.when(i == 0)
        def _():
            st_ref[...] = jnp.zeros_like(st_ref)
            dbg_ref[...] = jnp.zeros_like(dbg_ref)
            dws_ref[...] = jnp.zeros_like(dws_ref)
            dbs_ref[...] = jnp.zeros_like(dbs_ref)
            d_carry[...] = jnp.zeros_like(d_carry)

        def pj(k):
            return proj_ref[:, _col(k)].astype(F32)

        def put(k, val):
            dproj_ref[:, _col(k)] = val.astype(BF16)

        dx1b = dx1_ref[...].astype(BF16)
        dstk_ref[:, _col(2)] = dx1b
        dm = _nt(dx1b, wo_w[...])
        s_a = jax.nn.sigmoid(pj(5) + bg_ref[:, :D])
        s_b = jax.nn.sigmoid(pj(6) + bg_ref[:, D:])
        dpa = dm * s_a
        dpb = dm * s_b
        dpa_b = dpa.astype(BF16)
        dpb_b = dpb.astype(BF16)
        dstk_ref[:, _col(0)] = dpa_b
        dstk_ref[:, _col(1)] = dpb_b
        dga = dpa * pa_ref[...].astype(F32) * (1.0 - s_a)
        dgb = dpb * pb_ref[...].astype(F32) * (1.0 - s_b)
        dbg_ref[0:1, :D] += jnp.sum(dga, axis=0, keepdims=True)
        dbg_ref[0:1, D:] += jnp.sum(dgb, axis=0, keepdims=True)
        put(5, dga)
        put(6, dgb)
        da = _nt(dpa_b, pa_w[...])
        dc = _nt(dpb_b, pb_w[...])

        v = pj(1)
        vg, v_cdf = _gelu(v)
        vh, rv = _rms(vg)
        vp = (vh * gv_ref[...]).astype(BF16)
        wm = _masked_ws(ws_ref[...]).astype(BF16)
        for n in range(nb):
            rows = slice(n * SG, (n + 1) * SG)
            for g in range(N_GROUPS):
                cols = slice(g * SG, (g + 1) * SG)
                mix_s[rows, cols] = _mm(wm[g], vp[rows, cols]) + bs_ref[g]
        u = pj(0)
        ug, u_cdf = _gelu(u)
        put(0, da * mix_s[...] * _gelu_grad(u, u_cdf))
        dmix = da * ug
        dmix_b = dmix.astype(BF16)
        for n in range(nb):
            rows = slice(n * SG, (n + 1) * SG)
            for g in range(N_GROUPS):
                cols = slice(g * SG, (g + 1) * SG)
                blk = dmix_b[rows, cols]
                dws_ref[g] += _nt(blk, vp[rows, cols])
                dbs_ref[g] += dmix[rows, cols]
                dvp_s[rows, cols] = _tn(wm[g], blk)
        dvp = dvp_s[...]
        st_ref[ST_GV:ST_GV + 1, :] += jnp.sum(dvp * vh, axis=0, keepdims=True)
        put(1, _rms_bwd(dvp * gv_ref[...], vh, rv) * _gelu_grad(v, v_cdf))

        bgate, cg, xs = pj(2), pj(3), pj(4)
        q = cg * xs
        has_prev = (i < nt - 1).astype(F32)
        halo = cgh_ref[...].astype(F32) * xsh_ref[...].astype(F32) * has_prev
        q2 = _shift_down(halo, q, 2)
        q1 = _shift_down(halo, q, 1)
        w0, w1, w2 = cw_ref[0:1, :], cw_ref[1:2, :], cw_ref[2:3, :]
        put(2, dc * (w0 * q2 + w1 * q1 + w2 * q))
        dcv = dc * bgate
        st_ref[ST_CONV:ST_CONV + 1, :] += jnp.sum(dcv * q2, axis=0, keepdims=True)
        st_ref[ST_CONV + 1:ST_CONV + 2, :] += jnp.sum(dcv * q1, axis=0, keepdims=True)
        st_ref[ST_CONV + 2:ST_CONV + 3, :] += jnp.sum(dcv * q, axis=0, keepdims=True)
        nxt = d_carry[...]
        dq = w2 * dcv + w1 * _shift_up(dcv, nxt, 1) + w0 * _shift_up(dcv, nxt, 2)
        d_carry[...] = dcv[:d_carry.shape[0], :]
        put(3, dq * xs)
        put(4, dq * cg)

    rev = lambda i: nt - 1 - i
    tile = lambda w: pl.BlockSpec((tt, w), lambda i: (rev(i), 0))
    halo_spec = lambda k: pl.BlockSpec((HALO, D), lambda i: (jnp.maximum(rev(i) * hb - 1, 0), k))
    res = lambda shape: pl.BlockSpec(shape, lambda i: (0,) * len(shape))
    out_shape = (jax.ShapeDtypeStruct((T, IN_COLS), BF16), jax.ShapeDtypeStruct((T, 3 * D), BF16),
                 jax.ShapeDtypeStruct((8, D), F32), jax.ShapeDtypeStruct((8, 2 * D), F32),
                 jax.ShapeDtypeStruct((N_GROUPS, SG, SG), F32), jax.ShapeDtypeStruct((N_GROUPS, SG, SG), F32))
    return _pcall(
        _after(body, 14, deps), name="mixer_bwd", grid=(nt,), out_shape=out_shape,
        in_specs=[tile(D), tile(IN_COLS), halo_spec(3), halo_spec(4), tile(D), tile(D),
                  _const_spec(b_gate.shape), _const_spec(g_v.shape), _const_spec(w_s.shape),
                  _const_spec(b_s3.shape), _const_spec(conv.shape),
                  _const_spec(w_pa.shape), _const_spec(w_pb.shape), _const_spec(w_o.shape)] + [_ANY] * len(deps),
        out_specs=[tile(IN_COLS), tile(3 * D), res((8, D)), res((8, 2 * D)),
                   res((N_GROUPS, SG, SG)), res((N_GROUPS, SG, SG))],
        scratch_shapes=[pltpu.VMEM((8, D), F32), pltpu.VMEM((tt, D), F32), pltpu.VMEM((tt, D), F32)],
        compiler_params=_params(("arbitrary",)),
    )(dx1, proj, proj, proj, pa, pb, b_gate, g_v, w_s, b_s3, conv, w_pa, w_pb, w_o, *deps)


def _in_proj_bwd(dproj, x, dx1, g_mix, w_in, tt, deps=()):
    T = x.shape[0]

    def body(dproj_ref, x_ref, dx1_ref, gmix_ref, win_ref, gx_ref, st_ref):
        @pl.when(pl.program_id(0) == 0)
        def _():
            st_ref[...] = jnp.zeros_like(st_ref)

        dh = _nt(dproj_ref[...], win_ref[...])
        xh, r = _rms(x_ref[...])
        st_ref[0:1, :] += jnp.sum(dh * xh, axis=0, keepdims=True)
        gx_ref[...] = dx1_ref[...] + _rms_bwd(dh * gmix_ref[...], xh, r)

    tile = lambda w: pl.BlockSpec((tt, w), lambda i: (i, 0))
    return _pcall(
        _after(body, 5, deps), name="in_proj_bwd", grid=(T // tt,),
        out_shape=(jax.ShapeDtypeStruct((T, D), F32), jax.ShapeDtypeStruct((8, D), F32)),
        in_specs=[tile(IN_COLS), tile(D), tile(D), _const_spec(g_mix.shape), _const_spec(w_in.shape)]
        + [_ANY] * len(deps),
        out_specs=[tile(D), pl.BlockSpec((8, D), lambda i: (0, 0))],
        compiler_params=_params(("arbitrary",)),
    )(dproj, x, dx1, g_mix, w_in, *deps)


def _weight_grad(name, act, dout, bc, tk, deps=()):
    T, n_in = act.shape
    n_out = dout.shape[1]
    nk = T // tk
    bi = min(n_in, D)

    def body(a_ref, d_ref, o_ref, acc):
        k = pl.program_id(2)

        @pl.when(k == 0)
        def _():
            acc[...] = jnp.zeros_like(acc)

        acc[...] += _tn(a_ref[...], d_ref[...])

        @pl.when(k == nk - 1)
        def _():
            o_ref[...] = acc[...].astype(o_ref.dtype)

    return _pcall(
        _after(body, 2, deps), name=name, grid=(n_in // bi, n_out // bc, nk),
        out_shape=jax.ShapeDtypeStruct((n_in, n_out), BF16),
        in_specs=[pl.BlockSpec((tk, bi), lambda i, j, k: (k, i)), pl.BlockSpec((tk, bc), lambda i, j, k: (k, j))]
        + [_ANY] * len(deps),
        out_specs=pl.BlockSpec((bi, bc), lambda i, j, k: (i, j)),
        scratch_shapes=[pltpu.VMEM((bi, bc), F32)],
        compiler_params=_params(("arbitrary", "arbitrary", "arbitrary")),
    )(act, dout, *deps)


def _weight_grad_stack(name, acts, douts, tk):
    T, n = acts.shape[0], acts.shape[1] // D
    nk = T // tk

    def body(a_ref, d_ref, o_ref, acc):
        k = pl.program_id(1)

        @pl.when(k == 0)
        def _():
            acc[...] = jnp.zeros_like(acc)

        acc[...] += _tn(a_ref[...], d_ref[...])

        @pl.when(k == nk - 1)
        def _():
            o_ref[...] = acc[...].astype(o_ref.dtype)

    tile = pl.BlockSpec((tk, D), lambda w, k: (k, w))
    return _pcall(
        body, name=name, grid=(n, nk), out_shape=jax.ShapeDtypeStruct((n, D, D), BF16),
        in_specs=[tile, tile], out_specs=pl.BlockSpec((None, D, D), lambda w, k: (w, 0, 0)),
        scratch_shapes=[pltpu.VMEM((D, D), F32)],
        compiler_params=_params(("arbitrary", "arbitrary")),
    )(acts, douts)


def _adamw(w, g, m, v):
    m = ADAM_B1 * m + (1.0 - ADAM_B1) * g
    v = ADAM_B2 * v + (1.0 - ADAM_B2) * (g * g)
    m_hat = m / (1.0 - ADAM_B1 ** ADAM_STEP)
    v_hat = v / (1.0 - ADAM_B2 ** ADAM_STEP)
    delta = -ADAM_LR * (m_hat / (jnp.sqrt(v_hat) + ADAM_EPS) + ADAM_WD * w)
    return delta, m, v


def _slot_sum(ref, own, me):
    g = None
    for s in range(N_DEV):
        term = jnp.where(me == s, own, ref[jnp.maximum((me ^ s) - 1, 0)])
        g = term.astype(F32) if g is None else g + term.astype(F32)
    return g


def _reduce_adamw(me, items, steps):
    n = len(items)

    def body(me_ref, *refs):
        ins, outs = refs[:5 * n], refs[5 * n:]
        for j in range(n):
            own_ref, slot_ref, w_ref, m_ref, v_ref = ins[5 * j:5 * j + 5]
            g_out, d_out, m_out, v_out = outs[4 * j:4 * j + 4]
            g = own_ref[...].astype(F32)
            for k in range(N_DEV - 1):
                g = g + slot_ref[k].astype(F32)
            g_out[...] = g
            d_out[...], m_out[...], v_out[...] = _adamw(w_ref[...], g, m_ref[...], v_ref[...])

    in_specs, out_specs, out_shape, args = [], [], [], []
    for partial, own_block, slots, w, m, v, lead in items:
        rows, cols = w.shape
        tr = rows // steps
        tile = pl.BlockSpec((tr, cols), lambda i, me_ref: (i, 0))
        if lead is None:
            in_specs += [
                pl.BlockSpec((tr, cols), lambda i, me_ref, own_block=own_block, tr=tr: own_block(i, me_ref[0], tr)),
                pl.BlockSpec((N_DEV - 1, tr, cols), lambda i, me_ref: (0, i, 0))]
        else:
            in_specs += [
                pl.BlockSpec((None, tr, cols),
                             lambda i, me_ref, own_block=own_block, tr=tr, lead=lead: (lead, *own_block(i, me_ref[0], tr))),
                pl.BlockSpec((N_DEV - 1, None, tr, cols), lambda i, me_ref, lead=lead: (0, lead, i, 0))]
        in_specs += [tile, tile, tile]
        out_specs += [tile] * 4
        out_shape += [jax.ShapeDtypeStruct((rows, cols), F32)] * 4
        args += [partial, slots, w, m, v]
    res = _pcall(
        body, name="reduce_adamw", out_shape=out_shape,
        grid_spec=pltpu.PrefetchScalarGridSpec(
            num_scalar_prefetch=1, grid=(steps,), in_specs=in_specs, out_specs=out_specs),
        compiler_params=_params(("arbitrary",)),
    )(me, *args)
    return [res[4 * j:4 * j + 4] for j in range(n)]


SM_GMIX, SM_GV, SM_GFF, SM_GFIN, SM_LOSS, SM_BGATE, SM_BS, SM_ROWS = 0, 1, 2, 3, 4, 5, 7, 8


def _pack_small(st_ffn, st_mix, st_in, dbg, dbs, conv_rows):
    def body(ffn_ref, mix_ref, in_ref, dbg_ref, dbs_ref, sm_ref, conv_ref):
        sm_ref[SM_GMIX:SM_GMIX + 1, :] = in_ref[0:1, :]
        sm_ref[SM_GV:SM_GV + 1, :] = mix_ref[ST_GV:ST_GV + 1, :]
        sm_ref[SM_GFF:SM_GFF + 1, :] = ffn_ref[ST_GFF:ST_GFF + 1, :]
        sm_ref[SM_GFIN:SM_GFIN + 1, :] = ffn_ref[ST_GFIN:ST_GFIN + 1, :]
        sm_ref[SM_LOSS:SM_LOSS + 1, :] = ffn_ref[ST_LOSS:ST_LOSS + 1, :]
        sm_ref[SM_BGATE:SM_BGATE + 1, :] = dbg_ref[0:1, :D]
        sm_ref[SM_BGATE + 1:SM_BGATE + 2, :] = dbg_ref[0:1, D:]
        for g in range(N_GROUPS):
            sm_ref[SM_BS:SM_BS + 1, g * SG:(g + 1) * SG] = jnp.sum(dbs_ref[g].T, axis=0, keepdims=True)
        conv_ref[...] = jnp.zeros_like(conv_ref)
        for p in range(N_DEV):
            conv_ref[p, 0:conv_rows, :] = mix_ref[ST_CONV:ST_CONV + conv_rows, p * LANE:(p + 1) * LANE]

    return _pcall(
        body, name="pack_small",
        out_shape=(jax.ShapeDtypeStruct((SM_ROWS, D), F32), jax.ShapeDtypeStruct((N_DEV, 8, LANE), F32)),
        in_specs=[_VMEM] * 5, out_specs=[_VMEM] * 2, compiler_params=_params(),
    )(st_ffn, st_mix, st_in, dbg, dbs)


def _small_update(sm_own, sm_slots, ws_own, ws_slots, conv_own, conv_slots, params, conv_rows):
    flat = [a for t in params for a in t]

    def body(smo_ref, sm_ref, wso_ref, ws_ref, convo_ref, conv_ref, *refs):
        ins, outs = refs[:len(flat)], refs[len(flat):]
        loss_ref, outs = outs[0], outs[1:]
        _, me = _position()
        sm = _slot_sum(sm_ref, smo_ref[...], me)
        loss_ref[...] = (0.5 / D) * jnp.sum(sm[SM_LOSS:SM_LOSS + 1, :], axis=1, keepdims=True)

        def update(n, g, at=lambda r: r):
            w_ref, m_ref, v_ref = [at(r) for r in ins[3 * n:3 * n + 3]]
            g_out, d_out, m_out, v_out = [at(r) for r in outs[4 * n:4 * n + 4]]
            g_out[...] = g
            d_out[...], m_out[...], v_out[...] = _adamw(w_ref[...], g, m_ref[...], v_ref[...])

        for n, row in enumerate((SM_GMIX, SM_GV, SM_GFF, SM_GFIN)):
            update(n, sm[row:row + 1, :])
        for half in range(2):
            update(4, sm[SM_BGATE + half:SM_BGATE + half + 1, :], lambda r, half=half: r.at[:, pl.ds(half * D, D)])
        for g in range(N_GROUPS):
            update(5, sm[SM_BS:SM_BS + 1, g * SG:(g + 1) * SG], lambda r, g=g: r.at[0, pl.ds(g, 1), :])
        update(6, _masked_ws(_slot_sum(ws_ref, wso_ref[...], me)), lambda r: r.at[0])
        d_conv = _slot_sum(conv_ref, convo_ref[me], me)
        for j in range(conv_rows):
            update(7, d_conv[j:j + 1, :], lambda r, j=j: r.at[:, pl.ds(j * LANE, LANE)])

    out_shape = [jax.ShapeDtypeStruct((1, 1), F32)]
    for w, _, _ in params:
        out_shape += [jax.ShapeDtypeStruct(w.shape, F32)] * 4
    return _pcall(
        body, name="small_update", out_shape=out_shape,
        in_specs=[_VMEM] * (6 + len(flat)), out_specs=[_VMEM] * len(out_shape), compiler_params=_params(),
    )(sm_own, sm_slots, ws_own, ws_slots, conv_own, conv_slots, *flat)


def kernel(x, norm_mix_g, w_in, b_gate, norm_v_g, w_s, b_s, conv_w, w_proj_a, w_proj_b, w_out, norm_ff_g, w_ff1, w_ff2, norm_final_g, loss_target, m_norm_mix_g, m_w_in, m_b_gate, m_norm_v_g, m_w_s, m_b_s, m_conv_w, m_w_proj_a, m_w_proj_b, m_w_out, m_norm_ff_g, m_w_ff1, m_w_ff2, m_norm_final_g, v_norm_mix_g, v_w_in, v_b_gate, v_norm_v_g, v_w_s, v_b_s, v_conv_w, v_w_proj_a, v_w_proj_b, v_w_out, v_norm_ff_g, v_w_ff1, v_w_ff2, v_norm_final_g):
    T = x.shape[1]
    tt = min(256, T)
    tk = min(4096, T)
    conv_rows = conv_w.shape[1]

    flat = lambda a: a.reshape(1, conv_rows * LANE)
    xs = x.reshape(T, D)
    tgt = loss_target.reshape(T, D)
    g_mix, g_v, g_ff, g_fin = norm_mix_g, norm_v_g, norm_ff_g, norm_final_g.reshape(1, D)
    ws = w_s[0]
    bs3 = b_s.reshape(N_GROUPS, SG, 1)

    c_in, c_ff1 = w_in.shape[2], w_ff1.shape[2]
    r_ff2, r_p = w_ff2.shape[1], w_proj_a.shape[1]
    lane_view = lambda width: (lambda ref, p: _lane_block(ref, p, width))
    row_view = lambda rows: (lambda ref, p: _row_block(ref, p, rows))
    peer_slots = lambda shape, dtype: lax.empty((N_DEV - 1,) + shape, dtype)

    placed = _place_weights(w_in[0], w_ff1[0], w_ff2[0], w_proj_a[0], w_proj_b[0], w_out[0], flat(conv_w))
    mixer_views = [row_view(r_p), row_view(r_p), row_view(r_p), lane_view(LANE)]
    ffn_views = [lane_view(c_ff1), row_view(r_ff2)]
    pairs = ((0, 1), (2, 4), (3, 5), (6, 7))
    in_group = lambda refs, ks: _gather_entries(refs[:1], [lane_view(c_in)], tuple(k for k in ks if k))

    def gather_groups(refs):
        return ([in_group(refs, ks) for ks in pairs]
                + [_gather_entries(refs[1:5], mixer_views), _gather_entries(refs[5:], ffn_views)])

    sems, placed, g_token = _start_copies("gather_start", placed, gather_groups)
    me = _position()[1].reshape(1)
    W_in = placed[0]
    tp = min(2048, T)
    proj = lax.empty((T, IN_COLS), BF16)
    for n, ks in enumerate(pairs):
        W_in, = _wait_copies(f"gather_wait_in_{n}", [W_in], sems[2 * n], sems[2 * n + 1],
                             lambda refs, ks=ks: in_group(refs, ks), proj if n else g_token)
        if n == 0:
            proj, h = _in_proj_blocks(f"in_proj_{n}", ks, me, xs, W_in, proj, c_in, tp, g_mix)
        else:
            proj = _in_proj_blocks(f"in_proj_{n}", ks, me, h, W_in, proj, c_in, tp)
    n = len(pairs)
    PA, PB, WO, conv = _wait_copies(
        "gather_wait_mixer", placed[1:5], sems[2 * n], sems[2 * n + 1],
        lambda refs: _gather_entries(refs, mixer_views), proj)
    acm, pa, pb, x1 = _mixer_fwd(xs, proj, b_gate, g_v, ws, bs3, conv, PA, PB, WO, min(512, T))
    W1, W2 = _wait_copies(
        "gather_wait_ffn", placed[5:], sems[2 * n + 2], sems[2 * n + 3],
        lambda refs: _gather_entries(refs, ffn_views), x1)
    hf, s, dpre, dx2, dx1, st_ffn = _ffn_fwd_bwd(x1, tgt, g_ff, g_fin, W1, W2, min(512, T))

    d_ff2 = _weight_grad("dw_ff2", s, dx2, D, tk)
    d_ff1 = _weight_grad("dw_ff1", hf, dpre, D, tk)
    ff_views = [lane_view(c_ff1), row_view(r_ff2)]

    dproj, dstk, st_mix, dbg, dws, dbs = _mixer_bwd(
        dx1, proj, pa, pb, b_gate, g_v, ws, bs3, conv, PA, PB, WO, tt)
    d_p3 = _weight_grad_stack("dw_proj", acm, dstk, tk)

    def proj_entries(refs):
        d3, dws_ref, land3, land_ws = refs
        entries = []
        for w in range(3):
            entries += _to_peers(lambda pid, me, w=w: d3.at[w, pl.ds(pid * r_p, r_p), :],
                                 lambda k, w=w: land3.at[k - 1, w], by_relation=True)
        return entries + _to_peers(lambda pid, me: dws_ref, lambda k: land_ws.at[k - 1], by_relation=True)

    pf_sems, pf_arrays, p_token = _start_copies(
        "scatter_start_proj_ffn",
        [d_p3, dws, peer_slots((3, r_p, D), BF16), peer_slots(dws.shape, F32),
         d_ff1, d_ff2, peer_slots((D, c_ff1), BF16), peer_slots((r_ff2, D), BF16)],
        lambda refs: [proj_entries(refs[:4]), _scatter_entries(ff_views)(refs[4:])])
    p_sems, p_arrays, ff_sems, ff_arrays = pf_sems[:2], pf_arrays[:4], pf_sems[2:], pf_arrays[4:]

    d_in = _weight_grad("dw_in", h, dproj, D, tk, deps=(p_token,))
    in_views = [lane_view(c_in)]
    in_sems, in_arrays, in_token = _start_copies(
        "scatter_start_in", [d_in, peer_slots((D, c_in), BF16)], lambda refs: [_scatter_entries(in_views)(refs)])
    grad_x, st_in = _in_proj_bwd(dproj, xs, dx1, g_mix, W_in, min(512, T), deps=(in_token,))

    small, d_conv = _pack_small(st_ffn, st_mix, st_in, dbg, dbs, conv_rows)

    def small_entries(refs):
        sm, dc, land_sm, land_dc = refs
        return (_to_peers(lambda pid, me: sm, lambda k: land_sm.at[k - 1], by_relation=True)
                + _to_peers(lambda pid, me: dc.at[pid], lambda k: land_dc.at[k - 1], by_relation=True))

    sm_sems, sm_arrays, sm_token = _start_copies(
        "small_start", [small, d_conv, peer_slots(small.shape, F32), peer_slots(d_conv.shape[1:], F32)],
        lambda refs: [small_entries(refs)])
    d_ff1, d_ff2, s_ff1, s_ff2 = _wait_copies(
        "scatter_wait_ffn", ff_arrays, ff_sems[0], ff_sems[1], _scatter_entries(ff_views), sm_token)
    d_p3, dws, s_p3, s_ws = _wait_copies(
        "scatter_wait_proj", p_arrays, p_sems[0], p_sems[1], proj_entries, sm_token)
    d_in, s_in = _wait_copies("scatter_wait_in", in_arrays, in_sems[0], in_sems[1], _scatter_entries(in_views), sm_token)

    own_cols = lambda i, me, tr: (i, me)
    own_rows = lambda rows: (lambda i, me, tr: (me * (rows // tr) + i, 0))
    names = ["w_in", "w_ff1", "w_ff2", "w_proj_a", "w_proj_b", "w_out"]
    quads = _reduce_adamw(me, [
        (d_in, own_cols, s_in, w_in[0], m_w_in[0], v_w_in[0], None),
        (d_ff1, own_cols, s_ff1, w_ff1[0], m_w_ff1[0], v_w_ff1[0], None),
        (d_ff2, own_rows(r_ff2), s_ff2, w_ff2[0], m_w_ff2[0], v_w_ff2[0], None),
        (d_p3, own_rows(r_p), s_p3, w_proj_a[0], m_w_proj_a[0], v_w_proj_a[0], 0),
        (d_p3, own_rows(r_p), s_p3, w_proj_b[0], m_w_proj_b[0], v_w_proj_b[0], 1),
        (d_p3, own_rows(r_p), s_p3, w_out[0], m_w_out[0], v_w_out[0], 2),
    ], 8)
    big = dict(zip(names, quads))

    one = lambda a: a.reshape(1, D)
    small_params = [
        (norm_mix_g, m_norm_mix_g, v_norm_mix_g),
        (norm_v_g, m_norm_v_g, v_norm_v_g),
        (norm_ff_g, m_norm_ff_g, v_norm_ff_g),
        (one(norm_final_g), one(m_norm_final_g), one(v_norm_final_g)),
        (b_gate, m_b_gate, v_b_gate),
        (b_s, m_b_s, v_b_s),
        (w_s, m_w_s, v_w_s),
        (flat(conv_w), flat(m_conv_w), flat(v_conv_w)),
    ]
    small, d_conv, r_small, r_conv = _wait_copies(
        "small_wait", sm_arrays, sm_sems[0], sm_sems[1], small_entries, quads[0][0])
    res = _small_update(small, r_small, dws, s_ws, d_conv, r_conv, small_params, conv_rows)
    loss = res[0].reshape(())
    names = ["norm_mix_g", "norm_v_g", "norm_ff_g", "norm_final_g", "b_gate", "b_s", "w_s", "conv_w"]
    out = {name: list(res[1 + 4 * n:5 + 4 * n]) for n, name in enumerate(names)}
    out["norm_final_g"] = [q.reshape(norm_final_g.shape) for q in out["norm_final_g"]]
    out["conv_w"] = [q.reshape(conv_w.shape) for q in out["conv_w"]]
    for name, quad in big.items():
        out[name] = [q[None] for q in quad]

    order = ["norm_mix_g", "w_in", "b_gate", "norm_v_g", "w_s", "b_s", "conv_w", "w_proj_a", "w_proj_b", "w_out",
             "norm_ff_g", "w_ff1", "w_ff2", "norm_final_g"]
    grads = [out[n][0] for n in order]
    deltas = [out[n][1] for n in order]
    new_m = [out[n][2] for n in order]
    new_v = [out[n][3] for n in order]
    return (loss, grad_x.reshape(x.shape), *grads, *deltas, *new_m, *new_v)
```
